```python
import math
import jax, jax.numpy as jnp
from jax import lax
import numpy as np

D_MODEL = 1024
BATCH = 8
SEQ = 2048
DEPTH = 2
DEC_BATCH = 128
DEC_SEQ = 1
PAST_LEN = 16384
PAGE_SIZE = 128

RET_HEADS = 8
RET_DK = 64
RET_DV = 64
RET_W = RET_HEADS * RET_DV
RET_CHUNK = 128
ROPE_BASE = 10000.0
WKV_HEADS = 8
WKV_HD = 64
WKV_W = WKV_HEADS * WKV_HD
DECAY_LORA = 64
AAA_LORA = 64
GATE_LORA = 128
SHIFT_W = 3 * WKV_W + DECAY_LORA + AAA_LORA + GATE_LORA
IN_W = 4 * RET_W + SHIFT_W
SSM_GROUP = 16
SSM_GROUPS = D_MODEL // SSM_GROUP
SSM_P = 64
DT_MIN = 1e-3
DT_MAX = 1e-1
D_FF = 4 * D_MODEL
N_RET_LAYERS = (DEPTH + 1) // 2
N_SSM_LAYERS = DEPTH // 2
RMS_EPS = 1e-6
GN_EPS = 1e-5
WKV_GN_EPS = 64e-5

kernel_name = "retnet_rwkv7_s5_hybrid_step"

F32 = jnp.float32


def rmsnorm(x, w):
    xf = x.astype(F32)
    y = xf * lax.rsqrt(jnp.mean(xf * xf, axis=-1, keepdims=True) + RMS_EPS) * w.astype(F32)
    return y.astype(x.dtype)


def head_norm(z, eps):
    mu = jnp.mean(z, axis=-1, keepdims=True)
    zc = z - mu
    return zc * lax.rsqrt(jnp.mean(zc * zc, axis=-1, keepdims=True) + eps)


def rotary(z, pos):
    half = z.shape[-1] // 2
    inv_freq = ROPE_BASE ** (-jnp.arange(half, dtype=F32) / half)
    ang = pos.astype(F32)[:, None] * inv_freq[None, :]
    cos = jnp.cos(ang)[None, :, None, :]
    sin = jnp.sin(ang)[None, :, None, :]
    z1, z2 = z[..., :half], z[..., half:]
    return jnp.concatenate([z1 * cos - z2 * sin, z1 * sin + z2 * cos], axis=-1)


def retention_chunkwise(q, k, v, s0):
    b, t, h, dk = q.shape
    dv = v.shape[-1]
    c = RET_CHUNK if t % RET_CHUNK == 0 else t
    n = t // c
    log_g = jnp.log1p(-jnp.exp2(-5.0 - jnp.arange(h, dtype=F32)))
    idx = jnp.arange(c, dtype=F32)
    rel = (idx[:, None] - idx[None, :])[None, :, :]
    decay_mask = jnp.where(rel >= 0, jnp.exp(rel * log_g[:, None, None]), 0.0)
    qc = q.reshape(b, n, c, h, dk)
    kc = k.reshape(b, n, c, h, dk)
    vc = v.reshape(b, n, c, h, dv)
    scores = jnp.einsum('bnihd,bnjhd->bnhij', qc, kc) * decay_mask
    inner = jnp.einsum('bnhij,bnjhe->bnihe', scores, vc)
    k_decay = jnp.exp((c - 1.0 - idx)[:, None] * log_g[None, :])
    kv = jnp.einsum('bnjhd,bnjhe->bnhde', kc * k_decay[:, :, None], vc)
    chunk_decay = jnp.exp(c * log_g)[:, None, None]

    def step(s, kv_n):
        return chunk_decay * s + kv_n, s

    s_last, s_prev = lax.scan(step, s0, jnp.moveaxis(kv, 1, 0))
    s_prev = jnp.moveaxis(s_prev, 0, 1)
    q_decay = jnp.exp((idx + 1.0)[:, None] * log_g[None, :])
    cross = jnp.einsum('bnihd,bnhde->bnihe', qc * q_decay[:, :, None], s_prev)
    return (inner + cross).reshape(b, t, h, dv), s_last


def wkv7_scan(r, w, k, v, a_vec, b_vec, s0):
    def step(s, inp):
        r_t, w_t, k_t, v_t, a_t, b_t = inp
        sa = jnp.einsum('bhij,bhj->bhi', s, a_t)
        s = s * w_t[:, :, None, :] + sa[..., None] * b_t[:, :, None, :] + v_t[..., None] * k_t[:, :, None, :]
        y = jnp.einsum('bhij,bhj->bhi', s, r_t)
        return s, y

    xs = tuple(jnp.moveaxis(z, 1, 0) for z in (r, w, k, v, a_vec, b_vec))
    s_last, ys = lax.scan(step, s0, xs)
    return jnp.moveaxis(ys, 0, 1), s_last


def retention_wkv_layer(x, pos, ret_s0, wkv_s0, shift0, norm_w, w_in, ret_gn, mu_shift,
                        w0, wB, a0, aB, gB, k_k, k_a, r_k, ln_w, ln_b, w_out):
    b, t, _ = x.shape
    hn = rmsnorm(x, norm_w)
    p = (hn @ w_in).astype(F32)
    ret_p, wkv_p = p[..., :4 * RET_W], p[..., 4 * RET_W:]
    q, k, v, g = jnp.split(ret_p, 4, axis=-1)
    q = rotary(q.reshape(b, t, RET_HEADS, RET_DK), pos)
    k = rotary(k.reshape(b, t, RET_HEADS, RET_DK), pos) * (RET_DK ** -0.5)
    v = v.reshape(b, t, RET_HEADS, RET_DV)
    ret, ret_s = retention_chunkwise(q, k, v, ret_s0.astype(F32))
    ret = head_norm(ret, GN_EPS).reshape(b, t, RET_W) * ret_gn.astype(F32)
    ret_out = jax.nn.silu(g) * ret
    prev = jnp.concatenate([shift0[:, None].astype(F32), wkv_p[:, :-1]], axis=1)
    xs = wkv_p + (prev - wkv_p) * mu_shift.astype(F32)
    new_shift = wkv_p[:, -1]
    s1 = WKV_W
    s2 = 2 * WKV_W
    s3 = 3 * WKV_W
    s4 = s3 + DECAY_LORA
    s5 = s4 + AAA_LORA
    r, kw, vw, wd, ad, gd = jnp.split(xs, [s1, s2, s3, s4, s5], axis=-1)
    w_log = -jax.nn.softplus(-(w0.astype(F32) + jnp.tanh(wd) @ wB.astype(F32))) - 0.5
    decay = jnp.exp(-jnp.exp(w_log))
    a = jax.nn.sigmoid(a0.astype(F32) + ad @ aB.astype(F32))
    gate = jax.nn.sigmoid(gd) @ gB.astype(F32)

    def heads(z):
        return z.reshape(b, t, WKV_HEADS, WKV_HD)

    kk = heads(kw * k_k.astype(F32))
    kk = kk / jnp.maximum(jnp.sqrt(jnp.sum(kk * kk, axis=-1, keepdims=True)), 1e-12)
    k_mod = kw * (1.0 + (a - 1.0) * k_a.astype(F32))
    rh, kh, vh, ah, dh = heads(r), heads(k_mod), heads(vw), heads(a), heads(decay)
    y, wkv_s = wkv7_scan(rh, dh, kh, vh, -kk, kk * ah, wkv_s0.astype(F32))
    y = head_norm(y, WKV_GN_EPS).reshape(b, t, WKV_W) * ln_w.astype(F32) + ln_b.astype(F32)
    bonus = jnp.sum(rh * kh * r_k.astype(F32), axis=-1, keepdims=True) * vh
    wkv_out = (y + bonus.reshape(b, t, WKV_W)) * gate
    out = jnp.concatenate([ret_out, wkv_out], axis=-1).astype(x.dtype) @ w_out
    return x + out, ret_s, wkv_s, new_shift


def s5_layer(x, ssm_re0, ssm_im0, norm_w, lam_re, lam_im, log_dt, B_re, B_im, C_re, C_im, d_skip, w_glu):
    b, t, _ = x.shape
    u = rmsnorm(x, norm_w).astype(F32)
    lam = lax.complex(jnp.minimum(lam_re.astype(F32), -1e-4), lam_im.astype(F32))
    dt = jnp.exp(log_dt.astype(F32))[:, None]
    a_bar = jnp.exp(lam * dt)
    bmat = lax.complex(B_re.astype(F32), B_im.astype(F32))
    b_bar = ((a_bar - 1.0) / lam)[..., None] * bmat
    cmat = lax.complex(C_re.astype(F32), C_im.astype(F32))
    ug = u.reshape(b, t, SSM_GROUPS, SSM_GROUP)
    bu = jnp.einsum('gpc,btgc->tbgp', b_bar, ug)
    h0 = lax.complex(ssm_re0.astype(F32), ssm_im0.astype(F32))
    bu = bu.at[0].add(a_bar[None] * h0)
    a_elems = jnp.broadcast_to(a_bar[None, None], (t, 1, SSM_GROUPS, SSM_P))

    def combine(e1, e2):
        a1, b1 = e1
        a2, b2 = e2
        return a1 * a2, a2 * b1 + b2

    _, states = lax.associative_scan(combine, (a_elems, bu), axis=0)
    y = jnp.einsum('gcp,tbgp->btgc', cmat, states).real.reshape(b, t, D_MODEL)
    y = y + d_skip.astype(F32) * u
    hg = jax.nn.gelu(y, approximate=False).astype(x.dtype)
    val, gt = jnp.split(hg @ w_glu, 2, axis=-1)
    out = val * jax.nn.sigmoid(gt)
    last = states[-1]
    return x + out.astype(x.dtype), jnp.real(last), jnp.imag(last)


def sqrelu_mlp(x, norm_w, w_up, w_down):
    hn = rmsnorm(x, norm_w)
    hid = jnp.square(jax.nn.relu(hn @ w_up))
    return x + hid @ w_down


def trunk(x, pos, ret_s0, wkv_s0, shift0, ssm_re0, ssm_im0, weights):
    (norm_mix, w_in, ret_gn, mu_shift, wkv_w0, wkv_wB, wkv_a0, wkv_aB, wkv_gB, wkv_kk, wkv_ka,
     wkv_rk, wkv_ln_w, wkv_ln_b, w_out, ssm_lambda_re, ssm_lambda_im, ssm_log_dt, ssm_B_re,
     ssm_B_im, ssm_C_re, ssm_C_im, ssm_D, ssm_w_glu, mlp_norm, mlp_up, mlp_down, norm_f) = weights
    ret_list, wkv_list, shift_list, re_list, im_list = [], [], [], [], []
    for layer in range(DEPTH):
        i = layer // 2
        if layer % 2 == 0:
            x, rs, ws, ss = retention_wkv_layer(
                x, pos, ret_s0[i], wkv_s0[i], shift0[i], norm_mix[layer], w_in[i], ret_gn[i],
                mu_shift[i], wkv_w0[i], wkv_wB[i], wkv_a0[i], wkv_aB[i], wkv_gB[i], wkv_kk[i],
                wkv_ka[i], wkv_rk[i], wkv_ln_w[i], wkv_ln_b[i], w_out[i])
            ret_list.append(rs)
            wkv_list.append(ws)
            shift_list.append(ss)
        else:
            x, sre, sim = s5_layer(
                x, ssm_re0[i], ssm_im0[i], norm_mix[layer], ssm_lambda_re[i], ssm_lambda_im[i],
                ssm_log_dt[i], ssm_B_re[i], ssm_B_im[i], ssm_C_re[i], ssm_C_im[i], ssm_D[i],
                ssm_w_glu[i])
            re_list.append(sre)
            im_list.append(sim)
        x = sqrelu_mlp(x, mlp_norm[layer], mlp_up[layer], mlp_down[layer])
    y = rmsnorm(x, norm_f)
    return (y, jnp.stack(ret_list), jnp.stack(wkv_list), jnp.stack(shift_list),
            jnp.stack(re_list), jnp.stack(im_list))


def setup_inputs(seed: int = 0) -> dict:
    key = jax.random.key(seed)
    ks = iter(jax.random.split(key, 64))

    def normal(shape, scale):
        return jax.random.normal(next(ks), shape, F32) * scale

    def gain(shape):
        return 1.0 + 0.02 * jax.random.normal(next(ks), shape, F32)

    def unif(shape, lo, hi):
        return jax.random.uniform(next(ks), shape, F32, lo, hi)

    ne, ns = N_RET_LAYERS, N_SSM_LAYERS
    d = D_MODEL
    inp = {}
    inp['x_prompt'] = normal((BATCH, SEQ, d), 1.0)
    inp['x_sample'] = normal((DEC_BATCH, DEC_SEQ, d), 1.0)
    inp['state_ret'] = normal((ne, DEC_BATCH, RET_HEADS, RET_DK, RET_DV), 0.5)
    inp['state_wkv'] = normal((ne, DEC_BATCH, WKV_HEADS, WKV_HD, WKV_HD), 0.3)
    inp['state_shift'] = normal((ne, DEC_BATCH, SHIFT_W), 1.0)
    inp['state_ssm_re'] = normal((ns, DEC_BATCH, SSM_GROUPS, SSM_P), 0.1)
    inp['state_ssm_im'] = normal((ns, DEC_BATCH, SSM_GROUPS, SSM_P), 0.1)
    inp['norm_mix'] = gain((DEPTH, d))
    inp['w_in'] = normal((ne, d, IN_W), d ** -0.5)
    inp['ret_gn'] = gain((ne, RET_W))
    inp['mu_shift'] = unif((ne, SHIFT_W), 0.0, 1.0)
    inp['wkv_w0'] = unif((ne, WKV_W), -6.0, 1.0)
    inp['wkv_wB'] = normal((ne, DECAY_LORA, WKV_W), 0.1)
    inp['wkv_a0'] = normal((ne, WKV_W), 0.5)
    inp['wkv_aB'] = normal((ne, AAA_LORA, WKV_W), 0.5 * AAA_LORA ** -0.5)
    inp['wkv_gB'] = normal((ne, GATE_LORA, WKV_W), GATE_LORA ** -0.5)
    inp['wkv_kk'] = 0.85 + normal((ne, WKV_W), 0.02)
    inp['wkv_ka'] = gain((ne, WKV_W))
    inp['wkv_rk'] = normal((ne, WKV_HEADS, WKV_HD), 0.1)
    inp['wkv_ln_w'] = gain((ne, WKV_W))
    inp['wkv_ln_b'] = normal((ne, WKV_W), 0.02)
    inp['w_out'] = normal((ne, d, d), d ** -0.5)
    inp['ssm_lambda_re'] = -0.5 + normal((ns, SSM_GROUPS, SSM_P), 0.01)
    inp['ssm_lambda_im'] = (math.pi * jnp.arange(SSM_P, dtype=F32))[None, None, :] + normal((ns, SSM_GROUPS, SSM_P), 0.01)
    inp['ssm_log_dt'] = unif((ns, SSM_GROUPS), math.log(DT_MIN), math.log(DT_MAX))
    inp['ssm_B_re'] = normal((ns, SSM_GROUPS, SSM_P, SSM_GROUP), (2 * SSM_GROUP) ** -0.5)
    inp['ssm_B_im'] = normal((ns, SSM_GROUPS, SSM_P, SSM_GROUP), (2 * SSM_GROUP) ** -0.5)
    inp['ssm_C_re'] = normal((ns, SSM_GROUPS, SSM_GROUP, SSM_P), (2 * SSM_P) ** -0.5)
    inp['ssm_C_im'] = normal((ns, SSM_GROUPS, SSM_GROUP, SSM_P), (2 * SSM_P) ** -0.5)
    inp['ssm_D'] = normal((ns, d), 1.0)
    inp['ssm_w_glu'] = normal((ns, d, 2 * d), d ** -0.5)
    inp['mlp_norm'] = gain((DEPTH, d))
    inp['mlp_up'] = normal((DEPTH, d, D_FF), d ** -0.5)
    inp['mlp_down'] = normal((DEPTH, D_FF, d), D_FF ** -0.5)
    inp['norm_f'] = gain((d,))
    return inp


def reference(x_prompt, x_sample, state_ret, state_wkv, state_shift, state_ssm_re, state_ssm_im,
              norm_mix, w_in, ret_gn, mu_shift, wkv_w0, wkv_wB, wkv_a0, wkv_aB, wkv_gB, wkv_kk,
              wkv_ka, wkv_rk, wkv_ln_w, wkv_ln_b, w_out, ssm_lambda_re, ssm_lambda_im, ssm_log_dt,
              ssm_B_re, ssm_B_im, ssm_C_re, ssm_C_im, ssm_D, ssm_w_glu, mlp_norm, mlp_up, mlp_down,
              norm_f):
    weights = (norm_mix, w_in, ret_gn, mu_shift, wkv_w0, wkv_wB, wkv_a0, wkv_aB, wkv_gB, wkv_kk,
               wkv_ka, wkv_rk, wkv_ln_w, wkv_ln_b, w_out, ssm_lambda_re, ssm_lambda_im, ssm_log_dt,
               ssm_B_re, ssm_B_im, ssm_C_re, ssm_C_im, ssm_D, ssm_w_glu, mlp_norm, mlp_up, mlp_down,
               norm_f)
    bp, tp = x_prompt.shape[0], x_prompt.shape[1]
    ts = x_sample.shape[1]
    ret0 = jnp.zeros((N_RET_LAYERS, bp, RET_HEADS, RET_DK, RET_DV), F32)
    wkv0 = jnp.zeros((N_RET_LAYERS, bp, WKV_HEADS, WKV_HD, WKV_HD), F32)
    shift0 = jnp.zeros((N_RET_LAYERS, bp, SHIFT_W), F32)
    ssm0 = jnp.zeros((N_SSM_LAYERS, bp, SSM_GROUPS, SSM_P), F32)
    pos_p = jnp.arange(tp, dtype=jnp.int32)
    y_prompt, ret_p, wkv_p, shift_p, ssm_re_p, ssm_im_p = trunk(
        x_prompt, pos_p, ret0, wkv0, shift0, ssm0, ssm0, weights)
    pos_s = PAST_LEN + jnp.arange(ts, dtype=jnp.int32)
    y_sample, ret_s, wkv_s, shift_s, ssm_re_s, ssm_im_s = trunk(
        x_sample, pos_s, state_ret, state_wkv, state_shift, state_ssm_re, state_ssm_im, weights)
    return (y_prompt, y_sample, ret_p, wkv_p, shift_p, ssm_re_p, ssm_im_p,
            ret_s, wkv_s, shift_s, ssm_re_s, ssm_im_s)
```

```python
import functools
import math

import jax
import jax.numpy as jnp
from jax import lax
from jax.experimental import pallas as pl
from jax.experimental.pallas import tpu as pltpu

F32 = jnp.float32
BF16 = jnp.bfloat16

D_MODEL = 1024
N_HEADS = 8
HEAD_DIM = 64
HEADS_W = N_HEADS * HEAD_DIM
N_PAIRS = N_HEADS // 2
PAIR_W = 2 * HEAD_DIM
ROPE_BASE = 10000.0
DECAY_LORA = 64
AAA_LORA = 64
GATE_LORA = 128
SHIFT_W = 3 * HEADS_W + DECAY_LORA + AAA_LORA + GATE_LORA
RET_COLS = 4 * HEADS_W
IN_W = RET_COLS + SHIFT_W
SSM_GROUP = 16
SSM_GROUPS = D_MODEL // SSM_GROUP
SSM_P = 64
SSM_BLOCKS = 8
SSM_BLOCK_G = SSM_GROUPS // SSM_BLOCKS
SSM_HALF = SSM_BLOCK_G * SSM_P
SSM_STATE_W = 2 * SSM_BLOCKS * SSM_HALF
D_FF = 4 * D_MODEL
RMS_EPS = 1e-6
GN_EPS = 1e-5
WKV_GN_EPS = 64e-5
PAST_LEN = 16384

MIX_CHUNK = 64
SSM_CHUNK = 64
MLP_ROWS = 512
FF_CHUNK = 1024
SAMPLE_SEQ_BLOCK = 8

VMEM_LIMIT = 56 * 1024 * 1024


def _dot(a, b):
    return jnp.dot(a.astype(BF16), b.astype(BF16), preferred_element_type=F32)


def _dot_nt(a, b):
    return lax.dot_general(a.astype(BF16), b.astype(BF16), (((1,), (1,)), ((), ())),
                           preferred_element_type=F32)


def _dot_tn(a, b):
    return lax.dot_general(a.astype(BF16), b.astype(BF16), (((0,), (0,)), ((), ())),
                           preferred_element_type=F32)


def _split3(x):
    hi = x.astype(BF16)
    r1 = x - hi.astype(F32)
    mid = r1.astype(BF16)
    lo = (r1 - mid.astype(F32)).astype(BF16)
    return hi, mid, lo


def _dot_exact_lhs(a_bf16, x):
    hi, mid, lo = _split3(x)
    f = lambda p: jnp.dot(a_bf16, p, preferred_element_type=F32)
    return f(hi) + f(mid) + f(lo)


def _segsum(x, ones_blk):
    hi, mid, lo = _split3(x)
    outs = []
    for c in range(x.shape[1] // 256):
        sl = slice(c * 256, (c + 1) * 256)
        f = lambda p: jnp.dot(p[:, sl], ones_blk, preferred_element_type=F32)
        outs.append(f(hi) + f(mid) + f(lo))
    return jnp.concatenate(outs, axis=1)


def _rms(x, w):
    return x * lax.rsqrt(jnp.mean(x * x, axis=-1, keepdims=True) + RMS_EPS) * w


def _sigmoid(x):
    return 1.0 / (1.0 + jnp.exp(-x))


def _softplus(x):
    return jnp.maximum(x, 0.0) + jnp.log1p(jnp.exp(-jnp.abs(x)))


def _head_norm(z, eps, ones_blk):
    mu = _segsum(z, ones_blk) * (1.0 / HEAD_DIM)
    zc = z - mu
    var = _segsum(zc * zc, ones_blk) * (1.0 / HEAD_DIM)
    return zc * lax.rsqrt(var + eps)


def _rope(z, cos, sin_signed):
    lane = lax.broadcasted_iota(jnp.int32, (1, HEADS_W), 1) % HEAD_DIM
    swapped = jnp.where(lane < HEAD_DIM // 2,
                        pltpu.roll(z, HEADS_W - HEAD_DIM // 2, axis=1),
                        pltpu.roll(z, HEAD_DIM // 2, axis=1))
    return z * cos + swapped * sin_signed


def _wkv_features(xs, w0, lora_w, a0, g_b, k_k, k_a, r_k, ones_blk):
    r = xs[:, 0:HEADS_W]
    kw = xs[:, HEADS_W:2 * HEADS_W]
    vw = xs[:, 2 * HEADS_W:3 * HEADS_W]
    lo = xs[:, 3 * HEADS_W:3 * HEADS_W + 128]
    lane = lax.broadcasted_iota(jnp.int32, (1, 128), 1)
    lo = jnp.where(lane < DECAY_LORA, jnp.tanh(lo), lo)
    ll = _dot(lo, lora_w)
    w_log = -_softplus(-(w0 + ll[:, 0:HEADS_W])) - 0.5
    log_decay = -jnp.exp(w_log)
    alr = _sigmoid(a0 + ll[:, HEADS_W:2 * HEADS_W])
    gate = _dot(_sigmoid(xs[:, 3 * HEADS_W + 128:SHIFT_W]), g_b)
    kk = kw * k_k
    kk = kk / jnp.maximum(jnp.sqrt(_segsum(kk * kk, ones_blk)), 1e-12)
    k_mod = kw * (1.0 + (alr - 1.0) * k_a)
    bonus = _segsum(r * k_mod * r_k, ones_blk) * vw
    return r, log_decay, k_mod, vw, -kk, kk * alr, gate, bonus


def _stack_masked(x2, m0):
    return jnp.concatenate([jnp.where(m0, x2, 0.0), jnp.where(m0, 0.0, x2)], axis=0)


def _stack_dup(x2):
    return jnp.concatenate([x2, x2], axis=0)


def _mixer_prompt_kernel(
        x_ref, normw_ref, win_ref, cos_ref, sin_ref, qdec_ref, kdec_ref, dmask_ref, sdec_ref,
        retgn_ref, mu_ref, w0_ref, lora_ref, a0_ref, gb_ref, kk_ref, ka_ref, rk_ref, lnw_ref,
        lnb_ref, wout_ref, ones_ref, tril_ref, strict_ref, incl_ref, bd_ref,
        xo_ref, rets_ref, wkvs_ref, shift_ref,
        p_s, cat_s, rs_s, ws_s, carry_s, *, n_seq, chunk):
    i = pl.program_id(0)
    n_steps = pl.num_programs(0)
    C = chunk

    @pl.when(i == 0)
    def _():
        rs_s[...] = jnp.zeros_like(rs_s)
        ws_s[...] = jnp.zeros_like(ws_s)
        carry_s[...] = jnp.zeros_like(carry_s)

    x = x_ref[...].reshape(n_seq * C, D_MODEL)
    hn = _rms(x, normw_ref[...])
    p_s[...] = _dot(hn, win_ref[...])

    m0 = lax.broadcasted_iota(jnp.int32, (1, PAIR_W), 1) < HEAD_DIM
    row0 = lax.broadcasted_iota(jnp.int32, (C, 1), 0) == 0
    ones_blk = ones_ref[...]

    def per_seq(b, carry):
        rows = pl.ds(pl.multiple_of(b * C, C), C)
        q = _rope(p_s[rows, 0:HEADS_W], cos_ref[...], sin_ref[...])
        k = _rope(p_s[rows, HEADS_W:2 * HEADS_W], cos_ref[...], sin_ref[...]) * (HEAD_DIM ** -0.5)
        v = p_s[rows, 2 * HEADS_W:3 * HEADS_W]
        g = p_s[rows, 3 * HEADS_W:4 * HEADS_W]
        qd = q * qdec_ref[...]
        kd = k * kdec_ref[...]
        o_parts = []
        for pr in range(N_PAIRS):
            sl = slice(pr * PAIR_W, (pr + 1) * PAIR_W)
            q2, k2, v2 = q[:, sl], k[:, sl], v[:, sl]
            sc = _dot_nt(_stack_masked(q2, m0), _stack_dup(k2)) * dmask_ref[pr]
            inner = _dot(sc, _stack_dup(v2))
            state = rs_s[pr, b]
            o_parts.append(jnp.where(m0, inner[0:C], inner[C:2 * C]) + _dot(qd[:, sl], state))
            rs_s[pr, b] = sdec_ref[pr] * state + _dot_tn(kd[:, sl], v2) * bd_ref[...]
        o = jnp.concatenate(o_parts, axis=1)
        ret = _head_norm(o, GN_EPS, ones_blk) * retgn_ref[...]
        cat_s[rows, 0:HEADS_W] = (g * _sigmoid(g) * ret).astype(BF16)

        wp = p_s[rows, RET_COLS:IN_W]
        prev = jnp.where(row0, carry_s[pl.ds(b, 1), :], pltpu.roll(wp, 1, axis=0))
        carry_s[pl.ds(b, 1), :] = wp[C - 1:C, :]
        xs = wp + (prev - wp) * mu_ref[...]
        r, lw, k_mod, vw, a_vec, b_vec, gate, bonus = _wkv_features(
            xs, w0_ref[...], lora_ref[...], a0_ref[...], gb_ref[...], kk_ref[...], ka_ref[...],
            rk_ref[...], ones_blk)
        cw = _dot_exact_lhs(tril_ref[...], lw)
        cwl = cw[C - 1:C, :]
        r_t = r * jnp.exp(cw)
        a_t = a_vec * jnp.exp(cw - lw)
        w_inv = jnp.exp(-cw)
        b_t = b_vec * w_inv
        k_t = k_mod * w_inv
        w_end = jnp.exp(cwl - cw)
        b_h = b_vec * w_end
        k_h = k_mod * w_end
        w_all = jnp.exp(cwl)
        y_parts = []
        for pr in range(N_PAIRS):
            sl = slice(pr * PAIR_W, (pr + 1) * PAIR_W)
            state = ws_s[pr, b]
            lhs = jnp.concatenate([_stack_masked(a_t[:, sl], m0), _stack_masked(r_t[:, sl], m0)], axis=0)
            rhs = jnp.concatenate([_stack_dup(b_t[:, sl]), _stack_dup(k_t[:, sl])], axis=0)
            sc = _dot_nt(lhs, rhs)
            n_pow = sc[0:2 * C, 0:2 * C] * strict_ref[...]
            ak = sc[0:2 * C, 2 * C:4 * C] * strict_ref[...]
            rb = sc[2 * C:4 * C, 0:2 * C] * incl_ref[...]
            rk = sc[2 * C:4 * C, 2 * C:4 * C] * incl_ref[...]
            on_state = _dot_nt(lhs, state)
            vv = _stack_dup(vw[:, sl])
            u = on_state[0:2 * C] + _dot(ak, vv)
            n_steps_solve = int(math.log2(C))
            for it in range(n_steps_solve):
                u = u + _dot(n_pow, u)
                if it + 1 < n_steps_solve:
                    n_pow = _dot(n_pow, n_pow)
            uv = jnp.concatenate([u, vv], axis=0)
            y = on_state[2 * C:4 * C] + _dot(jnp.concatenate([rb, rk], axis=1), uv)
            y_parts.append(jnp.where(m0, y[0:C], y[C:2 * C]))
            upd = _dot_tn(uv, jnp.concatenate([_stack_masked(b_h[:, sl], m0),
                                               _stack_masked(k_h[:, sl], m0)], axis=0))
            ws_s[pr, b] = state * w_all[:, sl] + upd * bd_ref[...]
        y = jnp.concatenate(y_parts, axis=1)
        yn = _head_norm(y, WKV_GN_EPS, ones_blk) * lnw_ref[...] + lnb_ref[...]
        cat_s[rows, HEADS_W:2 * HEADS_W] = ((yn + bonus) * gate).astype(BF16)
        return carry

    lax.fori_loop(0, n_seq, per_seq, 0)

    out = jnp.dot(cat_s[...], wout_ref[...], preferred_element_type=F32)
    xo_ref[...] = (x_ref[...].reshape(n_seq * C, D_MODEL) + out).reshape(n_seq, C, D_MODEL)

    @pl.when(i == n_steps - 1)
    def _():
        shift_ref[...] = carry_s[...]
        for b in range(n_seq):
            for pr in range(N_PAIRS):
                rs = rs_s[pr, b]
                ws = ws_s[pr, b]
                rets_ref[b, 2 * pr] = rs[0:HEAD_DIM, 0:HEAD_DIM]
                rets_ref[b, 2 * pr + 1] = rs[HEAD_DIM:PAIR_W, HEAD_DIM:PAIR_W]
                wkvs_ref[b, 2 * pr] = ws[0:HEAD_DIM, 0:HEAD_DIM]
                wkvs_ref[b, 2 * pr + 1] = ws[HEAD_DIM:PAIR_W, HEAD_DIM:PAIR_W]


def _const_spec(shape):
    nd = len(shape)
    return pl.BlockSpec(shape, lambda *_: (0,) * nd)


def _retention_tables(chunk):
    log_g = jnp.log1p(-jnp.exp2(-5.0 - jnp.arange(N_HEADS, dtype=F32)))
    lane_g = jnp.repeat(log_g, HEAD_DIM)[None, :]
    idx = jnp.arange(chunk, dtype=F32)
    qdec = jnp.exp((idx + 1.0)[:, None] * lane_g)
    kdec = jnp.exp((chunk - 1.0 - idx)[:, None] * lane_g)
    rel = idx[:, None] - idx[None, :]
    dm = jnp.where(rel >= 0, jnp.exp(rel[None] * log_g[:, None, None]), 0.0)
    zero = jnp.zeros((chunk, chunk), F32)
    dmask = jnp.stack([jnp.block([[dm[2 * p], zero], [zero, dm[2 * p + 1]]]) for p in range(N_PAIRS)])
    cdec = jnp.exp(chunk * log_g)
    hz = jnp.zeros((HEAD_DIM, HEAD_DIM), F32)
    ho = jnp.ones((HEAD_DIM, HEAD_DIM), F32)
    sdec = jnp.stack([jnp.block([[cdec[2 * p] * ho, hz], [hz, cdec[2 * p + 1] * ho]])
                      for p in range(N_PAIRS)])
    return qdec, kdec, dmask, sdec, cdec


def _rope_tables(pos):
    half = HEAD_DIM // 2
    inv_freq = ROPE_BASE ** (-jnp.arange(half, dtype=F32) / half)
    ang = pos.astype(F32)[:, None] * inv_freq[None, :]
    cos = jnp.cos(ang)
    sin = jnp.sin(ang)
    cos_t = jnp.tile(jnp.concatenate([cos, cos], axis=1), (1, N_HEADS))
    sin_t = jnp.tile(jnp.concatenate([-sin, sin], axis=1), (1, N_HEADS))
    return cos_t, sin_t


def _block_masks(chunk):
    i = jnp.arange(2 * chunk)
    same = (i[:, None] // chunk) == (i[None, :] // chunk)
    strict = (same & (i[:, None] > i[None, :])).astype(F32)
    incl = (same & (i[:, None] >= i[None, :])).astype(F32)
    j = jnp.arange(PAIR_W)
    bd = ((j[:, None] // HEAD_DIM) == (j[None, :] // HEAD_DIM)).astype(F32)
    t = jnp.arange(chunk)
    tril = (t[:, None] >= t[None, :]).astype(BF16)
    o = jnp.arange(256)
    ones_blk = ((o[:, None] // HEAD_DIM) == (o[None, :] // HEAD_DIM)).astype(BF16)
    return strict, incl, bd, tril, ones_blk


def _lora_block(w_b, a_b):
    z = jnp.zeros_like(w_b)
    return jnp.concatenate([jnp.concatenate([w_b, z], axis=1),
                            jnp.concatenate([z, a_b], axis=1)], axis=0).astype(BF16)


def _row(v):
    return v.reshape(1, -1).astype(F32)


def _mixer_prompt(x, lw):
    n_seq, t_len, _ = x.shape
    C = MIX_CHUNK
    n_steps = t_len // C
    cos_t, sin_t = _rope_tables(jnp.arange(t_len, dtype=jnp.int32))
    qdec, kdec, dmask, sdec, _ = _retention_tables(C)
    strict, incl, bd, tril, ones_blk = _block_masks(C)
    in_specs = [
        pl.BlockSpec((n_seq, C, D_MODEL), lambda i: (0, i, 0)),
        _const_spec((1, D_MODEL)),
        _const_spec((D_MODEL, IN_W)),
        pl.BlockSpec((C, HEADS_W), lambda i: (i, 0)),
        pl.BlockSpec((C, HEADS_W), lambda i: (i, 0)),
    ]
    tail = [qdec, kdec, dmask, sdec, lw["ret_gn"], lw["mu"], lw["w0"], lw["lora"], lw["a0"], lw["g_b"],
            lw["k_k"], lw["k_a"], lw["r_k"], lw["ln_w"], lw["ln_b"], lw["w_out"], ones_blk, tril,
            strict, incl, bd]
    in_specs += [_const_spec(a.shape) for a in tail]
    out_shape = (
        jax.ShapeDtypeStruct((n_seq, t_len, D_MODEL), F32),
        jax.ShapeDtypeStruct((n_seq, N_HEADS, HEAD_DIM, HEAD_DIM), F32),
        jax.ShapeDtypeStruct((n_seq, N_HEADS, HEAD_DIM, HEAD_DIM), F32),
        jax.ShapeDtypeStruct((n_seq, SHIFT_W), F32),
    )
    out_specs = (
        pl.BlockSpec((n_seq, C, D_MODEL), lambda i: (0, i, 0)),
        _const_spec((n_seq, N_HEADS, HEAD_DIM, HEAD_DIM)),
        _const_spec((n_seq, N_HEADS, HEAD_DIM, HEAD_DIM)),
        _const_spec((n_seq, SHIFT_W)),
    )
    scratch = [
        pltpu.VMEM((n_seq * C, IN_W), F32),
        pltpu.VMEM((n_seq * C, 2 * HEADS_W), BF16),
        pltpu.VMEM((N_PAIRS, n_seq, PAIR_W, PAIR_W), F32),
        pltpu.VMEM((N_PAIRS, n_seq, PAIR_W, PAIR_W), F32),
        pltpu.VMEM((n_seq, SHIFT_W), F32),
    ]
    return pl.pallas_call(
        functools.partial(_mixer_prompt_kernel, n_seq=n_seq, chunk=C),
        grid=(n_steps,),
        in_specs=in_specs, out_specs=out_specs, out_shape=out_shape, scratch_shapes=scratch,
        compiler_params=pltpu.CompilerParams(dimension_semantics=("arbitrary",),
                                             vmem_limit_bytes=VMEM_LIMIT),
        name="mixer_prompt",
    )(x, lw["norm"], lw["w_in"], cos_t, sin_t, *tail)


def _mixer_sample_pre_kernel(
        x_ref, shift_ref, normw_ref, win_ref, cos_ref, sin_ref, qdec_ref, mu_ref, w0_ref, lora_ref,
        a0_ref, gb_ref, kk_ref, ka_ref, rk_ref, ones_ref, feat_ref, newshift_ref):
    x = x_ref[...]
    p = _dot(_rms(x, normw_ref[...]), win_ref[...])
    q = _rope(p[:, 0:HEADS_W], cos_ref[...], sin_ref[...])
    k = _rope(p[:, HEADS_W:2 * HEADS_W], cos_ref[...], sin_ref[...]) * (HEAD_DIM ** -0.5)
    wp = p[:, RET_COLS:IN_W]
    xs = wp + (shift_ref[...] - wp) * mu_ref[...]
    r, lw, k_mod, vw, a_vec, b_vec, gate, bonus = _wkv_features(
        xs, w0_ref[...], lora_ref[...], a0_ref[...], gb_ref[...], kk_ref[...], ka_ref[...],
        rk_ref[...], ones_ref[...])
    newshift_ref[...] = wp
    feats = [q, q * qdec_ref[...], k, p[:, 2 * HEADS_W:3 * HEADS_W], p[:, 3 * HEADS_W:4 * HEADS_W],
             r, jnp.exp(lw), k_mod, vw, a_vec, b_vec, gate, bonus]
    for n, f in enumerate(feats):
        feat_ref[:, n * HEADS_W:(n + 1) * HEADS_W] = f


_F_Q, _F_QD, _F_K, _F_V, _F_G, _F_R, _F_W, _F_KM, _F_VW, _F_A, _F_B, _F_GATE, _F_BONUS = range(13)
_N_FEATS = 13


def _mixer_sample_state_kernel(feat_ref, sdec_ref, ret_ref, wkv_ref, o_ref, reto_ref, wkvo_ref, *, n_blk):
    def vec(b, n, h):
        c0 = n * HEADS_W + h * HEAD_DIM
        return jnp.broadcast_to(feat_ref[b:b + 1, c0:c0 + HEAD_DIM], (8, HEAD_DIM))

    row = lax.broadcasted_iota(jnp.int32, (8, 1), 0)
    for b in range(n_blk):
        for h in range(N_HEADS):
            q, qd, k, v = vec(b, _F_Q, h), vec(b, _F_QD, h), vec(b, _F_K, h), vec(b, _F_V, h)
            s0 = ret_ref[b, h]
            qk = jnp.sum(q * k, axis=-1, keepdims=True)
            o = qk * v + _dot(qd, s0)
            k1 = jnp.where(row == 0, k, 0.0)
            reto_ref[b, h] = s0 * sdec_ref[h] + _dot_tn(k1, v)
            o_ref[b:b + 1, h * HEAD_DIM:(h + 1) * HEAD_DIM] = o[0:1]

            r, w, km = vec(b, _F_R, h), vec(b, _F_W, h), vec(b, _F_KM, h)
            vw, a, bb = vec(b, _F_VW, h), vec(b, _F_A, h), vec(b, _F_B, h)
            st = wkv_ref[b, h]
            sa = _dot_nt(a, st)
            lhs = jnp.where(row == 0, sa, jnp.where(row == 1, vw, 0.0))
            rhs = jnp.where(row == 0, bb, jnp.where(row == 1, km, 0.0))
            st_new = st * w[0:1] + _dot_tn(lhs, rhs)
            wkvo_ref[b, h] = st_new
            y = _dot_nt(r, st_new)
            o_ref[b:b + 1, HEADS_W + h * HEAD_DIM:HEADS_W + (h + 1) * HEAD_DIM] = y[0:1]


def _mixer_sample_post_kernel(
        x_ref, feat_ref, o_ref, retgn_ref, lnw_ref, lnb_ref, wout_ref, ones_ref, xo_ref):
    ones_blk = ones_ref[...]
    g = feat_ref[:, _F_G * HEADS_W:(_F_G + 1) * HEADS_W]
    gate = feat_ref[:, _F_GATE * HEADS_W:(_F_GATE + 1) * HEADS_W]
    bonus = feat_ref[:, _F_BONUS * HEADS_W:(_F_BONUS + 1) * HEADS_W]
    ret = _head_norm(o_ref[:, 0:HEADS_W], GN_EPS, ones_blk) * retgn_ref[...]
    ret_out = g * _sigmoid(g) * ret
    yn = _head_norm(o_ref[:, HEADS_W:2 * HEADS_W], WKV_GN_EPS, ones_blk) * lnw_ref[...] + lnb_ref[...]
    wkv_out = (yn + bonus) * gate
    cat = jnp.concatenate([ret_out, wkv_out], axis=1)
    xo_ref[...] = x_ref[...] + _dot(cat, wout_ref[...])


def _mixer_sample(x, ret0, wkv0, shift0, lw):
    n = x.shape[0]
    cos_t, sin_t = _rope_tables(jnp.full((1,), PAST_LEN, jnp.int32))
    qdec, _, _, _, cdec = _retention_tables(1)
    _, _, _, _, ones_blk = _block_masks(1)
    args = [x, shift0, lw["norm"], lw["w_in"], cos_t, sin_t, qdec, lw["mu"], lw["w0"], lw["lora"],
            lw["a0"], lw["g_b"], lw["k_k"], lw["k_a"], lw["r_k"], ones_blk]
    feat, new_shift = pl.pallas_call(
        _mixer_sample_pre_kernel,
        grid=(1,),
        in_specs=[_const_spec(a.shape) for a in args],
        out_specs=(_const_spec((n, _N_FEATS * HEADS_W)), _const_spec((n, SHIFT_W))),
        out_shape=(jax.ShapeDtypeStruct((n, _N_FEATS * HEADS_W), F32),
                   jax.ShapeDtypeStruct((n, SHIFT_W), F32)),
        compiler_params=pltpu.CompilerParams(vmem_limit_bytes=VMEM_LIMIT),
        name="mixer_sample_pre",
    )(*args)

    nb = SAMPLE_SEQ_BLOCK
    sdec = jnp.broadcast_to(cdec[:, None, None], (N_HEADS, 1, HEAD_DIM))
    state_spec = pl.BlockSpec((nb, N_HEADS, HEAD_DIM, HEAD_DIM), lambda i: (i, 0, 0, 0))
    o, ret_new, wkv_new = pl.pallas_call(
        functools.partial(_mixer_sample_state_kernel, n_blk=nb),
        grid=(n // nb,),
        in_specs=[pl.BlockSpec((nb, _N_FEATS * HEADS_W), lambda i: (i, 0)),
                  _const_spec(sdec.shape), state_spec, state_spec],
        out_specs=(pl.BlockSpec((nb, 2 * HEADS_W), lambda i: (i, 0)), state_spec, state_spec),
        out_shape=(jax.ShapeDtypeStruct((n, 2 * HEADS_W), F32),
                   jax.ShapeDtypeStruct(ret0.shape, F32), jax.ShapeDtypeStruct(wkv0.shape, F32)),
        compiler_params=pltpu.CompilerParams(dimension_semantics=("arbitrary",),
                                             vmem_limit_bytes=VMEM_LIMIT),
        name="mixer_sample_state",
    )(feat, sdec, ret0, wkv0)

    args = [x, feat, o, lw["ret_gn"], lw["ln_w"], lw["ln_b"], lw["w_out"], ones_blk]
    x1 = pl.pallas_call(
        _mixer_sample_post_kernel,
        grid=(1,),
        in_specs=[_const_spec(a.shape) for a in args],
        out_specs=_const_spec((n, D_MODEL)),
        out_shape=jax.ShapeDtypeStruct((n, D_MODEL), F32),
        compiler_params=pltpu.CompilerParams(vmem_limit_bytes=VMEM_LIMIT),
        name="mixer_sample_post",
    )(*args)
    return x1, ret_new, wkv_new, new_shift


def _mlp_kernel(x_ref, normw_ref, wup_ref, wdown_ref, normf_ref, o_ref, *, final_norm):
    x = x_ref[...]
    hn = _rms(x, normw_ref[...]).astype(BF16)
    acc = x
    for c in range(D_FF // FF_CHUNK):
        sl = slice(c * FF_CHUNK, (c + 1) * FF_CHUNK)
        hid = jnp.dot(hn, wup_ref[:, sl], preferred_element_type=F32)
        hid = jnp.square(jnp.maximum(hid, 0.0)).astype(BF16)
        acc = acc + jnp.dot(hid, wdown_ref[sl, :], preferred_element_type=F32)
    if final_norm:
        acc = _rms(acc, normf_ref[...])
    o_ref[...] = acc


def _mlp(x, norm_w, w_up, w_down, norm_f, final_norm):
    rows = x.shape[0]
    tm = min(MLP_ROWS, rows)
    return pl.pallas_call(
        functools.partial(_mlp_kernel, final_norm=final_norm),
        grid=(rows // tm,),
        in_specs=[pl.BlockSpec((tm, D_MODEL), lambda i: (i, 0)),
                  _const_spec((1, D_MODEL)), _const_spec((D_MODEL, D_FF)),
                  _const_spec((D_FF, D_MODEL)), _const_spec((1, D_MODEL))],
        out_specs=pl.BlockSpec((tm, D_MODEL), lambda i: (i, 0)),
        out_shape=jax.ShapeDtypeStruct((rows, D_MODEL), F32),
        compiler_params=pltpu.CompilerParams(dimension_semantics=("arbitrary",),
                                             vmem_limit_bytes=VMEM_LIMIT),
        name="mlp",
    )(x, norm_w, w_up, w_down, norm_f)


def _ssm_prep_kernel(lre_ref, lim_ref, logdt_ref, bre_ref, bim_ref, are_ref, aim_ref, bbre_ref, bbim_ref):
    lre = jnp.minimum(lre_ref[...], -1e-4)
    lim = lim_ref[...]
    dt = jnp.exp(logdt_ref[...])
    mag = jnp.exp(lre * dt)
    are = mag * jnp.cos(lim * dt)
    aim = mag * jnp.sin(lim * dt)
    are_ref[...] = are
    aim_ref[...] = aim
    den = lre * lre + lim * lim
    nre = are - 1.0
    cre = (nre * lre + aim * lim) / den
    cim = (aim * lre - nre * lim) / den
    bre = bre_ref[...]
    bim = bim_ref[...]
    bbre_ref[...] = cre * bre - cim * bim
    bbim_ref[...] = cre * bim + cim * bre


def _gelu_exact(x):
    return 0.5 * x * (1.0 + lax.erf(x * (2.0 ** -0.5)))


def _ssm_kernel(x_ref, h0_ref, normw_ref, arow_ref, wb_ref, cre_ref, cim_ref, dskip_ref, wglu_ref,
                xo_ref, ht_ref, u_s, bu_s, y_s, h_s, *, n_seq, chunk):
    i = pl.program_id(0)
    n_steps = pl.num_programs(0)
    rows_all = chunk * n_seq

    @pl.when(i == 0)
    def _():
        h_s[...] = h0_ref[...]

    x = x_ref[...].reshape(rows_all, D_MODEL)
    u_s[...] = _rms(x, normw_ref[...])
    for blk in range(SSM_BLOCKS):
        c0 = blk * 2 * SSM_HALF
        bu_s[...] = _dot(u_s[:, blk * 128:(blk + 1) * 128], wb_ref[blk])
        a_re = jnp.broadcast_to(arow_ref[:, c0:c0 + SSM_HALF], (n_seq, SSM_HALF))
        a_im = jnp.broadcast_to(arow_ref[:, c0 + SSM_HALF:c0 + 2 * SSM_HALF], (n_seq, SSM_HALF))

        def step(t, carry):
            h_re, h_im = carry
            rows = pl.ds(pl.multiple_of(t * n_seq, n_seq), n_seq)
            n_re = a_re * h_re - a_im * h_im + bu_s[rows, 0:SSM_HALF]
            n_im = a_re * h_im + a_im * h_re + bu_s[rows, SSM_HALF:2 * SSM_HALF]
            bu_s[rows, 0:SSM_HALF] = n_re
            bu_s[rows, SSM_HALF:2 * SSM_HALF] = n_im
            return n_re, n_im

        h_re, h_im = lax.fori_loop(
            0, chunk, step, (h_s[:, c0:c0 + SSM_HALF], h_s[:, c0 + SSM_HALF:c0 + 2 * SSM_HALF]))
        h_s[:, c0:c0 + SSM_HALF] = h_re
        h_s[:, c0 + SSM_HALF:c0 + 2 * SSM_HALF] = h_im
        y_s[:, blk * 128:(blk + 1) * 128] = (_dot(bu_s[:, 0:SSM_HALF], cre_ref[blk])
                                             - _dot(bu_s[:, SSM_HALF:2 * SSM_HALF], cim_ref[blk]))
    y = y_s[...] + dskip_ref[...] * u_s[...]
    z = _dot(_gelu_exact(y), wglu_ref[...])
    out = z[:, 0:D_MODEL] * _sigmoid(z[:, D_MODEL:2 * D_MODEL])
    xo_ref[...] = (x_ref[...].reshape(rows_all, D_MODEL) + out).reshape(chunk, n_seq, D_MODEL)

    @pl.when(i == n_steps - 1)
    def _():
        ht_ref[...] = h_s[...]


def _ssm_weights(lam_re, lam_im, log_dt, b_re, b_im, c_re, c_im):
    g, p = SSM_GROUPS, SSM_P
    n = g * SSM_GROUP
    rep = lambda z: jnp.repeat(z, SSM_GROUP, axis=0)
    bt_re = jnp.swapaxes(b_re, 1, 2).reshape(n, p)
    bt_im = jnp.swapaxes(b_im, 1, 2).reshape(n, p)
    args = [rep(lam_re), rep(lam_im), rep(log_dt.reshape(g, 1)), bt_re, bt_im]
    a_re, a_im, bb_re, bb_im = pl.pallas_call(
        _ssm_prep_kernel,
        grid=(1,),
        in_specs=[_const_spec(a.shape) for a in args],
        out_specs=tuple(_const_spec((n, p)) for _ in range(4)),
        out_shape=tuple(jax.ShapeDtypeStruct((n, p), F32) for _ in range(4)),
        name="ssm_prep",
    )(*args)
    a_re = a_re[::SSM_GROUP]
    a_im = a_im[::SSM_GROUP]
    eye = jnp.eye(SSM_BLOCK_G, dtype=F32)

    def in_block(bb):
        bb = bb.reshape(SSM_BLOCKS, SSM_BLOCK_G, SSM_GROUP, p)
        return jnp.einsum("bgcp,gh->bgchp", bb, eye).reshape(SSM_BLOCKS, 128, SSM_HALF)

    def out_block(cc):
        cc = cc.reshape(SSM_BLOCKS, SSM_BLOCK_G, SSM_GROUP, p)
        return jnp.einsum("bgcp,gh->bgphc", cc, eye).reshape(SSM_BLOCKS, SSM_HALF, 128)

    w_b = jnp.concatenate([in_block(bb_re), in_block(bb_im)], axis=2).astype(BF16)
    a_row = jnp.concatenate([a_re.reshape(SSM_BLOCKS, SSM_HALF), a_im.reshape(SSM_BLOCKS, SSM_HALF)],
                            axis=1).reshape(1, SSM_STATE_W)
    return a_row, w_b, out_block(c_re).astype(BF16), out_block(c_im).astype(BF16)


def _ssm_state_in(re, im):
    n = re.shape[0]
    return jnp.concatenate([re.reshape(n, SSM_BLOCKS, SSM_HALF), im.reshape(n, SSM_BLOCKS, SSM_HALF)],
                           axis=2).reshape(n, SSM_STATE_W)


def _ssm_state_out(h):
    n = h.shape[0]
    h = h.reshape(n, SSM_BLOCKS, 2, SSM_HALF)
    return (h[:, :, 0].reshape(n, SSM_GROUPS, SSM_P), h[:, :, 1].reshape(n, SSM_GROUPS, SSM_P))


def _ssm_layer(x_tb, h0, sw, chunk):
    t_len, n_seq, _ = x_tb.shape
    rows = chunk * n_seq
    args = [x_tb, h0, sw["norm"], sw["a_row"], sw["w_b"], sw["c_re"], sw["c_im"], sw["d_skip"], sw["w_glu"]]
    in_specs = [pl.BlockSpec((chunk, n_seq, D_MODEL), lambda i: (i, 0, 0))]
    in_specs += [_const_spec(a.shape) for a in args[1:]]
    return pl.pallas_call(
        functools.partial(_ssm_kernel, n_seq=n_seq, chunk=chunk),
        grid=(t_len // chunk,),
        in_specs=in_specs,
        out_specs=(pl.BlockSpec((chunk, n_seq, D_MODEL), lambda i: (i, 0, 0)),
                   _const_spec((n_seq, SSM_STATE_W))),
        out_shape=(jax.ShapeDtypeStruct((t_len, n_seq, D_MODEL), F32),
                   jax.ShapeDtypeStruct((n_seq, SSM_STATE_W), F32)),
        scratch_shapes=[pltpu.VMEM((rows, D_MODEL), F32), pltpu.VMEM((rows, 2 * SSM_HALF), F32),
                        pltpu.VMEM((rows, D_MODEL), F32), pltpu.VMEM((n_seq, SSM_STATE_W), F32)],
        compiler_params=pltpu.CompilerParams(dimension_semantics=("arbitrary",),
                                             vmem_limit_bytes=VMEM_LIMIT),
        name="ssm_layer",
    )(*args)


def kernel(x_prompt, x_sample, state_ret, state_wkv, state_shift, state_ssm_re, state_ssm_im, norm_mix, w_in, ret_gn, mu_shift, wkv_w0, wkv_wB, wkv_a0, wkv_aB, wkv_gB, wkv_kk, wkv_ka, wkv_rk, wkv_ln_w, wkv_ln_b, w_out, ssm_lambda_re, ssm_lambda_im, ssm_log_dt, ssm_B_re, ssm_B_im, ssm_C_re, ssm_C_im, ssm_D, ssm_w_glu, mlp_norm, mlp_up, mlp_down, norm_f):
    lw = dict(
        norm=_row(norm_mix[0]), w_in=w_in[0].astype(BF16), ret_gn=_row(ret_gn[0]), mu=_row(mu_shift[0]),
        w0=_row(wkv_w0[0]), lora=_lora_block(wkv_wB[0], wkv_aB[0]), a0=_row(wkv_a0[0]),
        g_b=wkv_gB[0].astype(BF16), k_k=_row(wkv_kk[0]), k_a=_row(wkv_ka[0]), r_k=_row(wkv_rk[0]),
        ln_w=_row(wkv_ln_w[0]), ln_b=_row(wkv_ln_b[0]), w_out=w_out[0].astype(BF16))
    a_row, w_b, c_re, c_im = _ssm_weights(ssm_lambda_re[0], ssm_lambda_im[0], ssm_log_dt[0], ssm_B_re[0],
                                          ssm_B_im[0], ssm_C_re[0], ssm_C_im[0])
    sw = dict(norm=_row(norm_mix[1]), a_row=a_row, w_b=w_b, c_re=c_re, c_im=c_im,
              d_skip=_row(ssm_D[0]), w_glu=ssm_w_glu[0].astype(BF16))
    up = [mlp_up[l].astype(BF16) for l in range(2)]
    down = [mlp_down[l].astype(BF16) for l in range(2)]
    nf = _row(norm_f)

    n_p, t_p, _ = x_prompt.shape
    x1, ret_p, wkv_p, shift_p = _mixer_prompt(x_prompt, lw)
    x1 = _mlp(x1.reshape(n_p * t_p, D_MODEL), _row(mlp_norm[0]), up[0], down[0], nf, False)
    x1 = jnp.swapaxes(x1.reshape(n_p, t_p, D_MODEL), 0, 1)
    x2, h_p = _ssm_layer(x1, jnp.zeros((n_p, SSM_STATE_W), F32), sw, SSM_CHUNK)
    y_p = _mlp(x2.reshape(t_p * n_p, D_MODEL), _row(mlp_norm[1]), up[1], down[1], nf, True)
    y_prompt = jnp.swapaxes(y_p.reshape(t_p, n_p, D_MODEL), 0, 1)
    ssm_re_p, ssm_im_p = _ssm_state_out(h_p)

    n_s = x_sample.shape[0]
    xs = x_sample.reshape(n_s, D_MODEL)
    xs1, ret_s, wkv_s, shift_s = _mixer_sample(xs, state_ret[0], state_wkv[0], state_shift[0], lw)
    xs1 = _mlp(xs1, _row(mlp_norm[0]), up[0], down[0], nf, False)
    xs2, h_s = _ssm_layer(xs1.reshape(1, n_s, D_MODEL), _ssm_state_in(state_ssm_re[0], state_ssm_im[0]), sw, 1)
    y_s = _mlp(xs2.reshape(n_s, D_MODEL), _row(mlp_norm[1]), up[1], down[1], nf, True)
    ssm_re_s, ssm_im_s = _ssm_state_out(h_s)

    return (y_prompt, y_s.reshape(n_s, 1, D_MODEL),
            ret_p[None], wkv_p[None], shift_p[None], ssm_re_p[None], ssm_im_p[None],
            ret_s[None], wkv_s[None], shift_s[None], ssm_re_s[None], ssm_im_s[None])
```

```python
import functools
import math

import jax
import jax.numpy as jnp
from jax import lax
from jax.experimental import pallas as pl
from jax.experimental.pallas import tpu as pltpu

F32 = jnp.float32
BF16 = jnp.bfloat16

D_MODEL = 1024
N_HEADS = 8
HEAD_DIM = 64
HEADS_W = N_HEADS * HEAD_DIM
N_PAIRS = N_HEADS // 2
PAIR_W = 2 * HEAD_DIM
ROPE_BASE = 10000.0
DECAY_LORA = 64
AAA_LORA = 64
GATE_LORA = 128
SHIFT_W = 3 * HEADS_W + DECAY_LORA + AAA_LORA + GATE_LORA
RET_COLS = 4 * HEADS_W
IN_W = RET_COLS + SHIFT_W
SSM_GROUP = 16
SSM_GROUPS = D_MODEL // SSM_GROUP
SSM_P = 64
SSM_BLOCKS = 8
SSM_BLOCK_G = SSM_GROUPS // SSM_BLOCKS
SSM_HALF = SSM_BLOCK_G * SSM_P
SSM_STATE_W = 2 * SSM_BLOCKS * SSM_HALF
D_FF = 4 * D_MODEL
RMS_EPS = 1e-6
GN_EPS = 1e-5
WKV_GN_EPS = 64e-5
PAST_LEN = 16384

MIX_CHUNK = 64
MIX_GROUP = 4
SSM_CHUNK = 64
MLP_ROWS = 512
FF_CHUNK = 1024
SAMPLE_SEQ_BLOCK = 8

VMEM_LIMIT = 56 * 1024 * 1024


def _dot(a, b):
    return jnp.dot(a.astype(BF16), b.astype(BF16), preferred_element_type=F32)


def _dot_nt(a, b):
    return lax.dot_general(a.astype(BF16), b.astype(BF16), (((1,), (1,)), ((), ())),
                           preferred_element_type=F32)


def _dot_tn(a, b):
    return lax.dot_general(a.astype(BF16), b.astype(BF16), (((0,), (0,)), ((), ())),
                           preferred_element_type=F32)


def _split3(x):
    hi = x.astype(BF16)
    r1 = x - hi.astype(F32)
    mid = r1.astype(BF16)
    lo = (r1 - mid.astype(F32)).astype(BF16)
    return hi, mid, lo


def _dot_exact_lhs(a_bf16, x):
    hi, mid, lo = _split3(x)
    f = lambda p: jnp.dot(a_bf16, p, preferred_element_type=F32)
    return f(hi) + f(mid) + f(lo)


def _segsum(x, ones_blk):
    hi, mid, lo = _split3(x)
    outs = []
    for c in range(x.shape[1] // 256):
        sl = slice(c * 256, (c + 1) * 256)
        f = lambda p: jnp.dot(p[:, sl], ones_blk, preferred_element_type=F32)
        outs.append(f(hi) + f(mid) + f(lo))
    return jnp.concatenate(outs, axis=1)


def _rms(x, w):
    return x * lax.rsqrt(jnp.mean(x * x, axis=-1, keepdims=True) + RMS_EPS) * w


def _sigmoid(x):
    return 1.0 / (1.0 + jnp.exp(-x))


def _softplus(x):
    return jnp.maximum(x, 0.0) + jnp.log1p(jnp.exp(-jnp.abs(x)))


def _head_norm(z, eps, ones_blk):
    mu = _segsum(z, ones_blk) * (1.0 / HEAD_DIM)
    zc = z - mu
    var = _segsum(zc * zc, ones_blk) * (1.0 / HEAD_DIM)
    return zc * lax.rsqrt(var + eps)


def _rope(z, cos, sin_signed):
    lane = lax.broadcasted_iota(jnp.int32, (1, HEADS_W), 1) % HEAD_DIM
    swapped = jnp.where(lane < HEAD_DIM // 2,
                        pltpu.roll(z, HEADS_W - HEAD_DIM // 2, axis=1),
                        pltpu.roll(z, HEAD_DIM // 2, axis=1))
    return z * cos + swapped * sin_signed


def _wkv_features(xs, w0, lora_w, a0, g_b, k_k, k_a, r_k, ones_blk):
    r = xs[:, 0:HEADS_W]
    kw = xs[:, HEADS_W:2 * HEADS_W]
    vw = xs[:, 2 * HEADS_W:3 * HEADS_W]
    lo = xs[:, 3 * HEADS_W:3 * HEADS_W + 128]
    lane = lax.broadcasted_iota(jnp.int32, (1, 128), 1)
    lo = jnp.where(lane < DECAY_LORA, jnp.tanh(lo), lo)
    ll = _dot(lo, lora_w)
    w_log = -_softplus(-(w0 + ll[:, 0:HEADS_W])) - 0.5
    log_decay = -jnp.exp(w_log)
    alr = _sigmoid(a0 + ll[:, HEADS_W:2 * HEADS_W])
    gate = _dot(_sigmoid(xs[:, 3 * HEADS_W + 128:SHIFT_W]), g_b)
    kk = kw * k_k
    kk = kk / jnp.maximum(jnp.sqrt(_segsum(kk * kk, ones_blk)), 1e-12)
    k_mod = kw * (1.0 + (alr - 1.0) * k_a)
    bonus = _segsum(r * k_mod * r_k, ones_blk) * vw
    return r, log_decay, k_mod, vw, -kk, kk * alr, gate, bonus


def _stack_masked(x2, m0):
    return jnp.concatenate([jnp.where(m0, x2, 0.0), jnp.where(m0, 0.0, x2)], axis=0)


def _stack_dup(x2):
    return jnp.concatenate([x2, x2], axis=0)


def _mixer_prompt_kernel(
        x_ref, normw_ref, win_ref, cos_ref, sin_ref, qdec_ref, kdec_ref, dmask_ref, sdec_ref,
        retgn_ref, mu_ref, w0_ref, lora_ref, a0_ref, gb_ref, kk_ref, ka_ref, rk_ref, lnw_ref,
        lnb_ref, wout_ref, ones_ref, tril_ref, strict_ref, incl_ref, bd_ref,
        xo_ref, rets_ref, wkvs_ref, shift_ref,
        p_s, cat_s, rs_s, ws_s, carry_s, *, n_seq, chunk, group):
    i = pl.program_id(0)
    n_steps = pl.num_programs(0)
    C = chunk

    @pl.when(i == 0)
    def _():
        rs_s[...] = jnp.zeros_like(rs_s)
        ws_s[...] = jnp.zeros_like(ws_s)
        carry_s[...] = jnp.zeros_like(carry_s)

    x = x_ref[...].reshape(n_seq * C, D_MODEL)
    hn = _rms(x, normw_ref[...])
    p_s[...] = _dot(hn, win_ref[...])

    m0 = lax.broadcasted_iota(jnp.int32, (1, PAIR_W), 1) < HEAD_DIM
    ones_blk = ones_ref[...]
    NB = group
    R = NB * C
    row_id = lax.broadcasted_iota(jnp.int32, (R, 1), 0)
    tile_rows = lambda ref: jnp.concatenate([ref[...]] * NB, axis=0)
    pairs = range(N_PAIRS)
    sls = [slice(pr * PAIR_W, (pr + 1) * PAIR_W) for pr in pairs]
    chains = [(s, pr) for s in range(NB) for pr in pairs]
    seq_rows = [slice(s * C, (s + 1) * C) for s in range(NB)]

    def per_group(gi, carry):
        rows = pl.ds(pl.multiple_of(gi * R, R), R)
        b0 = gi * NB
        ret_states = {(s, pr): rs_s[pr, b0 + s] for s, pr in chains}
        wkv_states = {(s, pr): ws_s[pr, b0 + s] for s, pr in chains}
        shift_rows = [carry_s[pl.ds(b0 + s, 1), :] for s in range(NB)]
        wp = p_s[rows, RET_COLS:IN_W]
        cos, sin = tile_rows(cos_ref), tile_rows(sin_ref)
        q = _rope(p_s[rows, 0:HEADS_W], cos, sin)
        k = _rope(p_s[rows, HEADS_W:2 * HEADS_W], cos, sin) * (HEAD_DIM ** -0.5)
        v = p_s[rows, 2 * HEADS_W:3 * HEADS_W]
        g = p_s[rows, 3 * HEADS_W:4 * HEADS_W]
        qd = q * tile_rows(qdec_ref)
        kd = k * tile_rows(kdec_ref)
        cut = lambda z, s, pr: z[seq_rows[s], sls[pr]]
        r_sc = {c: _dot_nt(_stack_masked(cut(q, *c), m0), _stack_dup(cut(k, *c))) * dmask_ref[c[1]]
                for c in chains}
        r_inner = {c: _dot(r_sc[c], _stack_dup(cut(v, *c))) for c in chains}
        r_cross = {c: _dot(cut(qd, *c), ret_states[c]) for c in chains}
        new_ret = {c: sdec_ref[c[1]] * ret_states[c] + _dot_tn(cut(kd, *c), cut(v, *c)) * bd_ref[...]
                   for c in chains}
        o = jnp.concatenate(
            [jnp.concatenate([jnp.where(m0, r_inner[(s, pr)][0:C], r_inner[(s, pr)][C:2 * C])
                              + r_cross[(s, pr)] for pr in pairs], axis=1) for s in range(NB)], axis=0)
        ret = _head_norm(o, GN_EPS, ones_blk) * retgn_ref[...]
        ret_out = (g * _sigmoid(g) * ret).astype(BF16)

        prev = pltpu.roll(wp, 1, axis=0)
        for s in range(NB):
            prev = jnp.where(row_id == s * C, shift_rows[s], prev)
        xs = wp + (prev - wp) * mu_ref[...]
        r, lw, k_mod, vw, a_vec, b_vec, gate, bonus = _wkv_features(
            xs, w0_ref[...], lora_ref[...], a0_ref[...], gb_ref[...], kk_ref[...], ka_ref[...],
            rk_ref[...], ones_blk)
        cw = _dot_exact_lhs(tril_ref[...], lw)
        cw_last = [cw[s * C + C - 1:(s + 1) * C, :] for s in range(NB)]
        cwl = jnp.concatenate([jnp.broadcast_to(z, (C, HEADS_W)) for z in cw_last], axis=0)
        r_t = r * jnp.exp(cw)
        a_t = a_vec * jnp.exp(cw - lw)
        w_inv = jnp.exp(-cw)
        b_t = b_vec * w_inv
        k_t = k_mod * w_inv
        w_end = jnp.exp(cwl - cw)
        b_h = b_vec * w_end
        k_h = k_mod * w_end
        w_all = [jnp.exp(z) for z in cw_last]
        lhs = {c: jnp.concatenate([_stack_masked(cut(a_t, *c), m0), _stack_masked(cut(r_t, *c), m0)], axis=0)
               for c in chains}
        sc = {c: _dot_nt(lhs[c], jnp.concatenate([_stack_dup(cut(b_t, *c)), _stack_dup(cut(k_t, *c))], axis=0))
              for c in chains}
        on_state = {c: _dot_nt(lhs[c], wkv_states[c]) for c in chains}
        vv = {c: _stack_dup(cut(vw, *c)) for c in chains}
        n_pow = {c: sc[c][0:2 * C, 0:2 * C] * strict_ref[...] for c in chains}
        u = {c: on_state[c][0:2 * C] + _dot(sc[c][0:2 * C, 2 * C:4 * C] * strict_ref[...], vv[c])
             for c in chains}
        n_steps_solve = int(math.log2(C))
        for it in range(n_steps_solve):
            u = {c: u[c] + _dot(n_pow[c], u[c]) for c in chains}
            if it + 1 < n_steps_solve:
                n_pow = {c: _dot(n_pow[c], n_pow[c]) for c in chains}
        uv = {c: jnp.concatenate([u[c], vv[c]], axis=0) for c in chains}
        y_st = {c: on_state[c][2 * C:4 * C] + _dot(
            jnp.concatenate([sc[c][2 * C:4 * C, 0:2 * C] * incl_ref[...],
                             sc[c][2 * C:4 * C, 2 * C:4 * C] * incl_ref[...]], axis=1), uv[c]) for c in chains}
        new_wkv = {c: wkv_states[c] * w_all[c[0]][:, sls[c[1]]] + bd_ref[...] * _dot_tn(
            uv[c], jnp.concatenate([_stack_masked(cut(b_h, *c), m0), _stack_masked(cut(k_h, *c), m0)], axis=0))
            for c in chains}
        y = jnp.concatenate(
            [jnp.concatenate([jnp.where(m0, y_st[(s, pr)][0:C], y_st[(s, pr)][C:2 * C]) for pr in pairs], axis=1)
             for s in range(NB)], axis=0)
        yn = _head_norm(y, WKV_GN_EPS, ones_blk) * lnw_ref[...] + lnb_ref[...]
        cat_s[rows, 0:HEADS_W] = ret_out
        cat_s[rows, HEADS_W:2 * HEADS_W] = ((yn + bonus) * gate).astype(BF16)
        for s in range(NB):
            carry_s[pl.ds(b0 + s, 1), :] = wp[s * C + C - 1:(s + 1) * C, :]
        for s, pr in chains:
            rs_s[pr, b0 + s] = new_ret[(s, pr)]
            ws_s[pr, b0 + s] = new_wkv[(s, pr)]
        return carry

    lax.fori_loop(0, n_seq // NB, per_group, 0)

    out = jnp.dot(cat_s[...], wout_ref[...], preferred_element_type=F32)
    xo_ref[...] = (x_ref[...].reshape(n_seq * C, D_MODEL) + out).reshape(n_seq, C, D_MODEL)

    @pl.when(i == n_steps - 1)
    def _():
        shift_ref[...] = carry_s[...]
        for b in range(n_seq):
            for pr in range(N_PAIRS):
                rs = rs_s[pr, b]
                ws = ws_s[pr, b]
                rets_ref[b, 2 * pr] = rs[0:HEAD_DIM, 0:HEAD_DIM]
                rets_ref[b, 2 * pr + 1] = rs[HEAD_DIM:PAIR_W, HEAD_DIM:PAIR_W]
                wkvs_ref[b, 2 * pr] = ws[0:HEAD_DIM, 0:HEAD_DIM]
                wkvs_ref[b, 2 * pr + 1] = ws[HEAD_DIM:PAIR_W, HEAD_DIM:PAIR_W]


def _const_spec(shape):
    nd = len(shape)
    return pl.BlockSpec(shape, lambda *_: (0,) * nd)


def _retention_tables(chunk):
    log_g = jnp.log1p(-jnp.exp2(-5.0 - jnp.arange(N_HEADS, dtype=F32)))
    lane_g = jnp.repeat(log_g, HEAD_DIM)[None, :]
    idx = jnp.arange(chunk, dtype=F32)
    qdec = jnp.exp((idx + 1.0)[:, None] * lane_g)
    kdec = jnp.exp((chunk - 1.0 - idx)[:, None] * lane_g)
    rel = idx[:, None] - idx[None, :]
    dm = jnp.where(rel >= 0, jnp.exp(rel[None] * log_g[:, None, None]), 0.0)
    zero = jnp.zeros((chunk, chunk), F32)
    dmask = jnp.stack([jnp.block([[dm[2 * p], zero], [zero, dm[2 * p + 1]]]) for p in range(N_PAIRS)])
    cdec = jnp.exp(chunk * log_g)
    hz = jnp.zeros((HEAD_DIM, HEAD_DIM), F32)
    ho = jnp.ones((HEAD_DIM, HEAD_DIM), F32)
    sdec = jnp.stack([jnp.block([[cdec[2 * p] * ho, hz], [hz, cdec[2 * p + 1] * ho]])
                      for p in range(N_PAIRS)])
    return qdec, kdec, dmask, sdec, cdec


def _rope_tables(pos):
    half = HEAD_DIM // 2
    inv_freq = ROPE_BASE ** (-jnp.arange(half, dtype=F32) / half)
    ang = pos.astype(F32)[:, None] * inv_freq[None, :]
    cos = jnp.cos(ang)
    sin = jnp.sin(ang)
    cos_t = jnp.tile(jnp.concatenate([cos, cos], axis=1), (1, N_HEADS))
    sin_t = jnp.tile(jnp.concatenate([-sin, sin], axis=1), (1, N_HEADS))
    return cos_t, sin_t


def _block_masks(chunk):
    i = jnp.arange(2 * chunk)
    same = (i[:, None] // chunk) == (i[None, :] // chunk)
    strict = (same & (i[:, None] > i[None, :])).astype(F32)
    incl = (same & (i[:, None] >= i[None, :])).astype(F32)
    j = jnp.arange(PAIR_W)
    bd = ((j[:, None] // HEAD_DIM) == (j[None, :] // HEAD_DIM)).astype(F32)
    t = jnp.arange(chunk)
    tril = (t[:, None] >= t[None, :]).astype(BF16)
    o = jnp.arange(256)
    ones_blk = ((o[:, None] // HEAD_DIM) == (o[None, :] // HEAD_DIM)).astype(BF16)
    return strict, incl, bd, tril, ones_blk


def _lora_block(w_b, a_b):
    z = jnp.zeros_like(w_b)
    return jnp.concatenate([jnp.concatenate([w_b, z], axis=1),
                            jnp.concatenate([z, a_b], axis=1)], axis=0).astype(BF16)


def _row(v):
    return v.reshape(1, -1).astype(F32)


def _mixer_prompt(x, lw):
    n_seq, t_len, _ = x.shape
    C = MIX_CHUNK
    n_steps = t_len // C
    cos_t, sin_t = _rope_tables(jnp.arange(t_len, dtype=jnp.int32))
    qdec, kdec, dmask, sdec, _ = _retention_tables(C)
    strict, incl, bd, tril, ones_blk = _block_masks(C)
    tril = jnp.kron(jnp.eye(MIX_GROUP, dtype=BF16), tril)
    in_specs = [
        pl.BlockSpec((n_seq, C, D_MODEL), lambda i: (0, i, 0)),
        _const_spec((1, D_MODEL)),
        _const_spec((D_MODEL, IN_W)),
        pl.BlockSpec((C, HEADS_W), lambda i: (i, 0)),
        pl.BlockSpec((C, HEADS_W), lambda i: (i, 0)),
    ]
    tail = [qdec, kdec, dmask, sdec, lw["ret_gn"], lw["mu"], lw["w0"], lw["lora"], lw["a0"], lw["g_b"],
            lw["k_k"], lw["k_a"], lw["r_k"], lw["ln_w"], lw["ln_b"], lw["w_out"], ones_blk, tril,
            strict, incl, bd]
    in_specs += [_const_spec(a.shape) for a in tail]
    out_shape = (
        jax.ShapeDtypeStruct((n_seq, t_len, D_MODEL), F32),
        jax.ShapeDtypeStruct((n_seq, N_HEADS, HEAD_DIM, HEAD_DIM), F32),
        jax.ShapeDtypeStruct((n_seq, N_HEADS, HEAD_DIM, HEAD_DIM), F32),
        jax.ShapeDtypeStruct((n_seq, SHIFT_W), F32),
    )
    out_specs = (
        pl.BlockSpec((n_seq, C, D_MODEL), lambda i: (0, i, 0)),
        _const_spec((n_seq, N_HEADS, HEAD_DIM, HEAD_DIM)),
        _const_spec((n_seq, N_HEADS, HEAD_DIM, HEAD_DIM)),
        _const_spec((n_seq, SHIFT_W)),
    )
    scratch = [
        pltpu.VMEM((n_seq * C, IN_W), F32),
        pltpu.VMEM((n_seq * C, 2 * HEADS_W), BF16),
        pltpu.VMEM((N_PAIRS, n_seq, PAIR_W, PAIR_W), F32),
        pltpu.VMEM((N_PAIRS, n_seq, PAIR_W, PAIR_W), F32),
        pltpu.VMEM((n_seq, SHIFT_W), F32),
    ]
    return pl.pallas_call(
        functools.partial(_mixer_prompt_kernel, n_seq=n_seq, chunk=C, group=MIX_GROUP),
        grid=(n_steps,),
        in_specs=in_specs, out_specs=out_specs, out_shape=out_shape, scratch_shapes=scratch,
        compiler_params=pltpu.CompilerParams(dimension_semantics=("arbitrary",),
                                             vmem_limit_bytes=VMEM_LIMIT),
        name="mixer_prompt",
    )(x, lw["norm"], lw["w_in"], cos_t, sin_t, *tail)


def _mixer_sample_pre_kernel(
        x_ref, shift_ref, normw_ref, win_ref, cos_ref, sin_ref, qdec_ref, mu_ref, w0_ref, lora_ref,
        a0_ref, gb_ref, kk_ref, ka_ref, rk_ref, ones_ref, feat_ref, newshift_ref):
    x = x_ref[...]
    p = _dot(_rms(x, normw_ref[...]), win_ref[...])
    q = _rope(p[:, 0:HEADS_W], cos_ref[...], sin_ref[...])
    k = _rope(p[:, HEADS_W:2 * HEADS_W], cos_ref[...], sin_ref[...]) * (HEAD_DIM ** -0.5)
    wp = p[:, RET_COLS:IN_W]
    xs = wp + (shift_ref[...] - wp) * mu_ref[...]
    r, lw, k_mod, vw, a_vec, b_vec, gate, bonus = _wkv_features(
        xs, w0_ref[...], lora_ref[...], a0_ref[...], gb_ref[...], kk_ref[...], ka_ref[...],
        rk_ref[...], ones_ref[...])
    newshift_ref[...] = wp
    feats = [q, q * qdec_ref[...], k, p[:, 2 * HEADS_W:3 * HEADS_W], p[:, 3 * HEADS_W:4 * HEADS_W],
             r, jnp.exp(lw), k_mod, vw, a_vec, b_vec, gate, bonus]
    for n, f in enumerate(feats):
        feat_ref[:, n * HEADS_W:(n + 1) * HEADS_W] = f


_F_Q, _F_QD, _F_K, _F_V, _F_G, _F_R, _F_W, _F_KM, _F_VW, _F_A, _F_B, _F_GATE, _F_BONUS = range(13)
_N_FEATS = 13


def _mixer_sample_state_kernel(feat_ref, sdec_ref, ret_ref, wkv_ref, o_ref, reto_ref, wkvo_ref, *, n_blk):
    def vec(b, n, h):
        c0 = n * HEADS_W + h * HEAD_DIM
        return jnp.broadcast_to(feat_ref[b:b + 1, c0:c0 + HEAD_DIM], (8, HEAD_DIM))

    row = lax.broadcasted_iota(jnp.int32, (8, 1), 0)
    keys = [(b, h) for b in range(n_blk) for h in range(N_HEADS)]
    s0 = {c: ret_ref[c[0], c[1]] for c in keys}
    st = {c: wkv_ref[c[0], c[1]] for c in keys}
    cross = {c: _dot(vec(c[0], _F_QD, c[1]), s0[c]) for c in keys}
    sa = {c: _dot_nt(vec(c[0], _F_A, c[1]), st[c]) for c in keys}
    kv = {c: _dot_tn(jnp.where(row == 0, vec(c[0], _F_K, c[1]), 0.0), vec(c[0], _F_V, c[1])) for c in keys}
    outer = {c: _dot_tn(jnp.where(row == 0, sa[c], jnp.where(row == 1, vec(c[0], _F_VW, c[1]), 0.0)),
                        jnp.where(row == 0, vec(c[0], _F_B, c[1]),
                                  jnp.where(row == 1, vec(c[0], _F_KM, c[1]), 0.0))) for c in keys}
    st_new = {c: st[c] * vec(c[0], _F_W, c[1])[0:1] + outer[c] for c in keys}
    y = {c: _dot_nt(vec(c[0], _F_R, c[1]), st_new[c]) for c in keys}
    for b in range(n_blk):
        o_row = []
        for h in range(N_HEADS):
            q, k, v = vec(b, _F_Q, h), vec(b, _F_K, h), vec(b, _F_V, h)
            qk = jnp.sum(q * k, axis=-1, keepdims=True)
            o_row.append((qk * v + cross[(b, h)])[0:1])
        o_ref[b:b + 1, :] = jnp.concatenate(o_row + [y[(b, h)][0:1] for h in range(N_HEADS)], axis=1)
    for c in keys:
        reto_ref[c[0], c[1]] = s0[c] * sdec_ref[c[1]] + kv[c]
        wkvo_ref[c[0], c[1]] = st_new[c]


def _mixer_sample_post_kernel(
        x_ref, feat_ref, o_ref, retgn_ref, lnw_ref, lnb_ref, wout_ref, ones_ref, xo_ref):
    ones_blk = ones_ref[...]
    g = feat_ref[:, _F_G * HEADS_W:(_F_G + 1) * HEADS_W]
    gate = feat_ref[:, _F_GATE * HEADS_W:(_F_GATE + 1) * HEADS_W]
    bonus = feat_ref[:, _F_BONUS * HEADS_W:(_F_BONUS + 1) * HEADS_W]
    ret = _head_norm(o_ref[:, 0:HEADS_W], GN_EPS, ones_blk) * retgn_ref[...]
    ret_out = g * _sigmoid(g) * ret
    yn = _head_norm(o_ref[:, HEADS_W:2 * HEADS_W], WKV_GN_EPS, ones_blk) * lnw_ref[...] + lnb_ref[...]
    wkv_out = (yn + bonus) * gate
    cat = jnp.concatenate([ret_out, wkv_out], axis=1)
    xo_ref[...] = x_ref[...] + _dot(cat, wout_ref[...])


def _mixer_sample(x, ret0, wkv0, shift0, lw):
    n = x.shape[0]
    cos_t, sin_t = _rope_tables(jnp.full((1,), PAST_LEN, jnp.int32))
    qdec, _, _, _, cdec = _retention_tables(1)
    _, _, _, _, ones_blk = _block_masks(1)
    args = [x, shift0, lw["norm"], lw["w_in"], cos_t, sin_t, qdec, lw["mu"], lw["w0"], lw["lora"],
            lw["a0"], lw["g_b"], lw["k_k"], lw["k_a"], lw["r_k"], ones_blk]
    feat, new_shift = pl.pallas_call(
        _mixer_sample_pre_kernel,
        grid=(1,),
        in_specs=[_const_spec(a.shape) for a in args],
        out_specs=(_const_spec((n, _N_FEATS * HEADS_W)), _const_spec((n, SHIFT_W))),
        out_shape=(jax.ShapeDtypeStruct((n, _N_FEATS * HEADS_W), F32),
                   jax.ShapeDtypeStruct((n, SHIFT_W), F32)),
        compiler_params=pltpu.CompilerParams(vmem_limit_bytes=VMEM_LIMIT),
        name="mixer_sample_pre",
    )(*args)

    nb = SAMPLE_SEQ_BLOCK
    sdec = jnp.broadcast_to(cdec[:, None, None], (N_HEADS, 1, HEAD_DIM))
    state_spec = pl.BlockSpec((nb, N_HEADS, HEAD_DIM, HEAD_DIM), lambda i: (i, 0, 0, 0))
    o, ret_new, wkv_new = pl.pallas_call(
        functools.partial(_mixer_sample_state_kernel, n_blk=nb),
        grid=(n // nb,),
        in_specs=[pl.BlockSpec((nb, _N_FEATS * HEADS_W), lambda i: (i, 0)),
                  _const_spec(sdec.shape), state_spec, state_spec],
        out_specs=(pl.BlockSpec((nb, 2 * HEADS_W), lambda i: (i, 0)), state_spec, state_spec),
        out_shape=(jax.ShapeDtypeStruct((n, 2 * HEADS_W), F32),
                   jax.ShapeDtypeStruct(ret0.shape, F32), jax.ShapeDtypeStruct(wkv0.shape, F32)),
        compiler_params=pltpu.CompilerParams(dimension_semantics=("arbitrary",),
                                             vmem_limit_bytes=VMEM_LIMIT),
        name="mixer_sample_state",
    )(feat, sdec, ret0, wkv0)

    args = [x, feat, o, lw["ret_gn"], lw["ln_w"], lw["ln_b"], lw["w_out"], ones_blk]
    x1 = pl.pallas_call(
        _mixer_sample_post_kernel,
        grid=(1,),
        in_specs=[_const_spec(a.shape) for a in args],
        out_specs=_const_spec((n, D_MODEL)),
        out_shape=jax.ShapeDtypeStruct((n, D_MODEL), F32),
        compiler_params=pltpu.CompilerParams(vmem_limit_bytes=VMEM_LIMIT),
        name="mixer_sample_post",
    )(*args)
    return x1, ret_new, wkv_new, new_shift


def _mlp_kernel(x_ref, normw_ref, wup_ref, wdown_ref, normf_ref, o_ref, *, final_norm):
    x = x_ref[...]
    hn = _rms(x, normw_ref[...]).astype(BF16)
    acc = x
    for c in range(D_FF // FF_CHUNK):
        sl = slice(c * FF_CHUNK, (c + 1) * FF_CHUNK)
        hid = jnp.dot(hn, wup_ref[:, sl], preferred_element_type=F32)
        hid = jnp.square(jnp.maximum(hid, 0.0)).astype(BF16)
        acc = acc + jnp.dot(hid, wdown_ref[sl, :], preferred_element_type=F32)
    if final_norm:
        acc = _rms(acc, normf_ref[...])
    o_ref[...] = acc


def _mlp(x, norm_w, w_up, w_down, norm_f, final_norm):
    rows = x.shape[0]
    tm = min(MLP_ROWS, rows)
    return pl.pallas_call(
        functools.partial(_mlp_kernel, final_norm=final_norm),
        grid=(rows // tm,),
        in_specs=[pl.BlockSpec((tm, D_MODEL), lambda i: (i, 0)),
                  _const_spec((1, D_MODEL)), _const_spec((D_MODEL, D_FF)),
                  _const_spec((D_FF, D_MODEL)), _const_spec((1, D_MODEL))],
        out_specs=pl.BlockSpec((tm, D_MODEL), lambda i: (i, 0)),
        out_shape=jax.ShapeDtypeStruct((rows, D_MODEL), F32),
        compiler_params=pltpu.CompilerParams(dimension_semantics=("arbitrary",),
                                             vmem_limit_bytes=VMEM_LIMIT),
        name="mlp",
    )(x, norm_w, w_up, w_down, norm_f)


def _ssm_prep_kernel(lre_ref, lim_ref, logdt_ref, bre_ref, bim_ref, are_ref, aim_ref, bbre_ref, bbim_ref):
    lre = jnp.minimum(lre_ref[...], -1e-4)
    lim = lim_ref[...]
    dt = jnp.exp(logdt_ref[...])
    mag = jnp.exp(lre * dt)
    are = mag * jnp.cos(lim * dt)
    aim = mag * jnp.sin(lim * dt)
    are_ref[...] = are
    aim_ref[...] = aim
    den = lre * lre + lim * lim
    nre = are - 1.0
    cre = (nre * lre + aim * lim) / den
    cim = (aim * lre - nre * lim) / den
    bre = bre_ref[...]
    bim = bim_ref[...]
    bbre_ref[...] = cre * bre - cim * bim
    bbim_ref[...] = cre * bim + cim * bre


def _gelu_exact(x):
    return 0.5 * x * (1.0 + lax.erf(x * (2.0 ** -0.5)))


def _ssm_kernel(x_ref, h0_ref, normw_ref, arow_ref, wb_ref, cre_ref, cim_ref, dskip_ref, wglu_ref,
                xo_ref, ht_ref, u_s, bu_s, y_s, h_s, *, n_seq, chunk):
    i = pl.program_id(0)
    n_steps = pl.num_programs(0)
    rows_all = chunk * n_seq

    @pl.when(i == 0)
    def _():
        h_s[...] = h0_ref[...]

    x = x_ref[...].reshape(rows_all, D_MODEL)
    u_s[...] = _rms(x, normw_ref[...])
    for blk in range(SSM_BLOCKS):
        c0 = blk * 2 * SSM_HALF
        bu_s[...] = _dot(u_s[:, blk * 128:(blk + 1) * 128], wb_ref[blk])
        a_re = jnp.broadcast_to(arow_ref[:, c0:c0 + SSM_HALF], (n_seq, SSM_HALF))
        a_im = jnp.broadcast_to(arow_ref[:, c0 + SSM_HALF:c0 + 2 * SSM_HALF], (n_seq, SSM_HALF))

        def step(t, carry):
            h_re, h_im = carry
            rows = pl.ds(pl.multiple_of(t * n_seq, n_seq), n_seq)
            n_re = a_re * h_re - a_im * h_im + bu_s[rows, 0:SSM_HALF]
            n_im = a_re * h_im + a_im * h_re + bu_s[rows, SSM_HALF:2 * SSM_HALF]
            bu_s[rows, 0:SSM_HALF] = n_re
            bu_s[rows, SSM_HALF:2 * SSM_HALF] = n_im
            return n_re, n_im

        h_re, h_im = lax.fori_loop(
            0, chunk, step, (h_s[:, c0:c0 + SSM_HALF], h_s[:, c0 + SSM_HALF:c0 + 2 * SSM_HALF]))
        h_s[:, c0:c0 + SSM_HALF] = h_re
        h_s[:, c0 + SSM_HALF:c0 + 2 * SSM_HALF] = h_im
        y_s[:, blk * 128:(blk + 1) * 128] = (_dot(bu_s[:, 0:SSM_HALF], cre_ref[blk])
                                             - _dot(bu_s[:, SSM_HALF:2 * SSM_HALF], cim_ref[blk]))
    y = y_s[...] + dskip_ref[...] * u_s[...]
    z = _dot(_gelu_exact(y), wglu_ref[...])
    out = z[:, 0:D_MODEL] * _sigmoid(z[:, D_MODEL:2 * D_MODEL])
    xo_ref[...] = (x_ref[...].reshape(rows_all, D_MODEL) + out).reshape(chunk, n_seq, D_MODEL)

    @pl.when(i == n_steps - 1)
    def _():
        ht_ref[...] = h_s[...]


def _ssm_weights(lam_re, lam_im, log_dt, b_re, b_im, c_re, c_im):
    g, p = SSM_GROUPS, SSM_P
    n = g * SSM_GROUP
    rep = lambda z: jnp.repeat(z, SSM_GROUP, axis=0)
    bt_re = jnp.swapaxes(b_re, 1, 2).reshape(n, p)
    bt_im = jnp.swapaxes(b_im, 1, 2).reshape(n, p)
    args = [rep(lam_re), rep(lam_im), rep(log_dt.reshape(g, 1)), bt_re, bt_im]
    a_re, a_im, bb_re, bb_im = pl.pallas_call(
        _ssm_prep_kernel,
        grid=(1,),
        in_specs=[_const_spec(a.shape) for a in args],
        out_specs=tuple(_const_spec((n, p)) for _ in range(4)),
        out_shape=tuple(jax.ShapeDtypeStruct((n, p), F32) for _ in range(4)),
        name="ssm_prep",
    )(*args)
    a_re = a_re[::SSM_GROUP]
    a_im = a_im[::SSM_GROUP]
    eye = jnp.eye(SSM_BLOCK_G, dtype=F32)

    def in_block(bb):
        bb = bb.reshape(SSM_BLOCKS, SSM_BLOCK_G, SSM_GROUP, p)
        return jnp.einsum("bgcp,gh->bgchp", bb, eye).reshape(SSM_BLOCKS, 128, SSM_HALF)

    def out_block(cc):
        cc = cc.reshape(SSM_BLOCKS, SSM_BLOCK_G, SSM_GROUP, p)
        return jnp.einsum("bgcp,gh->bgphc", cc, eye).reshape(SSM_BLOCKS, SSM_HALF, 128)

    w_b = jnp.concatenate([in_block(bb_re), in_block(bb_im)], axis=2).astype(BF16)
    a_row = jnp.concatenate([a_re.reshape(SSM_BLOCKS, SSM_HALF), a_im.reshape(SSM_BLOCKS, SSM_HALF)],
                            axis=1).reshape(1, SSM_STATE_W)
    return a_row, w_b, out_block(c_re).astype(BF16), out_block(c_im).astype(BF16)


def _ssm_state_in(re, im):
    n = re.shape[0]
    return jnp.concatenate([re.reshape(n, SSM_BLOCKS, SSM_HALF), im.reshape(n, SSM_BLOCKS, SSM_HALF)],
                           axis=2).reshape(n, SSM_STATE_W)


def _ssm_state_out(h):
    n = h.shape[0]
    h = h.reshape(n, SSM_BLOCKS, 2, SSM_HALF)
    return (h[:, :, 0].reshape(n, SSM_GROUPS, SSM_P), h[:, :, 1].reshape(n, SSM_GROUPS, SSM_P))


def _ssm_layer(x_tb, h0, sw, chunk):
    t_len, n_seq, _ = x_tb.shape
    rows = chunk * n_seq
    args = [x_tb, h0, sw["norm"], sw["a_row"], sw["w_b"], sw["c_re"], sw["c_im"], sw["d_skip"], sw["w_glu"]]
    in_specs = [pl.BlockSpec((chunk, n_seq, D_MODEL), lambda i: (i, 0, 0))]
    in_specs += [_const_spec(a.shape) for a in args[1:]]
    return pl.pallas_call(
        functools.partial(_ssm_kernel, n_seq=n_seq, chunk=chunk),
        grid=(t_len // chunk,),
        in_specs=in_specs,
        out_specs=(pl.BlockSpec((chunk, n_seq, D_MODEL), lambda i: (i, 0, 0)),
                   _const_spec((n_seq, SSM_STATE_W))),
        out_shape=(jax.ShapeDtypeStruct((t_len, n_seq, D_MODEL), F32),
                   jax.ShapeDtypeStruct((n_seq, SSM_STATE_W), F32)),
        scratch_shapes=[pltpu.VMEM((rows, D_MODEL), F32), pltpu.VMEM((rows, 2 * SSM_HALF), F32),
                        pltpu.VMEM((rows, D_MODEL), F32), pltpu.VMEM((n_seq, SSM_STATE_W), F32)],
        compiler_params=pltpu.CompilerParams(dimension_semantics=("arbitrary",),
                                             vmem_limit_bytes=VMEM_LIMIT),
        name="ssm_layer",
    )(*args)


def kernel(x_prompt, x_sample, state_ret, state_wkv, state_shift, state_ssm_re, state_ssm_im, norm_mix, w_in, ret_gn, mu_shift, wkv_w0, wkv_wB, wkv_a0, wkv_aB, wkv_gB, wkv_kk, wkv_ka, wkv_rk, wkv_ln_w, wkv_ln_b, w_out, ssm_lambda_re, ssm_lambda_im, ssm_log_dt, ssm_B_re, ssm_B_im, ssm_C_re, ssm_C_im, ssm_D, ssm_w_glu, mlp_norm, mlp_up, mlp_down, norm_f):
    lw = dict(
        norm=_row(norm_mix[0]), w_in=w_in[0].astype(BF16), ret_gn=_row(ret_gn[0]), mu=_row(mu_shift[0]),
        w0=_row(wkv_w0[0]), lora=_lora_block(wkv_wB[0], wkv_aB[0]), a0=_row(wkv_a0[0]),
        g_b=wkv_gB[0].astype(BF16), k_k=_row(wkv_kk[0]), k_a=_row(wkv_ka[0]), r_k=_row(wkv_rk[0]),
        ln_w=_row(wkv_ln_w[0]), ln_b=_row(wkv_ln_b[0]), w_out=w_out[0].astype(BF16))
    a_row, w_b, c_re, c_im = _ssm_weights(ssm_lambda_re[0], ssm_lambda_im[0], ssm_log_dt[0], ssm_B_re[0],
                                          ssm_B_im[0], ssm_C_re[0], ssm_C_im[0])
    sw = dict(norm=_row(norm_mix[1]), a_row=a_row, w_b=w_b, c_re=c_re, c_im=c_im,
              d_skip=_row(ssm_D[0]), w_glu=ssm_w_glu[0].astype(BF16))
    up = [mlp_up[l].astype(BF16) for l in range(2)]
    down = [mlp_down[l].astype(BF16) for l in range(2)]
    nf = _row(norm_f)

    n_p, t_p, _ = x_prompt.shape
    x1, ret_p, wkv_p, shift_p = _mixer_prompt(x_prompt, lw)
    x1 = _mlp(x1.reshape(n_p * t_p, D_MODEL), _row(mlp_norm[0]), up[0], down[0], nf, False)
    x1 = jnp.swapaxes(x1.reshape(n_p, t_p, D_MODEL), 0, 1)
    x2, h_p = _ssm_layer(x1, jnp.zeros((n_p, SSM_STATE_W), F32), sw, SSM_CHUNK)
    y_p = _mlp(x2.reshape(t_p * n_p, D_MODEL), _row(mlp_norm[1]), up[1], down[1], nf, True)
    y_prompt = jnp.swapaxes(y_p.reshape(t_p, n_p, D_MODEL), 0, 1)
    ssm_re_p, ssm_im_p = _ssm_state_out(h_p)

    n_s = x_sample.shape[0]
    xs = x_sample.reshape(n_s, D_MODEL)
    xs1, ret_s, wkv_s, shift_s = _mixer_sample(xs, state_ret[0], state_wkv[0], state_shift[0], lw)
    xs1 = _mlp(xs1, _row(mlp_norm[0]), up[0], down[0], nf, False)
    xs2, h_s = _ssm_layer(xs1.reshape(1, n_s, D_MODEL), _ssm_state_in(state_ssm_re[0], state_ssm_im[0]), sw, 1)
    y_s = _mlp(xs2.reshape(n_s, D_MODEL), _row(mlp_norm[1]), up[1], down[1], nf, True)
    ssm_re_s, ssm_im_s = _ssm_state_out(h_s)

    return (y_prompt, y_s.reshape(n_s, 1, D_MODEL),
            ret_p[None], wkv_p[None], shift_p[None], ssm_re_p[None], ssm_im_p[None],
            ret_s[None], wkv_s[None], shift_s[None], ssm_re_s[None], ssm_im_s[None])
```

```python
import functools
import math

import numpy as np
import jax
import jax.numpy as jnp
from jax import lax
from jax.experimental import pallas as pl
from jax.experimental.pallas import tpu as pltpu

F32 = jnp.float32
BF16 = jnp.bfloat16

D_MODEL = 1024
N_HEADS = 8
HEAD_DIM = 64
HEADS_W = N_HEADS * HEAD_DIM
N_PAIRS = N_HEADS // 2
PAIR_W = 2 * HEAD_DIM
ROPE_BASE = 10000.0
DECAY_LORA = 64
AAA_LORA = 64
GATE_LORA = 128
SHIFT_W = 3 * HEADS_W + DECAY_LORA + AAA_LORA + GATE_LORA
RET_COLS = 4 * HEADS_W
IN_W = RET_COLS + SHIFT_W
SSM_GROUP = 16
SSM_GROUPS = D_MODEL // SSM_GROUP
SSM_P = 64
SSM_BLOCKS = 8
SSM_BLOCK_G = SSM_GROUPS // SSM_BLOCKS
SSM_HALF = SSM_BLOCK_G * SSM_P
SSM_STATE_W = 2 * SSM_BLOCKS * SSM_HALF
D_FF = 4 * D_MODEL
RMS_EPS = 1e-6
GN_EPS = 1e-5
WKV_GN_EPS = 64e-5
PAST_LEN = 16384

MIX_CHUNK = 64
MIX_GROUP = 4
SSM_CHUNK = 64
MLP_ROWS = 512
FF_CHUNK = 1024
SAMPLE_SEQ_BLOCK = 8

VMEM_LIMIT = 56 * 1024 * 1024


def _dot(a, b):
    return jnp.dot(a.astype(BF16), b.astype(BF16), preferred_element_type=F32)


def _dot_nt(a, b):
    return lax.dot_general(a.astype(BF16), b.astype(BF16), (((1,), (1,)), ((), ())),
                           preferred_element_type=F32)


def _dot_tn(a, b):
    return lax.dot_general(a.astype(BF16), b.astype(BF16), (((0,), (0,)), ((), ())),
                           preferred_element_type=F32)


def _split3(x):
    hi = x.astype(BF16)
    r1 = x - hi.astype(F32)
    mid = r1.astype(BF16)
    lo = (r1 - mid.astype(F32)).astype(BF16)
    return hi, mid, lo


def _dot_exact_lhs(a_bf16, x):
    hi, mid, lo = _split3(x)
    f = lambda p: jnp.dot(a_bf16, p, preferred_element_type=F32)
    return f(hi) + f(mid) + f(lo)


def _segsum(x, ones_blk):
    hi, mid, lo = _split3(x)
    outs = []
    for c in range(x.shape[1] // 256):
        sl = slice(c * 256, (c + 1) * 256)
        f = lambda p: jnp.dot(p[:, sl], ones_blk, preferred_element_type=F32)
        outs.append(f(hi) + f(mid) + f(lo))
    return jnp.concatenate(outs, axis=1)


def _rms(x, w):
    return x * lax.rsqrt(jnp.mean(x * x, axis=-1, keepdims=True) + RMS_EPS) * w


def _sigmoid(x):
    return 1.0 / (1.0 + jnp.exp(-x))


def _softplus(x):
    return jnp.maximum(x, 0.0) + jnp.log1p(jnp.exp(-jnp.abs(x)))


def _head_norm(z, eps, ones_blk):
    mu = _segsum(z, ones_blk) * (1.0 / HEAD_DIM)
    zc = z - mu
    var = _segsum(zc * zc, ones_blk) * (1.0 / HEAD_DIM)
    return zc * lax.rsqrt(var + eps)


def _rope(z, cos, sin_signed):
    lane = lax.broadcasted_iota(jnp.int32, (1, HEADS_W), 1) % HEAD_DIM
    swapped = jnp.where(lane < HEAD_DIM // 2,
                        pltpu.roll(z, HEADS_W - HEAD_DIM // 2, axis=1),
                        pltpu.roll(z, HEAD_DIM // 2, axis=1))
    return z * cos + swapped * sin_signed


def _wkv_features(xs, w0, lora_w, a0, g_b, k_k, k_a, r_k, ones_blk):
    r = xs[:, 0:HEADS_W]
    kw = xs[:, HEADS_W:2 * HEADS_W]
    vw = xs[:, 2 * HEADS_W:3 * HEADS_W]
    lo = xs[:, 3 * HEADS_W:3 * HEADS_W + 128]
    lane = lax.broadcasted_iota(jnp.int32, (1, 128), 1)
    lo = jnp.where(lane < DECAY_LORA, jnp.tanh(lo), lo)
    ll = _dot(lo, lora_w)
    w_log = -_softplus(-(w0 + ll[:, 0:HEADS_W])) - 0.5
    log_decay = -jnp.exp(w_log)
    alr = _sigmoid(a0 + ll[:, HEADS_W:2 * HEADS_W])
    gate = _dot(_sigmoid(xs[:, 3 * HEADS_W + 128:SHIFT_W]), g_b)
    kk = kw * k_k
    kk = kk / jnp.maximum(jnp.sqrt(_segsum(kk * kk, ones_blk)), 1e-12)
    k_mod = kw * (1.0 + (alr - 1.0) * k_a)
    bonus = _segsum(r * k_mod * r_k, ones_blk) * vw
    return r, log_decay, k_mod, vw, -kk, kk * alr, gate, bonus


def _stack_masked(x2, m0):
    return jnp.concatenate([jnp.where(m0, x2, 0.0), jnp.where(m0, 0.0, x2)], axis=0)


def _stack_dup(x2):
    return jnp.concatenate([x2, x2], axis=0)


def _mixer_prompt_kernel(
        x_ref, normw_ref, win_ref, cos_ref, sin_ref, qdec_ref, kdec_ref, dmask_ref, sdec_ref,
        retgn_ref, mu_ref, w0_ref, lora_ref, a0_ref, gb_ref, kk_ref, ka_ref, rk_ref, lnw_ref,
        lnb_ref, wout_ref, ones_ref, tril_ref, strict_ref, incl_ref, bd_ref,
        xo_ref, rets_ref, wkvs_ref, shift_ref,
        p_s, cat_s, rs_s, ws_s, carry_s, *, n_seq, chunk, group):
    i = pl.program_id(0)
    n_steps = pl.num_programs(0)
    C = chunk

    @pl.when(i == 0)
    def _():
        rs_s[...] = jnp.zeros_like(rs_s)
        ws_s[...] = jnp.zeros_like(ws_s)
        carry_s[...] = jnp.zeros_like(carry_s)

    x = x_ref[...].reshape(n_seq * C, D_MODEL)
    hn = _rms(x, normw_ref[...])
    p_s[...] = _dot(hn, win_ref[...])

    m0 = lax.broadcasted_iota(jnp.int32, (1, PAIR_W), 1) < HEAD_DIM
    ones_blk = ones_ref[...]
    NB = group
    R = NB * C
    row_id = lax.broadcasted_iota(jnp.int32, (R, 1), 0)
    tile_rows = lambda ref: jnp.concatenate([ref[...]] * NB, axis=0)
    pairs = range(N_PAIRS)
    sls = [slice(pr * PAIR_W, (pr + 1) * PAIR_W) for pr in pairs]
    chains = [(s, pr) for s in range(NB) for pr in pairs]
    seq_rows = [slice(s * C, (s + 1) * C) for s in range(NB)]

    def per_group(gi, carry):
        rows = pl.ds(pl.multiple_of(gi * R, R), R)
        b0 = gi * NB
        ret_states = {(s, pr): rs_s[pr, b0 + s] for s, pr in chains}
        wkv_states = {(s, pr): ws_s[pr, b0 + s] for s, pr in chains}
        shift_rows = [carry_s[pl.ds(b0 + s, 1), :] for s in range(NB)]
        wp = p_s[rows, RET_COLS:IN_W]
        cos, sin = tile_rows(cos_ref), tile_rows(sin_ref)
        q = _rope(p_s[rows, 0:HEADS_W], cos, sin)
        k = _rope(p_s[rows, HEADS_W:2 * HEADS_W], cos, sin) * (HEAD_DIM ** -0.5)
        v = p_s[rows, 2 * HEADS_W:3 * HEADS_W]
        g = p_s[rows, 3 * HEADS_W:4 * HEADS_W]
        qd = q * tile_rows(qdec_ref)
        kd = k * tile_rows(kdec_ref)
        cut = lambda z, s, pr: z[seq_rows[s], sls[pr]]
        r_sc = {c: _dot_nt(_stack_masked(cut(q, *c), m0), _stack_dup(cut(k, *c))) * dmask_ref[c[1]]
                for c in chains}
        r_inner = {c: _dot(r_sc[c], _stack_dup(cut(v, *c))) for c in chains}
        r_cross = {c: _dot(cut(qd, *c), ret_states[c]) for c in chains}
        new_ret = {c: sdec_ref[c[1]] * ret_states[c] + _dot_tn(cut(kd, *c), cut(v, *c)) * bd_ref[...]
                   for c in chains}
        o = jnp.concatenate(
            [jnp.concatenate([jnp.where(m0, r_inner[(s, pr)][0:C], r_inner[(s, pr)][C:2 * C])
                              + r_cross[(s, pr)] for pr in pairs], axis=1) for s in range(NB)], axis=0)
        ret = _head_norm(o, GN_EPS, ones_blk) * retgn_ref[...]
        ret_out = (g * _sigmoid(g) * ret).astype(BF16)

        prev = pltpu.roll(wp, 1, axis=0)
        for s in range(NB):
            prev = jnp.where(row_id == s * C, shift_rows[s], prev)
        xs = wp + (prev - wp) * mu_ref[...]
        r, lw, k_mod, vw, a_vec, b_vec, gate, bonus = _wkv_features(
            xs, w0_ref[...], lora_ref[...], a0_ref[...], gb_ref[...], kk_ref[...], ka_ref[...],
            rk_ref[...], ones_blk)
        cw = _dot_exact_lhs(tril_ref[...], lw)
        cw_last = [cw[s * C + C - 1:(s + 1) * C, :] for s in range(NB)]
        cwl = jnp.concatenate([jnp.broadcast_to(z, (C, HEADS_W)) for z in cw_last], axis=0)
        r_t = r * jnp.exp(cw)
        a_t = a_vec * jnp.exp(cw - lw)
        w_inv = jnp.exp(-cw)
        b_t = b_vec * w_inv
        k_t = k_mod * w_inv
        w_end = jnp.exp(cwl - cw)
        b_h = b_vec * w_end
        k_h = k_mod * w_end
        w_all = [jnp.exp(z) for z in cw_last]
        lhs = {c: jnp.concatenate([_stack_masked(cut(a_t, *c), m0), _stack_masked(cut(r_t, *c), m0)], axis=0)
               for c in chains}
        sc = {c: _dot_nt(lhs[c], jnp.concatenate([_stack_dup(cut(b_t, *c)), _stack_dup(cut(k_t, *c))], axis=0))
              for c in chains}
        on_state = {c: _dot_nt(lhs[c], wkv_states[c]) for c in chains}
        vv = {c: _stack_dup(cut(vw, *c)) for c in chains}
        n_pow = {c: sc[c][0:2 * C, 0:2 * C] * strict_ref[...] for c in chains}
        u = {c: on_state[c][0:2 * C] + _dot(sc[c][0:2 * C, 2 * C:4 * C] * strict_ref[...], vv[c])
             for c in chains}
        n_steps_solve = int(math.log2(C))
        for it in range(n_steps_solve):
            u = {c: u[c] + _dot(n_pow[c], u[c]) for c in chains}
            if it + 1 < n_steps_solve:
                n_pow = {c: _dot(n_pow[c], n_pow[c]) for c in chains}
        uv = {c: jnp.concatenate([u[c], vv[c]], axis=0) for c in chains}
        y_st = {c: on_state[c][2 * C:4 * C] + _dot(
            jnp.concatenate([sc[c][2 * C:4 * C, 0:2 * C] * incl_ref[...],
                             sc[c][2 * C:4 * C, 2 * C:4 * C] * incl_ref[...]], axis=1), uv[c]) for c in chains}
        new_wkv = {c: wkv_states[c] * w_all[c[0]][:, sls[c[1]]] + bd_ref[...] * _dot_tn(
            uv[c], jnp.concatenate([_stack_masked(cut(b_h, *c), m0), _stack_masked(cut(k_h, *c), m0)], axis=0))
            for c in chains}
        y = jnp.concatenate(
            [jnp.concatenate([jnp.where(m0, y_st[(s, pr)][0:C], y_st[(s, pr)][C:2 * C]) for pr in pairs], axis=1)
             for s in range(NB)], axis=0)
        yn = _head_norm(y, WKV_GN_EPS, ones_blk) * lnw_ref[...] + lnb_ref[...]
        cat_s[rows, 0:HEADS_W] = ret_out
        cat_s[rows, HEADS_W:2 * HEADS_W] = ((yn + bonus) * gate).astype(BF16)
        for s in range(NB):
            carry_s[pl.ds(b0 + s, 1), :] = wp[s * C + C - 1:(s + 1) * C, :]
        for s, pr in chains:
            rs_s[pr, b0 + s] = new_ret[(s, pr)]
            ws_s[pr, b0 + s] = new_wkv[(s, pr)]
        return carry

    lax.fori_loop(0, n_seq // NB, per_group, 0)

    out = jnp.dot(cat_s[...], wout_ref[...], preferred_element_type=F32)
    xo_ref[...] = (x_ref[...].reshape(n_seq * C, D_MODEL) + out).reshape(n_seq, C, D_MODEL)

    @pl.when(i == n_steps - 1)
    def _():
        shift_ref[...] = carry_s[...]
        for b in range(n_seq):
            for pr in range(N_PAIRS):
                rs = rs_s[pr, b]
                ws = ws_s[pr, b]
                rets_ref[b, 2 * pr] = rs[0:HEAD_DIM, 0:HEAD_DIM]
                rets_ref[b, 2 * pr + 1] = rs[HEAD_DIM:PAIR_W, HEAD_DIM:PAIR_W]
                wkvs_ref[b, 2 * pr] = ws[0:HEAD_DIM, 0:HEAD_DIM]
                wkvs_ref[b, 2 * pr + 1] = ws[HEAD_DIM:PAIR_W, HEAD_DIM:PAIR_W]


def _const_spec(shape):
    nd = len(shape)
    return pl.BlockSpec(shape, lambda *_: (0,) * nd)


def _const(a, dtype=F32):
    return jnp.asarray(np.asarray(a, np.float64), dtype=dtype)


def _retention_tables(chunk):
    log_g = np.log1p(-np.exp2(-5.0 - np.arange(N_HEADS, dtype=np.float64)))
    lane_g = np.repeat(log_g, HEAD_DIM)[None, :]
    idx = np.arange(chunk, dtype=np.float64)
    qdec = np.exp((idx + 1.0)[:, None] * lane_g)
    kdec = np.exp((chunk - 1.0 - idx)[:, None] * lane_g)
    rel = idx[:, None] - idx[None, :]
    dm = np.where(rel >= 0, np.exp(np.maximum(rel, 0.0)[None] * log_g[:, None, None]), 0.0)
    zero = np.zeros((chunk, chunk))
    dmask = np.stack([np.block([[dm[2 * p], zero], [zero, dm[2 * p + 1]]]) for p in range(N_PAIRS)])
    cdec = np.exp(chunk * log_g)
    hz = np.zeros((HEAD_DIM, HEAD_DIM))
    ho = np.ones((HEAD_DIM, HEAD_DIM))
    sdec = np.stack([np.block([[cdec[2 * p] * ho, hz], [hz, cdec[2 * p + 1] * ho]])
                     for p in range(N_PAIRS)])
    return _const(qdec), _const(kdec), _const(dmask), _const(sdec), cdec


def _rope_tables(pos):
    half = HEAD_DIM // 2
    inv_freq = ROPE_BASE ** (-np.arange(half, dtype=np.float64) / half)
    ang = np.asarray(pos, np.float64)[:, None] * inv_freq[None, :]
    cos = np.cos(ang)
    sin = np.sin(ang)
    cos_t = np.tile(np.concatenate([cos, cos], axis=1), (1, N_HEADS))
    sin_t = np.tile(np.concatenate([-sin, sin], axis=1), (1, N_HEADS))
    return _const(cos_t), _const(sin_t)


def _block_masks(chunk, group):
    i = np.arange(2 * chunk)
    same = (i[:, None] // chunk) == (i[None, :] // chunk)
    strict = same & (i[:, None] > i[None, :])
    incl = same & (i[:, None] >= i[None, :])
    j = np.arange(PAIR_W)
    bd = (j[:, None] // HEAD_DIM) == (j[None, :] // HEAD_DIM)
    t = np.arange(chunk)
    tril = np.kron(np.eye(group), t[:, None] >= t[None, :])
    o = np.arange(256)
    ones_blk = (o[:, None] // HEAD_DIM) == (o[None, :] // HEAD_DIM)
    return _const(strict), _const(incl), _const(bd), _const(tril, BF16), _const(ones_blk, BF16)


def _lora_block(w_b, a_b):
    z = jnp.zeros_like(w_b)
    return jnp.concatenate([jnp.concatenate([w_b, z], axis=1),
                            jnp.concatenate([z, a_b], axis=1)], axis=0).astype(BF16)


def _row(v):
    return v.reshape(1, -1).astype(F32)


def _mixer_prompt(x, lw):
    n_seq, t_len, _ = x.shape
    C = MIX_CHUNK
    n_steps = t_len // C
    cos_t, sin_t = _rope_tables(np.arange(t_len))
    qdec, kdec, dmask, sdec, _ = _retention_tables(C)
    strict, incl, bd, tril, ones_blk = _block_masks(C, MIX_GROUP)
    in_specs = [
        pl.BlockSpec((n_seq, C, D_MODEL), lambda i: (0, i, 0)),
        _const_spec((1, D_MODEL)),
        _const_spec((D_MODEL, IN_W)),
        pl.BlockSpec((C, HEADS_W), lambda i: (i, 0)),
        pl.BlockSpec((C, HEADS_W), lambda i: (i, 0)),
    ]
    tail = [qdec, kdec, dmask, sdec, lw["ret_gn"], lw["mu"], lw["w0"], lw["lora"], lw["a0"], lw["g_b"],
            lw["k_k"], lw["k_a"], lw["r_k"], lw["ln_w"], lw["ln_b"], lw["w_out"], ones_blk, tril,
            strict, incl, bd]
    in_specs += [_const_spec(a.shape) for a in tail]
    out_shape = (
        jax.ShapeDtypeStruct((n_seq, t_len, D_MODEL), F32),
        jax.ShapeDtypeStruct((n_seq, N_HEADS, HEAD_DIM, HEAD_DIM), F32),
        jax.ShapeDtypeStruct((n_seq, N_HEADS, HEAD_DIM, HEAD_DIM), F32),
        jax.ShapeDtypeStruct((n_seq, SHIFT_W), F32),
    )
    out_specs = (
        pl.BlockSpec((n_seq, C, D_MODEL), lambda i: (0, i, 0)),
        _const_spec((n_seq, N_HEADS, HEAD_DIM, HEAD_DIM)),
        _const_spec((n_seq, N_HEADS, HEAD_DIM, HEAD_DIM)),
        _const_spec((n_seq, SHIFT_W)),
    )
    scratch = [
        pltpu.VMEM((n_seq * C, IN_W), F32),
        pltpu.VMEM((n_seq * C, 2 * HEADS_W), BF16),
        pltpu.VMEM((N_PAIRS, n_seq, PAIR_W, PAIR_W), F32),
        pltpu.VMEM((N_PAIRS, n_seq, PAIR_W, PAIR_W), F32),
        pltpu.VMEM((n_seq, SHIFT_W), F32),
    ]
    return pl.pallas_call(
        functools.partial(_mixer_prompt_kernel, n_seq=n_seq, chunk=C, group=MIX_GROUP),
        grid=(n_steps,),
        in_specs=in_specs, out_specs=out_specs, out_shape=out_shape, scratch_shapes=scratch,
        compiler_params=pltpu.CompilerParams(dimension_semantics=("arbitrary",),
                                             vmem_limit_bytes=VMEM_LIMIT),
        name="mixer_prompt",
    )(x, lw["norm"], lw["w_in"], cos_t, sin_t, *tail)


def _mixer_sample_pre_kernel(
        x_ref, shift_ref, normw_ref, win_ref, cos_ref, sin_ref, qdec_ref, mu_ref, w0_ref, lora_ref,
        a0_ref, gb_ref, kk_ref, ka_ref, rk_ref, ones_ref, feat_ref, newshift_ref):
    x = x_ref[...]
    p = _dot(_rms(x, normw_ref[...]), win_ref[...])
    q = _rope(p[:, 0:HEADS_W], cos_ref[...], sin_ref[...])
    k = _rope(p[:, HEADS_W:2 * HEADS_W], cos_ref[...], sin_ref[...]) * (HEAD_DIM ** -0.5)
    wp = p[:, RET_COLS:IN_W]
    xs = wp + (shift_ref[...] - wp) * mu_ref[...]
    r, lw, k_mod, vw, a_vec, b_vec, gate, bonus = _wkv_features(
        xs, w0_ref[...], lora_ref[...], a0_ref[...], gb_ref[...], kk_ref[...], ka_ref[...],
        rk_ref[...], ones_ref[...])
    newshift_ref[...] = wp
    feats = [q, q * qdec_ref[...], k, p[:, 2 * HEADS_W:3 * HEADS_W], p[:, 3 * HEADS_W:4 * HEADS_W],
             r, jnp.exp(lw), k_mod, vw, a_vec, b_vec, gate, bonus]
    for n, f in enumerate(feats):
        feat_ref[:, n * HEADS_W:(n + 1) * HEADS_W] = f


_F_Q, _F_QD, _F_K, _F_V, _F_G, _F_R, _F_W, _F_KM, _F_VW, _F_A, _F_B, _F_GATE, _F_BONUS = range(13)
_N_FEATS = 13


def _mixer_sample_state_kernel(feat_ref, sdec_ref, ret_ref, wkv_ref, o_ref, reto_ref, wkvo_ref, *, n_blk):
    def vec(b, n, h):
        c0 = n * HEADS_W + h * HEAD_DIM
        return jnp.broadcast_to(feat_ref[b:b + 1, c0:c0 + HEAD_DIM], (8, HEAD_DIM))

    row = lax.broadcasted_iota(jnp.int32, (8, 1), 0)
    keys = [(b, h) for b in range(n_blk) for h in range(N_HEADS)]
    s0 = {c: ret_ref[c[0], c[1]] for c in keys}
    st = {c: wkv_ref[c[0], c[1]] for c in keys}
    cross = {c: _dot(vec(c[0], _F_QD, c[1]), s0[c]) for c in keys}
    sa = {c: _dot_nt(vec(c[0], _F_A, c[1]), st[c]) for c in keys}
    kv = {c: _dot_tn(jnp.where(row == 0, vec(c[0], _F_K, c[1]), 0.0), vec(c[0], _F_V, c[1])) for c in keys}
    outer = {c: _dot_tn(jnp.where(row == 0, sa[c], jnp.where(row == 1, vec(c[0], _F_VW, c[1]), 0.0)),
                        jnp.where(row == 0, vec(c[0], _F_B, c[1]),
                                  jnp.where(row == 1, vec(c[0], _F_KM, c[1]), 0.0))) for c in keys}
    st_new = {c: st[c] * vec(c[0], _F_W, c[1])[0:1] + outer[c] for c in keys}
    y = {c: _dot_nt(vec(c[0], _F_R, c[1]), st_new[c]) for c in keys}
    for b in range(n_blk):
        o_row = []
        for h in range(N_HEADS):
            q, k, v = vec(b, _F_Q, h), vec(b, _F_K, h), vec(b, _F_V, h)
            qk = jnp.sum(q * k, axis=-1, keepdims=True)
            o_row.append((qk * v + cross[(b, h)])[0:1])
        o_ref[b:b + 1, :] = jnp.concatenate(o_row + [y[(b, h)][0:1] for h in range(N_HEADS)], axis=1)
    for c in keys:
        reto_ref[c[0], c[1]] = s0[c] * sdec_ref[c[1]] + kv[c]
        wkvo_ref[c[0], c[1]] = st_new[c]


def _mixer_sample_post_kernel(
        x_ref, feat_ref, o_ref, retgn_ref, lnw_ref, lnb_ref, wout_ref, ones_ref, xo_ref):
    ones_blk = ones_ref[...]
    g = feat_ref[:, _F_G * HEADS_W:(_F_G + 1) * HEADS_W]
    gate = feat_ref[:, _F_GATE * HEADS_W:(_F_GATE + 1) * HEADS_W]
    bonus = feat_ref[:, _F_BONUS * HEADS_W:(_F_BONUS + 1) * HEADS_W]
    ret = _head_norm(o_ref[:, 0:HEADS_W], GN_EPS, ones_blk) * retgn_ref[...]
    ret_out = g * _sigmoid(g) * ret
    yn = _head_norm(o_ref[:, HEADS_W:2 * HEADS_W], WKV_GN_EPS, ones_blk) * lnw_ref[...] + lnb_ref[...]
    wkv_out = (yn + bonus) * gate
    cat = jnp.concatenate([ret_out, wkv_out], axis=1)
    xo_ref[...] = x_ref[...] + _dot(cat, wout_ref[...])


def _mixer_sample(x, ret0, wkv0, shift0, lw):
    n = x.shape[0]
    cos_t, sin_t = _rope_tables(np.full((1,), PAST_LEN))
    qdec, _, _, _, cdec = _retention_tables(1)
    _, _, _, _, ones_blk = _block_masks(1, 1)
    args = [x, shift0, lw["norm"], lw["w_in"], cos_t, sin_t, qdec, lw["mu"], lw["w0"], lw["lora"],
            lw["a0"], lw["g_b"], lw["k_k"], lw["k_a"], lw["r_k"], ones_blk]
    feat, new_shift = pl.pallas_call(
        _mixer_sample_pre_kernel,
        grid=(1,),
        in_specs=[_const_spec(a.shape) for a in args],
        out_specs=(_const_spec((n, _N_FEATS * HEADS_W)), _const_spec((n, SHIFT_W))),
        out_shape=(jax.ShapeDtypeStruct((n, _N_FEATS * HEADS_W), F32),
                   jax.ShapeDtypeStruct((n, SHIFT_W), F32)),
        compiler_params=pltpu.CompilerParams(vmem_limit_bytes=VMEM_LIMIT),
        name="mixer_sample_pre",
    )(*args)

    nb = SAMPLE_SEQ_BLOCK
    sdec = _const(np.broadcast_to(cdec[:, None, None], (N_HEADS, 1, HEAD_DIM)))
    state_spec = pl.BlockSpec((nb, N_HEADS, HEAD_DIM, HEAD_DIM), lambda i: (i, 0, 0, 0))
    o, ret_new, wkv_new = pl.pallas_call(
        functools.partial(_mixer_sample_state_kernel, n_blk=nb),
        grid=(n // nb,),
        in_specs=[pl.BlockSpec((nb, _N_FEATS * HEADS_W), lambda i: (i, 0)),
                  _const_spec(sdec.shape), state_spec, state_spec],
        out_specs=(pl.BlockSpec((nb, 2 * HEADS_W), lambda i: (i, 0)), state_spec, state_spec),
        out_shape=(jax.ShapeDtypeStruct((n, 2 * HEADS_W), F32),
                   jax.ShapeDtypeStruct(ret0.shape, F32), jax.ShapeDtypeStruct(wkv0.shape, F32)),
        compiler_params=pltpu.CompilerParams(dimension_semantics=("arbitrary",),
                                             vmem_limit_bytes=VMEM_LIMIT),
        name="mixer_sample_state",
    )(feat, sdec, ret0, wkv0)

    args = [x, feat, o, lw["ret_gn"], lw["ln_w"], lw["ln_b"], lw["w_out"], ones_blk]
    x1 = pl.pallas_call(
        _mixer_sample_post_kernel,
        grid=(1,),
        in_specs=[_const_spec(a.shape) for a in args],
        out_specs=_const_spec((n, D_MODEL)),
        out_shape=jax.ShapeDtypeStruct((n, D_MODEL), F32),
        compiler_params=pltpu.CompilerParams(vmem_limit_bytes=VMEM_LIMIT),
        name="mixer_sample_post",
    )(*args)
    return x1, ret_new, wkv_new, new_shift


def _mlp_kernel(x_ref, normw_ref, wup_ref, wdown_ref, normf_ref, o_ref, *, final_norm):
    x = x_ref[...]
    hn = _rms(x, normw_ref[...]).astype(BF16)
    acc = x
    for c in range(D_FF // FF_CHUNK):
        sl = slice(c * FF_CHUNK, (c + 1) * FF_CHUNK)
        hid = jnp.dot(hn, wup_ref[:, sl], preferred_element_type=F32)
        hid = jnp.square(jnp.maximum(hid, 0.0)).astype(BF16)
        acc = acc + jnp.dot(hid, wdown_ref[sl, :], preferred_element_type=F32)
    if final_norm:
        acc = _rms(acc, normf_ref[...])
    o_ref[...] = acc


def _mlp(x, norm_w, w_up, w_down, norm_f, final_norm):
    rows = x.shape[0]
    tm = min(MLP_ROWS, rows)
    return pl.pallas_call(
        functools.partial(_mlp_kernel, final_norm=final_norm),
        grid=(rows // tm,),
        in_specs=[pl.BlockSpec((tm, D_MODEL), lambda i: (i, 0)),
                  _const_spec((1, D_MODEL)), _const_spec((D_MODEL, D_FF)),
                  _const_spec((D_FF, D_MODEL)), _const_spec((1, D_MODEL))],
        out_specs=pl.BlockSpec((tm, D_MODEL), lambda i: (i, 0)),
        out_shape=jax.ShapeDtypeStruct((rows, D_MODEL), F32),
        compiler_params=pltpu.CompilerParams(dimension_semantics=("arbitrary",),
                                             vmem_limit_bytes=VMEM_LIMIT),
        name="mlp",
    )(x, norm_w, w_up, w_down, norm_f)


def _ssm_prep_kernel(lre_ref, lim_ref, logdt_ref, bre_ref, bim_ref, are_ref, aim_ref, bbre_ref, bbim_ref):
    lre = jnp.minimum(lre_ref[...], -1e-4)
    lim = lim_ref[...]
    dt = jnp.exp(logdt_ref[...])
    mag = jnp.exp(lre * dt)
    are = mag * jnp.cos(lim * dt)
    aim = mag * jnp.sin(lim * dt)
    are_ref[...] = are
    aim_ref[...] = aim
    den = lre * lre + lim * lim
    nre = are - 1.0
    cre = (nre * lre + aim * lim) / den
    cim = (aim * lre - nre * lim) / den
    bre = bre_ref[...]
    bim = bim_ref[...]
    bbre_ref[...] = cre * bre - cim * bim
    bbim_ref[...] = cre * bim + cim * bre


def _gelu_exact(x):
    return 0.5 * x * (1.0 + lax.erf(x * (2.0 ** -0.5)))


def _ssm_kernel(x_ref, hre0_ref, him0_ref, normw_ref, are_ref, aim_ref, wb_ref, cre_ref, cim_ref, dskip_ref,
                wglu_ref, xo_ref, hre_ref, him_ref, xt_s, u_s, bu_s, y_s, hg_s, hre_s, him_s,
                *, n_seq, chunk, batch_major):
    i = pl.program_id(0)
    n_steps = pl.num_programs(0)
    rows_all = chunk * n_seq
    n_slabs = D_MODEL // 128

    @pl.when(i == 0)
    def _():
        hre_s[...] = hre0_ref[...]
        him_s[...] = him0_ref[...]

    if batch_major:
        for b in range(n_seq):
            for sl in range(n_slabs):
                xt_s[sl, pl.ds(b, chunk, stride=n_seq), :] = x_ref[b, :, sl * 128:(sl + 1) * 128]
        ssq = sum(jnp.sum(jnp.square(xt_s[sl]), axis=-1, keepdims=True) for sl in range(n_slabs))
        inv = lax.rsqrt(ssq * (1.0 / D_MODEL) + RMS_EPS)
        for sl in range(n_slabs):
            cols = slice(sl * 128, (sl + 1) * 128)
            u_s[:, cols] = xt_s[sl] * inv * normw_ref[:, cols]
    else:
        u_s[...] = _rms(x_ref[...].reshape(rows_all, D_MODEL), normw_ref[...])

    def input_proj(blk):
        bu_s[blk % 2] = _dot(u_s[:, blk * 128:(blk + 1) * 128], wb_ref[blk])

    input_proj(0)
    for blk in range(SSM_BLOCKS):
        if blk + 1 < SSM_BLOCKS:
            input_proj(blk + 1)
        buf = bu_s.at[blk % 2]
        cols = slice(blk * SSM_HALF, (blk + 1) * SSM_HALF)
        a_re = jnp.broadcast_to(are_ref[:, cols], (n_seq, SSM_HALF))
        a_im = jnp.broadcast_to(aim_ref[:, cols], (n_seq, SSM_HALF))
        h_re = hre_s[:, cols]
        h_im = him_s[:, cols]
        for t in range(chunk):
            rows = slice(t * n_seq, (t + 1) * n_seq)
            n_re = a_re * h_re - a_im * h_im + buf[rows, 0:SSM_HALF]
            n_im = a_re * h_im + a_im * h_re + buf[rows, SSM_HALF:2 * SSM_HALF]
            buf[rows, 0:SSM_HALF] = n_re
            buf[rows, SSM_HALF:2 * SSM_HALF] = n_im
            h_re, h_im = n_re, n_im
        hre_s[:, cols] = h_re
        him_s[:, cols] = h_im
        y_s[:, blk * 128:(blk + 1) * 128] = (_dot(buf[:, 0:SSM_HALF], cre_ref[blk])
                                             - _dot(buf[:, SSM_HALF:2 * SSM_HALF], cim_ref[blk]))
    hg_s[...] = _gelu_exact(y_s[...] + dskip_ref[...] * u_s[...]).astype(BF16)
    glu_w = 256
    for c in range(D_MODEL // glu_w):
        cols = slice(c * glu_w, (c + 1) * glu_w)
        hg = hg_s[...]
        val = jnp.dot(hg, wglu_ref[:, cols], preferred_element_type=F32)
        gate = jnp.dot(hg, wglu_ref[:, D_MODEL + c * glu_w:D_MODEL + (c + 1) * glu_w],
                       preferred_element_type=F32)
        out = val * _sigmoid(gate)
        if batch_major:
            for sl in range(c * glu_w // 128, (c + 1) * glu_w // 128):
                xt_s[sl] = xt_s[sl] + out[:, sl * 128 - c * glu_w:(sl + 1) * 128 - c * glu_w]
        else:
            y_s[:, cols] = x_ref[...].reshape(rows_all, D_MODEL)[:, cols] + out
    if batch_major:
        for b in range(n_seq):
            for sl in range(n_slabs):
                xo_ref[b, :, sl * 128:(sl + 1) * 128] = xt_s[sl, pl.ds(b, chunk, stride=n_seq), :]
    else:
        xo_ref[...] = y_s[...].reshape(chunk, n_seq, D_MODEL)

    @pl.when(i == n_steps - 1)
    def _():
        hre_ref[...] = hre_s[...]
        him_ref[...] = him_s[...]


def _ssm_weights(lam_re, lam_im, log_dt, b_re, b_im, c_re, c_im):
    g, p = SSM_GROUPS, SSM_P
    n = g * SSM_GROUP
    rep = lambda z: jnp.repeat(z, SSM_GROUP, axis=0)
    bt_re = jnp.swapaxes(b_re, 1, 2).reshape(n, p)
    bt_im = jnp.swapaxes(b_im, 1, 2).reshape(n, p)
    args = [rep(lam_re), rep(lam_im), rep(log_dt.reshape(g, 1)), bt_re, bt_im]
    a_re, a_im, bb_re, bb_im = pl.pallas_call(
        _ssm_prep_kernel,
        grid=(1,),
        in_specs=[_const_spec(a.shape) for a in args],
        out_specs=tuple(_const_spec((n, p)) for _ in range(4)),
        out_shape=tuple(jax.ShapeDtypeStruct((n, p), F32) for _ in range(4)),
        name="ssm_prep",
    )(*args)
    a_re = a_re[::SSM_GROUP]
    a_im = a_im[::SSM_GROUP]
    eye = jnp.eye(SSM_BLOCK_G, dtype=F32)

    def in_block(bb):
        bb = bb.reshape(SSM_BLOCKS, SSM_BLOCK_G, SSM_GROUP, p)
        return jnp.einsum("bgcp,gh->bgchp", bb, eye).reshape(SSM_BLOCKS, 128, SSM_HALF)

    def out_block(cc):
        cc = cc.reshape(SSM_BLOCKS, SSM_BLOCK_G, SSM_GROUP, p)
        return jnp.einsum("bgcp,gh->bgphc", cc, eye).reshape(SSM_BLOCKS, SSM_HALF, 128)

    w_b = jnp.concatenate([in_block(bb_re), in_block(bb_im)], axis=2).astype(BF16)
    n_state = SSM_GROUPS * SSM_P
    return (a_re.reshape(1, n_state), a_im.reshape(1, n_state), w_b,
            out_block(c_re).astype(BF16), out_block(c_im).astype(BF16))


def _ssm_layer(x, h_re0, h_im0, sw, chunk, batch_major):
    if batch_major:
        n_seq, t_len, _ = x.shape
        x_block = (n_seq, chunk, D_MODEL)
        x_map = lambda i: (0, i, 0)
    else:
        t_len, n_seq, _ = x.shape
        x_block = (chunk, n_seq, D_MODEL)
        x_map = lambda i: (i, 0, 0)
    rows = chunk * n_seq
    n_state = SSM_GROUPS * SSM_P
    args = [x, h_re0, h_im0, sw["norm"], sw["a_re"], sw["a_im"], sw["w_b"], sw["c_re"], sw["c_im"],
            sw["d_skip"], sw["w_glu"]]
    in_specs = [pl.BlockSpec(x_block, x_map)] + [_const_spec(a.shape) for a in args[1:]]
    state = jax.ShapeDtypeStruct((n_seq, n_state), F32)
    return pl.pallas_call(
        functools.partial(_ssm_kernel, n_seq=n_seq, chunk=chunk, batch_major=batch_major),
        grid=(t_len // chunk,),
        in_specs=in_specs,
        out_specs=(pl.BlockSpec(x_block, x_map), _const_spec(state.shape), _const_spec(state.shape)),
        out_shape=(jax.ShapeDtypeStruct(x.shape, F32), state, state),
        scratch_shapes=[pltpu.VMEM((D_MODEL // 128, rows, 128), F32),
                        pltpu.VMEM((rows, D_MODEL), F32), pltpu.VMEM((2, rows, 2 * SSM_HALF), F32),
                        pltpu.VMEM((rows, D_MODEL), F32), pltpu.VMEM((rows, D_MODEL), BF16),
                        pltpu.VMEM((n_seq, n_state), F32), pltpu.VMEM((n_seq, n_state), F32)],
        compiler_params=pltpu.CompilerParams(dimension_semantics=("arbitrary",),
                                             vmem_limit_bytes=VMEM_LIMIT),
        name="ssm_layer",
    )(*args)


def kernel(x_prompt, x_sample, state_ret, state_wkv, state_shift, state_ssm_re, state_ssm_im, norm_mix, w_in, ret_gn, mu_shift, wkv_w0, wkv_wB, wkv_a0, wkv_aB, wkv_gB, wkv_kk, wkv_ka, wkv_rk, wkv_ln_w, wkv_ln_b, w_out, ssm_lambda_re, ssm_lambda_im, ssm_log_dt, ssm_B_re, ssm_B_im, ssm_C_re, ssm_C_im, ssm_D, ssm_w_glu, mlp_norm, mlp_up, mlp_down, norm_f):
    lw = dict(
        norm=_row(norm_mix[0]), w_in=w_in[0].astype(BF16), ret_gn=_row(ret_gn[0]), mu=_row(mu_shift[0]),
        w0=_row(wkv_w0[0]), lora=_lora_block(wkv_wB[0], wkv_aB[0]), a0=_row(wkv_a0[0]),
        g_b=wkv_gB[0].astype(BF16), k_k=_row(wkv_kk[0]), k_a=_row(wkv_ka[0]), r_k=_row(wkv_rk[0]),
        ln_w=_row(wkv_ln_w[0]), ln_b=_row(wkv_ln_b[0]), w_out=w_out[0].astype(BF16))
    a_re, a_im, w_b, c_re, c_im = _ssm_weights(ssm_lambda_re[0], ssm_lambda_im[0], ssm_log_dt[0],
                                               ssm_B_re[0], ssm_B_im[0], ssm_C_re[0], ssm_C_im[0])
    sw = dict(norm=_row(norm_mix[1]), a_re=a_re, a_im=a_im, w_b=w_b, c_re=c_re, c_im=c_im,
              d_skip=_row(ssm_D[0]), w_glu=ssm_w_glu[0].astype(BF16))
    n_state = SSM_GROUPS * SSM_P
    up = [mlp_up[l].astype(BF16) for l in range(2)]
    down = [mlp_down[l].astype(BF16) for l in range(2)]
    nf = _row(norm_f)

    n_p, t_p, _ = x_prompt.shape
    x1, ret_p, wkv_p, shift_p = _mixer_prompt(x_prompt, lw)
    x1 = _mlp(x1.reshape(n_p * t_p, D_MODEL), _row(mlp_norm[0]), up[0], down[0], nf, False)
    zero_state = jnp.zeros((n_p, n_state), F32)
    x2, ssm_re_p, ssm_im_p = _ssm_layer(x1.reshape(n_p, t_p, D_MODEL), zero_state, zero_state, sw,
                                        SSM_CHUNK, True)
    y_p = _mlp(x2.reshape(n_p * t_p, D_MODEL), _row(mlp_norm[1]), up[1], down[1], nf, True)
    y_prompt = y_p.reshape(n_p, t_p, D_MODEL)
    ssm_re_p = ssm_re_p.reshape(n_p, SSM_GROUPS, SSM_P)
    ssm_im_p = ssm_im_p.reshape(n_p, SSM_GROUPS, SSM_P)

    n_s = x_sample.shape[0]
    xs = x_sample.reshape(n_s, D_MODEL)
    xs1, ret_s, wkv_s, shift_s = _mixer_sample(xs, state_ret[0], state_wkv[0], state_shift[0], lw)
    xs1 = _mlp(xs1, _row(mlp_norm[0]), up[0], down[0], nf, False)
    xs2, ssm_re_s, ssm_im_s = _ssm_layer(
        xs1.reshape(1, n_s, D_MODEL), state_ssm_re[0].reshape(n_s, n_state),
        state_ssm_im[0].reshape(n_s, n_state), sw, 1, False)
    y_s = _mlp(xs2.reshape(n_s, D_MODEL), _row(mlp_norm[1]), up[1], down[1], nf, True)
    ssm_re_s = ssm_re_s.reshape(n_s, SSM_GROUPS, SSM_P)
    ssm_im_s = ssm_im_s.reshape(n_s, SSM_GROUPS, SSM_P)

    return (y_prompt, y_s.reshape(n_s, 1, D_MODEL),
            ret_p[None], wkv_p[None], shift_p[None], ssm_re_p[None], ssm_im_p[None],
            ret_s[None], wkv_s[None], shift_s[None], ssm_re_s[None], ssm_im_s[None])
```

```python
import functools
import math

import numpy as np
import jax
import jax.numpy as jnp
from jax import lax
from jax.experimental import pallas as pl
from jax.experimental.pallas import tpu as pltpu

F32 = jnp.float32
BF16 = jnp.bfloat16

D_MODEL = 1024
N_HEADS = 8
HEAD_DIM = 64
HEADS_W = N_HEADS * HEAD_DIM
N_PAIRS = N_HEADS // 2
PAIR_W = 2 * HEAD_DIM
ROPE_BASE = 10000.0
DECAY_LORA = 64
AAA_LORA = 64
GATE_LORA = 128
SHIFT_W = 3 * HEADS_W + DECAY_LORA + AAA_LORA + GATE_LORA
RET_COLS = 4 * HEADS_W
IN_W = RET_COLS + SHIFT_W
SSM_GROUP = 16
SSM_GROUPS = D_MODEL // SSM_GROUP
SSM_P = 64
SSM_BLOCKS = 8
SSM_BLOCK_G = SSM_GROUPS // SSM_BLOCKS
SSM_HALF = SSM_BLOCK_G * SSM_P
SSM_STATE_W = 2 * SSM_BLOCKS * SSM_HALF
D_FF = 4 * D_MODEL
RMS_EPS = 1e-6
GN_EPS = 1e-5
WKV_GN_EPS = 64e-5
PAST_LEN = 16384

MIX_CHUNK = 64
MIX_GROUP = 4
SSM_CHUNK = 64
MLP_ROWS = 512
FF_CHUNK = 1024

VMEM_LIMIT = 56 * 1024 * 1024


def _dot(a, b):
    return jnp.dot(a.astype(BF16), b.astype(BF16), preferred_element_type=F32)


def _dot_nt(a, b):
    return lax.dot_general(a.astype(BF16), b.astype(BF16), (((1,), (1,)), ((), ())),
                           preferred_element_type=F32)


def _dot_tn(a, b):
    return lax.dot_general(a.astype(BF16), b.astype(BF16), (((0,), (0,)), ((), ())),
                           preferred_element_type=F32)


def _split3(x):
    hi = x.astype(BF16)
    r1 = x - hi.astype(F32)
    mid = r1.astype(BF16)
    lo = (r1 - mid.astype(F32)).astype(BF16)
    return hi, mid, lo


def _dot_exact_lhs(a_bf16, x):
    hi, mid, lo = _split3(x)
    f = lambda p: jnp.dot(a_bf16, p, preferred_element_type=F32)
    return f(hi) + f(mid) + f(lo)


def _segsum(x, ones_blk):
    hi, mid, lo = _split3(x)
    outs = []
    for c in range(x.shape[1] // 256):
        sl = slice(c * 256, (c + 1) * 256)
        f = lambda p: jnp.dot(p[:, sl], ones_blk, preferred_element_type=F32)
        outs.append(f(hi) + f(mid) + f(lo))
    return jnp.concatenate(outs, axis=1)


def _rms(x, w):
    return x * lax.rsqrt(jnp.mean(x * x, axis=-1, keepdims=True) + RMS_EPS) * w


def _sigmoid(x):
    return 1.0 / (1.0 + jnp.exp(-x))


def _softplus(x):
    return jnp.maximum(x, 0.0) + jnp.log1p(jnp.exp(-jnp.abs(x)))


def _head_norm(z, eps, ones_blk):
    mu = _segsum(z, ones_blk) * (1.0 / HEAD_DIM)
    zc = z - mu
    var = _segsum(zc * zc, ones_blk) * (1.0 / HEAD_DIM)
    return zc * lax.rsqrt(var + eps)


def _rope(z, cos, sin_signed):
    lane = lax.broadcasted_iota(jnp.int32, (1, HEADS_W), 1) % HEAD_DIM
    swapped = jnp.where(lane < HEAD_DIM // 2,
                        pltpu.roll(z, HEADS_W - HEAD_DIM // 2, axis=1),
                        pltpu.roll(z, HEAD_DIM // 2, axis=1))
    return z * cos + swapped * sin_signed


def _wkv_features(xs, w0, lora_w, a0, g_b, k_k, k_a, r_k, ones_blk):
    r = xs[:, 0:HEADS_W]
    kw = xs[:, HEADS_W:2 * HEADS_W]
    vw = xs[:, 2 * HEADS_W:3 * HEADS_W]
    lo = xs[:, 3 * HEADS_W:3 * HEADS_W + 128]
    lane = lax.broadcasted_iota(jnp.int32, (1, 128), 1)
    lo = jnp.where(lane < DECAY_LORA, jnp.tanh(lo), lo)
    ll = _dot(lo, lora_w)
    w_log = -_softplus(-(w0 + ll[:, 0:HEADS_W])) - 0.5
    log_decay = -jnp.exp(w_log)
    alr = _sigmoid(a0 + ll[:, HEADS_W:2 * HEADS_W])
    gate = _dot(_sigmoid(xs[:, 3 * HEADS_W + 128:SHIFT_W]), g_b)
    kk = kw * k_k
    kk = kk / jnp.maximum(jnp.sqrt(_segsum(kk * kk, ones_blk)), 1e-12)
    k_mod = kw * (1.0 + (alr - 1.0) * k_a)
    bonus = _segsum(r * k_mod * r_k, ones_blk) * vw
    return r, log_decay, k_mod, vw, -kk, kk * alr, gate, bonus


def _stack_masked(x2, m0):
    return jnp.concatenate([jnp.where(m0, x2, 0.0), jnp.where(m0, 0.0, x2)], axis=0)


def _stack_dup(x2):
    return jnp.concatenate([x2, x2], axis=0)


def _mixer_prompt_kernel(
        x_ref, normw_ref, win_ref, cos_ref, sin_ref, qdec_ref, kdec_ref, dmask_ref, sdec_ref,
        retgn_ref, mu_ref, w0_ref, lora_ref, a0_ref, gb_ref, kk_ref, ka_ref, rk_ref, lnw_ref,
        lnb_ref, wout_ref, ones_ref, tril_ref, strict_ref, incl_ref, bd_ref,
        xo_ref, rets_ref, wkvs_ref, shift_ref,
        p_s, cat_s, rs_s, ws_s, carry_s, *, n_seq, chunk, group):
    i = pl.program_id(0)
    n_steps = pl.num_programs(0)
    C = chunk

    @pl.when(i == 0)
    def _():
        rs_s[...] = jnp.zeros_like(rs_s)
        ws_s[...] = jnp.zeros_like(ws_s)
        carry_s[...] = jnp.zeros_like(carry_s)

    x = x_ref[...].reshape(n_seq * C, D_MODEL)
    hn = _rms(x, normw_ref[...])
    p_s[...] = _dot(hn, win_ref[...])

    m0 = lax.broadcasted_iota(jnp.int32, (1, PAIR_W), 1) < HEAD_DIM
    ones_blk = ones_ref[...]
    NB = group
    R = NB * C
    row_id = lax.broadcasted_iota(jnp.int32, (R, 1), 0)
    tile_rows = lambda ref: jnp.concatenate([ref[...]] * NB, axis=0)
    pairs = range(N_PAIRS)
    sls = [slice(pr * PAIR_W, (pr + 1) * PAIR_W) for pr in pairs]
    chains = [(s, pr) for s in range(NB) for pr in pairs]
    seq_rows = [slice(s * C, (s + 1) * C) for s in range(NB)]

    def per_group(gi, carry):
        rows = pl.ds(pl.multiple_of(gi * R, R), R)
        b0 = gi * NB
        ret_states = {(s, pr): rs_s[pr, b0 + s] for s, pr in chains}
        wkv_states = {(s, pr): ws_s[pr, b0 + s] for s, pr in chains}
        shift_rows = [carry_s[pl.ds(b0 + s, 1), :] for s in range(NB)]
        wp = p_s[rows, RET_COLS:IN_W]
        cos, sin = tile_rows(cos_ref), tile_rows(sin_ref)
        q = _rope(p_s[rows, 0:HEADS_W], cos, sin)
        k = _rope(p_s[rows, HEADS_W:2 * HEADS_W], cos, sin) * (HEAD_DIM ** -0.5)
        v = p_s[rows, 2 * HEADS_W:3 * HEADS_W]
        g = p_s[rows, 3 * HEADS_W:4 * HEADS_W]
        qd = q * tile_rows(qdec_ref)
        kd = k * tile_rows(kdec_ref)
        cut = lambda z, s, pr: z[seq_rows[s], sls[pr]]
        r_sc = {c: _dot_nt(_stack_masked(cut(q, *c), m0), _stack_dup(cut(k, *c))) * dmask_ref[c[1]]
                for c in chains}
        r_inner = {c: _dot(r_sc[c], _stack_dup(cut(v, *c))) for c in chains}
        r_cross = {c: _dot(cut(qd, *c), ret_states[c]) for c in chains}
        new_ret = {c: sdec_ref[c[1]] * ret_states[c] + _dot_tn(cut(kd, *c), cut(v, *c)) * bd_ref[...]
                   for c in chains}
        o = jnp.concatenate(
            [jnp.concatenate([jnp.where(m0, r_inner[(s, pr)][0:C], r_inner[(s, pr)][C:2 * C])
                              + r_cross[(s, pr)] for pr in pairs], axis=1) for s in range(NB)], axis=0)
        ret = _head_norm(o, GN_EPS, ones_blk) * retgn_ref[...]
        ret_out = (g * _sigmoid(g) * ret).astype(BF16)

        prev = pltpu.roll(wp, 1, axis=0)
        for s in range(NB):
            prev = jnp.where(row_id == s * C, shift_rows[s], prev)
        xs = wp + (prev - wp) * mu_ref[...]
        r, lw, k_mod, vw, a_vec, b_vec, gate, bonus = _wkv_features(
            xs, w0_ref[...], lora_ref[...], a0_ref[...], gb_ref[...], kk_ref[...], ka_ref[...],
            rk_ref[...], ones_blk)
        cw = _dot_exact_lhs(tril_ref[...], lw)
        cw_last = [cw[s * C + C - 1:(s + 1) * C, :] for s in range(NB)]
        cwl = jnp.concatenate([jnp.broadcast_to(z, (C, HEADS_W)) for z in cw_last], axis=0)
        r_t = r * jnp.exp(cw)
        a_t = a_vec * jnp.exp(cw - lw)
        w_inv = jnp.exp(-cw)
        b_t = b_vec * w_inv
        k_t = k_mod * w_inv
        w_end = jnp.exp(cwl - cw)
        b_h = b_vec * w_end
        k_h = k_mod * w_end
        w_all = [jnp.exp(z) for z in cw_last]
        lhs = {c: jnp.concatenate([_stack_masked(cut(a_t, *c), m0), _stack_masked(cut(r_t, *c), m0)], axis=0)
               for c in chains}
        sc = {c: _dot_nt(lhs[c], jnp.concatenate([_stack_dup(cut(b_t, *c)), _stack_dup(cut(k_t, *c))], axis=0))
              for c in chains}
        on_state = {c: _dot_nt(lhs[c], wkv_states[c]) for c in chains}
        vv = {c: _stack_dup(cut(vw, *c)) for c in chains}
        n_pow = {c: sc[c][0:2 * C, 0:2 * C] * strict_ref[...] for c in chains}
        u = {c: on_state[c][0:2 * C] + _dot(sc[c][0:2 * C, 2 * C:4 * C] * strict_ref[...], vv[c])
             for c in chains}
        n_steps_solve = int(math.log2(C))
        for it in range(n_steps_solve):
            u = {c: u[c] + _dot(n_pow[c], u[c]) for c in chains}
            if it + 1 < n_steps_solve:
                n_pow = {c: _dot(n_pow[c], n_pow[c]) for c in chains}
        uv = {c: jnp.concatenate([u[c], vv[c]], axis=0) for c in chains}
        y_st = {c: on_state[c][2 * C:4 * C] + _dot(
            jnp.concatenate([sc[c][2 * C:4 * C, 0:2 * C] * incl_ref[...],
                             sc[c][2 * C:4 * C, 2 * C:4 * C] * incl_ref[...]], axis=1), uv[c]) for c in chains}
        new_wkv = {c: wkv_states[c] * w_all[c[0]][:, sls[c[1]]] + bd_ref[...] * _dot_tn(
            uv[c], jnp.concatenate([_stack_masked(cut(b_h, *c), m0), _stack_masked(cut(k_h, *c), m0)], axis=0))
            for c in chains}
        y = jnp.concatenate(
            [jnp.concatenate([jnp.where(m0, y_st[(s, pr)][0:C], y_st[(s, pr)][C:2 * C]) for pr in pairs], axis=1)
             for s in range(NB)], axis=0)
        yn = _head_norm(y, WKV_GN_EPS, ones_blk) * lnw_ref[...] + lnb_ref[...]
        cat_s[rows, 0:HEADS_W] = ret_out
        cat_s[rows, HEADS_W:2 * HEADS_W] = ((yn + bonus) * gate).astype(BF16)
        for s in range(NB):
            carry_s[pl.ds(b0 + s, 1), :] = wp[s * C + C - 1:(s + 1) * C, :]
        for s, pr in chains:
            rs_s[pr, b0 + s] = new_ret[(s, pr)]
            ws_s[pr, b0 + s] = new_wkv[(s, pr)]
        return carry

    lax.fori_loop(0, n_seq // NB, per_group, 0)

    out = jnp.dot(cat_s[...], wout_ref[...], preferred_element_type=F32)
    xo_ref[...] = (x_ref[...].reshape(n_seq * C, D_MODEL) + out).reshape(n_seq, C, D_MODEL)

    @pl.when(i == n_steps - 1)
    def _():
        shift_ref[...] = carry_s[...]
        for b in range(n_seq):
            for pr in range(N_PAIRS):
                rs = rs_s[pr, b]
                ws = ws_s[pr, b]
                rets_ref[b, 2 * pr] = rs[0:HEAD_DIM, 0:HEAD_DIM]
                rets_ref[b, 2 * pr + 1] = rs[HEAD_DIM:PAIR_W, HEAD_DIM:PAIR_W]
                wkvs_ref[b, 2 * pr] = ws[0:HEAD_DIM, 0:HEAD_DIM]
                wkvs_ref[b, 2 * pr + 1] = ws[HEAD_DIM:PAIR_W, HEAD_DIM:PAIR_W]


def _const_spec(shape):
    nd = len(shape)
    return pl.BlockSpec(shape, lambda *_: (0,) * nd)


def _const(a, dtype=F32):
    return jnp.asarray(np.asarray(a, np.float64), dtype=dtype)


def _retention_tables(chunk):
    log_g = np.log1p(-np.exp2(-5.0 - np.arange(N_HEADS, dtype=np.float64)))
    lane_g = np.repeat(log_g, HEAD_DIM)[None, :]
    idx = np.arange(chunk, dtype=np.float64)
    qdec = np.exp((idx + 1.0)[:, None] * lane_g)
    kdec = np.exp((chunk - 1.0 - idx)[:, None] * lane_g)
    rel = idx[:, None] - idx[None, :]
    dm = np.where(rel >= 0, np.exp(np.maximum(rel, 0.0)[None] * log_g[:, None, None]), 0.0)
    zero = np.zeros((chunk, chunk))
    dmask = np.stack([np.block([[dm[2 * p], zero], [zero, dm[2 * p + 1]]]) for p in range(N_PAIRS)])
    cdec = np.exp(chunk * log_g)
    hz = np.zeros((HEAD_DIM, HEAD_DIM))
    ho = np.ones((HEAD_DIM, HEAD_DIM))
    sdec = np.stack([np.block([[cdec[2 * p] * ho, hz], [hz, cdec[2 * p + 1] * ho]])
                     for p in range(N_PAIRS)])
    return _const(qdec), _const(kdec), _const(dmask), _const(sdec), cdec


def _rope_tables(pos):
    half = HEAD_DIM // 2
    inv_freq = ROPE_BASE ** (-np.arange(half, dtype=np.float64) / half)
    ang = np.asarray(pos, np.float64)[:, None] * inv_freq[None, :]
    cos = np.cos(ang)
    sin = np.sin(ang)
    cos_t = np.tile(np.concatenate([cos, cos], axis=1), (1, N_HEADS))
    sin_t = np.tile(np.concatenate([-sin, sin], axis=1), (1, N_HEADS))
    return _const(cos_t), _const(sin_t)


def _block_masks(chunk, group):
    i = np.arange(2 * chunk)
    same = (i[:, None] // chunk) == (i[None, :] // chunk)
    strict = same & (i[:, None] > i[None, :])
    incl = same & (i[:, None] >= i[None, :])
    j = np.arange(PAIR_W)
    bd = (j[:, None] // HEAD_DIM) == (j[None, :] // HEAD_DIM)
    t = np.arange(chunk)
    tril = np.kron(np.eye(group), t[:, None] >= t[None, :])
    o = np.arange(256)
    ones_blk = (o[:, None] // HEAD_DIM) == (o[None, :] // HEAD_DIM)
    return _const(strict), _const(incl), _const(bd), _const(tril, BF16), _const(ones_blk, BF16)


def _lora_block(w_b, a_b):
    z = jnp.zeros_like(w_b)
    return jnp.concatenate([jnp.concatenate([w_b, z], axis=1),
                            jnp.concatenate([z, a_b], axis=1)], axis=0).astype(BF16)


def _row(v):
    return v.reshape(1, -1).astype(F32)


def _mixer_prompt(x, lw):
    n_seq, t_len, _ = x.shape
    C = MIX_CHUNK
    n_steps = t_len // C
    cos_t, sin_t = _rope_tables(np.arange(t_len))
    qdec, kdec, dmask, sdec, _ = _retention_tables(C)
    strict, incl, bd, tril, ones_blk = _block_masks(C, MIX_GROUP)
    in_specs = [
        pl.BlockSpec((n_seq, C, D_MODEL), lambda i: (0, i, 0)),
        _const_spec((1, D_MODEL)),
        _const_spec((D_MODEL, IN_W)),
        pl.BlockSpec((C, HEADS_W), lambda i: (i, 0)),
        pl.BlockSpec((C, HEADS_W), lambda i: (i, 0)),
    ]
    tail = [qdec, kdec, dmask, sdec, lw["ret_gn"], lw["mu"], lw["w0"], lw["lora"], lw["a0"], lw["g_b"],
            lw["k_k"], lw["k_a"], lw["r_k"], lw["ln_w"], lw["ln_b"], lw["w_out"], ones_blk, tril,
            strict, incl, bd]
    in_specs += [_const_spec(a.shape) for a in tail]
    out_shape = (
        jax.ShapeDtypeStruct((n_seq, t_len, D_MODEL), F32),
        jax.ShapeDtypeStruct((n_seq, N_HEADS, HEAD_DIM, HEAD_DIM), F32),
        jax.ShapeDtypeStruct((n_seq, N_HEADS, HEAD_DIM, HEAD_DIM), F32),
        jax.ShapeDtypeStruct((n_seq, SHIFT_W), F32),
    )
    out_specs = (
        pl.BlockSpec((n_seq, C, D_MODEL), lambda i: (0, i, 0)),
        _const_spec((n_seq, N_HEADS, HEAD_DIM, HEAD_DIM)),
        _const_spec((n_seq, N_HEADS, HEAD_DIM, HEAD_DIM)),
        _const_spec((n_seq, SHIFT_W)),
    )
    scratch = [
        pltpu.VMEM((n_seq * C, IN_W), F32),
        pltpu.VMEM((n_seq * C, 2 * HEADS_W), BF16),
        pltpu.VMEM((N_PAIRS, n_seq, PAIR_W, PAIR_W), F32),
        pltpu.VMEM((N_PAIRS, n_seq, PAIR_W, PAIR_W), F32),
        pltpu.VMEM((n_seq, SHIFT_W), F32),
    ]
    return pl.pallas_call(
        functools.partial(_mixer_prompt_kernel, n_seq=n_seq, chunk=C, group=MIX_GROUP),
        grid=(n_steps,),
        in_specs=in_specs, out_specs=out_specs, out_shape=out_shape, scratch_shapes=scratch,
        compiler_params=pltpu.CompilerParams(dimension_semantics=("arbitrary",),
                                             vmem_limit_bytes=VMEM_LIMIT),
        name="mixer_prompt",
    )(x, lw["norm"], lw["w_in"], cos_t, sin_t, *tail)


def _mixer_sample_pre_kernel(
        x_ref, shift_ref, normw_ref, win_ref, cos_ref, sin_ref, qdec_ref, mu_ref, w0_ref, lora_ref,
        a0_ref, gb_ref, kk_ref, ka_ref, rk_ref, ones_ref, feat_ref, feat_t_ref, newshift_ref):
    x = x_ref[...]
    p = _dot(_rms(x, normw_ref[...]), win_ref[...])
    q = _rope(p[:, 0:HEADS_W], cos_ref[...], sin_ref[...])
    k = _rope(p[:, HEADS_W:2 * HEADS_W], cos_ref[...], sin_ref[...]) * (HEAD_DIM ** -0.5)
    wp = p[:, RET_COLS:IN_W]
    xs = wp + (shift_ref[...] - wp) * mu_ref[...]
    r, lw, k_mod, vw, a_vec, b_vec, gate, bonus = _wkv_features(
        xs, w0_ref[...], lora_ref[...], a0_ref[...], gb_ref[...], kk_ref[...], ka_ref[...],
        rk_ref[...], ones_ref[...])
    newshift_ref[...] = wp
    state_feats = [q, q * qdec_ref[...], k, p[:, 2 * HEADS_W:3 * HEADS_W], r, jnp.exp(lw), k_mod, vw,
                   a_vec, b_vec]
    for n, f in enumerate(state_feats):
        feat_t_ref[n * HEADS_W:(n + 1) * HEADS_W, :] = f.T
    for n, f in enumerate([p[:, 3 * HEADS_W:4 * HEADS_W], gate, bonus]):
        feat_ref[:, n * HEADS_W:(n + 1) * HEADS_W] = f


_F_Q, _F_QD, _F_K, _F_V, _F_R, _F_W, _F_KM, _F_VW, _F_A, _F_B = range(10)
_N_STATE_FEATS = 10
_F_G, _F_GATE, _F_BONUS = range(3)
_N_ROW_FEATS = 3


def _mixer_sample_state_kernel(feat_t_ref, sdec_ref, ret_ref, wkv_ref, o_t_ref, reto_ref, wkvo_ref):
    h = pl.program_id(0)

    def head_rows(n):
        return feat_t_ref[pl.ds(pl.multiple_of(n * HEADS_W + h * HEAD_DIM, HEAD_DIM), HEAD_DIM), :]

    def head_row(n, i):
        return feat_t_ref[pl.ds(n * HEADS_W + h * HEAD_DIM + i, 1), :]

    a, w, b_vec, k_mod, r = (head_rows(n) for n in (_F_A, _F_W, _F_B, _F_KM, _F_R))

    def wkv_row(i, carry):
        s = wkv_ref[0, i]
        sa = jnp.sum(s * a, axis=0, keepdims=True)
        s_new = s * w + sa * b_vec + head_row(_F_VW, i) * k_mod
        wkvo_ref[0, i] = s_new
        o_t_ref[1, 0, pl.ds(i, 1), :] = jnp.sum(s_new * r, axis=0, keepdims=True)
        return carry

    lax.fori_loop(0, HEAD_DIM, wkv_row, 0, unroll=4)

    v, q, k = head_rows(_F_V), head_rows(_F_Q), head_rows(_F_K)
    g = sdec_ref[h]

    def ret_row(d, acc):
        s = ret_ref[0, d]
        reto_ref[0, d] = g * s + head_row(_F_K, d) * v
        return acc + head_row(_F_QD, d) * s

    cross = lax.fori_loop(0, HEAD_DIM, ret_row, jnp.zeros_like(v), unroll=4)
    o_t_ref[0, 0] = cross + jnp.sum(q * k, axis=0, keepdims=True) * v


def _mixer_sample_post_kernel(
        x_ref, feat_ref, o_t_ref, retgn_ref, lnw_ref, lnb_ref, wout_ref, ones_ref, xo_ref):
    ones_blk = ones_ref[...]
    g = feat_ref[:, _F_G * HEADS_W:(_F_G + 1) * HEADS_W]
    gate = feat_ref[:, _F_GATE * HEADS_W:(_F_GATE + 1) * HEADS_W]
    bonus = feat_ref[:, _F_BONUS * HEADS_W:(_F_BONUS + 1) * HEADS_W]
    o = o_t_ref[...].T
    ret = _head_norm(o[:, 0:HEADS_W], GN_EPS, ones_blk) * retgn_ref[...]
    ret_out = g * _sigmoid(g) * ret
    yn = _head_norm(o[:, HEADS_W:2 * HEADS_W], WKV_GN_EPS, ones_blk) * lnw_ref[...] + lnb_ref[...]
    wkv_out = (yn + bonus) * gate
    cat = jnp.concatenate([ret_out, wkv_out], axis=1)
    xo_ref[...] = x_ref[...] + _dot(cat, wout_ref[...])


def _mixer_sample(x, ret_t, wkv_t, shift0, lw):
    n = x.shape[0]
    cos_t, sin_t = _rope_tables(np.full((1,), PAST_LEN))
    qdec, _, _, _, cdec = _retention_tables(1)
    _, _, _, _, ones_blk = _block_masks(1, 1)
    args = [x, shift0, lw["norm"], lw["w_in"], cos_t, sin_t, qdec, lw["mu"], lw["w0"], lw["lora"],
            lw["a0"], lw["g_b"], lw["k_k"], lw["k_a"], lw["r_k"], ones_blk]
    feat, feat_t, new_shift = pl.pallas_call(
        _mixer_sample_pre_kernel,
        grid=(1,),
        in_specs=[_const_spec(a.shape) for a in args],
        out_specs=(_const_spec((n, _N_ROW_FEATS * HEADS_W)), _const_spec((_N_STATE_FEATS * HEADS_W, n)),
                   _const_spec((n, SHIFT_W))),
        out_shape=(jax.ShapeDtypeStruct((n, _N_ROW_FEATS * HEADS_W), F32),
                   jax.ShapeDtypeStruct((_N_STATE_FEATS * HEADS_W, n), F32),
                   jax.ShapeDtypeStruct((n, SHIFT_W), F32)),
        compiler_params=pltpu.CompilerParams(vmem_limit_bytes=VMEM_LIMIT),
        name="mixer_sample_pre",
    )(*args)

    sdec = _const(np.broadcast_to(cdec[:, None, None], (N_HEADS, 1, n)))
    state_spec = pl.BlockSpec((1, HEAD_DIM, HEAD_DIM, n), lambda h: (h, 0, 0, 0))
    o_spec = pl.BlockSpec((2, 1, HEAD_DIM, n), lambda h: (0, h, 0, 0))
    o_t, ret_new, wkv_new = pl.pallas_call(
        _mixer_sample_state_kernel,
        grid=(N_HEADS,),
        in_specs=[_const_spec(feat_t.shape), _const_spec(sdec.shape), state_spec, state_spec],
        out_specs=(o_spec, state_spec, state_spec),
        out_shape=(jax.ShapeDtypeStruct((2, N_HEADS, HEAD_DIM, n), F32),
                   jax.ShapeDtypeStruct(ret_t.shape, F32), jax.ShapeDtypeStruct(wkv_t.shape, F32)),
        compiler_params=pltpu.CompilerParams(dimension_semantics=("arbitrary",),
                                             vmem_limit_bytes=VMEM_LIMIT),
        name="mixer_sample_state",
    )(feat_t, sdec, ret_t, wkv_t)

    args = [x, feat, o_t.reshape(2 * HEADS_W, n), lw["ret_gn"], lw["ln_w"], lw["ln_b"], lw["w_out"], ones_blk]
    x1 = pl.pallas_call(
        _mixer_sample_post_kernel,
        grid=(1,),
        in_specs=[_const_spec(a.shape) for a in args],
        out_specs=_const_spec((n, D_MODEL)),
        out_shape=jax.ShapeDtypeStruct((n, D_MODEL), F32),
        compiler_params=pltpu.CompilerParams(vmem_limit_bytes=VMEM_LIMIT),
        name="mixer_sample_post",
    )(*args)
    return x1, ret_new, wkv_new, new_shift


def _mlp_kernel(x_ref, normw_ref, wup_ref, wdown_ref, normf_ref, o_ref, *, final_norm):
    x = x_ref[...]
    hn = _rms(x, normw_ref[...]).astype(BF16)
    acc = x
    for c in range(D_FF // FF_CHUNK):
        sl = slice(c * FF_CHUNK, (c + 1) * FF_CHUNK)
        hid = jnp.dot(hn, wup_ref[:, sl], preferred_element_type=F32)
        hid = jnp.square(jnp.maximum(hid, 0.0)).astype(BF16)
        acc = acc + jnp.dot(hid, wdown_ref[sl, :], preferred_element_type=F32)
    if final_norm:
        acc = _rms(acc, normf_ref[...])
    o_ref[...] = acc


def _mlp(x, norm_w, w_up, w_down, norm_f, final_norm):
    rows = x.shape[0]
    tm = min(MLP_ROWS, rows)
    return pl.pallas_call(
        functools.partial(_mlp_kernel, final_norm=final_norm),
        grid=(rows // tm,),
        in_specs=[pl.BlockSpec((tm, D_MODEL), lambda i: (i, 0)),
                  _const_spec((1, D_MODEL)), _const_spec((D_MODEL, D_FF)),
                  _const_spec((D_FF, D_MODEL)), _const_spec((1, D_MODEL))],
        out_specs=pl.BlockSpec((tm, D_MODEL), lambda i: (i, 0)),
        out_shape=jax.ShapeDtypeStruct((rows, D_MODEL), F32),
        compiler_params=pltpu.CompilerParams(dimension_semantics=("arbitrary",),
                                             vmem_limit_bytes=VMEM_LIMIT),
        name="mlp",
    )(x, norm_w, w_up, w_down, norm_f)


def _ssm_prep_kernel(lre_ref, lim_ref, logdt_ref, bre_ref, bim_ref, are_ref, aim_ref, bbre_ref, bbim_ref):
    lre = jnp.minimum(lre_ref[...], -1e-4)
    lim = lim_ref[...]
    dt = jnp.exp(logdt_ref[...])
    mag = jnp.exp(lre * dt)
    are = mag * jnp.cos(lim * dt)
    aim = mag * jnp.sin(lim * dt)
    are_ref[...] = are
    aim_ref[...] = aim
    den = lre * lre + lim * lim
    nre = are - 1.0
    cre = (nre * lre + aim * lim) / den
    cim = (aim * lre - nre * lim) / den
    bre = bre_ref[...]
    bim = bim_ref[...]
    bbre_ref[...] = cre * bre - cim * bim
    bbim_ref[...] = cre * bim + cim * bre


def _gelu_exact(x):
    return 0.5 * x * (1.0 + lax.erf(x * (2.0 ** -0.5)))


def _ssm_kernel(x_ref, hre0_ref, him0_ref, normw_ref, are_ref, aim_ref, wb_ref, cre_ref, cim_ref, dskip_ref,
                wglu_ref, xo_ref, hre_ref, him_ref, xt_s, u_s, bu_s, y_s, hg_s, hre_s, him_s,
                *, n_seq, chunk, batch_major):
    i = pl.program_id(0)
    n_steps = pl.num_programs(0)
    rows_all = chunk * n_seq
    n_slabs = D_MODEL // 128

    @pl.when(i == 0)
    def _():
        hre_s[...] = hre0_ref[...]
        him_s[...] = him0_ref[...]

    if batch_major:
        for b in range(n_seq):
            for sl in range(n_slabs):
                xt_s[sl, pl.ds(b, chunk, stride=n_seq), :] = x_ref[b, :, sl * 128:(sl + 1) * 128]
        ssq = sum(jnp.sum(jnp.square(xt_s[sl]), axis=-1, keepdims=True) for sl in range(n_slabs))
        inv = lax.rsqrt(ssq * (1.0 / D_MODEL) + RMS_EPS)
        for sl in range(n_slabs):
            cols = slice(sl * 128, (sl + 1) * 128)
            u_s[:, cols] = xt_s[sl] * inv * normw_ref[:, cols]
    else:
        u_s[...] = _rms(x_ref[...].reshape(rows_all, D_MODEL), normw_ref[...])

    def input_proj(blk):
        bu_s[blk % 2] = _dot(u_s[:, blk * 128:(blk + 1) * 128], wb_ref[blk])

    input_proj(0)
    for blk in range(SSM_BLOCKS):
        if blk + 1 < SSM_BLOCKS:
            input_proj(blk + 1)
        buf = bu_s.at[blk % 2]
        cols = slice(blk * SSM_HALF, (blk + 1) * SSM_HALF)
        a_re = jnp.broadcast_to(are_ref[:, cols], (n_seq, SSM_HALF))
        a_im = jnp.broadcast_to(aim_ref[:, cols], (n_seq, SSM_HALF))
        h_re = hre_s[:, cols]
        h_im = him_s[:, cols]
        for t in range(chunk):
            rows = slice(t * n_seq, (t + 1) * n_seq)
            n_re = a_re * h_re - a_im * h_im + buf[rows, 0:SSM_HALF]
            n_im = a_re * h_im + a_im * h_re + buf[rows, SSM_HALF:2 * SSM_HALF]
            buf[rows, 0:SSM_HALF] = n_re
            buf[rows, SSM_HALF:2 * SSM_HALF] = n_im
            h_re, h_im = n_re, n_im
        hre_s[:, cols] = h_re
        him_s[:, cols] = h_im
        y_s[:, blk * 128:(blk + 1) * 128] = (_dot(buf[:, 0:SSM_HALF], cre_ref[blk])
                                             - _dot(buf[:, SSM_HALF:2 * SSM_HALF], cim_ref[blk]))
    hg_s[...] = _gelu_exact(y_s[...] + dskip_ref[...] * u_s[...]).astype(BF16)
    glu_w = 256
    for c in range(D_MODEL // glu_w):
        cols = slice(c * glu_w, (c + 1) * glu_w)
        hg = hg_s[...]
        val = jnp.dot(hg, wglu_ref[:, cols], preferred_element_type=F32)
        gate = jnp.dot(hg, wglu_ref[:, D_MODEL + c * glu_w:D_MODEL + (c + 1) * glu_w],
                       preferred_element_type=F32)
        out = val * _sigmoid(gate)
        if batch_major:
            for sl in range(c * glu_w // 128, (c + 1) * glu_w // 128):
                xt_s[sl] = xt_s[sl] + out[:, sl * 128 - c * glu_w:(sl + 1) * 128 - c * glu_w]
        else:
            y_s[:, cols] = x_ref[...].reshape(rows_all, D_MODEL)[:, cols] + out
    if batch_major:
        for b in range(n_seq):
            for sl in range(n_slabs):
                xo_ref[b, :, sl * 128:(sl + 1) * 128] = xt_s[sl, pl.ds(b, chunk, stride=n_seq), :]
    else:
        xo_ref[...] = y_s[...].reshape(chunk, n_seq, D_MODEL)

    @pl.when(i == n_steps - 1)
    def _():
        hre_ref[...] = hre_s[...]
        him_ref[...] = him_s[...]


def _ssm_weights(lam_re, lam_im, log_dt, b_re, b_im, c_re, c_im):
    g, p = SSM_GROUPS, SSM_P
    n = g * SSM_GROUP
    rep = lambda z: jnp.repeat(z, SSM_GROUP, axis=0)
    bt_re = jnp.swapaxes(b_re, 1, 2).reshape(n, p)
    bt_im = jnp.swapaxes(b_im, 1, 2).reshape(n, p)
    args = [rep(lam_re), rep(lam_im), rep(log_dt.reshape(g, 1)), bt_re, bt_im]
    a_re, a_im, bb_re, bb_im = pl.pallas_call(
        _ssm_prep_kernel,
        grid=(1,),
        in_specs=[_const_spec(a.shape) for a in args],
        out_specs=tuple(_const_spec((n, p)) for _ in range(4)),
        out_shape=tuple(jax.ShapeDtypeStruct((n, p), F32) for _ in range(4)),
        name="ssm_prep",
    )(*args)
    a_re = a_re[::SSM_GROUP]
    a_im = a_im[::SSM_GROUP]
    eye = jnp.eye(SSM_BLOCK_G, dtype=F32)

    def in_block(bb):
        bb = bb.reshape(SSM_BLOCKS, SSM_BLOCK_G, SSM_GROUP, p)
        return jnp.einsum("bgcp,gh->bgchp", bb, eye).reshape(SSM_BLOCKS, 128, SSM_HALF)

    def out_block(cc):
        cc = cc.reshape(SSM_BLOCKS, SSM_BLOCK_G, SSM_GROUP, p)
        return jnp.einsum("bgcp,gh->bgphc", cc, eye).reshape(SSM_BLOCKS, SSM_HALF, 128)

    w_b = jnp.concatenate([in_block(bb_re), in_block(bb_im)], axis=2).astype(BF16)
    n_state = SSM_GROUPS * SSM_P
    return (a_re.reshape(1, n_state), a_im.reshape(1, n_state), w_b,
            out_block(c_re).astype(BF16), out_block(c_im).astype(BF16))


def _ssm_layer(x, h_re0, h_im0, sw, chunk, batch_major):
    if batch_major:
        n_seq, t_len, _ = x.shape
        x_block = (n_seq, chunk, D_MODEL)
        x_map = lambda i: (0, i, 0)
    else:
        t_len, n_seq, _ = x.shape
        x_block = (chunk, n_seq, D_MODEL)
        x_map = lambda i: (i, 0, 0)
    rows = chunk * n_seq
    n_state = SSM_GROUPS * SSM_P
    args = [x, h_re0, h_im0, sw["norm"], sw["a_re"], sw["a_im"], sw["w_b"], sw["c_re"], sw["c_im"],
            sw["d_skip"], sw["w_glu"]]
    in_specs = [pl.BlockSpec(x_block, x_map)] + [_const_spec(a.shape) for a in args[1:]]
    state = jax.ShapeDtypeStruct((n_seq, n_state), F32)
    return pl.pallas_call(
        functools.partial(_ssm_kernel, n_seq=n_seq, chunk=chunk, batch_major=batch_major),
        grid=(t_len // chunk,),
        in_specs=in_specs,
        out_specs=(pl.BlockSpec(x_block, x_map), _const_spec(state.shape), _const_spec(state.shape)),
        out_shape=(jax.ShapeDtypeStruct(x.shape, F32), state, state),
        scratch_shapes=[pltpu.VMEM((D_MODEL // 128, rows, 128), F32),
                        pltpu.VMEM((rows, D_MODEL), F32), pltpu.VMEM((2, rows, 2 * SSM_HALF), F32),
                        pltpu.VMEM((rows, D_MODEL), F32), pltpu.VMEM((rows, D_MODEL), BF16),
                        pltpu.VMEM((n_seq, n_state), F32), pltpu.VMEM((n_seq, n_state), F32)],
        compiler_params=pltpu.CompilerParams(dimension_semantics=("arbitrary",),
                                             vmem_limit_bytes=VMEM_LIMIT),
        name="ssm_layer",
    )(*args)


def kernel(x_prompt, x_sample, state_ret, state_wkv, state_shift, state_ssm_re, state_ssm_im, norm_mix, w_in, ret_gn, mu_shift, wkv_w0, wkv_wB, wkv_a0, wkv_aB, wkv_gB, wkv_kk, wkv_ka, wkv_rk, wkv_ln_w, wkv_ln_b, w_out, ssm_lambda_re, ssm_lambda_im, ssm_log_dt, ssm_B_re, ssm_B_im, ssm_C_re, ssm_C_im, ssm_D, ssm_w_glu, mlp_norm, mlp_up, mlp_down, norm_f):
    lw = dict(
        norm=_row(norm_mix[0]), w_in=w_in[0].astype(BF16), ret_gn=_row(ret_gn[0]), mu=_row(mu_shift[0]),
        w0=_row(wkv_w0[0]), lora=_lora_block(wkv_wB[0], wkv_aB[0]), a0=_row(wkv_a0[0]),
        g_b=wkv_gB[0].astype(BF16), k_k=_row(wkv_kk[0]), k_a=_row(wkv_ka[0]), r_k=_row(wkv_rk[0]),
        ln_w=_row(wkv_ln_w[0]), ln_b=_row(wkv_ln_b[0]), w_out=w_out[0].astype(BF16))
    a_re, a_im, w_b, c_re, c_im = _ssm_weights(ssm_lambda_re[0], ssm_lambda_im[0], ssm_log_dt[0],
                                               ssm_B_re[0], ssm_B_im[0], ssm_C_re[0], ssm_C_im[0])
    sw = dict(norm=_row(norm_mix[1]), a_re=a_re, a_im=a_im, w_b=w_b, c_re=c_re, c_im=c_im,
              d_skip=_row(ssm_D[0]), w_glu=ssm_w_glu[0].astype(BF16))
    n_state = SSM_GROUPS * SSM_P
    up = [mlp_up[l].astype(BF16) for l in range(2)]
    down = [mlp_down[l].astype(BF16) for l in range(2)]
    nf = _row(norm_f)

    n_p, t_p, _ = x_prompt.shape
    x1, ret_p, wkv_p, shift_p = _mixer_prompt(x_prompt, lw)
    x1 = _mlp(x1.reshape(n_p * t_p, D_MODEL), _row(mlp_norm[0]), up[0], down[0], nf, False)
    zero_state = jnp.zeros((n_p, n_state), F32)
    x2, ssm_re_p, ssm_im_p = _ssm_layer(x1.reshape(n_p, t_p, D_MODEL), zero_state, zero_state, sw,
                                        SSM_CHUNK, True)
    y_p = _mlp(x2.reshape(n_p * t_p, D_MODEL), _row(mlp_norm[1]), up[1], down[1], nf, True)
    y_prompt = y_p.reshape(n_p, t_p, D_MODEL)
    ssm_re_p = ssm_re_p.reshape(n_p, SSM_GROUPS, SSM_P)
    ssm_im_p = ssm_im_p.reshape(n_p, SSM_GROUPS, SSM_P)

    n_s = x_sample.shape[0]
    xs = x_sample.reshape(n_s, D_MODEL)
    seq_last = lambda s: jnp.transpose(s, (1, 2, 3, 0))
    seq_first = lambda s: jnp.transpose(s, (3, 0, 1, 2))
    xs1, ret_s, wkv_s, shift_s = _mixer_sample(xs, seq_last(state_ret[0]), seq_last(state_wkv[0]),
                                               state_shift[0], lw)
    ret_s, wkv_s = seq_first(ret_s), seq_first(wkv_s)
    xs1 = _mlp(xs1, _row(mlp_norm[0]), up[0], down[0], nf, False)
    xs2, ssm_re_s, ssm_im_s = _ssm_layer(
        xs1.reshape(1, n_s, D_MODEL), state_ssm_re[0].reshape(n_s, n_state),
        state_ssm_im[0].reshape(n_s, n_state), sw, 1, False)
    y_s = _mlp(xs2.reshape(n_s, D_MODEL), _row(mlp_norm[1]), up[1], down[1], nf, True)
    ssm_re_s = ssm_re_s.reshape(n_s, SSM_GROUPS, SSM_P)
    ssm_im_s = ssm_im_s.reshape(n_s, SSM_GROUPS, SSM_P)

    return (y_prompt, y_s.reshape(n_s, 1, D_MODEL),
            ret_p[None], wkv_p[None], shift_p[None], ssm_re_p[None], ssm_im_p[None],
            ret_s[None], wkv_s[None], shift_s[None], ssm_re_s[None], ssm_im_s[None])
```

```python
import functools
import math

import numpy as np
import jax
import jax.numpy as jnp
from jax import lax
from jax.experimental import pallas as pl
from jax.experimental.pallas import tpu as pltpu

F32 = jnp.float32
BF16 = jnp.bfloat16

D_MODEL = 1024
N_HEADS = 8
HEAD_DIM = 64
HEADS_W = N_HEADS * HEAD_DIM
N_PAIRS = N_HEADS // 2
PAIR_W = 2 * HEAD_DIM
ROPE_BASE = 10000.0
DECAY_LORA = 64
AAA_LORA = 64
GATE_LORA = 128
SHIFT_W = 3 * HEADS_W + DECAY_LORA + AAA_LORA + GATE_LORA
RET_COLS = 4 * HEADS_W
IN_W = RET_COLS + SHIFT_W
SSM_GROUP = 16
SSM_GROUPS = D_MODEL // SSM_GROUP
SSM_P = 64
SSM_BLOCKS = 8
SSM_BLOCK_G = SSM_GROUPS // SSM_BLOCKS
SSM_HALF = SSM_BLOCK_G * SSM_P
SSM_STATE_W = 2 * SSM_BLOCKS * SSM_HALF
D_FF = 4 * D_MODEL
RMS_EPS = 1e-6
GN_EPS = 1e-5
WKV_GN_EPS = 64e-5
PAST_LEN = 16384

MIX_CHUNK = 64
PROJ_PIECE = 256
MIX_GROUP = 4
SSM_CHUNK = 64
MLP_ROWS = 512
FF_CHUNK = 1024

VMEM_LIMIT = 58 * 1024 * 1024


def _dot(a, b):
    return jnp.dot(a.astype(BF16), b.astype(BF16), preferred_element_type=F32)


def _dot_nt(a, b):
    return lax.dot_general(a.astype(BF16), b.astype(BF16), (((1,), (1,)), ((), ())),
                           preferred_element_type=F32)


def _dot_tn(a, b):
    return lax.dot_general(a.astype(BF16), b.astype(BF16), (((0,), (0,)), ((), ())),
                           preferred_element_type=F32)


def _split3(x):
    hi = x.astype(BF16)
    r1 = x - hi.astype(F32)
    mid = r1.astype(BF16)
    lo = (r1 - mid.astype(F32)).astype(BF16)
    return hi, mid, lo


def _dot_exact_lhs(a_bf16, x):
    hi, mid, lo = _split3(x)
    f = lambda p: jnp.dot(a_bf16, p, preferred_element_type=F32)
    return f(hi) + f(mid) + f(lo)


def _segsum(x, ones_blk):
    hi = x.astype(BF16)
    lo = (x - hi.astype(F32)).astype(BF16)
    outs = []
    for c in range(x.shape[1] // 256):
        sl = slice(c * 256, (c + 1) * 256)
        f = lambda p: jnp.dot(p[:, sl], ones_blk, preferred_element_type=F32)
        outs.append(f(hi) + f(lo))
    return jnp.concatenate(outs, axis=1)


def _rms(x, w):
    return x * lax.rsqrt(jnp.mean(x * x, axis=-1, keepdims=True) + RMS_EPS) * w


def _sigmoid(x):
    return 1.0 / (1.0 + jnp.exp(-x))


def _softplus(x):
    return jnp.maximum(x, 0.0) + jnp.log1p(jnp.exp(-jnp.abs(x)))


def _head_norm(z, eps, ones_blk):
    mu = _segsum(z, ones_blk) * (1.0 / HEAD_DIM)
    zc = z - mu
    var = _segsum(zc * zc, ones_blk) * (1.0 / HEAD_DIM)
    return zc * lax.rsqrt(var + eps)


def _rope(z, cos, sin_signed):
    lane = lax.broadcasted_iota(jnp.int32, (1, HEADS_W), 1) % HEAD_DIM
    swapped = jnp.where(lane < HEAD_DIM // 2,
                        pltpu.roll(z, HEADS_W - HEAD_DIM // 2, axis=1),
                        pltpu.roll(z, HEAD_DIM // 2, axis=1))
    return z * cos + swapped * sin_signed


def _wkv_features(xs, w0, lora_w, a0, g_b, k_k, k_a, r_k, ones_blk, between=lambda: None):
    r = xs[:, 0:HEADS_W]
    kw = xs[:, HEADS_W:2 * HEADS_W]
    vw = xs[:, 2 * HEADS_W:3 * HEADS_W]
    lo = xs[:, 3 * HEADS_W:3 * HEADS_W + 128]
    lane = lax.broadcasted_iota(jnp.int32, (1, 128), 1)
    lo = jnp.where(lane < DECAY_LORA, jnp.tanh(lo), lo)
    ll = _dot(lo, lora_w)
    w_log = -_softplus(-(w0 + ll[:, 0:HEADS_W])) - 0.5
    log_decay = -jnp.exp(w_log)
    between()
    alr = _sigmoid(a0 + ll[:, HEADS_W:2 * HEADS_W])
    gate = _dot(_sigmoid(xs[:, 3 * HEADS_W + 128:SHIFT_W]), g_b)
    between()
    kk = kw * k_k
    kk = kk / jnp.maximum(jnp.sqrt(_segsum(kk * kk, ones_blk)), 1e-12)
    k_mod = kw * (1.0 + (alr - 1.0) * k_a)
    between()
    bonus = _segsum(r * k_mod * r_k, ones_blk) * vw
    return r, log_decay, k_mod, vw, -kk, kk * alr, gate, bonus


def _stack_masked(x2, m0):
    return jnp.concatenate([jnp.where(m0, x2, 0.0), jnp.where(m0, 0.0, x2)], axis=0)


def _stack_dup(x2):
    return jnp.concatenate([x2, x2], axis=0)


def _mixer_prompt_kernel(
        x_ref, xnext_ref, normw_ref, win_ref, cos_ref, sin_ref, qdec_ref, kdec_ref, dmask_ref, sdec_ref,
        retgn_ref, mu_ref, w0_ref, lora_ref, a0_ref, gb_ref, kk_ref, ka_ref, rk_ref, lnw_ref,
        lnb_ref, wout_ref, ones_ref, tril_ref, strict_ref, incl_ref, bd_ref,
        xo_ref, rets_ref, wkvs_ref, shift_ref,
        p_s, cat_s, rs_s, ws_s, carry_s, *, n_seq, chunk, group):
    i = pl.program_id(0)
    n_steps = pl.num_programs(0)
    C = chunk
    slot = i % 2

    def in_proj(x_rows):
        return _dot(_rms(x_rows, normw_ref[...]), win_ref[...])

    @pl.when(i == 0)
    def _():
        rs_s[...] = jnp.zeros_like(rs_s)
        ws_s[...] = jnp.zeros_like(ws_s)
        carry_s[...] = jnp.zeros_like(carry_s)
        p_s[0] = in_proj(x_ref[...].reshape(n_seq * C, D_MODEL))

    m0 = lax.broadcasted_iota(jnp.int32, (1, PAIR_W), 1) < HEAD_DIM
    ones_blk = ones_ref[...]
    NB = group
    R = NB * C
    row_id = lax.broadcasted_iota(jnp.int32, (R, 1), 0)
    tile_rows = lambda ref: jnp.concatenate([ref[...]] * NB, axis=0)
    pairs = range(N_PAIRS)
    sls = [slice(pr * PAIR_W, (pr + 1) * PAIR_W) for pr in pairs]
    chains = [(s, pr) for s in range(NB) for pr in pairs]
    seq_rows = [slice(s * C, (s + 1) * C) for s in range(NB)]

    def per_group(gi, carry):
        rows = pl.ds(pl.multiple_of(gi * R, R), R)
        b0 = gi * NB
        ret_states = {(s, pr): rs_s[pr, b0 + s] for s, pr in chains}
        wkv_states = {(s, pr): ws_s[pr, b0 + s] for s, pr in chains}
        shift_rows = [carry_s[pl.ds(b0 + s, 1), :] for s in range(NB)]
        p_cur = p_s.at[slot]
        wp = p_cur[rows, RET_COLS:IN_W]
        hn_next = _rms(xnext_ref[pl.ds(b0, NB)].reshape(R, D_MODEL), normw_ref[...]).astype(BF16)
        pieces = list(range(0, IN_W, PROJ_PIECE))

        def fill(n=1):
            for _ in range(min(n, len(pieces))):
                c0 = pieces.pop(0)
                p_s[1 - slot, rows, c0:c0 + PROJ_PIECE] = jnp.dot(
                    hn_next, win_ref[:, c0:c0 + PROJ_PIECE], preferred_element_type=F32)

        cos, sin = tile_rows(cos_ref), tile_rows(sin_ref)
        q = _rope(p_cur[rows, 0:HEADS_W], cos, sin)
        fill()
        k = _rope(p_cur[rows, HEADS_W:2 * HEADS_W], cos, sin) * (HEAD_DIM ** -0.5)
        fill()
        v = p_cur[rows, 2 * HEADS_W:3 * HEADS_W]
        g = p_cur[rows, 3 * HEADS_W:4 * HEADS_W]
        qd = q * tile_rows(qdec_ref)
        kd = k * tile_rows(kdec_ref)
        fill()
        cut = lambda z, s, pr: z[seq_rows[s], sls[pr]]
        r_sc = {c: _dot_nt(_stack_masked(cut(q, *c), m0), _stack_dup(cut(k, *c))) * dmask_ref[c[1]]
                for c in chains}
        r_inner = {c: _dot(r_sc[c], _stack_dup(cut(v, *c))) for c in chains}
        r_cross = {c: _dot(cut(qd, *c), ret_states[c]) for c in chains}
        new_ret = {c: sdec_ref[c[1]] * ret_states[c] + _dot_tn(cut(kd, *c), cut(v, *c)) * bd_ref[...]
                   for c in chains}
        o = jnp.concatenate(
            [jnp.concatenate([jnp.where(m0, r_inner[(s, pr)][0:C], r_inner[(s, pr)][C:2 * C])
                              + r_cross[(s, pr)] for pr in pairs], axis=1) for s in range(NB)], axis=0)
        fill()
        ret = _head_norm(o, GN_EPS, ones_blk) * retgn_ref[...]
        ret_out = (g * _sigmoid(g) * ret).astype(BF16)
        fill()

        prev = pltpu.roll(wp, 1, axis=0)
        for s in range(NB):
            prev = jnp.where(row_id == s * C, shift_rows[s], prev)
        xs = wp + (prev - wp) * mu_ref[...]
        fill()
        r, lw, k_mod, vw, a_vec, b_vec, gate, bonus = _wkv_features(
            xs, w0_ref[...], lora_ref[...], a0_ref[...], gb_ref[...], kk_ref[...], ka_ref[...],
            rk_ref[...], ones_blk, fill)
        cw = _dot_exact_lhs(tril_ref[...], lw)
        fill()
        cw_last = [cw[s * C + C - 1:(s + 1) * C, :] for s in range(NB)]
        cwl = jnp.concatenate([jnp.broadcast_to(z, (C, HEADS_W)) for z in cw_last], axis=0)
        r_t = r * jnp.exp(cw)
        a_t = a_vec * jnp.exp(cw - lw)
        fill()
        w_inv = jnp.exp(-cw)
        b_t = b_vec * w_inv
        k_t = k_mod * w_inv
        fill()
        w_end = jnp.exp(cwl - cw)
        b_h = b_vec * w_end
        k_h = k_mod * w_end
        fill()
        w_all = [jnp.exp(z) for z in cw_last]
        lhs = {c: jnp.concatenate([_stack_masked(cut(a_t, *c), m0), _stack_masked(cut(r_t, *c), m0)], axis=0)
               for c in chains}
        fill(len(pieces))
        sc = {c: _dot_nt(lhs[c], jnp.concatenate([_stack_dup(cut(b_t, *c)), _stack_dup(cut(k_t, *c))], axis=0))
              for c in chains}
        on_state = {c: _dot_nt(lhs[c], wkv_states[c]) for c in chains}
        vv = {c: _stack_dup(cut(vw, *c)) for c in chains}
        n_pow = {c: sc[c][0:2 * C, 0:2 * C] * strict_ref[...] for c in chains}
        u = {c: on_state[c][0:2 * C] + _dot(sc[c][0:2 * C, 2 * C:4 * C] * strict_ref[...], vv[c])
             for c in chains}
        n_steps_solve = int(math.log2(C))
        for it in range(n_steps_solve):
            u = {c: u[c] + _dot(n_pow[c], u[c]) for c in chains}
            if it + 1 < n_steps_solve:
                n_pow = {c: _dot(n_pow[c], n_pow[c]) for c in chains}
        uv = {c: jnp.concatenate([u[c], vv[c]], axis=0) for c in chains}
        y_st = {c: on_state[c][2 * C:4 * C] + _dot(
            jnp.concatenate([sc[c][2 * C:4 * C, 0:2 * C] * incl_ref[...],
                             sc[c][2 * C:4 * C, 2 * C:4 * C] * incl_ref[...]], axis=1), uv[c]) for c in chains}
        new_wkv = {c: wkv_states[c] * w_all[c[0]][:, sls[c[1]]] + bd_ref[...] * _dot_tn(
            uv[c], jnp.concatenate([_stack_masked(cut(b_h, *c), m0), _stack_masked(cut(k_h, *c), m0)], axis=0))
            for c in chains}
        y = jnp.concatenate(
            [jnp.concatenate([jnp.where(m0, y_st[(s, pr)][0:C], y_st[(s, pr)][C:2 * C]) for pr in pairs], axis=1)
             for s in range(NB)], axis=0)
        yn = _head_norm(y, WKV_GN_EPS, ones_blk) * lnw_ref[...] + lnb_ref[...]
        cat_s[rows, 0:HEADS_W] = ret_out
        cat_s[rows, HEADS_W:2 * HEADS_W] = ((yn + bonus) * gate).astype(BF16)
        for s in range(NB):
            carry_s[pl.ds(b0 + s, 1), :] = wp[s * C + C - 1:(s + 1) * C, :]
        for s, pr in chains:
            rs_s[pr, b0 + s] = new_ret[(s, pr)]
            ws_s[pr, b0 + s] = new_wkv[(s, pr)]
        return carry

    lax.fori_loop(0, n_seq // NB, per_group, 0)

    out = jnp.dot(cat_s[...], wout_ref[...], preferred_element_type=F32)
    xo_ref[...] = (x_ref[...].reshape(n_seq * C, D_MODEL) + out).reshape(n_seq, C, D_MODEL)

    @pl.when(i == n_steps - 1)
    def _():
        shift_ref[...] = carry_s[...]
        for b in range(n_seq):
            for pr in range(N_PAIRS):
                rs = rs_s[pr, b]
                ws = ws_s[pr, b]
                rets_ref[b, 2 * pr] = rs[0:HEAD_DIM, 0:HEAD_DIM]
                rets_ref[b, 2 * pr + 1] = rs[HEAD_DIM:PAIR_W, HEAD_DIM:PAIR_W]
                wkvs_ref[b, 2 * pr] = ws[0:HEAD_DIM, 0:HEAD_DIM]
                wkvs_ref[b, 2 * pr + 1] = ws[HEAD_DIM:PAIR_W, HEAD_DIM:PAIR_W]


def _const_spec(shape):
    nd = len(shape)
    return pl.BlockSpec(shape, lambda *_: (0,) * nd, pipeline_mode=pl.Buffered(1))


def _const_out(shape):
    nd = len(shape)
    return pl.BlockSpec(shape, lambda *_: (0,) * nd)


def _const(a, dtype=F32):
    return jnp.asarray(np.asarray(a, np.float64), dtype=dtype)


def _retention_tables(chunk):
    log_g = np.log1p(-np.exp2(-5.0 - np.arange(N_HEADS, dtype=np.float64)))
    lane_g = np.repeat(log_g, HEAD_DIM)[None, :]
    idx = np.arange(chunk, dtype=np.float64)
    qdec = np.exp((idx + 1.0)[:, None] * lane_g)
    kdec = np.exp((chunk - 1.0 - idx)[:, None] * lane_g)
    rel = idx[:, None] - idx[None, :]
    dm = np.where(rel >= 0, np.exp(np.maximum(rel, 0.0)[None] * log_g[:, None, None]), 0.0)
    zero = np.zeros((chunk, chunk))
    dmask = np.stack([np.block([[dm[2 * p], zero], [zero, dm[2 * p + 1]]]) for p in range(N_PAIRS)])
    cdec = np.exp(chunk * log_g)
    hz = np.zeros((HEAD_DIM, HEAD_DIM))
    ho = np.ones((HEAD_DIM, HEAD_DIM))
    sdec = np.stack([np.block([[cdec[2 * p] * ho, hz], [hz, cdec[2 * p + 1] * ho]])
                     for p in range(N_PAIRS)])
    return _const(qdec), _const(kdec), _const(dmask), _const(sdec), cdec


def _rope_tables(pos):
    half = HEAD_DIM // 2
    inv_freq = ROPE_BASE ** (-np.arange(half, dtype=np.float64) / half)
    ang = np.asarray(pos, np.float64)[:, None] * inv_freq[None, :]
    cos = np.cos(ang)
    sin = np.sin(ang)
    cos_t = np.tile(np.concatenate([cos, cos], axis=1), (1, N_HEADS))
    sin_t = np.tile(np.concatenate([-sin, sin], axis=1), (1, N_HEADS))
    return _const(cos_t), _const(sin_t)


def _block_masks(chunk, group):
    i = np.arange(2 * chunk)
    same = (i[:, None] // chunk) == (i[None, :] // chunk)
    strict = same & (i[:, None] > i[None, :])
    incl = same & (i[:, None] >= i[None, :])
    j = np.arange(PAIR_W)
    bd = (j[:, None] // HEAD_DIM) == (j[None, :] // HEAD_DIM)
    t = np.arange(chunk)
    tril = np.kron(np.eye(group), t[:, None] >= t[None, :])
    o = np.arange(256)
    ones_blk = (o[:, None] // HEAD_DIM) == (o[None, :] // HEAD_DIM)
    return _const(strict), _const(incl), _const(bd), _const(tril, BF16), _const(ones_blk, BF16)


def _lora_block(w_b, a_b):
    z = jnp.zeros_like(w_b)
    return jnp.concatenate([jnp.concatenate([w_b, z], axis=1),
                            jnp.concatenate([z, a_b], axis=1)], axis=0).astype(BF16)


def _row(v):
    return v.reshape(1, -1).astype(F32)


def _mixer_prompt(x, lw):
    n_seq, t_len, _ = x.shape
    C = MIX_CHUNK
    n_steps = t_len // C
    cos_t, sin_t = _rope_tables(np.arange(t_len))
    qdec, kdec, dmask, sdec, _ = _retention_tables(C)
    strict, incl, bd, tril, ones_blk = _block_masks(C, MIX_GROUP)
    in_specs = [
        pl.BlockSpec((n_seq, C, D_MODEL), lambda i: (0, i, 0)),
        pl.BlockSpec((n_seq, C, D_MODEL), lambda i: (0, jnp.minimum(i + 1, n_steps - 1), 0)),
        _const_spec((1, D_MODEL)),
        _const_spec((D_MODEL, IN_W)),
        pl.BlockSpec((C, HEADS_W), lambda i: (i, 0)),
        pl.BlockSpec((C, HEADS_W), lambda i: (i, 0)),
    ]
    tail = [qdec, kdec, dmask, sdec, lw["ret_gn"], lw["mu"], lw["w0"], lw["lora"], lw["a0"], lw["g_b"],
            lw["k_k"], lw["k_a"], lw["r_k"], lw["ln_w"], lw["ln_b"], lw["w_out"], ones_blk, tril,
            strict, incl, bd]
    in_specs += [_const_spec(a.shape) for a in tail]
    out_shape = (
        jax.ShapeDtypeStruct((n_seq, t_len, D_MODEL), F32),
        jax.ShapeDtypeStruct((n_seq, N_HEADS, HEAD_DIM, HEAD_DIM), F32),
        jax.ShapeDtypeStruct((n_seq, N_HEADS, HEAD_DIM, HEAD_DIM), F32),
        jax.ShapeDtypeStruct((n_seq, SHIFT_W), F32),
    )
    out_specs = (
        pl.BlockSpec((n_seq, C, D_MODEL), lambda i: (0, i, 0)),
        _const_out((n_seq, N_HEADS, HEAD_DIM, HEAD_DIM)),
        _const_out((n_seq, N_HEADS, HEAD_DIM, HEAD_DIM)),
        _const_out((n_seq, SHIFT_W)),
    )
    scratch = [
        pltpu.VMEM((2, n_seq * C, IN_W), F32),
        pltpu.VMEM((n_seq * C, 2 * HEADS_W), BF16),
        pltpu.VMEM((N_PAIRS, n_seq, PAIR_W, PAIR_W), F32),
        pltpu.VMEM((N_PAIRS, n_seq, PAIR_W, PAIR_W), F32),
        pltpu.VMEM((n_seq, SHIFT_W), F32),
    ]
    return pl.pallas_call(
        functools.partial(_mixer_prompt_kernel, n_seq=n_seq, chunk=C, group=MIX_GROUP),
        grid=(n_steps,),
        in_specs=in_specs, out_specs=out_specs, out_shape=out_shape, scratch_shapes=scratch,
        compiler_params=pltpu.CompilerParams(dimension_semantics=("arbitrary",),
                                             vmem_limit_bytes=VMEM_LIMIT),
        name="mixer_prompt",
    )(x, x, lw["norm"], lw["w_in"], cos_t, sin_t, *tail)


def _mixer_sample_pre_kernel(
        x_ref, shift_ref, normw_ref, win_ref, cos_ref, sin_ref, qdec_ref, mu_ref, w0_ref, lora_ref,
        a0_ref, gb_ref, kk_ref, ka_ref, rk_ref, ones_ref, feat_ref, feat_t_ref, newshift_ref):
    x = x_ref[...]
    p = _dot(_rms(x, normw_ref[...]), win_ref[...])
    q = _rope(p[:, 0:HEADS_W], cos_ref[...], sin_ref[...])
    k = _rope(p[:, HEADS_W:2 * HEADS_W], cos_ref[...], sin_ref[...]) * (HEAD_DIM ** -0.5)
    wp = p[:, RET_COLS:IN_W]
    xs = wp + (shift_ref[...] - wp) * mu_ref[...]
    r, lw, k_mod, vw, a_vec, b_vec, gate, bonus = _wkv_features(
        xs, w0_ref[...], lora_ref[...], a0_ref[...], gb_ref[...], kk_ref[...], ka_ref[...],
        rk_ref[...], ones_ref[...])
    newshift_ref[...] = wp
    state_feats = [q, q * qdec_ref[...], k, p[:, 2 * HEADS_W:3 * HEADS_W], r, jnp.exp(lw), k_mod, vw,
                   a_vec, b_vec]
    for n, f in enumerate(state_feats):
        feat_t_ref[n * HEADS_W:(n + 1) * HEADS_W, :] = f.T
    for n, f in enumerate([p[:, 3 * HEADS_W:4 * HEADS_W], gate, bonus]):
        feat_ref[:, n * HEADS_W:(n + 1) * HEADS_W] = f


_F_Q, _F_QD, _F_K, _F_V, _F_R, _F_W, _F_KM, _F_VW, _F_A, _F_B = range(10)
_N_STATE_FEATS = 10
_F_G, _F_GATE, _F_BONUS = range(3)
_N_ROW_FEATS = 3


def _mixer_sample_state_kernel(feat_t_ref, sdec_ref, ret_ref, wkv_ref, o_t_ref, reto_ref, wkvo_ref):
    h = pl.program_id(0)

    def head_rows(n):
        return feat_t_ref[pl.ds(pl.multiple_of(n * HEADS_W + h * HEAD_DIM, HEAD_DIM), HEAD_DIM), :]

    def head_row(n, i):
        return feat_t_ref[pl.ds(n * HEADS_W + h * HEAD_DIM + i, 1), :]

    a, w, b_vec, k_mod, r = (head_rows(n) for n in (_F_A, _F_W, _F_B, _F_KM, _F_R))

    def wkv_row(i, carry):
        s = wkv_ref[0, i]
        sa = jnp.sum(s * a, axis=0, keepdims=True)
        s_new = s * w + sa * b_vec + head_row(_F_VW, i) * k_mod
        wkvo_ref[0, i] = s_new
        o_t_ref[1, 0, pl.ds(i, 1), :] = jnp.sum(s_new * r, axis=0, keepdims=True)
        return carry

    lax.fori_loop(0, HEAD_DIM, wkv_row, 0, unroll=4)

    v, q, k = head_rows(_F_V), head_rows(_F_Q), head_rows(_F_K)
    g = sdec_ref[h]

    def ret_row(d, acc):
        s = ret_ref[0, d]
        reto_ref[0, d] = g * s + head_row(_F_K, d) * v
        return acc + head_row(_F_QD, d) * s

    cross = lax.fori_loop(0, HEAD_DIM, ret_row, jnp.zeros_like(v), unroll=4)
    o_t_ref[0, 0] = cross + jnp.sum(q * k, axis=0, keepdims=True) * v


def _mixer_sample_post_kernel(
        x_ref, feat_ref, o_t_ref, retgn_ref, lnw_ref, lnb_ref, wout_ref, ones_ref, xo_ref):
    ones_blk = ones_ref[...]
    g = feat_ref[:, _F_G * HEADS_W:(_F_G + 1) * HEADS_W]
    gate = feat_ref[:, _F_GATE * HEADS_W:(_F_GATE + 1) * HEADS_W]
    bonus = feat_ref[:, _F_BONUS * HEADS_W:(_F_BONUS + 1) * HEADS_W]
    o = o_t_ref[...].T
    ret = _head_norm(o[:, 0:HEADS_W], GN_EPS, ones_blk) * retgn_ref[...]
    ret_out = g * _sigmoid(g) * ret
    yn = _head_norm(o[:, HEADS_W:2 * HEADS_W], WKV_GN_EPS, ones_blk) * lnw_ref[...] + lnb_ref[...]
    wkv_out = (yn + bonus) * gate
    cat = jnp.concatenate([ret_out, wkv_out], axis=1)
    xo_ref[...] = x_ref[...] + _dot(cat, wout_ref[...])


def _mixer_sample(x, ret_t, wkv_t, shift0, lw):
    n = x.shape[0]
    cos_t, sin_t = _rope_tables(np.full((1,), PAST_LEN))
    qdec, _, _, _, cdec = _retention_tables(1)
    _, _, _, _, ones_blk = _block_masks(1, 1)
    args = [x, shift0, lw["norm"], lw["w_in"], cos_t, sin_t, qdec, lw["mu"], lw["w0"], lw["lora"],
            lw["a0"], lw["g_b"], lw["k_k"], lw["k_a"], lw["r_k"], ones_blk]
    feat, feat_t, new_shift = pl.pallas_call(
        _mixer_sample_pre_kernel,
        grid=(1,),
        in_specs=[_const_spec(a.shape) for a in args],
        out_specs=(_const_out((n, _N_ROW_FEATS * HEADS_W)), _const_out((_N_STATE_FEATS * HEADS_W, n)),
                   _const_out((n, SHIFT_W))),
        out_shape=(jax.ShapeDtypeStruct((n, _N_ROW_FEATS * HEADS_W), F32),
                   jax.ShapeDtypeStruct((_N_STATE_FEATS * HEADS_W, n), F32),
                   jax.ShapeDtypeStruct((n, SHIFT_W), F32)),
        compiler_params=pltpu.CompilerParams(vmem_limit_bytes=VMEM_LIMIT),
        name="mixer_sample_pre",
    )(*args)

    sdec = _const(np.broadcast_to(cdec[:, None, None], (N_HEADS, 1, n)))
    state_spec = pl.BlockSpec((1, HEAD_DIM, HEAD_DIM, n), lambda h: (h, 0, 0, 0))
    o_spec = pl.BlockSpec((2, 1, HEAD_DIM, n), lambda h: (0, h, 0, 0))
    o_t, ret_new, wkv_new = pl.pallas_call(
        _mixer_sample_state_kernel,
        grid=(N_HEADS,),
        in_specs=[_const_spec(feat_t.shape), _const_spec(sdec.shape), state_spec, state_spec],
        out_specs=(o_spec, state_spec, state_spec),
        out_shape=(jax.ShapeDtypeStruct((2, N_HEADS, HEAD_DIM, n), F32),
                   jax.ShapeDtypeStruct(ret_t.shape, F32), jax.ShapeDtypeStruct(wkv_t.shape, F32)),
        compiler_params=pltpu.CompilerParams(dimension_semantics=("arbitrary",),
                                             vmem_limit_bytes=VMEM_LIMIT),
        name="mixer_sample_state",
    )(feat_t, sdec, ret_t, wkv_t)

    args = [x, feat, o_t.reshape(2 * HEADS_W, n), lw["ret_gn"], lw["ln_w"], lw["ln_b"], lw["w_out"], ones_blk]
    x1 = pl.pallas_call(
        _mixer_sample_post_kernel,
        grid=(1,),
        in_specs=[_const_spec(a.shape) for a in args],
        out_specs=_const_out((n, D_MODEL)),
        out_shape=jax.ShapeDtypeStruct((n, D_MODEL), F32),
        compiler_params=pltpu.CompilerParams(vmem_limit_bytes=VMEM_LIMIT),
        name="mixer_sample_post",
    )(*args)
    return x1, ret_new, wkv_new, new_shift


def _mlp_kernel(x_ref, normw_ref, wup_ref, wdown_ref, normf_ref, o_ref, *, final_norm):
    x = x_ref[...]
    hn = _rms(x, normw_ref[...]).astype(BF16)
    acc = x
    for c in range(D_FF // FF_CHUNK):
        sl = slice(c * FF_CHUNK, (c + 1) * FF_CHUNK)
        hid = jnp.dot(hn, wup_ref[:, sl], preferred_element_type=F32)
        hid = jnp.square(jnp.maximum(hid, 0.0)).astype(BF16)
        acc = acc + jnp.dot(hid, wdown_ref[sl, :], preferred_element_type=F32)
    if final_norm:
        acc = _rms(acc, normf_ref[...])
    o_ref[...] = acc


def _mlp(x, norm_w, w_up, w_down, norm_f, final_norm):
    rows = x.shape[0]
    tm = min(MLP_ROWS, rows)
    return pl.pallas_call(
        functools.partial(_mlp_kernel, final_norm=final_norm),
        grid=(rows // tm,),
        in_specs=[pl.BlockSpec((tm, D_MODEL), lambda i: (i, 0)),
                  _const_spec((1, D_MODEL)), _const_spec((D_MODEL, D_FF)),
                  _const_spec((D_FF, D_MODEL)), _const_spec((1, D_MODEL))],
        out_specs=pl.BlockSpec((tm, D_MODEL), lambda i: (i, 0)),
        out_shape=jax.ShapeDtypeStruct((rows, D_MODEL), F32),
        compiler_params=pltpu.CompilerParams(dimension_semantics=("arbitrary",),
                                             vmem_limit_bytes=VMEM_LIMIT),
        name="mlp",
    )(x, norm_w, w_up, w_down, norm_f)


def _ssm_prep_kernel(lre_ref, lim_ref, logdt_ref, bre_ref, bim_ref, are_ref, aim_ref, bbre_ref, bbim_ref):
    lre = jnp.minimum(lre_ref[...], -1e-4)
    lim = lim_ref[...]
    dt = jnp.exp(logdt_ref[...])
    mag = jnp.exp(lre * dt)
    are = mag * jnp.cos(lim * dt)
    aim = mag * jnp.sin(lim * dt)
    are_ref[...] = are
    aim_ref[...] = aim
    den = lre * lre + lim * lim
    nre = are - 1.0
    cre = (nre * lre + aim * lim) / den
    cim = (aim * lre - nre * lim) / den
    bre = bre_ref[...]
    bim = bim_ref[...]
    bbre_ref[...] = cre * bre - cim * bim
    bbim_ref[...] = cre * bim + cim * bre


def _gelu_exact(x):
    return 0.5 * x * (1.0 + lax.erf(x * (2.0 ** -0.5)))


def _ssm_kernel(x_ref, hre0_ref, him0_ref, normw_ref, are_ref, aim_ref, wb_ref, cre_ref, cim_ref, dskip_ref,
                wglu_ref, xo_ref, hre_ref, him_ref, xt_s, u_s, bu_s, y_s, hg_s, hre_s, him_s,
                *, n_seq, chunk, batch_major):
    i = pl.program_id(0)
    n_steps = pl.num_programs(0)
    rows_all = chunk * n_seq
    n_slabs = D_MODEL // 128

    @pl.when(i == 0)
    def _():
        hre_s[...] = hre0_ref[...]
        him_s[...] = him0_ref[...]

    if batch_major:
        for b in range(n_seq):
            for sl in range(n_slabs):
                xt_s[sl, pl.ds(b, chunk, stride=n_seq), :] = x_ref[b, :, sl * 128:(sl + 1) * 128]
        ssq = sum(jnp.sum(jnp.square(xt_s[sl]), axis=-1, keepdims=True) for sl in range(n_slabs))
        inv = lax.rsqrt(ssq * (1.0 / D_MODEL) + RMS_EPS)
        for sl in range(n_slabs):
            cols = slice(sl * 128, (sl + 1) * 128)
            u_s[:, cols] = xt_s[sl] * inv * normw_ref[:, cols]
    else:
        u_s[...] = _rms(x_ref[...].reshape(rows_all, D_MODEL), normw_ref[...])

    def input_proj(blk):
        bu_s[blk % 2] = _dot(u_s[:, blk * 128:(blk + 1) * 128], wb_ref[blk])

    input_proj(0)
    for blk in range(SSM_BLOCKS):
        if blk + 1 < SSM_BLOCKS:
            input_proj(blk + 1)
        buf = bu_s.at[blk % 2]
        cols = slice(blk * SSM_HALF, (blk + 1) * SSM_HALF)
        a_re = jnp.broadcast_to(are_ref[:, cols], (n_seq, SSM_HALF))
        a_im = jnp.broadcast_to(aim_ref[:, cols], (n_seq, SSM_HALF))
        h_re = hre_s[:, cols]
        h_im = him_s[:, cols]
        for t in range(chunk):
            rows = slice(t * n_seq, (t + 1) * n_seq)
            n_re = a_re * h_re - a_im * h_im + buf[rows, 0:SSM_HALF]
            n_im = a_re * h_im + a_im * h_re + buf[rows, SSM_HALF:2 * SSM_HALF]
            buf[rows, 0:SSM_HALF] = n_re
            buf[rows, SSM_HALF:2 * SSM_HALF] = n_im
            h_re, h_im = n_re, n_im
        hre_s[:, cols] = h_re
        him_s[:, cols] = h_im
        y_s[:, blk * 128:(blk + 1) * 128] = (_dot(buf[:, 0:SSM_HALF], cre_ref[blk])
                                             - _dot(buf[:, SSM_HALF:2 * SSM_HALF], cim_ref[blk]))
    hg_s[...] = _gelu_exact(y_s[...] + dskip_ref[...] * u_s[...]).astype(BF16)
    glu_w = 256
    for c in range(D_MODEL // glu_w):
        cols = slice(c * glu_w, (c + 1) * glu_w)
        hg = hg_s[...]
        val = jnp.dot(hg, wglu_ref[:, cols], preferred_element_type=F32)
        gate = jnp.dot(hg, wglu_ref[:, D_MODEL + c * glu_w:D_MODEL + (c + 1) * glu_w],
                       preferred_element_type=F32)
        out = val * _sigmoid(gate)
        if batch_major:
            for sl in range(c * glu_w // 128, (c + 1) * glu_w // 128):
                xt_s[sl] = xt_s[sl] + out[:, sl * 128 - c * glu_w:(sl + 1) * 128 - c * glu_w]
        else:
            y_s[:, cols] = x_ref[...].reshape(rows_all, D_MODEL)[:, cols] + out
    if batch_major:
        for b in range(n_seq):
            for sl in range(n_slabs):
                xo_ref[b, :, sl * 128:(sl + 1) * 128] = xt_s[sl, pl.ds(b, chunk, stride=n_seq), :]
    else:
        xo_ref[...] = y_s[...].reshape(chunk, n_seq, D_MODEL)

    @pl.when(i == n_steps - 1)
    def _():
        hre_ref[...] = hre_s[...]
        him_ref[...] = him_s[...]


def _ssm_weights(lam_re, lam_im, log_dt, b_re, b_im, c_re, c_im):
    g, p = SSM_GROUPS, SSM_P
    n = g * SSM_GROUP
    rep = lambda z: jnp.repeat(z, SSM_GROUP, axis=0)
    bt_re = jnp.swapaxes(b_re, 1, 2).reshape(n, p)
    bt_im = jnp.swapaxes(b_im, 1, 2).reshape(n, p)
    args = [rep(lam_re), rep(lam_im), rep(log_dt.reshape(g, 1)), bt_re, bt_im]
    a_re, a_im, bb_re, bb_im = pl.pallas_call(
        _ssm_prep_kernel,
        grid=(1,),
        in_specs=[_const_spec(a.shape) for a in args],
        out_specs=tuple(_const_out((n, p)) for _ in range(4)),
        out_shape=tuple(jax.ShapeDtypeStruct((n, p), F32) for _ in range(4)),
        name="ssm_prep",
    )(*args)
    a_re = a_re[::SSM_GROUP]
    a_im = a_im[::SSM_GROUP]
    eye = jnp.eye(SSM_BLOCK_G, dtype=F32)

    def in_block(bb):
        bb = bb.reshape(SSM_BLOCKS, SSM_BLOCK_G, SSM_GROUP, p)
        return jnp.einsum("bgcp,gh->bgchp", bb, eye).reshape(SSM_BLOCKS, 128, SSM_HALF)

    def out_block(cc):
        cc = cc.reshape(SSM_BLOCKS, SSM_BLOCK_G, SSM_GROUP, p)
        return jnp.einsum("bgcp,gh->bgphc", cc, eye).reshape(SSM_BLOCKS, SSM_HALF, 128)

    w_b = jnp.concatenate([in_block(bb_re), in_block(bb_im)], axis=2).astype(BF16)
    n_state = SSM_GROUPS * SSM_P
    return (a_re.reshape(1, n_state), a_im.reshape(1, n_state), w_b,
            out_block(c_re).astype(BF16), out_block(c_im).astype(BF16))


def _ssm_layer(x, h_re0, h_im0, sw, chunk, batch_major):
    if batch_major:
        n_seq, t_len, _ = x.shape
        x_block = (n_seq, chunk, D_MODEL)
        x_map = lambda i: (0, i, 0)
    else:
        t_len, n_seq, _ = x.shape
        x_block = (chunk, n_seq, D_MODEL)
        x_map = lambda i: (i, 0, 0)
    rows = chunk * n_seq
    n_state = SSM_GROUPS * SSM_P
    args = [x, h_re0, h_im0, sw["norm"], sw["a_re"], sw["a_im"], sw["w_b"], sw["c_re"], sw["c_im"],
            sw["d_skip"], sw["w_glu"]]
    in_specs = [pl.BlockSpec(x_block, x_map)] + [_const_spec(a.shape) for a in args[1:]]
    state = jax.ShapeDtypeStruct((n_seq, n_state), F32)
    return pl.pallas_call(
        functools.partial(_ssm_kernel, n_seq=n_seq, chunk=chunk, batch_major=batch_major),
        grid=(t_len // chunk,),
        in_specs=in_specs,
        out_specs=(pl.BlockSpec(x_block, x_map), _const_out(state.shape), _const_out(state.shape)),
        out_shape=(jax.ShapeDtypeStruct(x.shape, F32), state, state),
        scratch_shapes=[pltpu.VMEM((D_MODEL // 128, rows, 128), F32),
                        pltpu.VMEM((rows, D_MODEL), F32), pltpu.VMEM((2, rows, 2 * SSM_HALF), F32),
                        pltpu.VMEM((rows, D_MODEL), F32), pltpu.VMEM((rows, D_MODEL), BF16),
                        pltpu.VMEM((n_seq, n_state), F32), pltpu.VMEM((n_seq, n_state), F32)],
        compiler_params=pltpu.CompilerParams(dimension_semantics=("arbitrary",),
                                             vmem_limit_bytes=VMEM_LIMIT),
        name="ssm_layer",
    )(*args)


def kernel(x_prompt, x_sample, state_ret, state_wkv, state_shift, state_ssm_re, state_ssm_im, norm_mix, w_in, ret_gn, mu_shift, wkv_w0, wkv_wB, wkv_a0, wkv_aB, wkv_gB, wkv_kk, wkv_ka, wkv_rk, wkv_ln_w, wkv_ln_b, w_out, ssm_lambda_re, ssm_lambda_im, ssm_log_dt, ssm_B_re, ssm_B_im, ssm_C_re, ssm_C_im, ssm_D, ssm_w_glu, mlp_norm, mlp_up, mlp_down, norm_f):
    lw = dict(
        norm=_row(norm_mix[0]), w_in=w_in[0].astype(BF16), ret_gn=_row(ret_gn[0]), mu=_row(mu_shift[0]),
        w0=_row(wkv_w0[0]), lora=_lora_block(wkv_wB[0], wkv_aB[0]), a0=_row(wkv_a0[0]),
        g_b=wkv_gB[0].astype(BF16), k_k=_row(wkv_kk[0]), k_a=_row(wkv_ka[0]), r_k=_row(wkv_rk[0]),
        ln_w=_row(wkv_ln_w[0]), ln_b=_row(wkv_ln_b[0]), w_out=w_out[0].astype(BF16))
    a_re, a_im, w_b, c_re, c_im = _ssm_weights(ssm_lambda_re[0], ssm_lambda_im[0], ssm_log_dt[0],
                                               ssm_B_re[0], ssm_B_im[0], ssm_C_re[0], ssm_C_im[0])
    sw = dict(norm=_row(norm_mix[1]), a_re=a_re, a_im=a_im, w_b=w_b, c_re=c_re, c_im=c_im,
              d_skip=_row(ssm_D[0]), w_glu=ssm_w_glu[0].astype(BF16))
    n_state = SSM_GROUPS * SSM_P
    up = [mlp_up[l].astype(BF16) for l in range(2)]
    down = [mlp_down[l].astype(BF16) for l in range(2)]
    nf = _row(norm_f)

    n_p, t_p, _ = x_prompt.shape
    x1, ret_p, wkv_p, shift_p = _mixer_prompt(x_prompt, lw)
    x1 = _mlp(x1.reshape(n_p * t_p, D_MODEL), _row(mlp_norm[0]), up[0], down[0], nf, False)
    zero_state = jnp.zeros((n_p, n_state), F32)
    x2, ssm_re_p, ssm_im_p = _ssm_layer(x1.reshape(n_p, t_p, D_MODEL), zero_state, zero_state, sw,
                                        SSM_CHUNK, True)
    y_p = _mlp(x2.reshape(n_p * t_p, D_MODEL), _row(mlp_norm[1]), up[1], down[1], nf, True)
    y_prompt = y_p.reshape(n_p, t_p, D_MODEL)
    ssm_re_p = ssm_re_p.reshape(n_p, SSM_GROUPS, SSM_P)
    ssm_im_p = ssm_im_p.reshape(n_p, SSM_GROUPS, SSM_P)

    n_s = x_sample.shape[0]
    xs = x_sample.reshape(n_s, D_MODEL)
    seq_last = lambda s: jnp.transpose(s, (1, 2, 3, 0))
    seq_first = lambda s: jnp.transpose(s, (3, 0, 1, 2))
    xs1, ret_s, wkv_s, shift_s = _mixer_sample(xs, seq_last(state_ret[0]), seq_last(state_wkv[0]),
                                               state_shift[0], lw)
    ret_s, wkv_s = seq_first(ret_s), seq_first(wkv_s)
    xs1 = _mlp(xs1, _row(mlp_norm[0]), up[0], down[0], nf, False)
    xs2, ssm_re_s, ssm_im_s = _ssm_layer(
        xs1.reshape(1, n_s, D_MODEL), state_ssm_re[0].reshape(n_s, n_state),
        state_ssm_im[0].reshape(n_s, n_state), sw, 1, False)
    y_s = _mlp(xs2.reshape(n_s, D_MODEL), _row(mlp_norm[1]), up[1], down[1], nf, True)
    ssm_re_s = ssm_re_s.reshape(n_s, SSM_GROUPS, SSM_P)
    ssm_im_s = ssm_im_s.reshape(n_s, SSM_GROUPS, SSM_P)

    return (y_prompt, y_s.reshape(n_s, 1, D_MODEL),
            ret_p[None], wkv_p[None], shift_p[None], ssm_re_p[None], ssm_im_p[None],
            ret_s[None], wkv_s[None], shift_s[None], ssm_re_s[None], ssm_im_s[None])
```

```python
import functools
import math

import numpy as np
import jax
import jax.numpy as jnp
from jax import lax
from jax.experimental import pallas as pl
from jax.experimental.pallas import tpu as pltpu

F32 = jnp.float32
BF16 = jnp.bfloat16

D_MODEL = 1024
N_HEADS = 8
HEAD_DIM = 64
HEADS_W = N_HEADS * HEAD_DIM
N_PAIRS = N_HEADS // 2
PAIR_W = 2 * HEAD_DIM
ROPE_BASE = 10000.0
DECAY_LORA = 64
AAA_LORA = 64
GATE_LORA = 128
SHIFT_W = 3 * HEADS_W + DECAY_LORA + AAA_LORA + GATE_LORA
RET_COLS = 4 * HEADS_W
IN_W = RET_COLS + SHIFT_W
SSM_GROUP = 16
SSM_GROUPS = D_MODEL // SSM_GROUP
SSM_P = 64
SSM_BLOCKS = 8
SSM_BLOCK_G = SSM_GROUPS // SSM_BLOCKS
SSM_HALF = SSM_BLOCK_G * SSM_P
SSM_STATE_W = 2 * SSM_BLOCKS * SSM_HALF
D_FF = 4 * D_MODEL
RMS_EPS = 1e-6
GN_EPS = 1e-5
WKV_GN_EPS = 64e-5
PAST_LEN = 16384

MIX_CHUNK = 64
PROJ_PIECE = 256
MIX_GROUP = 4
SSM_CHUNK = 64
MLP_ROWS = 512
FF_CHUNK = 1024

VMEM_LIMIT = 58 * 1024 * 1024


def _dot(a, b):
    return jnp.dot(a.astype(BF16), b.astype(BF16), preferred_element_type=F32)


def _dot_nt(a, b):
    return lax.dot_general(a.astype(BF16), b.astype(BF16), (((1,), (1,)), ((), ())),
                           preferred_element_type=F32)


def _dot_tn(a, b):
    return lax.dot_general(a.astype(BF16), b.astype(BF16), (((0,), (0,)), ((), ())),
                           preferred_element_type=F32)


def _split3(x):
    hi = x.astype(BF16)
    r1 = x - hi.astype(F32)
    mid = r1.astype(BF16)
    lo = (r1 - mid.astype(F32)).astype(BF16)
    return hi, mid, lo


def _dot_exact_lhs(a_bf16, x):
    hi, mid, lo = _split3(x)
    f = lambda p: jnp.dot(a_bf16, p, preferred_element_type=F32)
    return f(hi) + f(mid) + f(lo)


def _segsum(x, ones_blk):
    hi = x.astype(BF16)
    lo = (x - hi.astype(F32)).astype(BF16)
    outs = []
    for c in range(x.shape[1] // 256):
        sl = slice(c * 256, (c + 1) * 256)
        f = lambda p: jnp.dot(p[:, sl], ones_blk, preferred_element_type=F32)
        outs.append(f(hi) + f(lo))
    return jnp.concatenate(outs, axis=1)


def _rms(x, w):
    return x * lax.rsqrt(jnp.mean(x * x, axis=-1, keepdims=True) + RMS_EPS) * w


def _sigmoid(x):
    return 1.0 / (1.0 + jnp.exp(-x))


def _softplus(x):
    return jnp.maximum(x, 0.0) + jnp.log1p(jnp.exp(-jnp.abs(x)))


def _head_norm(z, eps, ones_blk):
    mu = _segsum(z, ones_blk) * (1.0 / HEAD_DIM)
    zc = z - mu
    var = _segsum(zc * zc, ones_blk) * (1.0 / HEAD_DIM)
    return zc * lax.rsqrt(var + eps)


def _rope(z, cos, sin_signed):
    lane = lax.broadcasted_iota(jnp.int32, (1, HEADS_W), 1) % HEAD_DIM
    swapped = jnp.where(lane < HEAD_DIM // 2,
                        pltpu.roll(z, HEADS_W - HEAD_DIM // 2, axis=1),
                        pltpu.roll(z, HEAD_DIM // 2, axis=1))
    return z * cos + swapped * sin_signed


def _wkv_features(xs, w0, lora_w, a0, g_b, k_k, k_a, r_k, ones_blk, between=lambda: None):
    r = xs[:, 0:HEADS_W]
    kw = xs[:, HEADS_W:2 * HEADS_W]
    vw = xs[:, 2 * HEADS_W:3 * HEADS_W]
    lo = xs[:, 3 * HEADS_W:3 * HEADS_W + 128]
    lane = lax.broadcasted_iota(jnp.int32, (1, 128), 1)
    lo = jnp.where(lane < DECAY_LORA, jnp.tanh(lo), lo)
    ll = _dot(lo, lora_w)
    w_log = -_softplus(-(w0 + ll[:, 0:HEADS_W])) - 0.5
    log_decay = -jnp.exp(w_log)
    between()
    alr = _sigmoid(a0 + ll[:, HEADS_W:2 * HEADS_W])
    gate = _dot(_sigmoid(xs[:, 3 * HEADS_W + 128:SHIFT_W]), g_b)
    between()
    kk = kw * k_k
    kk = kk / jnp.maximum(jnp.sqrt(_segsum(kk * kk, ones_blk)), 1e-12)
    k_mod = kw * (1.0 + (alr - 1.0) * k_a)
    between()
    bonus = _segsum(r * k_mod * r_k, ones_blk) * vw
    return r, log_decay, k_mod, vw, -kk, kk * alr, gate, bonus


def _stack_masked(x2, m0):
    return jnp.concatenate([jnp.where(m0, x2, 0.0), jnp.where(m0, 0.0, x2)], axis=0)


def _stack_dup(x2):
    return jnp.concatenate([x2, x2], axis=0)


def _mixer_prompt_kernel(
        xprev_ref, xnext_ref, normw_ref, win_ref, cos_ref, sin_ref, qdec_ref, kdec_ref, dmask_ref, sdec_ref,
        retgn_ref, mu_ref, w0_ref, lora_ref, a0_ref, gb_ref, kk_ref, ka_ref, rk_ref, lnw_ref,
        lnb_ref, wout_ref, ones_ref, tril_ref, strict_ref, incl_ref, bd_ref,
        xo_ref, rets_ref, wkvs_ref, shift_ref,
        p_s, cat_s, rs_s, ws_s, carry_s, *, n_seq, chunk, group):
    i = pl.program_id(0)
    n_chunks = pl.num_programs(0) - 1
    C = chunk
    slot = i % 2

    @pl.when(i == 0)
    def _():
        rs_s[...] = jnp.zeros_like(rs_s)
        ws_s[...] = jnp.zeros_like(ws_s)
        carry_s[...] = jnp.zeros_like(carry_s)
        cat_s[...] = jnp.zeros_like(cat_s)
        p_s[0] = _dot(_rms(xprev_ref[...].reshape(n_seq * C, D_MODEL), normw_ref[...]), win_ref[...])

    m0 = lax.broadcasted_iota(jnp.int32, (1, PAIR_W), 1) < HEAD_DIM
    ones_blk = ones_ref[...]
    NB = group
    R = NB * C
    row_id = lax.broadcasted_iota(jnp.int32, (R, 1), 0)
    tile_rows = lambda ref: jnp.concatenate([ref[...]] * NB, axis=0)
    pairs = range(N_PAIRS)
    sls = [slice(pr * PAIR_W, (pr + 1) * PAIR_W) for pr in pairs]
    chains = [(s, pr) for s in range(NB) for pr in pairs]
    seq_rows = [slice(s * C, (s + 1) * C) for s in range(NB)]

    def per_group(gi, carry):
        rows = pl.ds(pl.multiple_of(gi * R, R), R)
        b0 = gi * NB
        ret_states = {(s, pr): rs_s[pr, b0 + s] for s, pr in chains}
        wkv_states = {(s, pr): ws_s[pr, b0 + s] for s, pr in chains}
        shift_rows = [carry_s[pl.ds(b0 + s, 1), :] for s in range(NB)]
        p_cur = p_s.at[slot]
        wp = p_cur[rows, RET_COLS:IN_W]
        hn_next = _rms(xnext_ref[pl.ds(b0, NB)].reshape(R, D_MODEL), normw_ref[...]).astype(BF16)
        cat_prev = cat_s[rows, :]

        def next_in_proj(c0):
            p_s[1 - slot, rows, c0:c0 + PROJ_PIECE] = jnp.dot(
                hn_next, win_ref[:, c0:c0 + PROJ_PIECE], preferred_element_type=F32)

        def prev_out_proj(c0):
            cols = slice(c0, c0 + PROJ_PIECE)
            out = jnp.dot(cat_prev, wout_ref[:, cols], preferred_element_type=F32)
            xo_ref[pl.ds(b0, NB), :, cols] = xprev_ref[pl.ds(b0, NB), :, cols] + out.reshape(NB, C, PROJ_PIECE)

        pieces = [functools.partial(next_in_proj, c0) for c0 in range(0, IN_W, PROJ_PIECE)]
        pieces += [functools.partial(prev_out_proj, c0) for c0 in range(0, D_MODEL, PROJ_PIECE)]

        def fill(n=1):
            for _ in range(min(n, len(pieces))):
                pieces.pop(0)()

        cos, sin = tile_rows(cos_ref), tile_rows(sin_ref)
        q = _rope(p_cur[rows, 0:HEADS_W], cos, sin)
        fill()
        k = _rope(p_cur[rows, HEADS_W:2 * HEADS_W], cos, sin) * (HEAD_DIM ** -0.5)
        fill()
        v = p_cur[rows, 2 * HEADS_W:3 * HEADS_W]
        g = p_cur[rows, 3 * HEADS_W:4 * HEADS_W]
        qd = q * tile_rows(qdec_ref)
        kd = k * tile_rows(kdec_ref)
        fill()
        cut = lambda z, s, pr: z[seq_rows[s], sls[pr]]
        r_sc = {c: _dot_nt(_stack_masked(cut(q, *c), m0), _stack_dup(cut(k, *c))) * dmask_ref[c[1]]
                for c in chains}
        r_inner = {c: _dot(r_sc[c], _stack_dup(cut(v, *c))) for c in chains}
        r_cross = {c: _dot(cut(qd, *c), ret_states[c]) for c in chains}
        new_ret = {c: sdec_ref[c[1]] * ret_states[c] + _dot_tn(cut(kd, *c), cut(v, *c)) * bd_ref[...]
                   for c in chains}
        o = jnp.concatenate(
            [jnp.concatenate([jnp.where(m0, r_inner[(s, pr)][0:C], r_inner[(s, pr)][C:2 * C])
                              + r_cross[(s, pr)] for pr in pairs], axis=1) for s in range(NB)], axis=0)
        fill()
        ret = _head_norm(o, GN_EPS, ones_blk) * retgn_ref[...]
        ret_out = (g * _sigmoid(g) * ret).astype(BF16)
        fill()

        prev = pltpu.roll(wp, 1, axis=0)
        for s in range(NB):
            prev = jnp.where(row_id == s * C, shift_rows[s], prev)
        xs = wp + (prev - wp) * mu_ref[...]
        fill()
        r, lw, k_mod, vw, a_vec, b_vec, gate, bonus = _wkv_features(
            xs, w0_ref[...], lora_ref[...], a0_ref[...], gb_ref[...], kk_ref[...], ka_ref[...],
            rk_ref[...], ones_blk, fill)
        cw = _dot_exact_lhs(tril_ref[...], lw)
        fill()
        cw_last = [cw[s * C + C - 1:(s + 1) * C, :] for s in range(NB)]
        cwl = jnp.concatenate([jnp.broadcast_to(z, (C, HEADS_W)) for z in cw_last], axis=0)
        r_t = r * jnp.exp(cw)
        a_t = a_vec * jnp.exp(cw - lw)
        fill()
        w_inv = jnp.exp(-cw)
        b_t = b_vec * w_inv
        k_t = k_mod * w_inv
        fill()
        w_end = jnp.exp(cwl - cw)
        b_h = b_vec * w_end
        k_h = k_mod * w_end
        fill()
        w_all = [jnp.exp(z) for z in cw_last]
        lhs = {c: jnp.concatenate([_stack_masked(cut(a_t, *c), m0), _stack_masked(cut(r_t, *c), m0)], axis=0)
               for c in chains}
        fill(len(pieces) - 2)
        sc = {c: _dot_nt(lhs[c], jnp.concatenate([_stack_dup(cut(b_t, *c)), _stack_dup(cut(k_t, *c))], axis=0))
              for c in chains}
        on_state = {c: _dot_nt(lhs[c], wkv_states[c]) for c in chains}
        vv = {c: _stack_dup(cut(vw, *c)) for c in chains}
        n_pow = {c: sc[c][0:2 * C, 0:2 * C] * strict_ref[...] for c in chains}
        u = {c: on_state[c][0:2 * C] + _dot(sc[c][0:2 * C, 2 * C:4 * C] * strict_ref[...], vv[c])
             for c in chains}
        n_steps_solve = int(math.log2(C))
        for it in range(n_steps_solve):
            u = {c: u[c] + _dot(n_pow[c], u[c]) for c in chains}
            if it + 1 < n_steps_solve:
                n_pow = {c: _dot(n_pow[c], n_pow[c]) for c in chains}
        uv = {c: jnp.concatenate([u[c], vv[c]], axis=0) for c in chains}
        y_st = {c: on_state[c][2 * C:4 * C] + _dot(
            jnp.concatenate([sc[c][2 * C:4 * C, 0:2 * C] * incl_ref[...],
                             sc[c][2 * C:4 * C, 2 * C:4 * C] * incl_ref[...]], axis=1), uv[c]) for c in chains}
        new_wkv = {c: wkv_states[c] * w_all[c[0]][:, sls[c[1]]] + bd_ref[...] * _dot_tn(
            uv[c], jnp.concatenate([_stack_masked(cut(b_h, *c), m0), _stack_masked(cut(k_h, *c), m0)], axis=0))
            for c in chains}
        y = jnp.concatenate(
            [jnp.concatenate([jnp.where(m0, y_st[(s, pr)][0:C], y_st[(s, pr)][C:2 * C]) for pr in pairs], axis=1)
             for s in range(NB)], axis=0)
        fill(len(pieces))
        yn = _head_norm(y, WKV_GN_EPS, ones_blk) * lnw_ref[...] + lnb_ref[...]
        cat_s[rows, 0:HEADS_W] = ret_out
        cat_s[rows, HEADS_W:2 * HEADS_W] = ((yn + bonus) * gate).astype(BF16)
        for s in range(NB):
            carry_s[pl.ds(b0 + s, 1), :] = wp[s * C + C - 1:(s + 1) * C, :]
        for s, pr in chains:
            rs_s[pr, b0 + s] = new_ret[(s, pr)]
            ws_s[pr, b0 + s] = new_wkv[(s, pr)]
        return carry

    @pl.when(i < n_chunks)
    def _():
        lax.fori_loop(0, n_seq // NB, per_group, 0)

    @pl.when(i == n_chunks)
    def _():
        out = jnp.dot(cat_s[...], wout_ref[...], preferred_element_type=F32)
        xo_ref[...] = xprev_ref[...] + out.reshape(n_seq, C, D_MODEL)

    @pl.when(i == n_chunks - 1)
    def _():
        shift_ref[...] = carry_s[...]
        for b in range(n_seq):
            for pr in range(N_PAIRS):
                rs = rs_s[pr, b]
                ws = ws_s[pr, b]
                rets_ref[b, 2 * pr] = rs[0:HEAD_DIM, 0:HEAD_DIM]
                rets_ref[b, 2 * pr + 1] = rs[HEAD_DIM:PAIR_W, HEAD_DIM:PAIR_W]
                wkvs_ref[b, 2 * pr] = ws[0:HEAD_DIM, 0:HEAD_DIM]
                wkvs_ref[b, 2 * pr + 1] = ws[HEAD_DIM:PAIR_W, HEAD_DIM:PAIR_W]


def _const_spec(shape):
    nd = len(shape)
    return pl.BlockSpec(shape, lambda *_: (0,) * nd, pipeline_mode=pl.Buffered(1))


def _const_out(shape):
    nd = len(shape)
    return pl.BlockSpec(shape, lambda *_: (0,) * nd)


def _const(a, dtype=F32):
    return jnp.asarray(np.asarray(a, np.float64), dtype=dtype)


def _retention_tables(chunk):
    log_g = np.log1p(-np.exp2(-5.0 - np.arange(N_HEADS, dtype=np.float64)))
    lane_g = np.repeat(log_g, HEAD_DIM)[None, :]
    idx = np.arange(chunk, dtype=np.float64)
    qdec = np.exp((idx + 1.0)[:, None] * lane_g)
    kdec = np.exp((chunk - 1.0 - idx)[:, None] * lane_g)
    rel = idx[:, None] - idx[None, :]
    dm = np.where(rel >= 0, np.exp(np.maximum(rel, 0.0)[None] * log_g[:, None, None]), 0.0)
    zero = np.zeros((chunk, chunk))
    dmask = np.stack([np.block([[dm[2 * p], zero], [zero, dm[2 * p + 1]]]) for p in range(N_PAIRS)])
    cdec = np.exp(chunk * log_g)
    hz = np.zeros((HEAD_DIM, HEAD_DIM))
    ho = np.ones((HEAD_DIM, HEAD_DIM))
    sdec = np.stack([np.block([[cdec[2 * p] * ho, hz], [hz, cdec[2 * p + 1] * ho]])
                     for p in range(N_PAIRS)])
    return _const(qdec), _const(kdec), _const(dmask), _const(sdec), cdec


def _rope_tables(pos):
    half = HEAD_DIM // 2
    inv_freq = ROPE_BASE ** (-np.arange(half, dtype=np.float64) / half)
    ang = np.asarray(pos, np.float64)[:, None] * inv_freq[None, :]
    cos = np.cos(ang)
    sin = np.sin(ang)
    cos_t = np.tile(np.concatenate([cos, cos], axis=1), (1, N_HEADS))
    sin_t = np.tile(np.concatenate([-sin, sin], axis=1), (1, N_HEADS))
    return _const(cos_t), _const(sin_t)


def _block_masks(chunk, group):
    i = np.arange(2 * chunk)
    same = (i[:, None] // chunk) == (i[None, :] // chunk)
    strict = same & (i[:, None] > i[None, :])
    incl = same & (i[:, None] >= i[None, :])
    j = np.arange(PAIR_W)
    bd = (j[:, None] // HEAD_DIM) == (j[None, :] // HEAD_DIM)
    t = np.arange(chunk)
    tril = np.kron(np.eye(group), t[:, None] >= t[None, :])
    o = np.arange(256)
    ones_blk = (o[:, None] // HEAD_DIM) == (o[None, :] // HEAD_DIM)
    return _const(strict), _const(incl), _const(bd), _const(tril, BF16), _const(ones_blk, BF16)


def _lora_block(w_b, a_b):
    z = jnp.zeros_like(w_b)
    return jnp.concatenate([jnp.concatenate([w_b, z], axis=1),
                            jnp.concatenate([z, a_b], axis=1)], axis=0).astype(BF16)


def _row(v):
    return v.reshape(1, -1).astype(F32)


def _mixer_prompt(x, lw):
    n_seq, t_len, _ = x.shape
    C = MIX_CHUNK
    n_chunks = t_len // C
    last = n_chunks - 1
    cos_t, sin_t = _rope_tables(np.arange(t_len))
    qdec, kdec, dmask, sdec, _ = _retention_tables(C)
    strict, incl, bd, tril, ones_blk = _block_masks(C, MIX_GROUP)
    prev_chunk = lambda i: (0, jnp.maximum(i - 1, 0), 0)
    in_specs = [
        pl.BlockSpec((n_seq, C, D_MODEL), prev_chunk),
        pl.BlockSpec((n_seq, C, D_MODEL), lambda i: (0, jnp.minimum(i + 1, last), 0)),
        _const_spec((1, D_MODEL)),
        _const_spec((D_MODEL, IN_W)),
        pl.BlockSpec((C, HEADS_W), lambda i: (jnp.minimum(i, last), 0)),
        pl.BlockSpec((C, HEADS_W), lambda i: (jnp.minimum(i, last), 0)),
    ]
    tail = [qdec, kdec, dmask, sdec, lw["ret_gn"], lw["mu"], lw["w0"], lw["lora"], lw["a0"], lw["g_b"],
            lw["k_k"], lw["k_a"], lw["r_k"], lw["ln_w"], lw["ln_b"], lw["w_out"], ones_blk, tril,
            strict, incl, bd]
    in_specs += [_const_spec(a.shape) for a in tail]
    out_shape = (
        jax.ShapeDtypeStruct((n_seq, t_len, D_MODEL), F32),
        jax.ShapeDtypeStruct((n_seq, N_HEADS, HEAD_DIM, HEAD_DIM), F32),
        jax.ShapeDtypeStruct((n_seq, N_HEADS, HEAD_DIM, HEAD_DIM), F32),
        jax.ShapeDtypeStruct((n_seq, SHIFT_W), F32),
    )
    out_specs = (
        pl.BlockSpec((n_seq, C, D_MODEL), prev_chunk),
        _const_out((n_seq, N_HEADS, HEAD_DIM, HEAD_DIM)),
        _const_out((n_seq, N_HEADS, HEAD_DIM, HEAD_DIM)),
        _const_out((n_seq, SHIFT_W)),
    )
    scratch = [
        pltpu.VMEM((2, n_seq * C, IN_W), F32),
        pltpu.VMEM((n_seq * C, 2 * HEADS_W), BF16),
        pltpu.VMEM((N_PAIRS, n_seq, PAIR_W, PAIR_W), F32),
        pltpu.VMEM((N_PAIRS, n_seq, PAIR_W, PAIR_W), F32),
        pltpu.VMEM((n_seq, SHIFT_W), F32),
    ]
    return pl.pallas_call(
        functools.partial(_mixer_prompt_kernel, n_seq=n_seq, chunk=C, group=MIX_GROUP),
        grid=(n_chunks + 1,),
        in_specs=in_specs, out_specs=out_specs, out_shape=out_shape, scratch_shapes=scratch,
        compiler_params=pltpu.CompilerParams(dimension_semantics=("arbitrary",),
                                             vmem_limit_bytes=VMEM_LIMIT),
        name="mixer_prompt",
    )(x, x, lw["norm"], lw["w_in"], cos_t, sin_t, *tail)


def _mixer_sample_pre_kernel(
        x_ref, shift_ref, normw_ref, win_ref, cos_ref, sin_ref, qdec_ref, mu_ref, w0_ref, lora_ref,
        a0_ref, gb_ref, kk_ref, ka_ref, rk_ref, ones_ref, feat_ref, feat_t_ref, newshift_ref):
    x = x_ref[...]
    p = _dot(_rms(x, normw_ref[...]), win_ref[...])
    q = _rope(p[:, 0:HEADS_W], cos_ref[...], sin_ref[...])
    k = _rope(p[:, HEADS_W:2 * HEADS_W], cos_ref[...], sin_ref[...]) * (HEAD_DIM ** -0.5)
    wp = p[:, RET_COLS:IN_W]
    xs = wp + (shift_ref[...] - wp) * mu_ref[...]
    r, lw, k_mod, vw, a_vec, b_vec, gate, bonus = _wkv_features(
        xs, w0_ref[...], lora_ref[...], a0_ref[...], gb_ref[...], kk_ref[...], ka_ref[...],
        rk_ref[...], ones_ref[...])
    newshift_ref[...] = wp
    state_feats = [q, q * qdec_ref[...], k, p[:, 2 * HEADS_W:3 * HEADS_W], r, jnp.exp(lw), k_mod, vw,
                   a_vec, b_vec]
    for n, f in enumerate(state_feats):
        feat_t_ref[n * HEADS_W:(n + 1) * HEADS_W, :] = f.T
    for n, f in enumerate([p[:, 3 * HEADS_W:4 * HEADS_W], gate, bonus]):
        feat_ref[:, n * HEADS_W:(n + 1) * HEADS_W] = f


_F_Q, _F_QD, _F_K, _F_V, _F_R, _F_W, _F_KM, _F_VW, _F_A, _F_B = range(10)
_N_STATE_FEATS = 10
_F_G, _F_GATE, _F_BONUS = range(3)
_N_ROW_FEATS = 3


def _mixer_sample_state_kernel(feat_t_ref, sdec_ref, ret_ref, wkv_ref, o_t_ref, reto_ref, wkvo_ref):
    h = pl.program_id(0)

    def head_rows(n):
        return feat_t_ref[pl.ds(pl.multiple_of(n * HEADS_W + h * HEAD_DIM, HEAD_DIM), HEAD_DIM), :]

    def head_row(n, i):
        return feat_t_ref[pl.ds(n * HEADS_W + h * HEAD_DIM + i, 1), :]

    a, w, b_vec, k_mod, r = (head_rows(n) for n in (_F_A, _F_W, _F_B, _F_KM, _F_R))

    def wkv_row(i, carry):
        s = wkv_ref[0, i]
        sa = jnp.sum(s * a, axis=0, keepdims=True)
        s_new = s * w + sa * b_vec + head_row(_F_VW, i) * k_mod
        wkvo_ref[0, i] = s_new
        o_t_ref[1, 0, pl.ds(i, 1), :] = jnp.sum(s_new * r, axis=0, keepdims=True)
        return carry

    lax.fori_loop(0, HEAD_DIM, wkv_row, 0, unroll=4)

    v, q, k = head_rows(_F_V), head_rows(_F_Q), head_rows(_F_K)
    g = sdec_ref[h]

    def ret_row(d, acc):
        s = ret_ref[0, d]
        reto_ref[0, d] = g * s + head_row(_F_K, d) * v
        return acc + head_row(_F_QD, d) * s

    cross = lax.fori_loop(0, HEAD_DIM, ret_row, jnp.zeros_like(v), unroll=4)
    o_t_ref[0, 0] = cross + jnp.sum(q * k, axis=0, keepdims=True) * v


def _mixer_sample_post_kernel(
        x_ref, feat_ref, o_t_ref, retgn_ref, lnw_ref, lnb_ref, wout_ref, ones_ref, xo_ref):
    ones_blk = ones_ref[...]
    g = feat_ref[:, _F_G * HEADS_W:(_F_G + 1) * HEADS_W]
    gate = feat_ref[:, _F_GATE * HEADS_W:(_F_GATE + 1) * HEADS_W]
    bonus = feat_ref[:, _F_BONUS * HEADS_W:(_F_BONUS + 1) * HEADS_W]
    o = o_t_ref[...].T
    ret = _head_norm(o[:, 0:HEADS_W], GN_EPS, ones_blk) * retgn_ref[...]
    ret_out = g * _sigmoid(g) * ret
    yn = _head_norm(o[:, HEADS_W:2 * HEADS_W], WKV_GN_EPS, ones_blk) * lnw_ref[...] + lnb_ref[...]
    wkv_out = (yn + bonus) * gate
    cat = jnp.concatenate([ret_out, wkv_out], axis=1)
    xo_ref[...] = x_ref[...] + _dot(cat, wout_ref[...])


def _mixer_sample(x, ret_t, wkv_t, shift0, lw):
    n = x.shape[0]
    cos_t, sin_t = _rope_tables(np.full((1,), PAST_LEN))
    qdec, _, _, _, cdec = _retention_tables(1)
    _, _, _, _, ones_blk = _block_masks(1, 1)
    args = [x, shift0, lw["norm"], lw["w_in"], cos_t, sin_t, qdec, lw["mu"], lw["w0"], lw["lora"],
            lw["a0"], lw["g_b"], lw["k_k"], lw["k_a"], lw["r_k"], ones_blk]
    feat, feat_t, new_shift = pl.pallas_call(
        _mixer_sample_pre_kernel,
        grid=(1,),
        in_specs=[_const_spec(a.shape) for a in args],
        out_specs=(_const_out((n, _N_ROW_FEATS * HEADS_W)), _const_out((_N_STATE_FEATS * HEADS_W, n)),
                   _const_out((n, SHIFT_W))),
        out_shape=(jax.ShapeDtypeStruct((n, _N_ROW_FEATS * HEADS_W), F32),
                   jax.ShapeDtypeStruct((_N_STATE_FEATS * HEADS_W, n), F32),
                   jax.ShapeDtypeStruct((n, SHIFT_W), F32)),
        compiler_params=pltpu.CompilerParams(vmem_limit_bytes=VMEM_LIMIT),
        name="mixer_sample_pre",
    )(*args)

    sdec = _const(np.broadcast_to(cdec[:, None, None], (N_HEADS, 1, n)))
    state_spec = pl.BlockSpec((1, HEAD_DIM, HEAD_DIM, n), lambda h: (h, 0, 0, 0))
    o_spec = pl.BlockSpec((2, 1, HEAD_DIM, n), lambda h: (0, h, 0, 0))
    o_t, ret_new, wkv_new = pl.pallas_call(
        _mixer_sample_state_kernel,
        grid=(N_HEADS,),
        in_specs=[_const_spec(feat_t.shape), _const_spec(sdec.shape), state_spec, state_spec],
        out_specs=(o_spec, state_spec, state_spec),
        out_shape=(jax.ShapeDtypeStruct((2, N_HEADS, HEAD_DIM, n), F32),
                   jax.ShapeDtypeStruct(ret_t.shape, F32), jax.ShapeDtypeStruct(wkv_t.shape, F32)),
        compiler_params=pltpu.CompilerParams(dimension_semantics=("arbitrary",),
                                             vmem_limit_bytes=VMEM_LIMIT),
        name="mixer_sample_state",
    )(feat_t, sdec, ret_t, wkv_t)

    args = [x, feat, o_t.reshape(2 * HEADS_W, n), lw["ret_gn"], lw["ln_w"], lw["ln_b"], lw["w_out"], ones_blk]
    x1 = pl.pallas_call(
        _mixer_sample_post_kernel,
        grid=(1,),
        in_specs=[_const_spec(a.shape) for a in args],
        out_specs=_const_out((n, D_MODEL)),
        out_shape=jax.ShapeDtypeStruct((n, D_MODEL), F32),
        compiler_params=pltpu.CompilerParams(vmem_limit_bytes=VMEM_LIMIT),
        name="mixer_sample_post",
    )(*args)
    return x1, ret_new, wkv_new, new_shift


def _mlp_kernel(x_ref, normw_ref, wup_ref, wdown_ref, normf_ref, o_ref, *, final_norm):
    x = x_ref[...]
    hn = _rms(x, normw_ref[...]).astype(BF16)
    acc = x
    for c in range(D_FF // FF_CHUNK):
        sl = slice(c * FF_CHUNK, (c + 1) * FF_CHUNK)
        hid = jnp.dot(hn, wup_ref[:, sl], preferred_element_type=F32)
        hid = jnp.square(jnp.maximum(hid, 0.0)).astype(BF16)
        acc = acc + jnp.dot(hid, wdown_ref[sl, :], preferred_element_type=F32)
    if final_norm:
        acc = _rms(acc, normf_ref[...])
    o_ref[...] = acc


def _mlp(x, norm_w, w_up, w_down, norm_f, final_norm):
    rows = x.shape[0]
    tm = min(MLP_ROWS, rows)
    return pl.pallas_call(
        functools.partial(_mlp_kernel, final_norm=final_norm),
        grid=(rows // tm,),
        in_specs=[pl.BlockSpec((tm, D_MODEL), lambda i: (i, 0)),
                  _const_spec((1, D_MODEL)), _const_spec((D_MODEL, D_FF)),
                  _const_spec((D_FF, D_MODEL)), _const_spec((1, D_MODEL))],
        out_specs=pl.BlockSpec((tm, D_MODEL), lambda i: (i, 0)),
        out_shape=jax.ShapeDtypeStruct((rows, D_MODEL), F32),
        compiler_params=pltpu.CompilerParams(dimension_semantics=("arbitrary",),
                                             vmem_limit_bytes=VMEM_LIMIT),
        name="mlp",
    )(x, norm_w, w_up, w_down, norm_f)


def _ssm_prep_kernel(lre_ref, lim_ref, logdt_ref, bre_ref, bim_ref, are_ref, aim_ref, bbre_ref, bbim_ref):
    lre = jnp.minimum(lre_ref[...], -1e-4)
    lim = lim_ref[...]
    dt = jnp.exp(logdt_ref[...])
    mag = jnp.exp(lre * dt)
    are = mag * jnp.cos(lim * dt)
    aim = mag * jnp.sin(lim * dt)
    are_ref[...] = are
    aim_ref[...] = aim
    den = lre * lre + lim * lim
    nre = are - 1.0
    cre = (nre * lre + aim * lim) / den
    cim = (aim * lre - nre * lim) / den
    bre = bre_ref[...]
    bim = bim_ref[...]
    bbre_ref[...] = cre * bre - cim * bim
    bbim_ref[...] = cre * bim + cim * bre


def _gelu_exact(x):
    return 0.5 * x * (1.0 + lax.erf(x * (2.0 ** -0.5)))


def _ssm_kernel(x_ref, hre0_ref, him0_ref, normw_ref, are_ref, aim_ref, wb_ref, cre_ref, cim_ref, dskip_ref,
                wglu_ref, xo_ref, hre_ref, him_ref, xt_s, u_s, bu_s, y_s, hg_s, hre_s, him_s,
                *, n_seq, chunk, batch_major):
    i = pl.program_id(0)
    n_steps = pl.num_programs(0)
    rows_all = chunk * n_seq
    n_slabs = D_MODEL // 128

    @pl.when(i == 0)
    def _():
        hre_s[...] = hre0_ref[...]
        him_s[...] = him0_ref[...]

    if batch_major:
        for b in range(n_seq):
            for sl in range(n_slabs):
                xt_s[sl, pl.ds(b, chunk, stride=n_seq), :] = x_ref[b, :, sl * 128:(sl + 1) * 128]
        ssq = sum(jnp.sum(jnp.square(xt_s[sl]), axis=-1, keepdims=True) for sl in range(n_slabs))
        inv = lax.rsqrt(ssq * (1.0 / D_MODEL) + RMS_EPS)
        for sl in range(n_slabs):
            cols = slice(sl * 128, (sl + 1) * 128)
            u_s[:, cols] = xt_s[sl] * inv * normw_ref[:, cols]
    else:
        u_s[...] = _rms(x_ref[...].reshape(rows_all, D_MODEL), normw_ref[...])

    def input_proj(blk, part):
        c = slice(part * SSM_HALF, (part + 1) * SSM_HALF)
        bu_s[blk % 2, :, c] = _dot(u_s[:, blk * 128:(blk + 1) * 128], wb_ref[blk, :, c])

    def output_proj(blk, part):
        buf = bu_s.at[blk % 2]
        cols = slice(blk * 128, (blk + 1) * 128)
        if part == 0:
            y_s[:, cols] = _dot(buf[:, 0:SSM_HALF], cre_ref[blk])
        else:
            y_s[:, cols] = y_s[:, cols] - _dot(buf[:, SSM_HALF:2 * SSM_HALF], cim_ref[blk])

    input_proj(0, 0)
    input_proj(0, 1)
    for blk in range(SSM_BLOCKS):
        neighbours = []
        if blk >= 1:
            neighbours += [functools.partial(output_proj, blk - 1, 0), functools.partial(output_proj, blk - 1, 1)]
        if blk + 1 < SSM_BLOCKS:
            neighbours += [functools.partial(input_proj, blk + 1, 0), functools.partial(input_proj, blk + 1, 1)]
        buf = bu_s.at[blk % 2]
        cols = slice(blk * SSM_HALF, (blk + 1) * SSM_HALF)
        a_re = jnp.broadcast_to(are_ref[:, cols], (n_seq, SSM_HALF))
        a_im = jnp.broadcast_to(aim_ref[:, cols], (n_seq, SSM_HALF))
        h_re = hre_s[:, cols]
        h_im = him_s[:, cols]
        seg = -(-chunk // max(len(neighbours), 1))
        for t in range(chunk):
            if t % seg == 0 and neighbours:
                neighbours.pop(0)()
            rows = slice(t * n_seq, (t + 1) * n_seq)
            n_re = a_re * h_re - a_im * h_im + buf[rows, 0:SSM_HALF]
            n_im = a_re * h_im + a_im * h_re + buf[rows, SSM_HALF:2 * SSM_HALF]
            buf[rows, 0:SSM_HALF] = n_re
            buf[rows, SSM_HALF:2 * SSM_HALF] = n_im
            h_re, h_im = n_re, n_im
        for f in neighbours:
            f()
        hre_s[:, cols] = h_re
        him_s[:, cols] = h_im
    output_proj(SSM_BLOCKS - 1, 0)
    output_proj(SSM_BLOCKS - 1, 1)
    hg_s[...] = _gelu_exact(y_s[...] + dskip_ref[...] * u_s[...]).astype(BF16)
    glu_w = 256
    for c in range(D_MODEL // glu_w):
        cols = slice(c * glu_w, (c + 1) * glu_w)
        hg = hg_s[...]
        val = jnp.dot(hg, wglu_ref[:, cols], preferred_element_type=F32)
        gate = jnp.dot(hg, wglu_ref[:, D_MODEL + c * glu_w:D_MODEL + (c + 1) * glu_w],
                       preferred_element_type=F32)
        out = val * _sigmoid(gate)
        if batch_major:
            for sl in range(c * glu_w // 128, (c + 1) * glu_w // 128):
                xt_s[sl] = xt_s[sl] + out[:, sl * 128 - c * glu_w:(sl + 1) * 128 - c * glu_w]
        else:
            y_s[:, cols] = x_ref[...].reshape(rows_all, D_MODEL)[:, cols] + out
    if batch_major:
        for b in range(n_seq):
            for sl in range(n_slabs):
                xo_ref[b, :, sl * 128:(sl + 1) * 128] = xt_s[sl, pl.ds(b, chunk, stride=n_seq), :]
    else:
        xo_ref[...] = y_s[...].reshape(chunk, n_seq, D_MODEL)

    @pl.when(i == n_steps - 1)
    def _():
        hre_ref[...] = hre_s[...]
        him_ref[...] = him_s[...]


def _ssm_weights(lam_re, lam_im, log_dt, b_re, b_im, c_re, c_im):
    g, p = SSM_GROUPS, SSM_P
    n = g * SSM_GROUP
    rep = lambda z: jnp.repeat(z, SSM_GROUP, axis=0)
    bt_re = jnp.swapaxes(b_re, 1, 2).reshape(n, p)
    bt_im = jnp.swapaxes(b_im, 1, 2).reshape(n, p)
    args = [rep(lam_re), rep(lam_im), rep(log_dt.reshape(g, 1)), bt_re, bt_im]
    a_re, a_im, bb_re, bb_im = pl.pallas_call(
        _ssm_prep_kernel,
        grid=(1,),
        in_specs=[_const_spec(a.shape) for a in args],
        out_specs=tuple(_const_out((n, p)) for _ in range(4)),
        out_shape=tuple(jax.ShapeDtypeStruct((n, p), F32) for _ in range(4)),
        name="ssm_prep",
    )(*args)
    a_re = a_re[::SSM_GROUP]
    a_im = a_im[::SSM_GROUP]
    eye = jnp.eye(SSM_BLOCK_G, dtype=F32)

    def in_block(bb):
        bb = bb.reshape(SSM_BLOCKS, SSM_BLOCK_G, SSM_GROUP, p)
        return jnp.einsum("bgcp,gh->bgchp", bb, eye).reshape(SSM_BLOCKS, 128, SSM_HALF)

    def out_block(cc):
        cc = cc.reshape(SSM_BLOCKS, SSM_BLOCK_G, SSM_GROUP, p)
        return jnp.einsum("bgcp,gh->bgphc", cc, eye).reshape(SSM_BLOCKS, SSM_HALF, 128)

    w_b = jnp.concatenate([in_block(bb_re), in_block(bb_im)], axis=2).astype(BF16)
    n_state = SSM_GROUPS * SSM_P
    return (a_re.reshape(1, n_state), a_im.reshape(1, n_state), w_b,
            out_block(c_re).astype(BF16), out_block(c_im).astype(BF16))


def _ssm_layer(x, h_re0, h_im0, sw, chunk, batch_major):
    if batch_major:
        n_seq, t_len, _ = x.shape
        x_block = (n_seq, chunk, D_MODEL)
        x_map = lambda i: (0, i, 0)
    else:
        t_len, n_seq, _ = x.shape
        x_block = (chunk, n_seq, D_MODEL)
        x_map = lambda i: (i, 0, 0)
    rows = chunk * n_seq
    n_state = SSM_GROUPS * SSM_P
    args = [x, h_re0, h_im0, sw["norm"], sw["a_re"], sw["a_im"], sw["w_b"], sw["c_re"], sw["c_im"],
            sw["d_skip"], sw["w_glu"]]
    in_specs = [pl.BlockSpec(x_block, x_map)] + [_const_spec(a.shape) for a in args[1:]]
    state = jax.ShapeDtypeStruct((n_seq, n_state), F32)
    return pl.pallas_call(
        functools.partial(_ssm_kernel, n_seq=n_seq, chunk=chunk, batch_major=batch_major),
        grid=(t_len // chunk,),
        in_specs=in_specs,
        out_specs=(pl.BlockSpec(x_block, x_map), _const_out(state.shape), _const_out(state.shape)),
        out_shape=(jax.ShapeDtypeStruct(x.shape, F32), state, state),
        scratch_shapes=[pltpu.VMEM((D_MODEL // 128, rows, 128), F32),
                        pltpu.VMEM((rows, D_MODEL), F32), pltpu.VMEM((2, rows, 2 * SSM_HALF), F32),
                        pltpu.VMEM((rows, D_MODEL), F32), pltpu.VMEM((rows, D_MODEL), BF16),
                        pltpu.VMEM((n_seq, n_state), F32), pltpu.VMEM((n_seq, n_state), F32)],
        compiler_params=pltpu.CompilerParams(dimension_semantics=("arbitrary",),
                                             vmem_limit_bytes=VMEM_LIMIT),
        name="ssm_layer",
    )(*args)


def kernel(x_prompt, x_sample, state_ret, state_wkv, state_shift, state_ssm_re, state_ssm_im, norm_mix, w_in, ret_gn, mu_shift, wkv_w0, wkv_wB, wkv_a0, wkv_aB, wkv_gB, wkv_kk, wkv_ka, wkv_rk, wkv_ln_w, wkv_ln_b, w_out, ssm_lambda_re, ssm_lambda_im, ssm_log_dt, ssm_B_re, ssm_B_im, ssm_C_re, ssm_C_im, ssm_D, ssm_w_glu, mlp_norm, mlp_up, mlp_down, norm_f):
    lw = dict(
        norm=_row(norm_mix[0]), w_in=w_in[0].astype(BF16), ret_gn=_row(ret_gn[0]), mu=_row(mu_shift[0]),
        w0=_row(wkv_w0[0]), lora=_lora_block(wkv_wB[0], wkv_aB[0]), a0=_row(wkv_a0[0]),
        g_b=wkv_gB[0].astype(BF16), k_k=_row(wkv_kk[0]), k_a=_row(wkv_ka[0]), r_k=_row(wkv_rk[0]),
        ln_w=_row(wkv_ln_w[0]), ln_b=_row(wkv_ln_b[0]), w_out=w_out[0].astype(BF16))
    a_re, a_im, w_b, c_re, c_im = _ssm_weights(ssm_lambda_re[0], ssm_lambda_im[0], ssm_log_dt[0],
                                               ssm_B_re[0], ssm_B_im[0], ssm_C_re[0], ssm_C_im[0])
    sw = dict(norm=_row(norm_mix[1]), a_re=a_re, a_im=a_im, w_b=w_b, c_re=c_re, c_im=c_im,
              d_skip=_row(ssm_D[0]), w_glu=ssm_w_glu[0].astype(BF16))
    n_state = SSM_GROUPS * SSM_P
    up = [mlp_up[l].astype(BF16) for l in range(2)]
    down = [mlp_down[l].astype(BF16) for l in range(2)]
    nf = _row(norm_f)

    n_p, t_p, _ = x_prompt.shape
    x1, ret_p, wkv_p, shift_p = _mixer_prompt(x_prompt, lw)
    x1 = _mlp(x1.reshape(n_p * t_p, D_MODEL), _row(mlp_norm[0]), up[0], down[0], nf, False)
    zero_state = jnp.zeros((n_p, n_state), F32)
    x2, ssm_re_p, ssm_im_p = _ssm_layer(x1.reshape(n_p, t_p, D_MODEL), zero_state, zero_state, sw,
                                        SSM_CHUNK, True)
    y_p = _mlp(x2.reshape(n_p * t_p, D_MODEL), _row(mlp_norm[1]), up[1], down[1], nf, True)
    y_prompt = y_p.reshape(n_p, t_p, D_MODEL)
    ssm_re_p = ssm_re_p.reshape(n_p, SSM_GROUPS, SSM_P)
    ssm_im_p = ssm_im_p.reshape(n_p, SSM_GROUPS, SSM_P)

    n_s = x_sample.shape[0]
    xs = x_sample.reshape(n_s, D_MODEL)
    seq_last = lambda s: jnp.transpose(s, (1, 2, 3, 0))
    seq_first = lambda s: jnp.transpose(s, (3, 0, 1, 2))
    xs1, ret_s, wkv_s, shift_s = _mixer_sample(xs, seq_last(state_ret[0]), seq_last(state_wkv[0]),
                                               state_shift[0], lw)
    ret_s, wkv_s = seq_first(ret_s), seq_first(wkv_s)
    xs1 = _mlp(xs1, _row(mlp_norm[0]), up[0], down[0], nf, False)
    xs2, ssm_re_s, ssm_im_s = _ssm_layer(
        xs1.reshape(1, n_s, D_MODEL), state_ssm_re[0].reshape(n_s, n_state),
        state_ssm_im[0].reshape(n_s, n_state), sw, 1, False)
    y_s = _mlp(xs2.reshape(n_s, D_MODEL), _row(mlp_norm[1]), up[1], down[1], nf, True)
    ssm_re_s = ssm_re_s.reshape(n_s, SSM_GROUPS, SSM_P)
    ssm_im_s = ssm_im_s.reshape(n_s, SSM_GROUPS, SSM_P)

    return (y_prompt, y_s.reshape(n_s, 1, D_MODEL),
            ret_p[None], wkv_p[None], shift_p[None], ssm_re_p[None], ssm_im_p[None],
            ret_s[None], wkv_s[None], shift_s[None], ssm_re_s[None], ssm_im_s[None])
```

```python
import functools
import math

import numpy as np
import jax
import jax.numpy as jnp
from jax import lax
from jax.experimental import pallas as pl
from jax.experimental.pallas import tpu as pltpu

F32 = jnp.float32
BF16 = jnp.bfloat16

D_MODEL = 1024
N_HEADS = 8
HEAD_DIM = 64
HEADS_W = N_HEADS * HEAD_DIM
N_PAIRS = N_HEADS // 2
PAIR_W = 2 * HEAD_DIM
ROPE_BASE = 10000.0
DECAY_LORA = 64
AAA_LORA = 64
GATE_LORA = 128
SHIFT_W = 3 * HEADS_W + DECAY_LORA + AAA_LORA + GATE_LORA
RET_COLS = 4 * HEADS_W
IN_W = RET_COLS + SHIFT_W
SSM_GROUP = 16
SSM_GROUPS = D_MODEL // SSM_GROUP
SSM_P = 64
SSM_BLOCKS = 8
SSM_BLOCK_G = SSM_GROUPS // SSM_BLOCKS
SSM_HALF = SSM_BLOCK_G * SSM_P
SSM_STATE_W = 2 * SSM_BLOCKS * SSM_HALF
D_FF = 4 * D_MODEL
RMS_EPS = 1e-6
GN_EPS = 1e-5
WKV_GN_EPS = 64e-5
PAST_LEN = 16384

MIX_CHUNK = 64
PROJ_PIECE = 256
MIX_GROUP = 4
SSM_CHUNK = 64
MLP_ROWS = 512
FF_CHUNK = 1024

VMEM_LIMIT = 58 * 1024 * 1024


def _dot(a, b):
    return jnp.dot(a.astype(BF16), b.astype(BF16), preferred_element_type=F32)


def _dot_nt(a, b):
    return lax.dot_general(a.astype(BF16), b.astype(BF16), (((1,), (1,)), ((), ())),
                           preferred_element_type=F32)


def _dot_tn(a, b):
    return lax.dot_general(a.astype(BF16), b.astype(BF16), (((0,), (0,)), ((), ())),
                           preferred_element_type=F32)


def _split3(x):
    hi = x.astype(BF16)
    r1 = x - hi.astype(F32)
    mid = r1.astype(BF16)
    lo = (r1 - mid.astype(F32)).astype(BF16)
    return hi, mid, lo


def _dot_exact_lhs(a_bf16, x):
    hi, mid, lo = _split3(x)
    f = lambda p: jnp.dot(a_bf16, p, preferred_element_type=F32)
    return f(hi) + f(mid) + f(lo)


def _segsum(x, ones_blk):
    hi = x.astype(BF16)
    lo = (x - hi.astype(F32)).astype(BF16)
    outs = []
    for c in range(x.shape[1] // 256):
        sl = slice(c * 256, (c + 1) * 256)
        f = lambda p: jnp.dot(p[:, sl], ones_blk, preferred_element_type=F32)
        outs.append(f(hi) + f(lo))
    return jnp.concatenate(outs, axis=1)


def _rms(x, w):
    return x * lax.rsqrt(jnp.mean(x * x, axis=-1, keepdims=True) + RMS_EPS) * w


def _sigmoid(x):
    return 1.0 / (1.0 + jnp.exp(-x))


def _softplus(x):
    return jnp.maximum(x, 0.0) + jnp.log1p(jnp.exp(-jnp.abs(x)))


def _head_norm(z, eps, ones_blk):
    mu = _segsum(z, ones_blk) * (1.0 / HEAD_DIM)
    zc = z - mu
    var = _segsum(zc * zc, ones_blk) * (1.0 / HEAD_DIM)
    return zc * lax.rsqrt(var + eps)


def _rope(z, cos, sin_signed):
    lane = lax.broadcasted_iota(jnp.int32, (1, HEADS_W), 1) % HEAD_DIM
    swapped = jnp.where(lane < HEAD_DIM // 2,
                        pltpu.roll(z, HEADS_W - HEAD_DIM // 2, axis=1),
                        pltpu.roll(z, HEAD_DIM // 2, axis=1))
    return z * cos + swapped * sin_signed


def _wkv_features(xs, w0, lora_w, a0, g_b, k_k, k_a, r_k, ones_blk, between=lambda: None):
    r = xs[:, 0:HEADS_W]
    kw = xs[:, HEADS_W:2 * HEADS_W]
    vw = xs[:, 2 * HEADS_W:3 * HEADS_W]
    lo = xs[:, 3 * HEADS_W:3 * HEADS_W + 128]
    lane = lax.broadcasted_iota(jnp.int32, (1, 128), 1)
    lo = jnp.where(lane < DECAY_LORA, jnp.tanh(lo), lo)
    ll = _dot(lo, lora_w)
    w_log = -_softplus(-(w0 + ll[:, 0:HEADS_W])) - 0.5
    log_decay = -jnp.exp(w_log)
    between()
    alr = _sigmoid(a0 + ll[:, HEADS_W:2 * HEADS_W])
    gate = _dot(_sigmoid(xs[:, 3 * HEADS_W + 128:SHIFT_W]), g_b)
    between()
    kk = kw * k_k
    kk = kk / jnp.maximum(jnp.sqrt(_segsum(kk * kk, ones_blk)), 1e-12)
    k_mod = kw * (1.0 + (alr - 1.0) * k_a)
    between()
    bonus = _segsum(r * k_mod * r_k, ones_blk) * vw
    return r, log_decay, k_mod, vw, -kk, kk * alr, gate, bonus


def _stack_masked(x2, m0):
    return jnp.concatenate([jnp.where(m0, x2, 0.0), jnp.where(m0, 0.0, x2)], axis=0)


def _stack_dup(x2):
    return jnp.concatenate([x2, x2], axis=0)


def _mixer_prompt_kernel(
        xprev_ref, xnext_ref, normw_ref, win_ref, cos_ref, sin_ref, qdec_ref, kdec_ref, dmask_ref, sdec_ref,
        retgn_ref, mu_ref, w0_ref, lora_ref, a0_ref, gb_ref, kk_ref, ka_ref, rk_ref, lnw_ref,
        lnb_ref, wout_ref, ones_ref, tril_ref, strict_ref, incl_ref, bd_ref,
        xo_ref, rets_ref, wkvs_ref, shift_ref,
        p_s, cat_s, rs_s, ws_s, carry_s, *, n_seq, chunk, group):
    i = pl.program_id(0)
    n_chunks = pl.num_programs(0) - 1
    C = chunk
    slot = i % 2

    @pl.when(i == 0)
    def _():
        rs_s[...] = jnp.zeros_like(rs_s)
        ws_s[...] = jnp.zeros_like(ws_s)
        carry_s[...] = jnp.zeros_like(carry_s)
        cat_s[...] = jnp.zeros_like(cat_s)
        p_s[0] = _dot(_rms(xprev_ref[...].reshape(n_seq * C, D_MODEL), normw_ref[...]), win_ref[...])

    m0 = lax.broadcasted_iota(jnp.int32, (1, PAIR_W), 1) < HEAD_DIM
    ones_blk = ones_ref[...]
    NB = group
    R = NB * C
    row_id = lax.broadcasted_iota(jnp.int32, (R, 1), 0)
    tile_rows = lambda ref: jnp.concatenate([ref[...]] * NB, axis=0)
    pairs = range(N_PAIRS)
    sls = [slice(pr * PAIR_W, (pr + 1) * PAIR_W) for pr in pairs]
    chains = [(s, pr) for s in range(NB) for pr in pairs]
    seq_rows = [slice(s * C, (s + 1) * C) for s in range(NB)]

    def per_group(gi, carry):
        rows = pl.ds(pl.multiple_of(gi * R, R), R)
        b0 = gi * NB
        ret_states = {(s, pr): rs_s[pr, b0 + s] for s, pr in chains}
        wkv_states = {(s, pr): ws_s[pr, b0 + s] for s, pr in chains}
        shift_rows = [carry_s[pl.ds(b0 + s, 1), :] for s in range(NB)]
        p_cur = p_s.at[slot]
        wp = p_cur[rows, RET_COLS:IN_W]
        hn_next = _rms(xnext_ref[pl.ds(b0, NB)].reshape(R, D_MODEL), normw_ref[...]).astype(BF16)
        cat_prev = cat_s[rows, :]

        def next_in_proj(c0):
            p_s[1 - slot, rows, c0:c0 + PROJ_PIECE] = jnp.dot(
                hn_next, win_ref[:, c0:c0 + PROJ_PIECE], preferred_element_type=F32)

        def prev_out_proj(c0):
            cols = slice(c0, c0 + PROJ_PIECE)
            out = jnp.dot(cat_prev, wout_ref[:, cols], preferred_element_type=F32)
            xo_ref[pl.ds(b0, NB), :, cols] = xprev_ref[pl.ds(b0, NB), :, cols] + out.reshape(NB, C, PROJ_PIECE)

        pieces = [functools.partial(next_in_proj, c0) for c0 in range(0, IN_W, PROJ_PIECE)]
        pieces += [functools.partial(prev_out_proj, c0) for c0 in range(0, D_MODEL, PROJ_PIECE)]

        def fill(n=1):
            for _ in range(min(n, len(pieces))):
                pieces.pop(0)()

        cos, sin = tile_rows(cos_ref), tile_rows(sin_ref)
        q = _rope(p_cur[rows, 0:HEADS_W], cos, sin)
        fill()
        k = _rope(p_cur[rows, HEADS_W:2 * HEADS_W], cos, sin) * (HEAD_DIM ** -0.5)
        fill()
        v = p_cur[rows, 2 * HEADS_W:3 * HEADS_W]
        g = p_cur[rows, 3 * HEADS_W:4 * HEADS_W]
        qd = q * tile_rows(qdec_ref)
        kd = k * tile_rows(kdec_ref)
        fill()
        cut = lambda z, s, pr: z[seq_rows[s], sls[pr]]
        r_sc = {c: _dot_nt(_stack_masked(cut(q, *c), m0), _stack_dup(cut(k, *c))) * dmask_ref[c[1]]
                for c in chains}
        r_inner = {c: _dot(r_sc[c], _stack_dup(cut(v, *c))) for c in chains}
        r_cross = {c: _dot(cut(qd, *c), ret_states[c]) for c in chains}
        new_ret = {c: sdec_ref[c[1]] * ret_states[c] + _dot_tn(cut(kd, *c), cut(v, *c)) * bd_ref[...]
                   for c in chains}
        o = jnp.concatenate(
            [jnp.concatenate([jnp.where(m0, r_inner[(s, pr)][0:C], r_inner[(s, pr)][C:2 * C])
                              + r_cross[(s, pr)] for pr in pairs], axis=1) for s in range(NB)], axis=0)
        fill()
        ret = _head_norm(o, GN_EPS, ones_blk) * retgn_ref[...]
        ret_out = (g * _sigmoid(g) * ret).astype(BF16)
        fill()

        prev = pltpu.roll(wp, 1, axis=0)
        for s in range(NB):
            prev = jnp.where(row_id == s * C, shift_rows[s], prev)
        xs = wp + (prev - wp) * mu_ref[...]
        fill()
        r, lw, k_mod, vw, a_vec, b_vec, gate, bonus = _wkv_features(
            xs, w0_ref[...], lora_ref[...], a0_ref[...], gb_ref[...], kk_ref[...], ka_ref[...],
            rk_ref[...], ones_blk, fill)
        cw = _dot_exact_lhs(tril_ref[...], lw)
        fill()
        cw_last = [cw[s * C + C - 1:(s + 1) * C, :] for s in range(NB)]
        cwl = jnp.concatenate([jnp.broadcast_to(z, (C, HEADS_W)) for z in cw_last], axis=0)
        r_t = r * jnp.exp(cw)
        a_t = a_vec * jnp.exp(cw - lw)
        fill()
        w_inv = jnp.exp(-cw)
        b_t = b_vec * w_inv
        k_t = k_mod * w_inv
        fill()
        w_end = jnp.exp(cwl - cw)
        b_h = b_vec * w_end
        k_h = k_mod * w_end
        fill()
        w_all = [jnp.exp(z) for z in cw_last]
        lhs = {c: jnp.concatenate([_stack_masked(cut(a_t, *c), m0), _stack_masked(cut(r_t, *c), m0)], axis=0)
               for c in chains}
        fill(len(pieces) - 2)
        sc = {c: _dot_nt(lhs[c], jnp.concatenate([_stack_dup(cut(b_t, *c)), _stack_dup(cut(k_t, *c))], axis=0))
              for c in chains}
        on_state = {c: _dot_nt(lhs[c], wkv_states[c]) for c in chains}
        vv = {c: _stack_dup(cut(vw, *c)) for c in chains}
        n_pow = {c: sc[c][0:2 * C, 0:2 * C] * strict_ref[...] for c in chains}
        u = {c: on_state[c][0:2 * C] + _dot(sc[c][0:2 * C, 2 * C:4 * C] * strict_ref[...], vv[c])
             for c in chains}
        n_steps_solve = int(math.log2(C))
        for it in range(n_steps_solve):
            u = {c: u[c] + _dot(n_pow[c], u[c]) for c in chains}
            if it + 1 < n_steps_solve:
                n_pow = {c: _dot(n_pow[c], n_pow[c]) for c in chains}
        uv = {c: jnp.concatenate([u[c], vv[c]], axis=0) for c in chains}
        y_st = {c: on_state[c][2 * C:4 * C] + _dot(
            jnp.concatenate([sc[c][2 * C:4 * C, 0:2 * C] * incl_ref[...],
                             sc[c][2 * C:4 * C, 2 * C:4 * C] * incl_ref[...]], axis=1), uv[c]) for c in chains}
        new_wkv = {c: wkv_states[c] * w_all[c[0]][:, sls[c[1]]] + bd_ref[...] * _dot_tn(
            uv[c], jnp.concatenate([_stack_masked(cut(b_h, *c), m0), _stack_masked(cut(k_h, *c), m0)], axis=0))
            for c in chains}
        y = jnp.concatenate(
            [jnp.concatenate([jnp.where(m0, y_st[(s, pr)][0:C], y_st[(s, pr)][C:2 * C]) for pr in pairs], axis=1)
             for s in range(NB)], axis=0)
        fill(len(pieces))
        yn = _head_norm(y, WKV_GN_EPS, ones_blk) * lnw_ref[...] + lnb_ref[...]
        cat_s[rows, 0:HEADS_W] = ret_out
        cat_s[rows, HEADS_W:2 * HEADS_W] = ((yn + bonus) * gate).astype(BF16)
        for s in range(NB):
            carry_s[pl.ds(b0 + s, 1), :] = wp[s * C + C - 1:(s + 1) * C, :]
        for s, pr in chains:
            rs_s[pr, b0 + s] = new_ret[(s, pr)]
            ws_s[pr, b0 + s] = new_wkv[(s, pr)]
        return carry

    @pl.when(i < n_chunks)
    def _():
        lax.fori_loop(0, n_seq // NB, per_group, 0)

    @pl.when(i == n_chunks)
    def _():
        out = jnp.dot(cat_s[...], wout_ref[...], preferred_element_type=F32)
        xo_ref[...] = xprev_ref[...] + out.reshape(n_seq, C, D_MODEL)

    @pl.when(i == n_chunks - 1)
    def _():
        shift_ref[...] = carry_s[...]
        for b in range(n_seq):
            for pr in range(N_PAIRS):
                rs = rs_s[pr, b]
                ws = ws_s[pr, b]
                rets_ref[b, 2 * pr] = rs[0:HEAD_DIM, 0:HEAD_DIM]
                rets_ref[b, 2 * pr + 1] = rs[HEAD_DIM:PAIR_W, HEAD_DIM:PAIR_W]
                wkvs_ref[b, 2 * pr] = ws[0:HEAD_DIM, 0:HEAD_DIM]
                wkvs_ref[b, 2 * pr + 1] = ws[HEAD_DIM:PAIR_W, HEAD_DIM:PAIR_W]


def _const_spec(shape):
    nd = len(shape)
    return pl.BlockSpec(shape, lambda *_: (0,) * nd, pipeline_mode=pl.Buffered(1))


def _const_out(shape):
    nd = len(shape)
    return pl.BlockSpec(shape, lambda *_: (0,) * nd)


def _const(a, dtype=F32):
    return jnp.asarray(np.asarray(a, np.float64), dtype=dtype)


def _retention_tables(chunk):
    log_g = np.log1p(-np.exp2(-5.0 - np.arange(N_HEADS, dtype=np.float64)))
    lane_g = np.repeat(log_g, HEAD_DIM)[None, :]
    idx = np.arange(chunk, dtype=np.float64)
    qdec = np.exp((idx + 1.0)[:, None] * lane_g)
    kdec = np.exp((chunk - 1.0 - idx)[:, None] * lane_g)
    rel = idx[:, None] - idx[None, :]
    dm = np.where(rel >= 0, np.exp(np.maximum(rel, 0.0)[None] * log_g[:, None, None]), 0.0)
    zero = np.zeros((chunk, chunk))
    dmask = np.stack([np.block([[dm[2 * p], zero], [zero, dm[2 * p + 1]]]) for p in range(N_PAIRS)])
    cdec = np.exp(chunk * log_g)
    hz = np.zeros((HEAD_DIM, HEAD_DIM))
    ho = np.ones((HEAD_DIM, HEAD_DIM))
    sdec = np.stack([np.block([[cdec[2 * p] * ho, hz], [hz, cdec[2 * p + 1] * ho]])
                     for p in range(N_PAIRS)])
    return _const(qdec), _const(kdec), _const(dmask), _const(sdec), cdec


def _rope_tables(pos):
    half = HEAD_DIM // 2
    inv_freq = ROPE_BASE ** (-np.arange(half, dtype=np.float64) / half)
    ang = np.asarray(pos, np.float64)[:, None] * inv_freq[None, :]
    cos = np.cos(ang)
    sin = np.sin(ang)
    cos_t = np.tile(np.concatenate([cos, cos], axis=1), (1, N_HEADS))
    sin_t = np.tile(np.concatenate([-sin, sin], axis=1), (1, N_HEADS))
    return _const(cos_t), _const(sin_t)


def _block_masks(chunk, group):
    i = np.arange(2 * chunk)
    same = (i[:, None] // chunk) == (i[None, :] // chunk)
    strict = same & (i[:, None] > i[None, :])
    incl = same & (i[:, None] >= i[None, :])
    j = np.arange(PAIR_W)
    bd = (j[:, None] // HEAD_DIM) == (j[None, :] // HEAD_DIM)
    t = np.arange(chunk)
    tril = np.kron(np.eye(group), t[:, None] >= t[None, :])
    o = np.arange(256)
    ones_blk = (o[:, None] // HEAD_DIM) == (o[None, :] // HEAD_DIM)
    return _const(strict), _const(incl), _const(bd), _const(tril, BF16), _const(ones_blk, BF16)


def _lora_block(w_b, a_b):
    z = jnp.zeros_like(w_b)
    return jnp.concatenate([jnp.concatenate([w_b, z], axis=1),
                            jnp.concatenate([z, a_b], axis=1)], axis=0).astype(BF16)


def _row(v):
    return v.reshape(1, -1).astype(F32)


def _mixer_prompt(x, lw):
    n_seq, t_len, _ = x.shape
    C = MIX_CHUNK
    n_chunks = t_len // C
    last = n_chunks - 1
    cos_t, sin_t = _rope_tables(np.arange(t_len))
    qdec, kdec, dmask, sdec, _ = _retention_tables(C)
    strict, incl, bd, tril, ones_blk = _block_masks(C, MIX_GROUP)
    prev_chunk = lambda i: (0, jnp.maximum(i - 1, 0), 0)
    in_specs = [
        pl.BlockSpec((n_seq, C, D_MODEL), prev_chunk),
        pl.BlockSpec((n_seq, C, D_MODEL), lambda i: (0, jnp.minimum(i + 1, last), 0)),
        _const_spec((1, D_MODEL)),
        _const_spec((D_MODEL, IN_W)),
        pl.BlockSpec((C, HEADS_W), lambda i: (jnp.minimum(i, last), 0)),
        pl.BlockSpec((C, HEADS_W), lambda i: (jnp.minimum(i, last), 0)),
    ]
    tail = [qdec, kdec, dmask, sdec, lw["ret_gn"], lw["mu"], lw["w0"], lw["lora"], lw["a0"], lw["g_b"],
            lw["k_k"], lw["k_a"], lw["r_k"], lw["ln_w"], lw["ln_b"], lw["w_out"], ones_blk, tril,
            strict, incl, bd]
    in_specs += [_const_spec(a.shape) for a in tail]
    out_shape = (
        jax.ShapeDtypeStruct((n_seq, t_len, D_MODEL), F32),
        jax.ShapeDtypeStruct((n_seq, N_HEADS, HEAD_DIM, HEAD_DIM), F32),
        jax.ShapeDtypeStruct((n_seq, N_HEADS, HEAD_DIM, HEAD_DIM), F32),
        jax.ShapeDtypeStruct((n_seq, SHIFT_W), F32),
    )
    out_specs = (
        pl.BlockSpec((n_seq, C, D_MODEL), prev_chunk),
        _const_out((n_seq, N_HEADS, HEAD_DIM, HEAD_DIM)),
        _const_out((n_seq, N_HEADS, HEAD_DIM, HEAD_DIM)),
        _const_out((n_seq, SHIFT_W)),
    )
    scratch = [
        pltpu.VMEM((2, n_seq * C, IN_W), F32),
        pltpu.VMEM((n_seq * C, 2 * HEADS_W), BF16),
        pltpu.VMEM((N_PAIRS, n_seq, PAIR_W, PAIR_W), F32),
        pltpu.VMEM((N_PAIRS, n_seq, PAIR_W, PAIR_W), F32),
        pltpu.VMEM((n_seq, SHIFT_W), F32),
    ]
    return pl.pallas_call(
        functools.partial(_mixer_prompt_kernel, n_seq=n_seq, chunk=C, group=MIX_GROUP),
        grid=(n_chunks + 1,),
        in_specs=in_specs, out_specs=out_specs, out_shape=out_shape, scratch_shapes=scratch,
        compiler_params=pltpu.CompilerParams(dimension_semantics=("arbitrary",),
                                             vmem_limit_bytes=VMEM_LIMIT),
        name="mixer_prompt",
    )(x, x, lw["norm"], lw["w_in"], cos_t, sin_t, *tail)


def _mixer_sample_pre_kernel(
        x_ref, shift_ref, normw_ref, win_ref, cos_ref, sin_ref, qdec_ref, mu_ref, w0_ref, lora_ref,
        a0_ref, gb_ref, kk_ref, ka_ref, rk_ref, ones_ref, feat_ref, feat_t_ref, newshift_ref):
    x = x_ref[...]
    p = _dot(_rms(x, normw_ref[...]), win_ref[...])
    q = _rope(p[:, 0:HEADS_W], cos_ref[...], sin_ref[...])
    k = _rope(p[:, HEADS_W:2 * HEADS_W], cos_ref[...], sin_ref[...]) * (HEAD_DIM ** -0.5)
    wp = p[:, RET_COLS:IN_W]
    xs = wp + (shift_ref[...] - wp) * mu_ref[...]
    r, lw, k_mod, vw, a_vec, b_vec, gate, bonus = _wkv_features(
        xs, w0_ref[...], lora_ref[...], a0_ref[...], gb_ref[...], kk_ref[...], ka_ref[...],
        rk_ref[...], ones_ref[...])
    newshift_ref[...] = wp
    state_feats = [q, q * qdec_ref[...], k, p[:, 2 * HEADS_W:3 * HEADS_W], r, jnp.exp(lw), k_mod, vw,
                   a_vec, b_vec]
    for n, f in enumerate(state_feats):
        feat_t_ref[n * HEADS_W:(n + 1) * HEADS_W, :] = f.T
    for n, f in enumerate([p[:, 3 * HEADS_W:4 * HEADS_W], gate, bonus]):
        feat_ref[:, n * HEADS_W:(n + 1) * HEADS_W] = f


_F_Q, _F_QD, _F_K, _F_V, _F_R, _F_W, _F_KM, _F_VW, _F_A, _F_B = range(10)
_N_STATE_FEATS = 10
_F_G, _F_GATE, _F_BONUS = range(3)
_N_ROW_FEATS = 3


def _mixer_sample_state_kernel(feat_t_ref, sdec_ref, ret_ref, wkv_ref, o_t_ref, reto_ref, wkvo_ref):
    h = pl.program_id(0)

    def head_rows(n):
        return feat_t_ref[pl.ds(pl.multiple_of(n * HEADS_W + h * HEAD_DIM, HEAD_DIM), HEAD_DIM), :]

    def head_row(n, i):
        return feat_t_ref[pl.ds(n * HEADS_W + h * HEAD_DIM + i, 1), :]

    a, w, b_vec, k_mod, r = (head_rows(n) for n in (_F_A, _F_W, _F_B, _F_KM, _F_R))

    def wkv_row(i, carry):
        s = wkv_ref[0, i]
        sa = jnp.sum(s * a, axis=0, keepdims=True)
        s_new = s * w + sa * b_vec + head_row(_F_VW, i) * k_mod
        wkvo_ref[0, i] = s_new
        o_t_ref[1, 0, pl.ds(i, 1), :] = jnp.sum(s_new * r, axis=0, keepdims=True)
        return carry

    lax.fori_loop(0, HEAD_DIM, wkv_row, 0, unroll=4)

    v, q, k = head_rows(_F_V), head_rows(_F_Q), head_rows(_F_K)
    g = sdec_ref[h]

    def ret_row(d, acc):
        s = ret_ref[0, d]
        reto_ref[0, d] = g * s + head_row(_F_K, d) * v
        return acc + head_row(_F_QD, d) * s

    cross = lax.fori_loop(0, HEAD_DIM, ret_row, jnp.zeros_like(v), unroll=4)
    o_t_ref[0, 0] = cross + jnp.sum(q * k, axis=0, keepdims=True) * v


def _mixer_sample_post_kernel(
        x_ref, feat_ref, o_t_ref, retgn_ref, lnw_ref, lnb_ref, wout_ref, ones_ref, xo_ref):
    ones_blk = ones_ref[...]
    g = feat_ref[:, _F_G * HEADS_W:(_F_G + 1) * HEADS_W]
    gate = feat_ref[:, _F_GATE * HEADS_W:(_F_GATE + 1) * HEADS_W]
    bonus = feat_ref[:, _F_BONUS * HEADS_W:(_F_BONUS + 1) * HEADS_W]
    o = o_t_ref[...].T
    ret = _head_norm(o[:, 0:HEADS_W], GN_EPS, ones_blk) * retgn_ref[...]
    ret_out = g * _sigmoid(g) * ret
    yn = _head_norm(o[:, HEADS_W:2 * HEADS_W], WKV_GN_EPS, ones_blk) * lnw_ref[...] + lnb_ref[...]
    wkv_out = (yn + bonus) * gate
    cat = jnp.concatenate([ret_out, wkv_out], axis=1)
    xo_ref[...] = x_ref[...] + _dot(cat, wout_ref[...])


def _mixer_sample(x, ret_t, wkv_t, shift0, lw):
    n = x.shape[0]
    cos_t, sin_t = _rope_tables(np.full((1,), PAST_LEN))
    qdec, _, _, _, cdec = _retention_tables(1)
    _, _, _, _, ones_blk = _block_masks(1, 1)
    args = [x, shift0, lw["norm"], lw["w_in"], cos_t, sin_t, qdec, lw["mu"], lw["w0"], lw["lora"],
            lw["a0"], lw["g_b"], lw["k_k"], lw["k_a"], lw["r_k"], ones_blk]
    feat, feat_t, new_shift = pl.pallas_call(
        _mixer_sample_pre_kernel,
        grid=(1,),
        in_specs=[_const_spec(a.shape) for a in args],
        out_specs=(_const_out((n, _N_ROW_FEATS * HEADS_W)), _const_out((_N_STATE_FEATS * HEADS_W, n)),
                   _const_out((n, SHIFT_W))),
        out_shape=(jax.ShapeDtypeStruct((n, _N_ROW_FEATS * HEADS_W), F32),
                   jax.ShapeDtypeStruct((_N_STATE_FEATS * HEADS_W, n), F32),
                   jax.ShapeDtypeStruct((n, SHIFT_W), F32)),
        compiler_params=pltpu.CompilerParams(vmem_limit_bytes=VMEM_LIMIT),
        name="mixer_sample_pre",
    )(*args)

    sdec = _const(np.broadcast_to(cdec[:, None, None], (N_HEADS, 1, n)))
    state_spec = pl.BlockSpec((1, HEAD_DIM, HEAD_DIM, n), lambda h: (h, 0, 0, 0))
    o_spec = pl.BlockSpec((2, 1, HEAD_DIM, n), lambda h: (0, h, 0, 0))
    o_t, ret_new, wkv_new = pl.pallas_call(
        _mixer_sample_state_kernel,
        grid=(N_HEADS,),
        in_specs=[_const_spec(feat_t.shape), _const_spec(sdec.shape), state_spec, state_spec],
        out_specs=(o_spec, state_spec, state_spec),
        out_shape=(jax.ShapeDtypeStruct((2, N_HEADS, HEAD_DIM, n), F32),
                   jax.ShapeDtypeStruct(ret_t.shape, F32), jax.ShapeDtypeStruct(wkv_t.shape, F32)),
        compiler_params=pltpu.CompilerParams(dimension_semantics=("arbitrary",),
                                             vmem_limit_bytes=VMEM_LIMIT),
        name="mixer_sample_state",
    )(feat_t, sdec, ret_t, wkv_t)

    args = [x, feat, o_t.reshape(2 * HEADS_W, n), lw["ret_gn"], lw["ln_w"], lw["ln_b"], lw["w_out"], ones_blk]
    x1 = pl.pallas_call(
        _mixer_sample_post_kernel,
        grid=(1,),
        in_specs=[_const_spec(a.shape) for a in args],
        out_specs=_const_out((n, D_MODEL)),
        out_shape=jax.ShapeDtypeStruct((n, D_MODEL), F32),
        compiler_params=pltpu.CompilerParams(vmem_limit_bytes=VMEM_LIMIT),
        name="mixer_sample_post",
    )(*args)
    return x1, ret_new, wkv_new, new_shift


def _mlp_kernel(x_ref, normw_ref, wup_ref, wdown_ref, normf_ref, o_ref, *, final_norm):
    x = x_ref[...]
    hn = _rms(x, normw_ref[...]).astype(BF16)
    acc = x
    for c in range(D_FF // FF_CHUNK):
        sl = slice(c * FF_CHUNK, (c + 1) * FF_CHUNK)
        hid = jnp.dot(hn, wup_ref[:, sl], preferred_element_type=F32)
        hid = jnp.square(jnp.maximum(hid, 0.0)).astype(BF16)
        acc = acc + jnp.dot(hid, wdown_ref[sl, :], preferred_element_type=F32)
    if final_norm:
        acc = _rms(acc, normf_ref[...])
    o_ref[...] = acc


def _mlp(x, norm_w, w_up, w_down, norm_f, layer, final_norm):
    rows = x.shape[0]
    tm = min(MLP_ROWS, rows)
    pick = lambda *_: (layer, 0, 0)
    return pl.pallas_call(
        functools.partial(_mlp_kernel, final_norm=final_norm),
        grid=(rows // tm,),
        in_specs=[pl.BlockSpec((tm, D_MODEL), lambda i: (i, 0)),
                  _const_spec((1, D_MODEL)),
                  pl.BlockSpec((None, D_MODEL, D_FF), pick, pipeline_mode=pl.Buffered(1)),
                  pl.BlockSpec((None, D_FF, D_MODEL), pick, pipeline_mode=pl.Buffered(1)),
                  _const_spec((1, D_MODEL))],
        out_specs=pl.BlockSpec((tm, D_MODEL), lambda i: (i, 0)),
        out_shape=jax.ShapeDtypeStruct((rows, D_MODEL), F32),
        compiler_params=pltpu.CompilerParams(dimension_semantics=("arbitrary",),
                                             vmem_limit_bytes=VMEM_LIMIT),
        name="mlp",
    )(x, norm_w, w_up, w_down, norm_f)


def _ssm_prep_kernel(lre_ref, lim_ref, logdt_ref, bre_ref, bim_ref, are_ref, aim_ref, bbre_ref, bbim_ref):
    lre = jnp.minimum(lre_ref[...], -1e-4)
    lim = lim_ref[...]
    dt = jnp.exp(logdt_ref[...])
    mag = jnp.exp(lre * dt)
    are = mag * jnp.cos(lim * dt)
    aim = mag * jnp.sin(lim * dt)
    are_ref[...] = are
    aim_ref[...] = aim
    den = lre * lre + lim * lim
    nre = are - 1.0
    cre = (nre * lre + aim * lim) / den
    cim = (aim * lre - nre * lim) / den
    bre = bre_ref[...]
    bim = bim_ref[...]
    bbre_ref[...] = cre * bre - cim * bim
    bbim_ref[...] = cre * bim + cim * bre


def _gelu_exact(x):
    return 0.5 * x * (1.0 + lax.erf(x * (2.0 ** -0.5)))


def _ssm_kernel(x_ref, hre0_ref, him0_ref, normw_ref, are_ref, aim_ref, wb_ref, cre_ref, cim_ref, dskip_ref,
                wglu_ref, xo_ref, hre_ref, him_ref, xt_s, u_s, bu_s, y_s, hg_s, hre_s, him_s,
                *, n_seq, chunk, batch_major):
    i = pl.program_id(0)
    n_steps = pl.num_programs(0)
    rows_all = chunk * n_seq
    n_slabs = D_MODEL // 128

    @pl.when(i == 0)
    def _():
        hre_s[...] = hre0_ref[...]
        him_s[...] = him0_ref[...]

    if batch_major:
        for b in range(n_seq):
            for sl in range(n_slabs):
                xt_s[sl, pl.ds(b, chunk, stride=n_seq), :] = x_ref[b, :, sl * 128:(sl + 1) * 128]
        ssq = sum(jnp.sum(jnp.square(xt_s[sl]), axis=-1, keepdims=True) for sl in range(n_slabs))
        inv = lax.rsqrt(ssq * (1.0 / D_MODEL) + RMS_EPS)
        for sl in range(n_slabs):
            cols = slice(sl * 128, (sl + 1) * 128)
            u_s[:, cols] = xt_s[sl] * inv * normw_ref[:, cols]
    else:
        u_s[...] = _rms(x_ref[...].reshape(rows_all, D_MODEL), normw_ref[...])

    def input_proj(blk, part):
        c = slice(part * SSM_HALF, (part + 1) * SSM_HALF)
        bu_s[blk % 2, :, c] = _dot(u_s[:, blk * 128:(blk + 1) * 128], wb_ref[blk, :, c])

    def output_proj(blk, part):
        buf = bu_s.at[blk % 2]
        cols = slice(blk * 128, (blk + 1) * 128)
        if part == 0:
            y_s[:, cols] = _dot(buf[:, 0:SSM_HALF], cre_ref[blk])
        else:
            y_s[:, cols] = y_s[:, cols] - _dot(buf[:, SSM_HALF:2 * SSM_HALF], cim_ref[blk])

    input_proj(0, 0)
    input_proj(0, 1)
    for blk in range(SSM_BLOCKS):
        neighbours = []
        if blk >= 1:
            neighbours += [functools.partial(output_proj, blk - 1, 0), functools.partial(output_proj, blk - 1, 1)]
        if blk + 1 < SSM_BLOCKS:
            neighbours += [functools.partial(input_proj, blk + 1, 0), functools.partial(input_proj, blk + 1, 1)]
        buf = bu_s.at[blk % 2]
        cols = slice(blk * SSM_HALF, (blk + 1) * SSM_HALF)
        a_re = jnp.broadcast_to(are_ref[:, cols], (n_seq, SSM_HALF))
        a_im = jnp.broadcast_to(aim_ref[:, cols], (n_seq, SSM_HALF))
        h_re = hre_s[:, cols]
        h_im = him_s[:, cols]
        seg = -(-chunk // max(len(neighbours), 1))
        for t in range(chunk):
            if t % seg == 0 and neighbours:
                neighbours.pop(0)()
            rows = slice(t * n_seq, (t + 1) * n_seq)
            n_re = a_re * h_re - a_im * h_im + buf[rows, 0:SSM_HALF]
            n_im = a_re * h_im + a_im * h_re + buf[rows, SSM_HALF:2 * SSM_HALF]
            buf[rows, 0:SSM_HALF] = n_re
            buf[rows, SSM_HALF:2 * SSM_HALF] = n_im
            h_re, h_im = n_re, n_im
        for f in neighbours:
            f()
        hre_s[:, cols] = h_re
        him_s[:, cols] = h_im
    output_proj(SSM_BLOCKS - 1, 0)
    output_proj(SSM_BLOCKS - 1, 1)
    hg_s[...] = _gelu_exact(y_s[...] + dskip_ref[...] * u_s[...]).astype(BF16)
    glu_w = 256
    for c in range(D_MODEL // glu_w):
        cols = slice(c * glu_w, (c + 1) * glu_w)
        hg = hg_s[...]
        val = jnp.dot(hg, wglu_ref[:, cols], preferred_element_type=F32)
        gate = jnp.dot(hg, wglu_ref[:, D_MODEL + c * glu_w:D_MODEL + (c + 1) * glu_w],
                       preferred_element_type=F32)
        out = val * _sigmoid(gate)
        if batch_major:
            for sl in range(c * glu_w // 128, (c + 1) * glu_w // 128):
                xt_s[sl] = xt_s[sl] + out[:, sl * 128 - c * glu_w:(sl + 1) * 128 - c * glu_w]
        else:
            y_s[:, cols] = x_ref[...].reshape(rows_all, D_MODEL)[:, cols] + out
    if batch_major:
        for b in range(n_seq):
            for sl in range(n_slabs):
                xo_ref[b, :, sl * 128:(sl + 1) * 128] = xt_s[sl, pl.ds(b, chunk, stride=n_seq), :]
    else:
        xo_ref[...] = y_s[...].reshape(chunk, n_seq, D_MODEL)

    @pl.when(i == n_steps - 1)
    def _():
        hre_ref[...] = hre_s[...]
        him_ref[...] = him_s[...]


def _ssm_weights(lam_re, lam_im, log_dt, b_re, b_im, c_re, c_im):
    g, p = SSM_GROUPS, SSM_P
    n = g * SSM_GROUP
    rep = lambda z: jnp.repeat(z, SSM_GROUP, axis=0)
    bt_re = jnp.swapaxes(b_re, 1, 2).reshape(n, p)
    bt_im = jnp.swapaxes(b_im, 1, 2).reshape(n, p)
    args = [rep(lam_re), rep(lam_im), rep(log_dt.reshape(g, 1)), bt_re, bt_im]
    a_re, a_im, bb_re, bb_im = pl.pallas_call(
        _ssm_prep_kernel,
        grid=(1,),
        in_specs=[_const_spec(a.shape) for a in args],
        out_specs=tuple(_const_out((n, p)) for _ in range(4)),
        out_shape=tuple(jax.ShapeDtypeStruct((n, p), F32) for _ in range(4)),
        name="ssm_prep",
    )(*args)
    a_re = a_re[::SSM_GROUP]
    a_im = a_im[::SSM_GROUP]
    eye = jnp.eye(SSM_BLOCK_G, dtype=F32)

    def in_block(bb):
        bb = bb.reshape(SSM_BLOCKS, SSM_BLOCK_G, SSM_GROUP, p)
        return jnp.einsum("bgcp,gh->bgchp", bb, eye).reshape(SSM_BLOCKS, 128, SSM_HALF)

    def out_block(cc):
        cc = cc.reshape(SSM_BLOCKS, SSM_BLOCK_G, SSM_GROUP, p)
        return jnp.einsum("bgcp,gh->bgphc", cc, eye).reshape(SSM_BLOCKS, SSM_HALF, 128)

    w_b = jnp.concatenate([in_block(bb_re), in_block(bb_im)], axis=2).astype(BF16)
    n_state = SSM_GROUPS * SSM_P
    return (a_re.reshape(1, n_state), a_im.reshape(1, n_state), w_b,
            out_block(c_re).astype(BF16), out_block(c_im).astype(BF16))


def _ssm_layer(x, h_re0, h_im0, sw, chunk, batch_major):
    if batch_major:
        n_seq, t_len, _ = x.shape
        x_block = (n_seq, chunk, D_MODEL)
        x_map = lambda i: (0, i, 0)
    else:
        t_len, n_seq, _ = x.shape
        x_block = (chunk, n_seq, D_MODEL)
        x_map = lambda i: (i, 0, 0)
    rows = chunk * n_seq
    n_state = SSM_GROUPS * SSM_P
    args = [x, h_re0, h_im0, sw["norm"], sw["a_re"], sw["a_im"], sw["w_b"], sw["c_re"], sw["c_im"],
            sw["d_skip"], sw["w_glu"]]
    in_specs = [pl.BlockSpec(x_block, x_map)] + [_const_spec(a.shape) for a in args[1:]]
    state = jax.ShapeDtypeStruct((n_seq, n_state), F32)
    return pl.pallas_call(
        functools.partial(_ssm_kernel, n_seq=n_seq, chunk=chunk, batch_major=batch_major),
        grid=(t_len // chunk,),
        in_specs=in_specs,
        out_specs=(pl.BlockSpec(x_block, x_map), _const_out(state.shape), _const_out(state.shape)),
        out_shape=(jax.ShapeDtypeStruct(x.shape, F32), state, state),
        scratch_shapes=[pltpu.VMEM((D_MODEL // 128, rows, 128), F32),
                        pltpu.VMEM((rows, D_MODEL), F32), pltpu.VMEM((2, rows, 2 * SSM_HALF), F32),
                        pltpu.VMEM((rows, D_MODEL), F32), pltpu.VMEM((rows, D_MODEL), BF16),
                        pltpu.VMEM((n_seq, n_state), F32), pltpu.VMEM((n_seq, n_state), F32)],
        compiler_params=pltpu.CompilerParams(dimension_semantics=("arbitrary",),
                                             vmem_limit_bytes=VMEM_LIMIT),
        name="ssm_layer",
    )(*args)


def kernel(x_prompt, x_sample, state_ret, state_wkv, state_shift, state_ssm_re, state_ssm_im, norm_mix, w_in, ret_gn, mu_shift, wkv_w0, wkv_wB, wkv_a0, wkv_aB, wkv_gB, wkv_kk, wkv_ka, wkv_rk, wkv_ln_w, wkv_ln_b, w_out, ssm_lambda_re, ssm_lambda_im, ssm_log_dt, ssm_B_re, ssm_B_im, ssm_C_re, ssm_C_im, ssm_D, ssm_w_glu, mlp_norm, mlp_up, mlp_down, norm_f):
    lw = dict(
        norm=_row(norm_mix[0]), w_in=w_in[0].astype(BF16), ret_gn=_row(ret_gn[0]), mu=_row(mu_shift[0]),
        w0=_row(wkv_w0[0]), lora=_lora_block(wkv_wB[0], wkv_aB[0]), a0=_row(wkv_a0[0]),
        g_b=wkv_gB[0].astype(BF16), k_k=_row(wkv_kk[0]), k_a=_row(wkv_ka[0]), r_k=_row(wkv_rk[0]),
        ln_w=_row(wkv_ln_w[0]), ln_b=_row(wkv_ln_b[0]), w_out=w_out[0].astype(BF16))
    a_re, a_im, w_b, c_re, c_im = _ssm_weights(ssm_lambda_re[0], ssm_lambda_im[0], ssm_log_dt[0],
                                               ssm_B_re[0], ssm_B_im[0], ssm_C_re[0], ssm_C_im[0])
    sw = dict(norm=_row(norm_mix[1]), a_re=a_re, a_im=a_im, w_b=w_b, c_re=c_re, c_im=c_im,
              d_skip=_row(ssm_D[0]), w_glu=ssm_w_glu[0].astype(BF16))
    n_state = SSM_GROUPS * SSM_P
    up = mlp_up.astype(BF16)
    down = mlp_down.astype(BF16)
    nf = _row(norm_f)

    n_p, t_p, _ = x_prompt.shape
    x1, ret_p, wkv_p, shift_p = _mixer_prompt(x_prompt, lw)
    x1 = _mlp(x1.reshape(n_p * t_p, D_MODEL), _row(mlp_norm[0]), up, down, nf, 0, False)
    zero_state = jnp.zeros((n_p, n_state), F32)
    x2, ssm_re_p, ssm_im_p = _ssm_layer(x1.reshape(n_p, t_p, D_MODEL), zero_state, zero_state, sw,
                                        SSM_CHUNK, True)
    y_p = _mlp(x2.reshape(n_p * t_p, D_MODEL), _row(mlp_norm[1]), up, down, nf, 1, True)
    y_prompt = y_p.reshape(n_p, t_p, D_MODEL)
    ssm_re_p = ssm_re_p.reshape(n_p, SSM_GROUPS, SSM_P)
    ssm_im_p = ssm_im_p.reshape(n_p, SSM_GROUPS, SSM_P)

    n_s = x_sample.shape[0]
    xs = x_sample.reshape(n_s, D_MODEL)
    seq_last = lambda s: jnp.transpose(s, (1, 2, 3, 0))
    seq_first = lambda s: jnp.transpose(s, (3, 0, 1, 2))
    xs1, ret_s, wkv_s, shift_s = _mixer_sample(xs, seq_last(state_ret[0]), seq_last(state_wkv[0]),
                                               state_shift[0], lw)
    ret_s, wkv_s = seq_first(ret_s), seq_first(wkv_s)
    xs1 = _mlp(xs1, _row(mlp_norm[0]), up, down, nf, 0, False)
    xs2, ssm_re_s, ssm_im_s = _ssm_layer(
        xs1.reshape(1, n_s, D_MODEL), state_ssm_re[0].reshape(n_s, n_state),
        state_ssm_im[0].reshape(n_s, n_state), sw, 1, False)
    y_s = _mlp(xs2.reshape(n_s, D_MODEL), _row(mlp_norm[1]), up, down, nf, 1, True)
    ssm_re_s = ssm_re_s.reshape(n_s, SSM_GROUPS, SSM_P)
    ssm_im_s = ssm_im_s.reshape(n_s, SSM_GROUPS, SSM_P)

    return (y_prompt, y_s.reshape(n_s, 1, D_MODEL),
            ret_p[None], wkv_p[None], shift_p[None], ssm_re_p[None], ssm_im_p[None],
            ret_s[None], wkv_s[None], shift_s[None], ssm_re_s[None], ssm_im_s[None])
```

```python
import functools
import math

import numpy as np
import jax
import jax.numpy as jnp
from jax import lax
from jax.experimental import pallas as pl
from jax.experimental.pallas import tpu as pltpu

F32 = jnp.float32
BF16 = jnp.bfloat16

D_MODEL = 1024
N_HEADS = 8
HEAD_DIM = 64
HEADS_W = N_HEADS * HEAD_DIM
N_PAIRS = N_HEADS // 2
PAIR_W = 2 * HEAD_DIM
ROPE_BASE = 10000.0
DECAY_LORA = 64
AAA_LORA = 64
GATE_LORA = 128
SHIFT_W = 3 * HEADS_W + DECAY_LORA + AAA_LORA + GATE_LORA
RET_COLS = 4 * HEADS_W
IN_W = RET_COLS + SHIFT_W
SSM_GROUP = 16
SSM_GROUPS = D_MODEL // SSM_GROUP
SSM_P = 64
SSM_BLOCKS = 8
SSM_BLOCK_G = SSM_GROUPS // SSM_BLOCKS
SSM_HALF = SSM_BLOCK_G * SSM_P
SSM_STATE_W = 2 * SSM_BLOCKS * SSM_HALF
D_FF = 4 * D_MODEL
RMS_EPS = 1e-6
GN_EPS = 1e-5
WKV_GN_EPS = 64e-5
PAST_LEN = 16384

MIX_CHUNK = 64
PROJ_PIECE = 256
MIX_GROUP = 4
SSM_CHUNK = 64
MLP_ROWS = 512
FF_CHUNK = 1024

VMEM_LIMIT = 58 * 1024 * 1024


def _dot(a, b):
    return jnp.dot(a.astype(BF16), b.astype(BF16), preferred_element_type=F32)


def _dot_nt(a, b):
    return lax.dot_general(a.astype(BF16), b.astype(BF16), (((1,), (1,)), ((), ())),
                           preferred_element_type=F32)


def _dot_tn(a, b):
    return lax.dot_general(a.astype(BF16), b.astype(BF16), (((0,), (0,)), ((), ())),
                           preferred_element_type=F32)


def _split3(x):
    hi = x.astype(BF16)
    r1 = x - hi.astype(F32)
    mid = r1.astype(BF16)
    lo = (r1 - mid.astype(F32)).astype(BF16)
    return hi, mid, lo


def _dot_exact_lhs(a_bf16, x):
    hi, mid, lo = _split3(x)
    f = lambda p: jnp.dot(a_bf16, p, preferred_element_type=F32)
    return f(hi) + f(mid) + f(lo)


def _segsum(x, ones_blk):
    hi = x.astype(BF16)
    lo = (x - hi.astype(F32)).astype(BF16)
    outs = []
    for c in range(x.shape[1] // 256):
        sl = slice(c * 256, (c + 1) * 256)
        f = lambda p: jnp.dot(p[:, sl], ones_blk, preferred_element_type=F32)
        outs.append(f(hi) + f(lo))
    return jnp.concatenate(outs, axis=1)


def _rms(x, w):
    return x * lax.rsqrt(jnp.mean(x * x, axis=-1, keepdims=True) + RMS_EPS) * w


def _sigmoid(x):
    return 1.0 / (1.0 + jnp.exp(-x))


def _softplus(x):
    return jnp.maximum(x, 0.0) + jnp.log1p(jnp.exp(-jnp.abs(x)))


def _head_norm(z, eps, ones_blk):
    mu = _segsum(z, ones_blk) * (1.0 / HEAD_DIM)
    zc = z - mu
    var = _segsum(zc * zc, ones_blk) * (1.0 / HEAD_DIM)
    return zc * lax.rsqrt(var + eps)


def _rope(z, cos, sin_signed):
    lane = lax.broadcasted_iota(jnp.int32, (1, HEADS_W), 1) % HEAD_DIM
    swapped = jnp.where(lane < HEAD_DIM // 2,
                        pltpu.roll(z, HEADS_W - HEAD_DIM // 2, axis=1),
                        pltpu.roll(z, HEAD_DIM // 2, axis=1))
    return z * cos + swapped * sin_signed


def _wkv_features(xs, w0, lora_w, a0, g_b, k_k, k_a, r_k, ones_blk, between=lambda: None):
    r = xs[:, 0:HEADS_W]
    kw = xs[:, HEADS_W:2 * HEADS_W]
    vw = xs[:, 2 * HEADS_W:3 * HEADS_W]
    lo = xs[:, 3 * HEADS_W:3 * HEADS_W + 128]
    lane = lax.broadcasted_iota(jnp.int32, (1, 128), 1)
    lo = jnp.where(lane < DECAY_LORA, jnp.tanh(lo), lo)
    ll = _dot(lo, lora_w)
    w_log = -_softplus(-(w0 + ll[:, 0:HEADS_W])) - 0.5
    log_decay = -jnp.exp(w_log)
    between()
    alr = _sigmoid(a0 + ll[:, HEADS_W:2 * HEADS_W])
    gate = _dot(_sigmoid(xs[:, 3 * HEADS_W + 128:SHIFT_W]), g_b)
    between()
    kk = kw * k_k
    kk = kk / jnp.maximum(jnp.sqrt(_segsum(kk * kk, ones_blk)), 1e-12)
    k_mod = kw * (1.0 + (alr - 1.0) * k_a)
    between()
    bonus = _segsum(r * k_mod * r_k, ones_blk) * vw
    return r, log_decay, k_mod, vw, -kk, kk * alr, gate, bonus


def _stack_masked(x2, m0):
    return jnp.concatenate([jnp.where(m0, x2, 0.0), jnp.where(m0, 0.0, x2)], axis=0)


def _stack_dup(x2):
    return jnp.concatenate([x2, x2], axis=0)


def _mixer_prompt_kernel(
        xprev_ref, xnext_ref, normw_ref, win_ref, cos_ref, sin_ref, qdec_ref, kdec_ref, dmask_ref, sdec_ref,
        retgn_ref, mu_ref, w0_ref, lora_ref, a0_ref, gb_ref, kk_ref, ka_ref, rk_ref, lnw_ref,
        lnb_ref, wout_ref, ones_ref, tril_ref, strict_ref, incl_ref, bd_ref,
        xo_ref, rets_ref, wkvs_ref, shift_ref,
        p_s, cat_s, rs_s, ws_s, carry_s, *, n_seq, chunk, group):
    i = pl.program_id(0)
    n_chunks = pl.num_programs(0) - 1
    C = chunk
    slot = i % 2

    @pl.when(i == 0)
    def _():
        rs_s[...] = jnp.zeros_like(rs_s)
        ws_s[...] = jnp.zeros_like(ws_s)
        carry_s[...] = jnp.zeros_like(carry_s)
        cat_s[...] = jnp.zeros_like(cat_s)
        p_s[0] = _dot(_rms(xprev_ref[...].reshape(n_seq * C, D_MODEL), normw_ref[...]), win_ref[...])

    m0 = lax.broadcasted_iota(jnp.int32, (1, PAIR_W), 1) < HEAD_DIM
    ones_blk = ones_ref[...]
    NB = group
    R = NB * C
    row_id = lax.broadcasted_iota(jnp.int32, (R, 1), 0)
    tile_rows = lambda ref: jnp.concatenate([ref[...]] * NB, axis=0)
    pairs = range(N_PAIRS)
    sls = [slice(pr * PAIR_W, (pr + 1) * PAIR_W) for pr in pairs]
    chains = [(s, pr) for s in range(NB) for pr in pairs]
    seq_rows = [slice(s * C, (s + 1) * C) for s in range(NB)]

    def per_group(gi, carry):
        rows = pl.ds(pl.multiple_of(gi * R, R), R)
        b0 = gi * NB
        ret_states = {(s, pr): rs_s[pr, b0 + s] for s, pr in chains}
        wkv_states = {(s, pr): ws_s[pr, b0 + s] for s, pr in chains}
        shift_rows = [carry_s[pl.ds(b0 + s, 1), :] for s in range(NB)]
        p_cur = p_s.at[slot]
        wp = p_cur[rows, RET_COLS:IN_W]
        hn_next = _rms(xnext_ref[pl.ds(b0, NB)].reshape(R, D_MODEL), normw_ref[...]).astype(BF16)
        cat_prev = cat_s[rows, :]

        def next_in_proj(c0):
            p_s[1 - slot, rows, c0:c0 + PROJ_PIECE] = jnp.dot(
                hn_next, win_ref[:, c0:c0 + PROJ_PIECE], preferred_element_type=F32)

        def prev_out_proj(c0):
            cols = slice(c0, c0 + PROJ_PIECE)
            out = jnp.dot(cat_prev, wout_ref[:, cols], preferred_element_type=F32)
            xo_ref[pl.ds(b0, NB), :, cols] = xprev_ref[pl.ds(b0, NB), :, cols] + out.reshape(NB, C, PROJ_PIECE)

        pieces = [functools.partial(next_in_proj, c0) for c0 in range(0, IN_W, PROJ_PIECE)]
        pieces += [functools.partial(prev_out_proj, c0) for c0 in range(0, D_MODEL, PROJ_PIECE)]

        def fill(n=1):
            for _ in range(min(n, len(pieces))):
                pieces.pop(0)()

        cos, sin = tile_rows(cos_ref), tile_rows(sin_ref)
        q = _rope(p_cur[rows, 0:HEADS_W], cos, sin)
        fill()
        k = _rope(p_cur[rows, HEADS_W:2 * HEADS_W], cos, sin) * (HEAD_DIM ** -0.5)
        fill()
        v = p_cur[rows, 2 * HEADS_W:3 * HEADS_W]
        g = p_cur[rows, 3 * HEADS_W:4 * HEADS_W]
        qd = q * tile_rows(qdec_ref)
        kd = k * tile_rows(kdec_ref)
        fill()
        cut = lambda z, s, pr: z[seq_rows[s], sls[pr]]
        r_sc = {c: _dot_nt(_stack_masked(cut(q, *c), m0), _stack_dup(cut(k, *c))) * dmask_ref[c[1]]
                for c in chains}
        r_inner = {c: _dot(r_sc[c], _stack_dup(cut(v, *c))) for c in chains}
        r_cross = {c: _dot(cut(qd, *c), ret_states[c]) for c in chains}
        new_ret = {c: sdec_ref[c[1]] * ret_states[c] + _dot_tn(cut(kd, *c), cut(v, *c)) * bd_ref[...]
                   for c in chains}
        o = jnp.concatenate(
            [jnp.concatenate([jnp.where(m0, r_inner[(s, pr)][0:C], r_inner[(s, pr)][C:2 * C])
                              + r_cross[(s, pr)] for pr in pairs], axis=1) for s in range(NB)], axis=0)
        fill()
        ret = _head_norm(o, GN_EPS, ones_blk) * retgn_ref[...]
        ret_out = (g * _sigmoid(g) * ret).astype(BF16)
        fill()

        prev = pltpu.roll(wp, 1, axis=0)
        for s in range(NB):
            prev = jnp.where(row_id == s * C, shift_rows[s], prev)
        xs = wp + (prev - wp) * mu_ref[...]
        fill()
        r, lw, k_mod, vw, a_vec, b_vec, gate, bonus = _wkv_features(
            xs, w0_ref[...], lora_ref[...], a0_ref[...], gb_ref[...], kk_ref[...], ka_ref[...],
            rk_ref[...], ones_blk, fill)
        cw = _dot_exact_lhs(tril_ref[...], lw)
        fill()
        cw_last = [cw[s * C + C - 1:(s + 1) * C, :] for s in range(NB)]
        cwl = jnp.concatenate([jnp.broadcast_to(z, (C, HEADS_W)) for z in cw_last], axis=0)
        r_t = r * jnp.exp(cw)
        a_t = a_vec * jnp.exp(cw - lw)
        fill()
        w_inv = jnp.exp(-cw)
        b_t = b_vec * w_inv
        k_t = k_mod * w_inv
        fill()
        w_end = jnp.exp(cwl - cw)
        b_h = b_vec * w_end
        k_h = k_mod * w_end
        fill()
        w_all = [jnp.exp(z) for z in cw_last]
        lhs = {c: jnp.concatenate([_stack_masked(cut(a_t, *c), m0), _stack_masked(cut(r_t, *c), m0)], axis=0)
               for c in chains}
        fill(len(pieces) - 2)
        sc = {c: _dot_nt(lhs[c], jnp.concatenate([_stack_dup(cut(b_t, *c)), _stack_dup(cut(k_t, *c))], axis=0))
              for c in chains}
        on_state = {c: _dot_nt(lhs[c], wkv_states[c]) for c in chains}
        vv = {c: _stack_dup(cut(vw, *c)) for c in chains}
        n_pow = {c: sc[c][0:2 * C, 0:2 * C] * strict_ref[...] for c in chains}
        u = {c: on_state[c][0:2 * C] + _dot(sc[c][0:2 * C, 2 * C:4 * C] * strict_ref[...], vv[c])
             for c in chains}
        n_steps_solve = int(math.log2(C))
        for it in range(n_steps_solve):
            u = {c: u[c] + _dot(n_pow[c], u[c]) for c in chains}
            if it + 1 < n_steps_solve:
                n_pow = {c: _dot(n_pow[c], n_pow[c]) for c in chains}
        uv = {c: jnp.concatenate([u[c], vv[c]], axis=0) for c in chains}
        y_st = {c: on_state[c][2 * C:4 * C] + _dot(
            jnp.concatenate([sc[c][2 * C:4 * C, 0:2 * C] * incl_ref[...],
                             sc[c][2 * C:4 * C, 2 * C:4 * C] * incl_ref[...]], axis=1), uv[c]) for c in chains}
        new_wkv = {c: wkv_states[c] * w_all[c[0]][:, sls[c[1]]] + bd_ref[...] * _dot_tn(
            uv[c], jnp.concatenate([_stack_masked(cut(b_h, *c), m0), _stack_masked(cut(k_h, *c), m0)], axis=0))
            for c in chains}
        y = jnp.concatenate(
            [jnp.concatenate([jnp.where(m0, y_st[(s, pr)][0:C], y_st[(s, pr)][C:2 * C]) for pr in pairs], axis=1)
             for s in range(NB)], axis=0)
        fill(len(pieces))
        yn = _head_norm(y, WKV_GN_EPS, ones_blk) * lnw_ref[...] + lnb_ref[...]
        cat_s[rows, 0:HEADS_W] = ret_out
        cat_s[rows, HEADS_W:2 * HEADS_W] = ((yn + bonus) * gate).astype(BF16)
        for s in range(NB):
            carry_s[pl.ds(b0 + s, 1), :] = wp[s * C + C - 1:(s + 1) * C, :]
        for s, pr in chains:
            rs_s[pr, b0 + s] = new_ret[(s, pr)]
            ws_s[pr, b0 + s] = new_wkv[(s, pr)]
        return carry

    @pl.when(i < n_chunks)
    def _():
        lax.fori_loop(0, n_seq // NB, per_group, 0)

    @pl.when(i == n_chunks)
    def _():
        out = jnp.dot(cat_s[...], wout_ref[...], preferred_element_type=F32)
        xo_ref[...] = xprev_ref[...] + out.reshape(n_seq, C, D_MODEL)

    @pl.when(i == n_chunks - 1)
    def _():
        shift_ref[...] = carry_s[...]
        for b in range(n_seq):
            for pr in range(N_PAIRS):
                rs = rs_s[pr, b]
                ws = ws_s[pr, b]
                rets_ref[b, 2 * pr] = rs[0:HEAD_DIM, 0:HEAD_DIM]
                rets_ref[b, 2 * pr + 1] = rs[HEAD_DIM:PAIR_W, HEAD_DIM:PAIR_W]
                wkvs_ref[b, 2 * pr] = ws[0:HEAD_DIM, 0:HEAD_DIM]
                wkvs_ref[b, 2 * pr + 1] = ws[HEAD_DIM:PAIR_W, HEAD_DIM:PAIR_W]


def _const_spec(shape):
    nd = len(shape)
    return pl.BlockSpec(shape, lambda *_: (0,) * nd, pipeline_mode=pl.Buffered(1))


def _const_out(shape):
    nd = len(shape)
    return pl.BlockSpec(shape, lambda *_: (0,) * nd)


def _const(a, dtype=F32):
    return jnp.asarray(np.asarray(a, np.float64), dtype=dtype)


def _retention_tables(chunk):
    log_g = np.log1p(-np.exp2(-5.0 - np.arange(N_HEADS, dtype=np.float64)))
    lane_g = np.repeat(log_g, HEAD_DIM)[None, :]
    idx = np.arange(chunk, dtype=np.float64)
    qdec = np.exp((idx + 1.0)[:, None] * lane_g)
    kdec = np.exp((chunk - 1.0 - idx)[:, None] * lane_g)
    rel = idx[:, None] - idx[None, :]
    dm = np.where(rel >= 0, np.exp(np.maximum(rel, 0.0)[None] * log_g[:, None, None]), 0.0)
    zero = np.zeros((chunk, chunk))
    dmask = np.stack([np.block([[dm[2 * p], zero], [zero, dm[2 * p + 1]]]) for p in range(N_PAIRS)])
    cdec = np.exp(chunk * log_g)
    hz = np.zeros((HEAD_DIM, HEAD_DIM))
    ho = np.ones((HEAD_DIM, HEAD_DIM))
    sdec = np.stack([np.block([[cdec[2 * p] * ho, hz], [hz, cdec[2 * p + 1] * ho]])
                     for p in range(N_PAIRS)])
    return _const(qdec), _const(kdec), _const(dmask), _const(sdec), cdec


def _rope_tables(pos):
    half = HEAD_DIM // 2
    inv_freq = ROPE_BASE ** (-np.arange(half, dtype=np.float64) / half)
    ang = np.asarray(pos, np.float64)[:, None] * inv_freq[None, :]
    cos = np.cos(ang)
    sin = np.sin(ang)
    cos_t = np.tile(np.concatenate([cos, cos], axis=1), (1, N_HEADS))
    sin_t = np.tile(np.concatenate([-sin, sin], axis=1), (1, N_HEADS))
    return _const(cos_t), _const(sin_t)


def _block_masks(chunk, group):
    i = np.arange(2 * chunk)
    same = (i[:, None] // chunk) == (i[None, :] // chunk)
    strict = same & (i[:, None] > i[None, :])
    incl = same & (i[:, None] >= i[None, :])
    j = np.arange(PAIR_W)
    bd = (j[:, None] // HEAD_DIM) == (j[None, :] // HEAD_DIM)
    t = np.arange(chunk)
    tril = np.kron(np.eye(group), t[:, None] >= t[None, :])
    o = np.arange(256)
    ones_blk = (o[:, None] // HEAD_DIM) == (o[None, :] // HEAD_DIM)
    return _const(strict), _const(incl), _const(bd), _const(tril, BF16), _const(ones_blk, BF16)


def _lora_block(w_b, a_b):
    z = jnp.zeros_like(w_b)
    return jnp.concatenate([jnp.concatenate([w_b, z], axis=1),
                            jnp.concatenate([z, a_b], axis=1)], axis=0).astype(BF16)


def _row(v):
    return v.reshape(1, -1).astype(F32)


def _mixer_prompt(x, lw):
    n_seq, t_len, _ = x.shape
    C = MIX_CHUNK
    n_chunks = t_len // C
    last = n_chunks - 1
    cos_t, sin_t = _rope_tables(np.arange(t_len))
    qdec, kdec, dmask, sdec, _ = _retention_tables(C)
    strict, incl, bd, tril, ones_blk = _block_masks(C, MIX_GROUP)
    prev_chunk = lambda i: (0, jnp.maximum(i - 1, 0), 0)
    in_specs = [
        pl.BlockSpec((n_seq, C, D_MODEL), prev_chunk),
        pl.BlockSpec((n_seq, C, D_MODEL), lambda i: (0, jnp.minimum(i + 1, last), 0)),
        _const_spec((1, D_MODEL)),
        _const_spec((D_MODEL, IN_W)),
        pl.BlockSpec((C, HEADS_W), lambda i: (jnp.minimum(i, last), 0)),
        pl.BlockSpec((C, HEADS_W), lambda i: (jnp.minimum(i, last), 0)),
    ]
    tail = [qdec, kdec, dmask, sdec, lw["ret_gn"], lw["mu"], lw["w0"], lw["lora"], lw["a0"], lw["g_b"],
            lw["k_k"], lw["k_a"], lw["r_k"], lw["ln_w"], lw["ln_b"], lw["w_out"], ones_blk, tril,
            strict, incl, bd]
    in_specs += [_const_spec(a.shape) for a in tail]
    out_shape = (
        jax.ShapeDtypeStruct((n_seq, t_len, D_MODEL), F32),
        jax.ShapeDtypeStruct((n_seq, N_HEADS, HEAD_DIM, HEAD_DIM), F32),
        jax.ShapeDtypeStruct((n_seq, N_HEADS, HEAD_DIM, HEAD_DIM), F32),
        jax.ShapeDtypeStruct((n_seq, SHIFT_W), F32),
    )
    out_specs = (
        pl.BlockSpec((n_seq, C, D_MODEL), prev_chunk),
        _const_out((n_seq, N_HEADS, HEAD_DIM, HEAD_DIM)),
        _const_out((n_seq, N_HEADS, HEAD_DIM, HEAD_DIM)),
        _const_out((n_seq, SHIFT_W)),
    )
    scratch = [
        pltpu.VMEM((2, n_seq * C, IN_W), F32),
        pltpu.VMEM((n_seq * C, 2 * HEADS_W), BF16),
        pltpu.VMEM((N_PAIRS, n_seq, PAIR_W, PAIR_W), F32),
        pltpu.VMEM((N_PAIRS, n_seq, PAIR_W, PAIR_W), F32),
        pltpu.VMEM((n_seq, SHIFT_W), F32),
    ]
    return pl.pallas_call(
        functools.partial(_mixer_prompt_kernel, n_seq=n_seq, chunk=C, group=MIX_GROUP),
        grid=(n_chunks + 1,),
        in_specs=in_specs, out_specs=out_specs, out_shape=out_shape, scratch_shapes=scratch,
        compiler_params=pltpu.CompilerParams(dimension_semantics=("arbitrary",),
                                             vmem_limit_bytes=VMEM_LIMIT),
        name="mixer_prompt",
    )(x, x, lw["norm"], lw["w_in"], cos_t, sin_t, *tail)


def _mixer_sample_pre_kernel(
        x_ref, shift_ref, normw_ref, win_ref, cos_ref, sin_ref, qdec_ref, mu_ref, w0_ref, lora_ref,
        a0_ref, gb_ref, kk_ref, ka_ref, rk_ref, ones_ref, feat_ref, feat_t_ref, newshift_ref):
    x = x_ref[...]
    p = _dot(_rms(x, normw_ref[...]), win_ref[...])
    q = _rope(p[:, 0:HEADS_W], cos_ref[...], sin_ref[...])
    k = _rope(p[:, HEADS_W:2 * HEADS_W], cos_ref[...], sin_ref[...]) * (HEAD_DIM ** -0.5)
    wp = p[:, RET_COLS:IN_W]
    xs = wp + (shift_ref[...] - wp) * mu_ref[...]
    r, lw, k_mod, vw, a_vec, b_vec, gate, bonus = _wkv_features(
        xs, w0_ref[...], lora_ref[...], a0_ref[...], gb_ref[...], kk_ref[...], ka_ref[...],
        rk_ref[...], ones_ref[...])
    newshift_ref[...] = wp
    state_feats = [q, q * qdec_ref[...], k, p[:, 2 * HEADS_W:3 * HEADS_W], r, jnp.exp(lw), k_mod, vw,
                   a_vec, b_vec]
    for n, f in enumerate(state_feats):
        feat_t_ref[n * HEADS_W:(n + 1) * HEADS_W, :] = f.T
    for n, f in enumerate([p[:, 3 * HEADS_W:4 * HEADS_W], gate, bonus]):
        feat_ref[:, n * HEADS_W:(n + 1) * HEADS_W] = f


_F_Q, _F_QD, _F_K, _F_V, _F_R, _F_W, _F_KM, _F_VW, _F_A, _F_B = range(10)
_N_STATE_FEATS = 10
_F_G, _F_GATE, _F_BONUS = range(3)
_N_ROW_FEATS = 3


def _mixer_sample_state_kernel(feat_t_ref, sdec_ref, ret_ref, wkv_ref, o_t_ref, reto_ref, wkvo_ref):
    h = pl.program_id(0)

    def head_rows(n):
        return feat_t_ref[pl.ds(pl.multiple_of(n * HEADS_W + h * HEAD_DIM, HEAD_DIM), HEAD_DIM), :]

    def head_row(n, i):
        return feat_t_ref[pl.ds(n * HEADS_W + h * HEAD_DIM + i, 1), :]

    a, w, b_vec, k_mod, r = (head_rows(n) for n in (_F_A, _F_W, _F_B, _F_KM, _F_R))

    def wkv_row(i, carry):
        s = wkv_ref[0, i]
        sa = jnp.sum(s * a, axis=0, keepdims=True)
        s_new = s * w + sa * b_vec + head_row(_F_VW, i) * k_mod
        wkvo_ref[0, i] = s_new
        o_t_ref[1, 0, pl.ds(i, 1), :] = jnp.sum(s_new * r, axis=0, keepdims=True)
        return carry

    lax.fori_loop(0, HEAD_DIM, wkv_row, 0, unroll=4)

    v, q, k = head_rows(_F_V), head_rows(_F_Q), head_rows(_F_K)
    g = sdec_ref[h]

    def ret_row(d, acc):
        s = ret_ref[0, d]
        reto_ref[0, d] = g * s + head_row(_F_K, d) * v
        return acc + head_row(_F_QD, d) * s

    cross = lax.fori_loop(0, HEAD_DIM, ret_row, jnp.zeros_like(v), unroll=4)
    o_t_ref[0, 0] = cross + jnp.sum(q * k, axis=0, keepdims=True) * v


def _mixer_sample_post_kernel(
        x_ref, feat_ref, o_t_ref, retgn_ref, lnw_ref, lnb_ref, wout_ref, ones_ref, xo_ref):
    ones_blk = ones_ref[...]
    g = feat_ref[:, _F_G * HEADS_W:(_F_G + 1) * HEADS_W]
    gate = feat_ref[:, _F_GATE * HEADS_W:(_F_GATE + 1) * HEADS_W]
    bonus = feat_ref[:, _F_BONUS * HEADS_W:(_F_BONUS + 1) * HEADS_W]
    o = o_t_ref[...].T
    ret = _head_norm(o[:, 0:HEADS_W], GN_EPS, ones_blk) * retgn_ref[...]
    ret_out = g * _sigmoid(g) * ret
    yn = _head_norm(o[:, HEADS_W:2 * HEADS_W], WKV_GN_EPS, ones_blk) * lnw_ref[...] + lnb_ref[...]
    wkv_out = (yn + bonus) * gate
    cat = jnp.concatenate([ret_out, wkv_out], axis=1)
    xo_ref[...] = x_ref[...] + _dot(cat, wout_ref[...])


def _mixer_sample(x, ret_t, wkv_t, shift0, lw):
    n = x.shape[0]
    cos_t, sin_t = _rope_tables(np.full((1,), PAST_LEN))
    qdec, _, _, _, cdec = _retention_tables(1)
    _, _, _, _, ones_blk = _block_masks(1, 1)
    args = [x, shift0, lw["norm"], lw["w_in"], cos_t, sin_t, qdec, lw["mu"], lw["w0"], lw["lora"],
            lw["a0"], lw["g_b"], lw["k_k"], lw["k_a"], lw["r_k"], ones_blk]
    feat, feat_t, new_shift = pl.pallas_call(
        _mixer_sample_pre_kernel,
        grid=(1,),
        in_specs=[_const_spec(a.shape) for a in args],
        out_specs=(_const_out((n, _N_ROW_FEATS * HEADS_W)), _const_out((_N_STATE_FEATS * HEADS_W, n)),
                   _const_out((n, SHIFT_W))),
        out_shape=(jax.ShapeDtypeStruct((n, _N_ROW_FEATS * HEADS_W), F32),
                   jax.ShapeDtypeStruct((_N_STATE_FEATS * HEADS_W, n), F32),
                   jax.ShapeDtypeStruct((n, SHIFT_W), F32)),
        compiler_params=pltpu.CompilerParams(vmem_limit_bytes=VMEM_LIMIT),
        name="mixer_sample_pre",
    )(*args)

    sdec = _const(np.broadcast_to(cdec[:, None, None], (N_HEADS, 1, n)))
    state_spec = pl.BlockSpec((1, HEAD_DIM, HEAD_DIM, n), lambda h: (h, 0, 0, 0))
    o_spec = pl.BlockSpec((2, 1, HEAD_DIM, n), lambda h: (0, h, 0, 0))
    o_t, ret_new, wkv_new = pl.pallas_call(
        _mixer_sample_state_kernel,
        grid=(N_HEADS,),
        in_specs=[_const_spec(feat_t.shape), _const_spec(sdec.shape), state_spec, state_spec],
        out_specs=(o_spec, state_spec, state_spec),
        out_shape=(jax.ShapeDtypeStruct((2, N_HEADS, HEAD_DIM, n), F32),
                   jax.ShapeDtypeStruct(ret_t.shape, F32), jax.ShapeDtypeStruct(wkv_t.shape, F32)),
        compiler_params=pltpu.CompilerParams(dimension_semantics=("arbitrary",),
                                             vmem_limit_bytes=VMEM_LIMIT),
        name="mixer_sample_state",
    )(feat_t, sdec, ret_t, wkv_t)

    args = [x, feat, o_t.reshape(2 * HEADS_W, n), lw["ret_gn"], lw["ln_w"], lw["ln_b"], lw["w_out"], ones_blk]
    x1 = pl.pallas_call(
        _mixer_sample_post_kernel,
        grid=(1,),
        in_specs=[_const_spec(a.shape) for a in args],
        out_specs=_const_out((n, D_MODEL)),
        out_shape=jax.ShapeDtypeStruct((n, D_MODEL), F32),
        compiler_params=pltpu.CompilerParams(vmem_limit_bytes=VMEM_LIMIT),
        name="mixer_sample_post",
    )(*args)
    return x1, ret_new, wkv_new, new_shift


def _mlp_kernel(xa_ref, xb_ref, normw_ref, wup_ref, wdown_ref, normf_ref, oa_ref, ob_ref,
                *, final_norm, steps_a):
    i = pl.program_id(0)

    def run(x_ref, o_ref):
        x = x_ref[...]
        hn = _rms(x, normw_ref[...]).astype(BF16)
        acc = x
        for c in range(D_FF // FF_CHUNK):
            sl = slice(c * FF_CHUNK, (c + 1) * FF_CHUNK)
            hid = jnp.dot(hn, wup_ref[:, sl].astype(BF16), preferred_element_type=F32)
            hid = jnp.square(jnp.maximum(hid, 0.0)).astype(BF16)
            acc = acc + jnp.dot(hid, wdown_ref[sl, :].astype(BF16), preferred_element_type=F32)
        if final_norm:
            acc = _rms(acc, normf_ref[...])
        o_ref[...] = acc

    @pl.when(i < steps_a)
    def _():
        run(xa_ref, oa_ref)

    @pl.when(i == steps_a)
    def _():
        run(xb_ref, ob_ref)


def _mlp(xa, xb, norm_w, w_up, w_down, norm_f, layer, final_norm):
    rows_a, rows_b = xa.shape[0], xb.shape[0]
    steps_a = rows_a // MLP_ROWS
    pick = lambda *_: (layer, 0, 0)
    block_a = pl.BlockSpec((MLP_ROWS, D_MODEL), lambda i: (jnp.minimum(i, steps_a - 1), 0))
    return pl.pallas_call(
        functools.partial(_mlp_kernel, final_norm=final_norm, steps_a=steps_a),
        grid=(steps_a + 1,),
        in_specs=[block_a, _const_spec((rows_b, D_MODEL)),
                  _const_spec((1, D_MODEL)),
                  pl.BlockSpec((None, D_MODEL, D_FF), pick, pipeline_mode=pl.Buffered(1)),
                  pl.BlockSpec((None, D_FF, D_MODEL), pick, pipeline_mode=pl.Buffered(1)),
                  _const_spec((1, D_MODEL))],
        out_specs=(block_a, _const_out((rows_b, D_MODEL))),
        out_shape=(jax.ShapeDtypeStruct((rows_a, D_MODEL), F32), jax.ShapeDtypeStruct((rows_b, D_MODEL), F32)),
        compiler_params=pltpu.CompilerParams(dimension_semantics=("arbitrary",),
                                             vmem_limit_bytes=VMEM_LIMIT),
        name="mlp",
    )(xa, xb, norm_w, w_up, w_down, norm_f)


def _ssm_prep_kernel(lre_ref, lim_ref, logdt_ref, bre_ref, bim_ref, are_ref, aim_ref, bbre_ref, bbim_ref):
    lre = jnp.minimum(lre_ref[...], -1e-4)
    lim = lim_ref[...]
    dt = jnp.exp(logdt_ref[...])
    mag = jnp.exp(lre * dt)
    are = mag * jnp.cos(lim * dt)
    aim = mag * jnp.sin(lim * dt)
    are_ref[...] = are
    aim_ref[...] = aim
    den = lre * lre + lim * lim
    nre = are - 1.0
    cre = (nre * lre + aim * lim) / den
    cim = (aim * lre - nre * lim) / den
    bre = bre_ref[...]
    bim = bim_ref[...]
    bbre_ref[...] = cre * bre - cim * bim
    bbim_ref[...] = cre * bim + cim * bre


def _gelu_exact(x):
    return 0.5 * x * (1.0 + lax.erf(x * (2.0 ** -0.5)))


def _ssm_kernel(x_ref, hre0_ref, him0_ref, normw_ref, are_ref, aim_ref, wb_ref, cre_ref, cim_ref, dskip_ref,
                wglu_ref, xo_ref, hre_ref, him_ref, xt_s, u_s, bu_s, y_s, hg_s, hre_s, him_s,
                *, n_seq, chunk, batch_major):
    i = pl.program_id(0)
    n_steps = pl.num_programs(0)
    rows_all = chunk * n_seq
    n_slabs = D_MODEL // 128

    @pl.when(i == 0)
    def _():
        hre_s[...] = hre0_ref[...]
        him_s[...] = him0_ref[...]

    if batch_major:
        for b in range(n_seq):
            for sl in range(n_slabs):
                xt_s[sl, pl.ds(b, chunk, stride=n_seq), :] = x_ref[b, :, sl * 128:(sl + 1) * 128]
        ssq = sum(jnp.sum(jnp.square(xt_s[sl]), axis=-1, keepdims=True) for sl in range(n_slabs))
        inv = lax.rsqrt(ssq * (1.0 / D_MODEL) + RMS_EPS)
        for sl in range(n_slabs):
            cols = slice(sl * 128, (sl + 1) * 128)
            u_s[:, cols] = xt_s[sl] * inv * normw_ref[:, cols]
    else:
        u_s[...] = _rms(x_ref[...].reshape(rows_all, D_MODEL), normw_ref[...])

    def input_proj(blk, part):
        c = slice(part * SSM_HALF, (part + 1) * SSM_HALF)
        bu_s[blk % 2, :, c] = _dot(u_s[:, blk * 128:(blk + 1) * 128], wb_ref[blk, :, c])

    def output_proj(blk, part):
        buf = bu_s.at[blk % 2]
        cols = slice(blk * 128, (blk + 1) * 128)
        if part == 0:
            y_s[:, cols] = _dot(buf[:, 0:SSM_HALF], cre_ref[blk])
        else:
            y_s[:, cols] = y_s[:, cols] - _dot(buf[:, SSM_HALF:2 * SSM_HALF], cim_ref[blk])

    input_proj(0, 0)
    input_proj(0, 1)
    for blk in range(SSM_BLOCKS):
        neighbours = []
        if blk >= 1:
            neighbours += [functools.partial(output_proj, blk - 1, 0), functools.partial(output_proj, blk - 1, 1)]
        if blk + 1 < SSM_BLOCKS:
            neighbours += [functools.partial(input_proj, blk + 1, 0), functools.partial(input_proj, blk + 1, 1)]
        buf = bu_s.at[blk % 2]
        cols = slice(blk * SSM_HALF, (blk + 1) * SSM_HALF)
        a_re = jnp.broadcast_to(are_ref[:, cols], (n_seq, SSM_HALF))
        a_im = jnp.broadcast_to(aim_ref[:, cols], (n_seq, SSM_HALF))
        h_re = hre_s[:, cols]
        h_im = him_s[:, cols]
        seg = -(-chunk // max(len(neighbours), 1))
        for t in range(chunk):
            if t % seg == 0 and neighbours:
                neighbours.pop(0)()
            rows = slice(t * n_seq, (t + 1) * n_seq)
            n_re = a_re * h_re - a_im * h_im + buf[rows, 0:SSM_HALF]
            n_im = a_re * h_im + a_im * h_re + buf[rows, SSM_HALF:2 * SSM_HALF]
            buf[rows, 0:SSM_HALF] = n_re
            buf[rows, SSM_HALF:2 * SSM_HALF] = n_im
            h_re, h_im = n_re, n_im
        for f in neighbours:
            f()
        hre_s[:, cols] = h_re
        him_s[:, cols] = h_im
    output_proj(SSM_BLOCKS - 1, 0)
    output_proj(SSM_BLOCKS - 1, 1)
    hg_s[...] = _gelu_exact(y_s[...] + dskip_ref[...] * u_s[...]).astype(BF16)
    glu_w = 256
    for c in range(D_MODEL // glu_w):
        cols = slice(c * glu_w, (c + 1) * glu_w)
        hg = hg_s[...]
        val = jnp.dot(hg, wglu_ref[:, cols], preferred_element_type=F32)
        gate = jnp.dot(hg, wglu_ref[:, D_MODEL + c * glu_w:D_MODEL + (c + 1) * glu_w],
                       preferred_element_type=F32)
        out = val * _sigmoid(gate)
        if batch_major:
            for sl in range(c * glu_w // 128, (c + 1) * glu_w // 128):
                xt_s[sl] = xt_s[sl] + out[:, sl * 128 - c * glu_w:(sl + 1) * 128 - c * glu_w]
        else:
            y_s[:, cols] = x_ref[...].reshape(rows_all, D_MODEL)[:, cols] + out
    if batch_major:
        for b in range(n_seq):
            for sl in range(n_slabs):
                xo_ref[b, :, sl * 128:(sl + 1) * 128] = xt_s[sl, pl.ds(b, chunk, stride=n_seq), :]
    else:
        xo_ref[...] = y_s[...].reshape(chunk, n_seq, D_MODEL)

    @pl.when(i == n_steps - 1)
    def _():
        hre_ref[...] = hre_s[...]
        him_ref[...] = him_s[...]


def _ssm_weights(lam_re, lam_im, log_dt, b_re, b_im, c_re, c_im):
    g, p = SSM_GROUPS, SSM_P
    n = g * SSM_GROUP
    rep = lambda z: jnp.repeat(z, SSM_GROUP, axis=0)
    bt_re = jnp.swapaxes(b_re, 1, 2).reshape(n, p)
    bt_im = jnp.swapaxes(b_im, 1, 2).reshape(n, p)
    args = [rep(lam_re), rep(lam_im), rep(log_dt.reshape(g, 1)), bt_re, bt_im]
    a_re, a_im, bb_re, bb_im = pl.pallas_call(
        _ssm_prep_kernel,
        grid=(1,),
        in_specs=[_const_spec(a.shape) for a in args],
        out_specs=tuple(_const_out((n, p)) for _ in range(4)),
        out_shape=tuple(jax.ShapeDtypeStruct((n, p), F32) for _ in range(4)),
        name="ssm_prep",
    )(*args)
    a_re = a_re[::SSM_GROUP]
    a_im = a_im[::SSM_GROUP]
    eye = jnp.eye(SSM_BLOCK_G, dtype=F32)

    def in_block(bb):
        bb = bb.reshape(SSM_BLOCKS, SSM_BLOCK_G, SSM_GROUP, p)
        return jnp.einsum("bgcp,gh->bgchp", bb, eye).reshape(SSM_BLOCKS, 128, SSM_HALF)

    def out_block(cc):
        cc = cc.reshape(SSM_BLOCKS, SSM_BLOCK_G, SSM_GROUP, p)
        return jnp.einsum("bgcp,gh->bgphc", cc, eye).reshape(SSM_BLOCKS, SSM_HALF, 128)

    w_b = jnp.concatenate([in_block(bb_re), in_block(bb_im)], axis=2).astype(BF16)
    n_state = SSM_GROUPS * SSM_P
    return (a_re.reshape(1, n_state), a_im.reshape(1, n_state), w_b,
            out_block(c_re).astype(BF16), out_block(c_im).astype(BF16))


def _ssm_layer(x, h_re0, h_im0, sw, chunk, batch_major):
    if batch_major:
        n_seq, t_len, _ = x.shape
        x_block = (n_seq, chunk, D_MODEL)
        x_map = lambda i: (0, i, 0)
    else:
        t_len, n_seq, _ = x.shape
        x_block = (chunk, n_seq, D_MODEL)
        x_map = lambda i: (i, 0, 0)
    rows = chunk * n_seq
    n_state = SSM_GROUPS * SSM_P
    args = [x, h_re0, h_im0, sw["norm"], sw["a_re"], sw["a_im"], sw["w_b"], sw["c_re"], sw["c_im"],
            sw["d_skip"], sw["w_glu"]]
    in_specs = [pl.BlockSpec(x_block, x_map)] + [_const_spec(a.shape) for a in args[1:]]
    state = jax.ShapeDtypeStruct((n_seq, n_state), F32)
    return pl.pallas_call(
        functools.partial(_ssm_kernel, n_seq=n_seq, chunk=chunk, batch_major=batch_major),
        grid=(t_len // chunk,),
        in_specs=in_specs,
        out_specs=(pl.BlockSpec(x_block, x_map), _const_out(state.shape), _const_out(state.shape)),
        out_shape=(jax.ShapeDtypeStruct(x.shape, F32), state, state),
        scratch_shapes=[pltpu.VMEM((D_MODEL // 128, rows, 128), F32),
                        pltpu.VMEM((rows, D_MODEL), F32), pltpu.VMEM((2, rows, 2 * SSM_HALF), F32),
                        pltpu.VMEM((rows, D_MODEL), F32), pltpu.VMEM((rows, D_MODEL), BF16),
                        pltpu.VMEM((n_seq, n_state), F32), pltpu.VMEM((n_seq, n_state), F32)],
        compiler_params=pltpu.CompilerParams(dimension_semantics=("arbitrary",),
                                             vmem_limit_bytes=VMEM_LIMIT),
        name="ssm_layer",
    )(*args)


def kernel(x_prompt, x_sample, state_ret, state_wkv, state_shift, state_ssm_re, state_ssm_im, norm_mix, w_in, ret_gn, mu_shift, wkv_w0, wkv_wB, wkv_a0, wkv_aB, wkv_gB, wkv_kk, wkv_ka, wkv_rk, wkv_ln_w, wkv_ln_b, w_out, ssm_lambda_re, ssm_lambda_im, ssm_log_dt, ssm_B_re, ssm_B_im, ssm_C_re, ssm_C_im, ssm_D, ssm_w_glu, mlp_norm, mlp_up, mlp_down, norm_f):
    lw = dict(
        norm=_row(norm_mix[0]), w_in=w_in[0].astype(BF16), ret_gn=_row(ret_gn[0]), mu=_row(mu_shift[0]),
        w0=_row(wkv_w0[0]), lora=_lora_block(wkv_wB[0], wkv_aB[0]), a0=_row(wkv_a0[0]),
        g_b=wkv_gB[0].astype(BF16), k_k=_row(wkv_kk[0]), k_a=_row(wkv_ka[0]), r_k=_row(wkv_rk[0]),
        ln_w=_row(wkv_ln_w[0]), ln_b=_row(wkv_ln_b[0]), w_out=w_out[0].astype(BF16))
    a_re, a_im, w_b, c_re, c_im = _ssm_weights(ssm_lambda_re[0], ssm_lambda_im[0], ssm_log_dt[0],
                                               ssm_B_re[0], ssm_B_im[0], ssm_C_re[0], ssm_C_im[0])
    sw = dict(norm=_row(norm_mix[1]), a_re=a_re, a_im=a_im, w_b=w_b, c_re=c_re, c_im=c_im,
              d_skip=_row(ssm_D[0]), w_glu=ssm_w_glu[0].astype(BF16))
    n_state = SSM_GROUPS * SSM_P
    nf = _row(norm_f)

    n_p, t_p, _ = x_prompt.shape
    n_s = x_sample.shape[0]
    x1, ret_p, wkv_p, shift_p = _mixer_prompt(x_prompt, lw)
    seq_last = lambda s: jnp.transpose(s, (1, 2, 3, 0))
    seq_first = lambda s: jnp.transpose(s, (3, 0, 1, 2))
    xs1, ret_s, wkv_s, shift_s = _mixer_sample(x_sample.reshape(n_s, D_MODEL), seq_last(state_ret[0]),
                                               seq_last(state_wkv[0]), state_shift[0], lw)
    ret_s, wkv_s = seq_first(ret_s), seq_first(wkv_s)
    x1, xs1 = _mlp(x1.reshape(n_p * t_p, D_MODEL), xs1, _row(mlp_norm[0]), mlp_up, mlp_down, nf, 0, False)
    zero_state = jnp.zeros((n_p, n_state), F32)
    x2, ssm_re_p, ssm_im_p = _ssm_layer(x1.reshape(n_p, t_p, D_MODEL), zero_state, zero_state, sw,
                                        SSM_CHUNK, True)
    xs2, ssm_re_s, ssm_im_s = _ssm_layer(
        xs1.reshape(1, n_s, D_MODEL), state_ssm_re[0].reshape(n_s, n_state),
        state_ssm_im[0].reshape(n_s, n_state), sw, 1, False)
    y_p, y_s = _mlp(x2.reshape(n_p * t_p, D_MODEL), xs2.reshape(n_s, D_MODEL), _row(mlp_norm[1]),
                    mlp_up, mlp_down, nf, 1, True)
    y_prompt = y_p.reshape(n_p, t_p, D_MODEL)
    ssm_re_p = ssm_re_p.reshape(n_p, SSM_GROUPS, SSM_P)
    ssm_im_p = ssm_im_p.reshape(n_p, SSM_GROUPS, SSM_P)
    ssm_re_s = ssm_re_s.reshape(n_s, SSM_GROUPS, SSM_P)
    ssm_im_s = ssm_im_s.reshape(n_s, SSM_GROUPS, SSM_P)

    return (y_prompt, y_s.reshape(n_s, 1, D_MODEL),
            ret_p[None], wkv_p[None], shift_p[None], ssm_re_p[None], ssm_im_p[None],
            ret_s[None], wkv_s[None], shift_s[None], ssm_re_s[None], ssm_im_s[None])
```

```python
import functools
import math

import numpy as np
import jax
import jax.numpy as jnp
from jax import lax
from jax.experimental import pallas as pl
from jax.experimental.pallas import tpu as pltpu

F32 = jnp.float32
BF16 = jnp.bfloat16

D_MODEL = 1024
N_HEADS = 8
HEAD_DIM = 64
HEADS_W = N_HEADS * HEAD_DIM
N_PAIRS = N_HEADS // 2
PAIR_W = 2 * HEAD_DIM
ROPE_BASE = 10000.0
DECAY_LORA = 64
AAA_LORA = 64
GATE_LORA = 128
SHIFT_W = 3 * HEADS_W + DECAY_LORA + AAA_LORA + GATE_LORA
RET_COLS = 4 * HEADS_W
IN_W = RET_COLS + SHIFT_W
SSM_GROUP = 16
SSM_GROUPS = D_MODEL // SSM_GROUP
SSM_P = 64
SSM_BLOCKS = 8
SSM_BLOCK_G = SSM_GROUPS // SSM_BLOCKS
SSM_HALF = SSM_BLOCK_G * SSM_P
SSM_STATE_W = 2 * SSM_BLOCKS * SSM_HALF
D_FF = 4 * D_MODEL
RMS_EPS = 1e-6
GN_EPS = 1e-5
WKV_GN_EPS = 64e-5
PAST_LEN = 16384

MIX_CHUNK = 64
PROJ_PIECE = 256
MIX_GROUP = 4
SSM_CHUNK = 64
MLP_ROWS = 512
FF_CHUNK = 1024

VMEM_LIMIT = 58 * 1024 * 1024


def _dot(a, b):
    return jnp.dot(a.astype(BF16), b.astype(BF16), preferred_element_type=F32)


def _dot_nt(a, b):
    return lax.dot_general(a.astype(BF16), b.astype(BF16), (((1,), (1,)), ((), ())),
                           preferred_element_type=F32)


def _dot_tn(a, b):
    return lax.dot_general(a.astype(BF16), b.astype(BF16), (((0,), (0,)), ((), ())),
                           preferred_element_type=F32)


def _split3(x):
    hi = x.astype(BF16)
    r1 = x - hi.astype(F32)
    mid = r1.astype(BF16)
    lo = (r1 - mid.astype(F32)).astype(BF16)
    return hi, mid, lo


def _dot_exact_lhs(a_bf16, x):
    hi, mid, lo = _split3(x)
    f = lambda p: jnp.dot(a_bf16, p, preferred_element_type=F32)
    return f(hi) + f(mid) + f(lo)


def _segsum(x, ones_blk):
    xb = x.astype(BF16)
    outs = [jnp.dot(xb[:, c * 256:(c + 1) * 256], ones_blk, preferred_element_type=F32)
            for c in range(x.shape[1] // 256)]
    return jnp.concatenate(outs, axis=1)


def _rms(x, w):
    return x * lax.rsqrt(jnp.mean(x * x, axis=-1, keepdims=True) + RMS_EPS) * w


def _sigmoid(x):
    return 1.0 / (1.0 + jnp.exp(-x))


def _softplus(x):
    return jnp.maximum(x, 0.0) + jnp.log1p(jnp.exp(-jnp.abs(x)))


def _head_norm(z, eps, ones_blk):
    mu = _segsum(z, ones_blk) * (1.0 / HEAD_DIM)
    zc = z - mu
    var = _segsum(zc * zc, ones_blk) * (1.0 / HEAD_DIM)
    return zc * lax.rsqrt(var + eps)


def _rope(z, cos, sin_signed):
    lane = lax.broadcasted_iota(jnp.int32, (1, HEADS_W), 1) % HEAD_DIM
    swapped = jnp.where(lane < HEAD_DIM // 2,
                        pltpu.roll(z, HEADS_W - HEAD_DIM // 2, axis=1),
                        pltpu.roll(z, HEAD_DIM // 2, axis=1))
    return z * cos + swapped * sin_signed


def _wkv_features(xs, w0, lora_w, a0, g_b, k_k, k_a, r_k, ones_blk, between=lambda: None):
    r = xs[:, 0:HEADS_W]
    kw = xs[:, HEADS_W:2 * HEADS_W]
    vw = xs[:, 2 * HEADS_W:3 * HEADS_W]
    lo = xs[:, 3 * HEADS_W:3 * HEADS_W + 128]
    lane = lax.broadcasted_iota(jnp.int32, (1, 128), 1)
    lo = jnp.where(lane < DECAY_LORA, jnp.tanh(lo), lo)
    ll = _dot(lo, lora_w)
    w_log = -_softplus(-(w0 + ll[:, 0:HEADS_W])) - 0.5
    log_decay = -jnp.exp(w_log)
    between()
    alr = _sigmoid(a0 + ll[:, HEADS_W:2 * HEADS_W])
    gate = _dot(_sigmoid(xs[:, 3 * HEADS_W + 128:SHIFT_W]), g_b)
    between()
    kk = kw * k_k
    kk = kk / jnp.maximum(jnp.sqrt(_segsum(kk * kk, ones_blk)), 1e-12)
    k_mod = kw * (1.0 + (alr - 1.0) * k_a)
    between()
    bonus = _segsum(r * k_mod * r_k, ones_blk) * vw
    return r, log_decay, k_mod, vw, -kk, kk * alr, gate, bonus


def _stack_masked(x2, m0):
    return jnp.concatenate([jnp.where(m0, x2, 0.0), jnp.where(m0, 0.0, x2)], axis=0)


def _stack_dup(x2):
    return jnp.concatenate([x2, x2], axis=0)


def _mixer_prompt_kernel(
        xprev_ref, xnext_ref, normw_ref, win_ref, cos_ref, sin_ref, qdec_ref, kdec_ref, dmask_ref, sdec_ref,
        retgn_ref, mu_ref, w0_ref, lora_ref, a0_ref, gb_ref, kk_ref, ka_ref, rk_ref, lnw_ref,
        lnb_ref, wout_ref, ones_ref, tril_ref, strict_ref, incl_ref, bd_ref,
        xo_ref, rets_ref, wkvs_ref, shift_ref,
        p_s, cat_s, rs_s, ws_s, carry_s, *, n_seq, chunk, group):
    i = pl.program_id(0)
    n_chunks = pl.num_programs(0) - 1
    C = chunk
    slot = i % 2

    @pl.when(i == 0)
    def _():
        rs_s[...] = jnp.zeros_like(rs_s)
        ws_s[...] = jnp.zeros_like(ws_s)
        carry_s[...] = jnp.zeros_like(carry_s)
        cat_s[...] = jnp.zeros_like(cat_s)
        p_s[0] = _dot(_rms(xprev_ref[...].reshape(n_seq * C, D_MODEL), normw_ref[...]), win_ref[...])

    m0 = lax.broadcasted_iota(jnp.int32, (1, PAIR_W), 1) < HEAD_DIM
    ones_blk = ones_ref[...]
    NB = group
    R = NB * C
    row_id = lax.broadcasted_iota(jnp.int32, (R, 1), 0)
    tile_rows = lambda ref: jnp.concatenate([ref[...]] * NB, axis=0)
    pairs = range(N_PAIRS)
    sls = [slice(pr * PAIR_W, (pr + 1) * PAIR_W) for pr in pairs]
    chains = [(s, pr) for s in range(NB) for pr in pairs]
    seq_rows = [slice(s * C, (s + 1) * C) for s in range(NB)]

    def per_group(gi, carry):
        rows = pl.ds(pl.multiple_of(gi * R, R), R)
        b0 = gi * NB
        ret_states = {(s, pr): rs_s[pr, b0 + s] for s, pr in chains}
        wkv_states = {(s, pr): ws_s[pr, b0 + s] for s, pr in chains}
        shift_rows = [carry_s[pl.ds(b0 + s, 1), :] for s in range(NB)]
        p_cur = p_s.at[slot]
        wp = p_cur[rows, RET_COLS:IN_W]
        hn_next = _rms(xnext_ref[pl.ds(b0, NB)].reshape(R, D_MODEL), normw_ref[...]).astype(BF16)
        cat_prev = cat_s[rows, :]

        def next_in_proj(c0):
            p_s[1 - slot, rows, c0:c0 + PROJ_PIECE] = jnp.dot(
                hn_next, win_ref[:, c0:c0 + PROJ_PIECE], preferred_element_type=F32)

        def prev_out_proj(c0):
            cols = slice(c0, c0 + PROJ_PIECE)
            out = jnp.dot(cat_prev, wout_ref[:, cols], preferred_element_type=F32)
            xo_ref[pl.ds(b0, NB), :, cols] = xprev_ref[pl.ds(b0, NB), :, cols] + out.reshape(NB, C, PROJ_PIECE)

        pieces = [functools.partial(next_in_proj, c0) for c0 in range(0, IN_W, PROJ_PIECE)]
        pieces += [functools.partial(prev_out_proj, c0) for c0 in range(0, D_MODEL, PROJ_PIECE)]

        def fill(n=1):
            for _ in range(min(n, len(pieces))):
                pieces.pop(0)()

        cos, sin = tile_rows(cos_ref), tile_rows(sin_ref)
        q = _rope(p_cur[rows, 0:HEADS_W], cos, sin)
        fill()
        k = _rope(p_cur[rows, HEADS_W:2 * HEADS_W], cos, sin) * (HEAD_DIM ** -0.5)
        fill()
        v = p_cur[rows, 2 * HEADS_W:3 * HEADS_W]
        g = p_cur[rows, 3 * HEADS_W:4 * HEADS_W]
        qd = q * tile_rows(qdec_ref)
        kd = k * tile_rows(kdec_ref)
        fill()
        cut = lambda z, s, pr: z[seq_rows[s], sls[pr]]
        r_sc = {c: _dot_nt(_stack_masked(cut(q, *c), m0), _stack_dup(cut(k, *c))) * dmask_ref[c[1]]
                for c in chains}
        r_inner = {c: _dot(r_sc[c], _stack_dup(cut(v, *c))) for c in chains}
        r_cross = {c: _dot(cut(qd, *c), ret_states[c]) for c in chains}
        new_ret = {c: sdec_ref[c[1]] * ret_states[c] + _dot_tn(cut(kd, *c), cut(v, *c)) * bd_ref[...]
                   for c in chains}
        o = jnp.concatenate(
            [jnp.concatenate([jnp.where(m0, r_inner[(s, pr)][0:C], r_inner[(s, pr)][C:2 * C])
                              + r_cross[(s, pr)] for pr in pairs], axis=1) for s in range(NB)], axis=0)
        fill()
        ret = _head_norm(o, GN_EPS, ones_blk) * retgn_ref[...]
        ret_out = (g * _sigmoid(g) * ret).astype(BF16)
        fill()

        prev = pltpu.roll(wp, 1, axis=0)
        for s in range(NB):
            prev = jnp.where(row_id == s * C, shift_rows[s], prev)
        xs = wp + (prev - wp) * mu_ref[...]
        fill()
        r, lw, k_mod, vw, a_vec, b_vec, gate, bonus = _wkv_features(
            xs, w0_ref[...], lora_ref[...], a0_ref[...], gb_ref[...], kk_ref[...], ka_ref[...],
            rk_ref[...], ones_blk, fill)
        cw = _dot_exact_lhs(tril_ref[...], lw)
        fill()
        cw_last = [cw[s * C + C - 1:(s + 1) * C, :] for s in range(NB)]
        cwl = jnp.concatenate([jnp.broadcast_to(z, (C, HEADS_W)) for z in cw_last], axis=0)
        r_t = r * jnp.exp(cw)
        a_t = a_vec * jnp.exp(cw - lw)
        fill()
        w_inv = jnp.exp(-cw)
        b_t = b_vec * w_inv
        k_t = k_mod * w_inv
        fill()
        w_end = jnp.exp(cwl - cw)
        b_h = b_vec * w_end
        k_h = k_mod * w_end
        fill()
        w_all = [jnp.exp(z) for z in cw_last]
        lhs = {c: jnp.concatenate([_stack_masked(cut(a_t, *c), m0), _stack_masked(cut(r_t, *c), m0)], axis=0)
               for c in chains}
        fill(len(pieces) - 2)
        sc = {c: _dot_nt(lhs[c], jnp.concatenate([_stack_dup(cut(b_t, *c)), _stack_dup(cut(k_t, *c))], axis=0))
              for c in chains}
        on_state = {c: _dot_nt(lhs[c], wkv_states[c]) for c in chains}
        vv = {c: _stack_dup(cut(vw, *c)) for c in chains}
        n_pow = {c: sc[c][0:2 * C, 0:2 * C] * strict_ref[...] for c in chains}
        u = {c: on_state[c][0:2 * C] + _dot(sc[c][0:2 * C, 2 * C:4 * C] * strict_ref[...], vv[c])
             for c in chains}
        n_steps_solve = int(math.log2(C))
        for it in range(n_steps_solve):
            u = {c: u[c] + _dot(n_pow[c], u[c]) for c in chains}
            if it + 1 < n_steps_solve:
                n_pow = {c: _dot(n_pow[c], n_pow[c]) for c in chains}
        uv = {c: jnp.concatenate([u[c], vv[c]], axis=0) for c in chains}
        y_st = {c: on_state[c][2 * C:4 * C] + _dot(
            jnp.concatenate([sc[c][2 * C:4 * C, 0:2 * C] * incl_ref[...],
                             sc[c][2 * C:4 * C, 2 * C:4 * C] * incl_ref[...]], axis=1), uv[c]) for c in chains}
        new_wkv = {c: wkv_states[c] * w_all[c[0]][:, sls[c[1]]] + bd_ref[...] * _dot_tn(
            uv[c], jnp.concatenate([_stack_masked(cut(b_h, *c), m0), _stack_masked(cut(k_h, *c), m0)], axis=0))
            for c in chains}
        y = jnp.concatenate(
            [jnp.concatenate([jnp.where(m0, y_st[(s, pr)][0:C], y_st[(s, pr)][C:2 * C]) for pr in pairs], axis=1)
             for s in range(NB)], axis=0)
        fill(len(pieces))
        yn = _head_norm(y, WKV_GN_EPS, ones_blk) * lnw_ref[...] + lnb_ref[...]
        cat_s[rows, 0:HEADS_W] = ret_out
        cat_s[rows, HEADS_W:2 * HEADS_W] = ((yn + bonus) * gate).astype(BF16)
        for s in range(NB):
            carry_s[pl.ds(b0 + s, 1), :] = wp[s * C + C - 1:(s + 1) * C, :]
        for s, pr in chains:
            rs_s[pr, b0 + s] = new_ret[(s, pr)]
            ws_s[pr, b0 + s] = new_wkv[(s, pr)]
        return carry

    @pl.when(i < n_chunks)
    def _():
        lax.fori_loop(0, n_seq // NB, per_group, 0)

    @pl.when(i == n_chunks)
    def _():
        out = jnp.dot(cat_s[...], wout_ref[...], preferred_element_type=F32)
        xo_ref[...] = xprev_ref[...] + out.reshape(n_seq, C, D_MODEL)

    @pl.when(i == n_chunks - 1)
    def _():
        shift_ref[...] = carry_s[...]
        for b in range(n_seq):
            for pr in range(N_PAIRS):
                rs = rs_s[pr, b]
                ws = ws_s[pr, b]
                rets_ref[b, 2 * pr] = rs[0:HEAD_DIM, 0:HEAD_DIM]
                rets_ref[b, 2 * pr + 1] = rs[HEAD_DIM:PAIR_W, HEAD_DIM:PAIR_W]
                wkvs_ref[b, 2 * pr] = ws[0:HEAD_DIM, 0:HEAD_DIM]
                wkvs_ref[b, 2 * pr + 1] = ws[HEAD_DIM:PAIR_W, HEAD_DIM:PAIR_W]


def _const_spec(shape):
    nd = len(shape)
    return pl.BlockSpec(shape, lambda *_: (0,) * nd, pipeline_mode=pl.Buffered(1))


def _const_out(shape):
    nd = len(shape)
    return pl.BlockSpec(shape, lambda *_: (0,) * nd)


def _const(a, dtype=F32):
    return jnp.asarray(np.asarray(a, np.float64), dtype=dtype)


def _retention_tables(chunk):
    log_g = np.log1p(-np.exp2(-5.0 - np.arange(N_HEADS, dtype=np.float64)))
    lane_g = np.repeat(log_g, HEAD_DIM)[None, :]
    idx = np.arange(chunk, dtype=np.float64)
    qdec = np.exp((idx + 1.0)[:, None] * lane_g)
    kdec = np.exp((chunk - 1.0 - idx)[:, None] * lane_g)
    rel = idx[:, None] - idx[None, :]
    dm = np.where(rel >= 0, np.exp(np.maximum(rel, 0.0)[None] * log_g[:, None, None]), 0.0)
    zero = np.zeros((chunk, chunk))
    dmask = np.stack([np.block([[dm[2 * p], zero], [zero, dm[2 * p + 1]]]) for p in range(N_PAIRS)])
    cdec = np.exp(chunk * log_g)
    hz = np.zeros((HEAD_DIM, HEAD_DIM))
    ho = np.ones((HEAD_DIM, HEAD_DIM))
    sdec = np.stack([np.block([[cdec[2 * p] * ho, hz], [hz, cdec[2 * p + 1] * ho]])
                     for p in range(N_PAIRS)])
    return _const(qdec), _const(kdec), _const(dmask), _const(sdec), cdec


def _rope_tables(pos):
    half = HEAD_DIM // 2
    inv_freq = ROPE_BASE ** (-np.arange(half, dtype=np.float64) / half)
    ang = np.asarray(pos, np.float64)[:, None] * inv_freq[None, :]
    cos = np.cos(ang)
    sin = np.sin(ang)
    cos_t = np.tile(np.concatenate([cos, cos], axis=1), (1, N_HEADS))
    sin_t = np.tile(np.concatenate([-sin, sin], axis=1), (1, N_HEADS))
    return _const(cos_t), _const(sin_t)


def _block_masks(chunk, group):
    i = np.arange(2 * chunk)
    same = (i[:, None] // chunk) == (i[None, :] // chunk)
    strict = same & (i[:, None] > i[None, :])
    incl = same & (i[:, None] >= i[None, :])
    j = np.arange(PAIR_W)
    bd = (j[:, None] // HEAD_DIM) == (j[None, :] // HEAD_DIM)
    t = np.arange(chunk)
    tril = np.kron(np.eye(group), t[:, None] >= t[None, :])
    o = np.arange(256)
    ones_blk = (o[:, None] // HEAD_DIM) == (o[None, :] // HEAD_DIM)
    return _const(strict), _const(incl), _const(bd), _const(tril, BF16), _const(ones_blk, BF16)


def _lora_block(w_b, a_b):
    z = jnp.zeros_like(w_b)
    return jnp.concatenate([jnp.concatenate([w_b, z], axis=1),
                            jnp.concatenate([z, a_b], axis=1)], axis=0).astype(BF16)


def _row(v):
    return v.reshape(1, -1).astype(F32)


def _mixer_prompt(x, lw):
    n_seq, t_len, _ = x.shape
    C = MIX_CHUNK
    n_chunks = t_len // C
    last = n_chunks - 1
    cos_t, sin_t = _rope_tables(np.arange(t_len))
    qdec, kdec, dmask, sdec, _ = _retention_tables(C)
    strict, incl, bd, tril, ones_blk = _block_masks(C, MIX_GROUP)
    prev_chunk = lambda i: (0, jnp.maximum(i - 1, 0), 0)
    in_specs = [
        pl.BlockSpec((n_seq, C, D_MODEL), prev_chunk),
        pl.BlockSpec((n_seq, C, D_MODEL), lambda i: (0, jnp.minimum(i + 1, last), 0)),
        _const_spec((1, D_MODEL)),
        _const_spec((D_MODEL, IN_W)),
        pl.BlockSpec((C, HEADS_W), lambda i: (jnp.minimum(i, last), 0)),
        pl.BlockSpec((C, HEADS_W), lambda i: (jnp.minimum(i, last), 0)),
    ]
    tail = [qdec, kdec, dmask, sdec, lw["ret_gn"], lw["mu"], lw["w0"], lw["lora"], lw["a0"], lw["g_b"],
            lw["k_k"], lw["k_a"], lw["r_k"], lw["ln_w"], lw["ln_b"], lw["w_out"], ones_blk, tril,
            strict, incl, bd]
    in_specs += [_const_spec(a.shape) for a in tail]
    out_shape = (
        jax.ShapeDtypeStruct((n_seq, t_len, D_MODEL), F32),
        jax.ShapeDtypeStruct((n_seq, N_HEADS, HEAD_DIM, HEAD_DIM), F32),
        jax.ShapeDtypeStruct((n_seq, N_HEADS, HEAD_DIM, HEAD_DIM), F32),
        jax.ShapeDtypeStruct((n_seq, SHIFT_W), F32),
    )
    out_specs = (
        pl.BlockSpec((n_seq, C, D_MODEL), prev_chunk),
        _const_out((n_seq, N_HEADS, HEAD_DIM, HEAD_DIM)),
        _const_out((n_seq, N_HEADS, HEAD_DIM, HEAD_DIM)),
        _const_out((n_seq, SHIFT_W)),
    )
    scratch = [
        pltpu.VMEM((2, n_seq * C, IN_W), F32),
        pltpu.VMEM((n_seq * C, 2 * HEADS_W), BF16),
        pltpu.VMEM((N_PAIRS, n_seq, PAIR_W, PAIR_W), F32),
        pltpu.VMEM((N_PAIRS, n_seq, PAIR_W, PAIR_W), F32),
        pltpu.VMEM((n_seq, SHIFT_W), F32),
    ]
    return pl.pallas_call(
        functools.partial(_mixer_prompt_kernel, n_seq=n_seq, chunk=C, group=MIX_GROUP),
        grid=(n_chunks + 1,),
        in_specs=in_specs, out_specs=out_specs, out_shape=out_shape, scratch_shapes=scratch,
        compiler_params=pltpu.CompilerParams(dimension_semantics=("arbitrary",),
                                             vmem_limit_bytes=VMEM_LIMIT),
        name="mixer_prompt",
    )(x, x, lw["norm"], lw["w_in"], cos_t, sin_t, *tail)


def _mixer_sample_pre_kernel(
        x_ref, shift_ref, normw_ref, win_ref, cos_ref, sin_ref, qdec_ref, mu_ref, w0_ref, lora_ref,
        a0_ref, gb_ref, kk_ref, ka_ref, rk_ref, ones_ref, feat_ref, feat_t_ref, newshift_ref):
    x = x_ref[...]
    p = _dot(_rms(x, normw_ref[...]), win_ref[...])
    q = _rope(p[:, 0:HEADS_W], cos_ref[...], sin_ref[...])
    k = _rope(p[:, HEADS_W:2 * HEADS_W], cos_ref[...], sin_ref[...]) * (HEAD_DIM ** -0.5)
    wp = p[:, RET_COLS:IN_W]
    xs = wp + (shift_ref[...] - wp) * mu_ref[...]
    r, lw, k_mod, vw, a_vec, b_vec, gate, bonus = _wkv_features(
        xs, w0_ref[...], lora_ref[...], a0_ref[...], gb_ref[...], kk_ref[...], ka_ref[...],
        rk_ref[...], ones_ref[...])
    newshift_ref[...] = wp
    state_feats = [q, q * qdec_ref[...], k, p[:, 2 * HEADS_W:3 * HEADS_W], r, jnp.exp(lw), k_mod, vw,
                   a_vec, b_vec]
    for n, f in enumerate(state_feats):
        feat_t_ref[n * HEADS_W:(n + 1) * HEADS_W, :] = f.T
    for n, f in enumerate([p[:, 3 * HEADS_W:4 * HEADS_W], gate, bonus]):
        feat_ref[:, n * HEADS_W:(n + 1) * HEADS_W] = f


_F_Q, _F_QD, _F_K, _F_V, _F_R, _F_W, _F_KM, _F_VW, _F_A, _F_B = range(10)
_N_STATE_FEATS = 10
_F_G, _F_GATE, _F_BONUS = range(3)
_N_ROW_FEATS = 3


def _mixer_sample_state_kernel(feat_t_ref, sdec_ref, ret_ref, wkv_ref, o_t_ref, reto_ref, wkvo_ref):
    h = pl.program_id(0)

    def head_rows(n):
        return feat_t_ref[pl.ds(pl.multiple_of(n * HEADS_W + h * HEAD_DIM, HEAD_DIM), HEAD_DIM), :]

    def head_row(n, i):
        return feat_t_ref[pl.ds(n * HEADS_W + h * HEAD_DIM + i, 1), :]

    a, w, b_vec, k_mod, r = (head_rows(n) for n in (_F_A, _F_W, _F_B, _F_KM, _F_R))

    def wkv_row(i, carry):
        s = wkv_ref[0, i]
        sa = jnp.sum(s * a, axis=0, keepdims=True)
        s_new = s * w + sa * b_vec + head_row(_F_VW, i) * k_mod
        wkvo_ref[0, i] = s_new
        o_t_ref[1, 0, pl.ds(i, 1), :] = jnp.sum(s_new * r, axis=0, keepdims=True)
        return carry

    lax.fori_loop(0, HEAD_DIM, wkv_row, 0, unroll=4)

    v, q, k = head_rows(_F_V), head_rows(_F_Q), head_rows(_F_K)
    g = sdec_ref[h]

    def ret_row(d, acc):
        s = ret_ref[0, d]
        reto_ref[0, d] = g * s + head_row(_F_K, d) * v
        return acc + head_row(_F_QD, d) * s

    cross = lax.fori_loop(0, HEAD_DIM, ret_row, jnp.zeros_like(v), unroll=4)
    o_t_ref[0, 0] = cross + jnp.sum(q * k, axis=0, keepdims=True) * v


def _mixer_sample_post_kernel(
        x_ref, feat_ref, o_t_ref, retgn_ref, lnw_ref, lnb_ref, wout_ref, ones_ref, xo_ref):
    ones_blk = ones_ref[...]
    g = feat_ref[:, _F_G * HEADS_W:(_F_G + 1) * HEADS_W]
    gate = feat_ref[:, _F_GATE * HEADS_W:(_F_GATE + 1) * HEADS_W]
    bonus = feat_ref[:, _F_BONUS * HEADS_W:(_F_BONUS + 1) * HEADS_W]
    o = o_t_ref[...].T
    ret = _head_norm(o[:, 0:HEADS_W], GN_EPS, ones_blk) * retgn_ref[...]
    ret_out = g * _sigmoid(g) * ret
    yn = _head_norm(o[:, HEADS_W:2 * HEADS_W], WKV_GN_EPS, ones_blk) * lnw_ref[...] + lnb_ref[...]
    wkv_out = (yn + bonus) * gate
    cat = jnp.concatenate([ret_out, wkv_out], axis=1)
    xo_ref[...] = x_ref[...] + _dot(cat, wout_ref[...])


def _mixer_sample(x, ret_t, wkv_t, shift0, lw):
    n = x.shape[0]
    cos_t, sin_t = _rope_tables(np.full((1,), PAST_LEN))
    qdec, _, _, _, cdec = _retention_tables(1)
    _, _, _, _, ones_blk = _block_masks(1, 1)
    args = [x, shift0, lw["norm"], lw["w_in"], cos_t, sin_t, qdec, lw["mu"], lw["w0"], lw["lora"],
            lw["a0"], lw["g_b"], lw["k_k"], lw["k_a"], lw["r_k"], ones_blk]
    feat, feat_t, new_shift = pl.pallas_call(
        _mixer_sample_pre_kernel,
        grid=(1,),
        in_specs=[_const_spec(a.shape) for a in args],
        out_specs=(_const_out((n, _N_ROW_FEATS * HEADS_W)), _const_out((_N_STATE_FEATS * HEADS_W, n)),
                   _const_out((n, SHIFT_W))),
        out_shape=(jax.ShapeDtypeStruct((n, _N_ROW_FEATS * HEADS_W), F32),
                   jax.ShapeDtypeStruct((_N_STATE_FEATS * HEADS_W, n), F32),
                   jax.ShapeDtypeStruct((n, SHIFT_W), F32)),
        compiler_params=pltpu.CompilerParams(vmem_limit_bytes=VMEM_LIMIT),
        name="mixer_sample_pre",
    )(*args)

    sdec = _const(np.broadcast_to(cdec[:, None, None], (N_HEADS, 1, n)))
    state_spec = pl.BlockSpec((1, HEAD_DIM, HEAD_DIM, n), lambda h: (h, 0, 0, 0))
    o_spec = pl.BlockSpec((2, 1, HEAD_DIM, n), lambda h: (0, h, 0, 0))
    o_t, ret_new, wkv_new = pl.pallas_call(
        _mixer_sample_state_kernel,
        grid=(N_HEADS,),
        in_specs=[_const_spec(feat_t.shape), _const_spec(sdec.shape), state_spec, state_spec],
        out_specs=(o_spec, state_spec, state_spec),
        out_shape=(jax.ShapeDtypeStruct((2, N_HEADS, HEAD_DIM, n), F32),
                   jax.ShapeDtypeStruct(ret_t.shape, F32), jax.ShapeDtypeStruct(wkv_t.shape, F32)),
        compiler_params=pltpu.CompilerParams(dimension_semantics=("arbitrary",),
                                             vmem_limit_bytes=VMEM_LIMIT),
        name="mixer_sample_state",
    )(feat_t, sdec, ret_t, wkv_t)

    args = [x, feat, o_t.reshape(2 * HEADS_W, n), lw["ret_gn"], lw["ln_w"], lw["ln_b"], lw["w_out"], ones_blk]
    x1 = pl.pallas_call(
        _mixer_sample_post_kernel,
        grid=(1,),
        in_specs=[_const_spec(a.shape) for a in args],
        out_specs=_const_out((n, D_MODEL)),
        out_shape=jax.ShapeDtypeStruct((n, D_MODEL), F32),
        compiler_params=pltpu.CompilerParams(vmem_limit_bytes=VMEM_LIMIT),
        name="mixer_sample_post",
    )(*args)
    return x1, ret_new, wkv_new, new_shift


def _mlp_kernel(xa_ref, xb_ref, normw_ref, wup_ref, wdown_ref, normf_ref, oa_ref, ob_ref,
                *, final_norm, steps_a):
    i = pl.program_id(0)

    def run(x_ref, o_ref):
        x = x_ref[...]
        hn = _rms(x, normw_ref[...]).astype(BF16)
        acc = x
        for c in range(D_FF // FF_CHUNK):
            sl = slice(c * FF_CHUNK, (c + 1) * FF_CHUNK)
            hid = jnp.dot(hn, wup_ref[:, sl].astype(BF16), preferred_element_type=F32)
            hid = jnp.square(jnp.maximum(hid, 0.0)).astype(BF16)
            acc = acc + jnp.dot(hid, wdown_ref[sl, :].astype(BF16), preferred_element_type=F32)
        if final_norm:
            acc = _rms(acc, normf_ref[...])
        o_ref[...] = acc

    @pl.when(i < steps_a)
    def _():
        run(xa_ref, oa_ref)

    @pl.when(i == steps_a)
    def _():
        run(xb_ref, ob_ref)


def _mlp(xa, xb, norm_w, w_up, w_down, norm_f, layer, final_norm):
    rows_a, rows_b = xa.shape[0], xb.shape[0]
    steps_a = rows_a // MLP_ROWS
    pick = lambda *_: (layer, 0, 0)
    block_a = pl.BlockSpec((MLP_ROWS, D_MODEL), lambda i: (jnp.minimum(i, steps_a - 1), 0))
    return pl.pallas_call(
        functools.partial(_mlp_kernel, final_norm=final_norm, steps_a=steps_a),
        grid=(steps_a + 1,),
        in_specs=[block_a, _const_spec((rows_b, D_MODEL)),
                  _const_spec((1, D_MODEL)),
                  pl.BlockSpec((None, D_MODEL, D_FF), pick, pipeline_mode=pl.Buffered(1)),
                  pl.BlockSpec((None, D_FF, D_MODEL), pick, pipeline_mode=pl.Buffered(1)),
                  _const_spec((1, D_MODEL))],
        out_specs=(block_a, _const_out((rows_b, D_MODEL))),
        out_shape=(jax.ShapeDtypeStruct((rows_a, D_MODEL), F32), jax.ShapeDtypeStruct((rows_b, D_MODEL), F32)),
        compiler_params=pltpu.CompilerParams(dimension_semantics=("arbitrary",),
                                             vmem_limit_bytes=VMEM_LIMIT),
        name="mlp",
    )(xa, xb, norm_w, w_up, w_down, norm_f)


def _ssm_prep_kernel(lre_ref, lim_ref, logdt_ref, bre_ref, bim_ref, are_ref, aim_ref, bbre_ref, bbim_ref):
    lre = jnp.minimum(lre_ref[...], -1e-4)
    lim = lim_ref[...]
    dt = jnp.exp(logdt_ref[...])
    mag = jnp.exp(lre * dt)
    are = mag * jnp.cos(lim * dt)
    aim = mag * jnp.sin(lim * dt)
    are_ref[...] = are
    aim_ref[...] = aim
    den = lre * lre + lim * lim
    nre = are - 1.0
    cre = (nre * lre + aim * lim) / den
    cim = (aim * lre - nre * lim) / den
    bre = bre_ref[...]
    bim = bim_ref[...]
    bbre_ref[...] = cre * bre - cim * bim
    bbim_ref[...] = cre * bim + cim * bre


def _gelu_exact(x):
    return 0.5 * x * (1.0 + lax.erf(x * (2.0 ** -0.5)))


def _ssm_kernel(x_ref, hre0_ref, him0_ref, normw_ref, are_ref, aim_ref, wb_ref, cre_ref, cim_ref, dskip_ref,
                wglu_ref, xo_ref, hre_ref, him_ref, xt_s, u_s, bu_s, y_s, hg_s, hre_s, him_s,
                *, n_seq, chunk, batch_major):
    i = pl.program_id(0)
    n_steps = pl.num_programs(0)
    rows_all = chunk * n_seq
    n_slabs = D_MODEL // 128

    @pl.when(i == 0)
    def _():
        hre_s[...] = hre0_ref[...]
        him_s[...] = him0_ref[...]

    if batch_major:
        for b in range(n_seq):
            for sl in range(n_slabs):
                xt_s[sl, pl.ds(b, chunk, stride=n_seq), :] = x_ref[b, :, sl * 128:(sl + 1) * 128]
        ssq = sum(jnp.sum(jnp.square(xt_s[sl]), axis=-1, keepdims=True) for sl in range(n_slabs))
        inv = lax.rsqrt(ssq * (1.0 / D_MODEL) + RMS_EPS)
        for sl in range(n_slabs):
            cols = slice(sl * 128, (sl + 1) * 128)
            u_s[:, cols] = xt_s[sl] * inv * normw_ref[:, cols]
    else:
        u_s[...] = _rms(x_ref[...].reshape(rows_all, D_MODEL), normw_ref[...])

    def input_proj(blk, part):
        c = slice(part * SSM_HALF, (part + 1) * SSM_HALF)
        bu_s[blk % 2, :, c] = _dot(u_s[:, blk * 128:(blk + 1) * 128], wb_ref[blk, :, c])

    def output_proj(blk, part):
        buf = bu_s.at[blk % 2]
        cols = slice(blk * 128, (blk + 1) * 128)
        if part == 0:
            y_s[:, cols] = _dot(buf[:, 0:SSM_HALF], cre_ref[blk])
        else:
            y_s[:, cols] = y_s[:, cols] - _dot(buf[:, SSM_HALF:2 * SSM_HALF], cim_ref[blk])

    input_proj(0, 0)
    input_proj(0, 1)
    for blk in range(SSM_BLOCKS):
        neighbours = []
        if blk >= 1:
            neighbours += [functools.partial(output_proj, blk - 1, 0), functools.partial(output_proj, blk - 1, 1)]
        if blk + 1 < SSM_BLOCKS:
            neighbours += [functools.partial(input_proj, blk + 1, 0), functools.partial(input_proj, blk + 1, 1)]
        buf = bu_s.at[blk % 2]
        cols = slice(blk * SSM_HALF, (blk + 1) * SSM_HALF)
        a_re = jnp.broadcast_to(are_ref[:, cols], (n_seq, SSM_HALF))
        a_im = jnp.broadcast_to(aim_ref[:, cols], (n_seq, SSM_HALF))
        h_re = hre_s[:, cols]
        h_im = him_s[:, cols]
        seg = -(-chunk // max(len(neighbours), 1))
        for t in range(chunk):
            if t % seg == 0 and neighbours:
                neighbours.pop(0)()
            rows = slice(t * n_seq, (t + 1) * n_seq)
            n_re = a_re * h_re - a_im * h_im + buf[rows, 0:SSM_HALF]
            n_im = a_re * h_im + a_im * h_re + buf[rows, SSM_HALF:2 * SSM_HALF]
            buf[rows, 0:SSM_HALF] = n_re
            buf[rows, SSM_HALF:2 * SSM_HALF] = n_im
            h_re, h_im = n_re, n_im
        for f in neighbours:
            f()
        hre_s[:, cols] = h_re
        him_s[:, cols] = h_im
    output_proj(SSM_BLOCKS - 1, 0)
    output_proj(SSM_BLOCKS - 1, 1)
    hg_s[...] = _gelu_exact(y_s[...] + dskip_ref[...] * u_s[...]).astype(BF16)
    glu_w = 256
    for c in range(D_MODEL // glu_w):
        cols = slice(c * glu_w, (c + 1) * glu_w)
        hg = hg_s[...]
        val = jnp.dot(hg, wglu_ref[:, cols], preferred_element_type=F32)
        gate = jnp.dot(hg, wglu_ref[:, D_MODEL + c * glu_w:D_MODEL + (c + 1) * glu_w],
                       preferred_element_type=F32)
        out = val * _sigmoid(gate)
        if batch_major:
            for sl in range(c * glu_w // 128, (c + 1) * glu_w // 128):
                xt_s[sl] = xt_s[sl] + out[:, sl * 128 - c * glu_w:(sl + 1) * 128 - c * glu_w]
        else:
            y_s[:, cols] = x_ref[...].reshape(rows_all, D_MODEL)[:, cols] + out
    if batch_major:
        for b in range(n_seq):
            for sl in range(n_slabs):
                xo_ref[b, :, sl * 128:(sl + 1) * 128] = xt_s[sl, pl.ds(b, chunk, stride=n_seq), :]
    else:
        xo_ref[...] = y_s[...].reshape(chunk, n_seq, D_MODEL)

    @pl.when(i == n_steps - 1)
    def _():
        hre_ref[...] = hre_s[...]
        him_ref[...] = him_s[...]


def _ssm_weights(lam_re, lam_im, log_dt, b_re, b_im, c_re, c_im):
    g, p = SSM_GROUPS, SSM_P
    n = g * SSM_GROUP
    rep = lambda z: jnp.repeat(z, SSM_GROUP, axis=0)
    bt_re = jnp.swapaxes(b_re, 1, 2).reshape(n, p)
    bt_im = jnp.swapaxes(b_im, 1, 2).reshape(n, p)
    args = [rep(lam_re), rep(lam_im), rep(log_dt.reshape(g, 1)), bt_re, bt_im]
    a_re, a_im, bb_re, bb_im = pl.pallas_call(
        _ssm_prep_kernel,
        grid=(1,),
        in_specs=[_const_spec(a.shape) for a in args],
        out_specs=tuple(_const_out((n, p)) for _ in range(4)),
        out_shape=tuple(jax.ShapeDtypeStruct((n, p), F32) for _ in range(4)),
        name="ssm_prep",
    )(*args)
    a_re = a_re[::SSM_GROUP]
    a_im = a_im[::SSM_GROUP]
    eye = jnp.eye(SSM_BLOCK_G, dtype=F32)

    def in_block(bb):
        bb = bb.reshape(SSM_BLOCKS, SSM_BLOCK_G, SSM_GROUP, p)
        return jnp.einsum("bgcp,gh->bgchp", bb, eye).reshape(SSM_BLOCKS, 128, SSM_HALF)

    def out_block(cc):
        cc = cc.reshape(SSM_BLOCKS, SSM_BLOCK_G, SSM_GROUP, p)
        return jnp.einsum("bgcp,gh->bgphc", cc, eye).reshape(SSM_BLOCKS, SSM_HALF, 128)

    w_b = jnp.concatenate([in_block(bb_re), in_block(bb_im)], axis=2).astype(BF16)
    n_state = SSM_GROUPS * SSM_P
    return (a_re.reshape(1, n_state), a_im.reshape(1, n_state), w_b,
            out_block(c_re).astype(BF16), out_block(c_im).astype(BF16))


def _ssm_layer(x, h_re0, h_im0, sw, chunk, batch_major):
    if batch_major:
        n_seq, t_len, _ = x.shape
        x_block = (n_seq, chunk, D_MODEL)
        x_map = lambda i: (0, i, 0)
    else:
        t_len, n_seq, _ = x.shape
        x_block = (chunk, n_seq, D_MODEL)
        x_map = lambda i: (i, 0, 0)
    rows = chunk * n_seq
    n_state = SSM_GROUPS * SSM_P
    args = [x, h_re0, h_im0, sw["norm"], sw["a_re"], sw["a_im"], sw["w_b"], sw["c_re"], sw["c_im"],
            sw["d_skip"], sw["w_glu"]]
    in_specs = [pl.BlockSpec(x_block, x_map)] + [_const_spec(a.shape) for a in args[1:]]
    state = jax.ShapeDtypeStruct((n_seq, n_state), F32)
    return pl.pallas_call(
        functools.partial(_ssm_kernel, n_seq=n_seq, chunk=chunk, batch_major=batch_major),
        grid=(t_len // chunk,),
        in_specs=in_specs,
        out_specs=(pl.BlockSpec(x_block, x_map), _const_out(state.shape), _const_out(state.shape)),
        out_shape=(jax.ShapeDtypeStruct(x.shape, F32), state, state),
        scratch_shapes=[pltpu.VMEM((D_MODEL // 128, rows, 128), F32),
                        pltpu.VMEM((rows, D_MODEL), F32), pltpu.VMEM((2, rows, 2 * SSM_HALF), F32),
                        pltpu.VMEM((rows, D_MODEL), F32), pltpu.VMEM((rows, D_MODEL), BF16),
                        pltpu.VMEM((n_seq, n_state), F32), pltpu.VMEM((n_seq, n_state), F32)],
        compiler_params=pltpu.CompilerParams(dimension_semantics=("arbitrary",),
                                             vmem_limit_bytes=VMEM_LIMIT),
        name="ssm_layer",
    )(*args)


def kernel(x_prompt, x_sample, state_ret, state_wkv, state_shift, state_ssm_re, state_ssm_im, norm_mix, w_in, ret_gn, mu_shift, wkv_w0, wkv_wB, wkv_a0, wkv_aB, wkv_gB, wkv_kk, wkv_ka, wkv_rk, wkv_ln_w, wkv_ln_b, w_out, ssm_lambda_re, ssm_lambda_im, ssm_log_dt, ssm_B_re, ssm_B_im, ssm_C_re, ssm_C_im, ssm_D, ssm_w_glu, mlp_norm, mlp_up, mlp_down, norm_f):
    lw = dict(
        norm=_row(norm_mix[0]), w_in=w_in[0].astype(BF16), ret_gn=_row(ret_gn[0]), mu=_row(mu_shift[0]),
        w0=_row(wkv_w0[0]), lora=_lora_block(wkv_wB[0], wkv_aB[0]), a0=_row(wkv_a0[0]),
        g_b=wkv_gB[0].astype(BF16), k_k=_row(wkv_kk[0]), k_a=_row(wkv_ka[0]), r_k=_row(wkv_rk[0]),
        ln_w=_row(wkv_ln_w[0]), ln_b=_row(wkv_ln_b[0]), w_out=w_out[0].astype(BF16))
    a_re, a_im, w_b, c_re, c_im = _ssm_weights(ssm_lambda_re[0], ssm_lambda_im[0], ssm_log_dt[0],
                                               ssm_B_re[0], ssm_B_im[0], ssm_C_re[0], ssm_C_im[0])
    sw = dict(norm=_row(norm_mix[1]), a_re=a_re, a_im=a_im, w_b=w_b, c_re=c_re, c_im=c_im,
              d_skip=_row(ssm_D[0]), w_glu=ssm_w_glu[0].astype(BF16))
    n_state = SSM_GROUPS * SSM_P
    nf = _row(norm_f)

    n_p, t_p, _ = x_prompt.shape
    n_s = x_sample.shape[0]
    x1, ret_p, wkv_p, shift_p = _mixer_prompt(x_prompt, lw)
    seq_last = lambda s: jnp.transpose(s, (1, 2, 3, 0))
    seq_first = lambda s: jnp.transpose(s, (3, 0, 1, 2))
    xs1, ret_s, wkv_s, shift_s = _mixer_sample(x_sample.reshape(n_s, D_MODEL), seq_last(state_ret[0]),
                                               seq_last(state_wkv[0]), state_shift[0], lw)
    ret_s, wkv_s = seq_first(ret_s), seq_first(wkv_s)
    x1, xs1 = _mlp(x1.reshape(n_p * t_p, D_MODEL), xs1, _row(mlp_norm[0]), mlp_up, mlp_down, nf, 0, False)
    zero_state = jnp.zeros((n_p, n_state), F32)
    x2, ssm_re_p, ssm_im_p = _ssm_layer(x1.reshape(n_p, t_p, D_MODEL), zero_state, zero_state, sw,
                                        SSM_CHUNK, True)
    xs2, ssm_re_s, ssm_im_s = _ssm_layer(
        xs1.reshape(1, n_s, D_MODEL), state_ssm_re[0].reshape(n_s, n_state),
        state_ssm_im[0].reshape(n_s, n_state), sw, 1, False)
    y_p, y_s = _mlp(x2.reshape(n_p * t_p, D_MODEL), xs2.reshape(n_s, D_MODEL), _row(mlp_norm[1]),
                    mlp_up, mlp_down, nf, 1, True)
    y_prompt = y_p.reshape(n_p, t_p, D_MODEL)
    ssm_re_p = ssm_re_p.reshape(n_p, SSM_GROUPS, SSM_P)
    ssm_im_p = ssm_im_p.reshape(n_p, SSM_GROUPS, SSM_P)
    ssm_re_s = ssm_re_s.reshape(n_s, SSM_GROUPS, SSM_P)
    ssm_im_s = ssm_im_s.reshape(n_s, SSM_GROUPS, SSM_P)

    return (y_prompt, y_s.reshape(n_s, 1, D_MODEL),
            ret_p[None], wkv_p[None], shift_p[None], ssm_re_p[None], ssm_im_p[None],
            ret_s[None], wkv_s[None], shift_s[None], ssm_re_s[None], ssm_im_s[None])
```

```python
import functools
import math

import numpy as np
import jax
import jax.numpy as jnp
from jax import lax
from jax.experimental import pallas as pl
from jax.experimental.pallas import tpu as pltpu

F32 = jnp.float32
BF16 = jnp.bfloat16

LANES = 128
MXU_TILE = 256
VMEM_BYTES = 64 * 1024 * 1024

D_MODEL = 1024
N_HEADS = 8
HEAD_DIM = 64
HEADS_W = N_HEADS * HEAD_DIM
N_PAIRS = N_HEADS // 2
PAIR_W = 2 * HEAD_DIM
assert PAIR_W == LANES
ROPE_BASE = 10000.0
DECAY_LORA = 64
AAA_LORA = 64
GATE_LORA = 128
SHIFT_W = 3 * HEADS_W + DECAY_LORA + AAA_LORA + GATE_LORA
RET_COLS = 4 * HEADS_W
IN_W = RET_COLS + SHIFT_W
SSM_GROUP = 16
SSM_GROUPS = D_MODEL // SSM_GROUP
SSM_P = 64
SSM_BLOCK_G = LANES // SSM_GROUP
SSM_BLOCKS = SSM_GROUPS // SSM_BLOCK_G
SSM_HALF = SSM_BLOCK_G * SSM_P
D_FF = 4 * D_MODEL
RMS_EPS = 1e-6
GN_EPS = 1e-5
WKV_GN_EPS = 64e-5
PAST_LEN = 16384

MIX_CHUNK = 64
PROJ_PIECE = MXU_TILE
MIX_GROUP = 4
SSM_CHUNK = 64
MLP_ROWS = 512
FF_CHUNK = 1024

VMEM_LIMIT = VMEM_BYTES - 6 * 1024 * 1024


def _dot(a, b):
    return jnp.dot(a.astype(BF16), b.astype(BF16), preferred_element_type=F32)


def _dot_nt(a, b):
    return lax.dot_general(a.astype(BF16), b.astype(BF16), (((1,), (1,)), ((), ())),
                           preferred_element_type=F32)


def _dot_tn(a, b):
    return lax.dot_general(a.astype(BF16), b.astype(BF16), (((0,), (0,)), ((), ())),
                           preferred_element_type=F32)


def _split3(x):
    hi = x.astype(BF16)
    r1 = x - hi.astype(F32)
    mid = r1.astype(BF16)
    lo = (r1 - mid.astype(F32)).astype(BF16)
    return hi, mid, lo


def _dot_exact_lhs(a_bf16, x):
    hi, mid, lo = _split3(x)
    f = lambda p: jnp.dot(a_bf16, p, preferred_element_type=F32)
    return f(hi) + f(mid) + f(lo)


def _segsum(x, ones_blk):
    xb = x.astype(BF16)
    outs = [jnp.dot(xb[:, c * MXU_TILE:(c + 1) * MXU_TILE], ones_blk, preferred_element_type=F32)
            for c in range(x.shape[1] // MXU_TILE)]
    return jnp.concatenate(outs, axis=1)


def _rms(x, w):
    return x * lax.rsqrt(jnp.mean(x * x, axis=-1, keepdims=True) + RMS_EPS) * w


def _sigmoid(x):
    return 1.0 / (1.0 + jnp.exp(-x))


def _softplus(x):
    return jnp.maximum(x, 0.0) + jnp.log1p(jnp.exp(-jnp.abs(x)))


def _head_norm(z, eps, ones_blk):
    mu = _segsum(z, ones_blk) * (1.0 / HEAD_DIM)
    zc = z - mu
    var = _segsum(zc * zc, ones_blk) * (1.0 / HEAD_DIM)
    return zc * lax.rsqrt(var + eps)


def _rope(z, cos, sin_signed):
    lane = lax.broadcasted_iota(jnp.int32, (1, HEADS_W), 1) % HEAD_DIM
    swapped = jnp.where(lane < HEAD_DIM // 2,
                        pltpu.roll(z, HEADS_W - HEAD_DIM // 2, axis=1),
                        pltpu.roll(z, HEAD_DIM // 2, axis=1))
    return z * cos + swapped * sin_signed


def _wkv_features(xs, w0, lora_w, a0, g_b, k_k, k_a, r_k, ones_blk, between=lambda: None):
    r = xs[:, 0:HEADS_W]
    kw = xs[:, HEADS_W:2 * HEADS_W]
    vw = xs[:, 2 * HEADS_W:3 * HEADS_W]
    lora_w_in = DECAY_LORA + AAA_LORA
    lo = xs[:, 3 * HEADS_W:3 * HEADS_W + lora_w_in]
    lane = lax.broadcasted_iota(jnp.int32, (1, lora_w_in), 1)
    lo = jnp.where(lane < DECAY_LORA, jnp.tanh(lo), lo)
    ll = _dot(lo, lora_w)
    w_log = -_softplus(-(w0 + ll[:, 0:HEADS_W])) - 0.5
    log_decay = -jnp.exp(w_log)
    between()
    alr = _sigmoid(a0 + ll[:, HEADS_W:2 * HEADS_W])
    gate = _dot(_sigmoid(xs[:, 3 * HEADS_W + lora_w_in:SHIFT_W]), g_b)
    between()
    kk = kw * k_k
    kk = kk / jnp.maximum(jnp.sqrt(_segsum(kk * kk, ones_blk)), 1e-12)
    k_mod = kw * (1.0 + (alr - 1.0) * k_a)
    between()
    bonus = _segsum(r * k_mod * r_k, ones_blk) * vw
    return r, log_decay, k_mod, vw, -kk, kk * alr, gate, bonus


def _stack_masked(x2, m0):
    return jnp.concatenate([jnp.where(m0, x2, 0.0), jnp.where(m0, 0.0, x2)], axis=0)


def _stack_dup(x2):
    return jnp.concatenate([x2, x2], axis=0)


def _mixer_prompt_kernel(
        xprev_ref, xnext_ref, normw_ref, win_ref, cos_ref, sin_ref, qdec_ref, kdec_ref, dmask_ref, sdec_ref,
        retgn_ref, mu_ref, w0_ref, lora_ref, a0_ref, gb_ref, kk_ref, ka_ref, rk_ref, lnw_ref,
        lnb_ref, wout_ref, ones_ref, tril_ref, strict_ref, incl_ref, bd_ref,
        xo_ref, rets_ref, wkvs_ref, shift_ref,
        p_s, cat_s, rs_s, ws_s, carry_s, *, n_seq, chunk, group):
    i = pl.program_id(0)
    n_chunks = pl.num_programs(0) - 1
    C = chunk
    slot = i % 2

    @pl.when(i == 0)
    def _():
        rs_s[...] = jnp.zeros_like(rs_s)
        ws_s[...] = jnp.zeros_like(ws_s)
        carry_s[...] = jnp.zeros_like(carry_s)
        cat_s[...] = jnp.zeros_like(cat_s)
        p_s[0] = _dot(_rms(xprev_ref[...].reshape(n_seq * C, D_MODEL), normw_ref[...]), win_ref[...])

    m0 = lax.broadcasted_iota(jnp.int32, (1, PAIR_W), 1) < HEAD_DIM
    ones_blk = ones_ref[...]
    NB = group
    R = NB * C
    row_id = lax.broadcasted_iota(jnp.int32, (R, 1), 0)
    tile_rows = lambda ref: jnp.concatenate([ref[...]] * NB, axis=0)
    pairs = range(N_PAIRS)
    sls = [slice(pr * PAIR_W, (pr + 1) * PAIR_W) for pr in pairs]
    chains = [(s, pr) for s in range(NB) for pr in pairs]
    seq_rows = [slice(s * C, (s + 1) * C) for s in range(NB)]

    def per_group(gi, carry):
        rows = pl.ds(pl.multiple_of(gi * R, R), R)
        b0 = gi * NB
        ret_states = {(s, pr): rs_s[pr, b0 + s] for s, pr in chains}
        wkv_states = {(s, pr): ws_s[pr, b0 + s] for s, pr in chains}
        shift_rows = [carry_s[pl.ds(b0 + s, 1), :] for s in range(NB)]
        p_cur = p_s.at[slot]
        wp = p_cur[rows, RET_COLS:IN_W]
        hn_next = _rms(xnext_ref[pl.ds(b0, NB)].reshape(R, D_MODEL), normw_ref[...]).astype(BF16)
        cat_prev = cat_s[rows, :]

        def next_in_proj(c0):
            p_s[1 - slot, rows, c0:c0 + PROJ_PIECE] = jnp.dot(
                hn_next, win_ref[:, c0:c0 + PROJ_PIECE], preferred_element_type=F32)

        def prev_out_proj(c0):
            cols = slice(c0, c0 + PROJ_PIECE)
            out = jnp.dot(cat_prev, wout_ref[:, cols], preferred_element_type=F32)
            xo_ref[pl.ds(b0, NB), :, cols] = xprev_ref[pl.ds(b0, NB), :, cols] + out.reshape(NB, C, PROJ_PIECE)

        pieces = [functools.partial(next_in_proj, c0) for c0 in range(0, IN_W, PROJ_PIECE)]
        pieces += [functools.partial(prev_out_proj, c0) for c0 in range(0, D_MODEL, PROJ_PIECE)]

        def fill(n=1):
            for _ in range(min(n, len(pieces))):
                pieces.pop(0)()

        cos, sin = tile_rows(cos_ref), tile_rows(sin_ref)
        q = _rope(p_cur[rows, 0:HEADS_W], cos, sin)
        fill()
        k = _rope(p_cur[rows, HEADS_W:2 * HEADS_W], cos, sin) * (HEAD_DIM ** -0.5)
        fill()
        v = p_cur[rows, 2 * HEADS_W:3 * HEADS_W]
        g = p_cur[rows, 3 * HEADS_W:4 * HEADS_W]
        qd = q * tile_rows(qdec_ref)
        kd = k * tile_rows(kdec_ref)
        fill()
        cut = lambda z, s, pr: z[seq_rows[s], sls[pr]]
        r_sc = {c: _dot_nt(_stack_masked(cut(q, *c), m0), _stack_dup(cut(k, *c))) * dmask_ref[c[1]]
                for c in chains}
        r_inner = {c: _dot(r_sc[c], _stack_dup(cut(v, *c))) for c in chains}
        r_cross = {c: _dot(cut(qd, *c), ret_states[c]) for c in chains}
        new_ret = {c: sdec_ref[c[1]] * ret_states[c] + _dot_tn(cut(kd, *c), cut(v, *c)) * bd_ref[...]
                   for c in chains}
        o = jnp.concatenate(
            [jnp.concatenate([jnp.where(m0, r_inner[(s, pr)][0:C], r_inner[(s, pr)][C:2 * C])
                              + r_cross[(s, pr)] for pr in pairs], axis=1) for s in range(NB)], axis=0)
        fill()
        ret = _head_norm(o, GN_EPS, ones_blk) * retgn_ref[...]
        ret_out = (g * _sigmoid(g) * ret).astype(BF16)
        fill()

        prev = pltpu.roll(wp, 1, axis=0)
        for s in range(NB):
            prev = jnp.where(row_id == s * C, shift_rows[s], prev)
        xs = wp + (prev - wp) * mu_ref[...]
        fill()
        r, lw, k_mod, vw, a_vec, b_vec, gate, bonus = _wkv_features(
            xs, w0_ref[...], lora_ref[...], a0_ref[...], gb_ref[...], kk_ref[...], ka_ref[...],
            rk_ref[...], ones_blk, fill)
        cw = _dot_exact_lhs(tril_ref[...], lw)
        fill()
        cw_last = [cw[s * C + C - 1:(s + 1) * C, :] for s in range(NB)]
        cwl = jnp.concatenate([jnp.broadcast_to(z, (C, HEADS_W)) for z in cw_last], axis=0)
        r_t = r * jnp.exp(cw)
        a_t = a_vec * jnp.exp(cw - lw)
        fill()
        w_inv = jnp.exp(-cw)
        b_t = b_vec * w_inv
        k_t = k_mod * w_inv
        fill()
        w_end = jnp.exp(cwl - cw)
        b_h = b_vec * w_end
        k_h = k_mod * w_end
        fill()
        w_all = [jnp.exp(z) for z in cw_last]
        lhs = {c: jnp.concatenate([_stack_masked(cut(a_t, *c), m0), _stack_masked(cut(r_t, *c), m0)], axis=0)
               for c in chains}
        fill(len(pieces) - 2)
        sc = {c: _dot_nt(lhs[c], jnp.concatenate([cut(b_t, *c), cut(k_t, *c)], axis=0)) for c in chains}
        sc_swapped = {c: pltpu.roll(sc[c], C, axis=1) for c in chains}
        sc_b = {c: jnp.where(m0, sc[c], sc_swapped[c]) for c in chains}
        sc_k = {c: jnp.where(m0, sc_swapped[c], sc[c]) for c in chains}
        on_state = {c: _dot_nt(lhs[c], wkv_states[c]) for c in chains}
        vv = {c: _stack_dup(cut(vw, *c)) for c in chains}
        n_pow = {c: sc_b[c][0:2 * C] * strict_ref[...] for c in chains}
        u = {c: on_state[c][0:2 * C] + _dot(sc_k[c][0:2 * C] * strict_ref[...], vv[c]) for c in chains}
        n_steps_solve = int(math.log2(C))
        for it in range(n_steps_solve):
            u = {c: u[c] + _dot(n_pow[c], u[c]) for c in chains}
            if it + 1 < n_steps_solve:
                n_pow = {c: _dot(n_pow[c], n_pow[c]) for c in chains}
        uv = {c: jnp.concatenate([u[c], vv[c]], axis=0) for c in chains}
        y_st = {c: on_state[c][2 * C:4 * C] + _dot(
            jnp.concatenate([sc_b[c][2 * C:4 * C] * incl_ref[...],
                             sc_k[c][2 * C:4 * C] * incl_ref[...]], axis=1), uv[c]) for c in chains}
        new_wkv = {c: wkv_states[c] * w_all[c[0]][:, sls[c[1]]] + bd_ref[...] * _dot_tn(
            uv[c], jnp.concatenate([_stack_masked(cut(b_h, *c), m0), _stack_masked(cut(k_h, *c), m0)], axis=0))
            for c in chains}
        y = jnp.concatenate(
            [jnp.concatenate([jnp.where(m0, y_st[(s, pr)][0:C], y_st[(s, pr)][C:2 * C]) for pr in pairs], axis=1)
             for s in range(NB)], axis=0)
        fill(len(pieces))
        yn = _head_norm(y, WKV_GN_EPS, ones_blk) * lnw_ref[...] + lnb_ref[...]
        cat_s[rows, 0:HEADS_W] = ret_out
        cat_s[rows, HEADS_W:2 * HEADS_W] = ((yn + bonus) * gate).astype(BF16)
        for s in range(NB):
            carry_s[pl.ds(b0 + s, 1), :] = wp[s * C + C - 1:(s + 1) * C, :]
        for s, pr in chains:
            rs_s[pr, b0 + s] = new_ret[(s, pr)]
            ws_s[pr, b0 + s] = new_wkv[(s, pr)]
        return carry

    @pl.when(i < n_chunks)
    def _():
        lax.fori_loop(0, n_seq // NB, per_group, 0)

    @pl.when(i == n_chunks)
    def _():
        out = jnp.dot(cat_s[...], wout_ref[...], preferred_element_type=F32)
        xo_ref[...] = xprev_ref[...] + out.reshape(n_seq, C, D_MODEL)

    @pl.when(i == n_chunks - 1)
    def _():
        shift_ref[...] = carry_s[...]
        for b in range(n_seq):
            for pr in range(N_PAIRS):
                rs = rs_s[pr, b]
                ws = ws_s[pr, b]
                rets_ref[b, 2 * pr] = rs[0:HEAD_DIM, 0:HEAD_DIM]
                rets_ref[b, 2 * pr + 1] = rs[HEAD_DIM:PAIR_W, HEAD_DIM:PAIR_W]
                wkvs_ref[b, 2 * pr] = ws[0:HEAD_DIM, 0:HEAD_DIM]
                wkvs_ref[b, 2 * pr + 1] = ws[HEAD_DIM:PAIR_W, HEAD_DIM:PAIR_W]


def _const_spec(shape):
    nd = len(shape)
    return pl.BlockSpec(shape, lambda *_: (0,) * nd, pipeline_mode=pl.Buffered(1))


def _const_out(shape):
    nd = len(shape)
    return pl.BlockSpec(shape, lambda *_: (0,) * nd)


def _const(a, dtype=F32):
    return jnp.asarray(np.asarray(a, np.float64), dtype=dtype)


def _retention_tables(chunk):
    log_g = np.log1p(-np.exp2(-5.0 - np.arange(N_HEADS, dtype=np.float64)))
    lane_g = np.repeat(log_g, HEAD_DIM)[None, :]
    idx = np.arange(chunk, dtype=np.float64)
    qdec = np.exp((idx + 1.0)[:, None] * lane_g)
    kdec = np.exp((chunk - 1.0 - idx)[:, None] * lane_g)
    rel = idx[:, None] - idx[None, :]
    dm = np.where(rel >= 0, np.exp(np.maximum(rel, 0.0)[None] * log_g[:, None, None]), 0.0)
    zero = np.zeros((chunk, chunk))
    dmask = np.stack([np.block([[dm[2 * p], zero], [zero, dm[2 * p + 1]]]) for p in range(N_PAIRS)])
    cdec = np.exp(chunk * log_g)
    hz = np.zeros((HEAD_DIM, HEAD_DIM))
    ho = np.ones((HEAD_DIM, HEAD_DIM))
    sdec = np.stack([np.block([[cdec[2 * p] * ho, hz], [hz, cdec[2 * p + 1] * ho]])
                     for p in range(N_PAIRS)])
    return _const(qdec), _const(kdec), _const(dmask), _const(sdec), cdec


def _rope_tables(pos):
    half = HEAD_DIM // 2
    inv_freq = ROPE_BASE ** (-np.arange(half, dtype=np.float64) / half)
    ang = np.asarray(pos, np.float64)[:, None] * inv_freq[None, :]
    cos = np.cos(ang)
    sin = np.sin(ang)
    cos_t = np.tile(np.concatenate([cos, cos], axis=1), (1, N_HEADS))
    sin_t = np.tile(np.concatenate([-sin, sin], axis=1), (1, N_HEADS))
    return _const(cos_t), _const(sin_t)


def _block_masks(chunk, group):
    i = np.arange(2 * chunk)
    same = (i[:, None] // chunk) == (i[None, :] // chunk)
    strict = same & (i[:, None] > i[None, :])
    incl = same & (i[:, None] >= i[None, :])
    j = np.arange(PAIR_W)
    bd = (j[:, None] // HEAD_DIM) == (j[None, :] // HEAD_DIM)
    t = np.arange(chunk)
    tril = np.kron(np.eye(group), t[:, None] >= t[None, :])
    o = np.arange(MXU_TILE)
    ones_blk = (o[:, None] // HEAD_DIM) == (o[None, :] // HEAD_DIM)
    return _const(strict), _const(incl), _const(bd), _const(tril, BF16), _const(ones_blk, BF16)


def _lora_block(w_b, a_b):
    z = jnp.zeros_like(w_b)
    return jnp.concatenate([jnp.concatenate([w_b, z], axis=1),
                            jnp.concatenate([z, a_b], axis=1)], axis=0).astype(BF16)


def _row(v):
    return v.reshape(1, -1).astype(F32)


def _mixer_prompt(x, lw):
    n_seq, t_len, _ = x.shape
    C = MIX_CHUNK
    n_chunks = t_len // C
    last = n_chunks - 1
    cos_t, sin_t = _rope_tables(np.arange(t_len))
    qdec, kdec, dmask, sdec, _ = _retention_tables(C)
    strict, incl, bd, tril, ones_blk = _block_masks(C, MIX_GROUP)
    prev_chunk = lambda i: (0, jnp.maximum(i - 1, 0), 0)
    in_specs = [
        pl.BlockSpec((n_seq, C, D_MODEL), prev_chunk),
        pl.BlockSpec((n_seq, C, D_MODEL), lambda i: (0, jnp.minimum(i + 1, last), 0)),
        _const_spec((1, D_MODEL)),
        _const_spec((D_MODEL, IN_W)),
        pl.BlockSpec((C, HEADS_W), lambda i: (jnp.minimum(i, last), 0)),
        pl.BlockSpec((C, HEADS_W), lambda i: (jnp.minimum(i, last), 0)),
    ]
    tail = [qdec, kdec, dmask, sdec, lw["ret_gn"], lw["mu"], lw["w0"], lw["lora"], lw["a0"], lw["g_b"],
            lw["k_k"], lw["k_a"], lw["r_k"], lw["ln_w"], lw["ln_b"], lw["w_out"], ones_blk, tril,
            strict, incl, bd]
    in_specs += [_const_spec(a.shape) for a in tail]
    out_shape = (
        jax.ShapeDtypeStruct((n_seq, t_len, D_MODEL), F32),
        jax.ShapeDtypeStruct((n_seq, N_HEADS, HEAD_DIM, HEAD_DIM), F32),
        jax.ShapeDtypeStruct((n_seq, N_HEADS, HEAD_DIM, HEAD_DIM), F32),
        jax.ShapeDtypeStruct((n_seq, SHIFT_W), F32),
    )
    out_specs = (
        pl.BlockSpec((n_seq, C, D_MODEL), prev_chunk),
        _const_out((n_seq, N_HEADS, HEAD_DIM, HEAD_DIM)),
        _const_out((n_seq, N_HEADS, HEAD_DIM, HEAD_DIM)),
        _const_out((n_seq, SHIFT_W)),
    )
    scratch = [
        pltpu.VMEM((2, n_seq * C, IN_W), F32),
        pltpu.VMEM((n_seq * C, 2 * HEADS_W), BF16),
        pltpu.VMEM((N_PAIRS, n_seq, PAIR_W, PAIR_W), F32),
        pltpu.VMEM((N_PAIRS, n_seq, PAIR_W, PAIR_W), F32),
        pltpu.VMEM((n_seq, SHIFT_W), F32),
    ]
    return pl.pallas_call(
        functools.partial(_mixer_prompt_kernel, n_seq=n_seq, chunk=C, group=MIX_GROUP),
        grid=(n_chunks + 1,),
        in_specs=in_specs, out_specs=out_specs, out_shape=out_shape, scratch_shapes=scratch,
        compiler_params=pltpu.CompilerParams(dimension_semantics=("arbitrary",),
                                             vmem_limit_bytes=VMEM_LIMIT),
        name="mixer_prompt",
    )(x, x, lw["norm"], lw["w_in"], cos_t, sin_t, *tail)


def _mixer_sample_pre_kernel(
        x_ref, shift_ref, normw_ref, win_ref, cos_ref, sin_ref, qdec_ref, mu_ref, w0_ref, lora_ref,
        a0_ref, gb_ref, kk_ref, ka_ref, rk_ref, ones_ref, feat_ref, feat_t_ref, newshift_ref):
    x = x_ref[...]
    p = _dot(_rms(x, normw_ref[...]), win_ref[...])
    q = _rope(p[:, 0:HEADS_W], cos_ref[...], sin_ref[...])
    k = _rope(p[:, HEADS_W:2 * HEADS_W], cos_ref[...], sin_ref[...]) * (HEAD_DIM ** -0.5)
    wp = p[:, RET_COLS:IN_W]
    xs = wp + (shift_ref[...] - wp) * mu_ref[...]
    r, lw, k_mod, vw, a_vec, b_vec, gate, bonus = _wkv_features(
        xs, w0_ref[...], lora_ref[...], a0_ref[...], gb_ref[...], kk_ref[...], ka_ref[...],
        rk_ref[...], ones_ref[...])
    newshift_ref[...] = wp
    state_feats = [q, q * qdec_ref[...], k, p[:, 2 * HEADS_W:3 * HEADS_W], r, jnp.exp(lw), k_mod, vw,
                   a_vec, b_vec]
    for n, f in enumerate(state_feats):
        feat_t_ref[n * HEADS_W:(n + 1) * HEADS_W, :] = f.T
    for n, f in enumerate([p[:, 3 * HEADS_W:4 * HEADS_W], gate, bonus]):
        feat_ref[:, n * HEADS_W:(n + 1) * HEADS_W] = f


_F_Q, _F_QD, _F_K, _F_V, _F_R, _F_W, _F_KM, _F_VW, _F_A, _F_B = range(10)
_N_STATE_FEATS = 10
_F_G, _F_GATE, _F_BONUS = range(3)
_N_ROW_FEATS = 3


def _mixer_sample_state_kernel(feat_t_ref, sdec_ref, ret_ref, wkv_ref, o_t_ref, reto_ref, wkvo_ref):
    h = pl.program_id(0)

    def head_rows(n):
        return feat_t_ref[pl.ds(pl.multiple_of(n * HEADS_W + h * HEAD_DIM, HEAD_DIM), HEAD_DIM), :]

    def head_row(n, i):
        return feat_t_ref[pl.ds(n * HEADS_W + h * HEAD_DIM + i, 1), :]

    a, w, b_vec, k_mod, r = (head_rows(n) for n in (_F_A, _F_W, _F_B, _F_KM, _F_R))

    def wkv_row(i, carry):
        s = wkv_ref[0, i]
        sa = jnp.sum(s * a, axis=0, keepdims=True)
        s_new = s * w + sa * b_vec + head_row(_F_VW, i) * k_mod
        wkvo_ref[0, i] = s_new
        o_t_ref[1, 0, pl.ds(i, 1), :] = jnp.sum(s_new * r, axis=0, keepdims=True)
        return carry

    lax.fori_loop(0, HEAD_DIM, wkv_row, 0, unroll=4)

    v, q, k = head_rows(_F_V), head_rows(_F_Q), head_rows(_F_K)
    g = sdec_ref[h]

    def ret_row(d, acc):
        s = ret_ref[0, d]
        reto_ref[0, d] = g * s + head_row(_F_K, d) * v
        return acc + head_row(_F_QD, d) * s

    cross = lax.fori_loop(0, HEAD_DIM, ret_row, jnp.zeros_like(v), unroll=4)
    o_t_ref[0, 0] = cross + jnp.sum(q * k, axis=0, keepdims=True) * v


def _mixer_sample_post_kernel(
        x_ref, feat_ref, o_t_ref, retgn_ref, lnw_ref, lnb_ref, wout_ref, ones_ref, xo_ref):
    ones_blk = ones_ref[...]
    g = feat_ref[:, _F_G * HEADS_W:(_F_G + 1) * HEADS_W]
    gate = feat_ref[:, _F_GATE * HEADS_W:(_F_GATE + 1) * HEADS_W]
    bonus = feat_ref[:, _F_BONUS * HEADS_W:(_F_BONUS + 1) * HEADS_W]
    o = o_t_ref[...].T
    ret = _head_norm(o[:, 0:HEADS_W], GN_EPS, ones_blk) * retgn_ref[...]
    ret_out = g * _sigmoid(g) * ret
    yn = _head_norm(o[:, HEADS_W:2 * HEADS_W], WKV_GN_EPS, ones_blk) * lnw_ref[...] + lnb_ref[...]
    wkv_out = (yn + bonus) * gate
    cat = jnp.concatenate([ret_out, wkv_out], axis=1)
    xo_ref[...] = x_ref[...] + _dot(cat, wout_ref[...])


def _mixer_sample(x, ret_t, wkv_t, shift0, lw):
    n = x.shape[0]
    cos_t, sin_t = _rope_tables(np.full((1,), PAST_LEN))
    qdec, _, _, _, cdec = _retention_tables(1)
    _, _, _, _, ones_blk = _block_masks(1, 1)
    args = [x, shift0, lw["norm"], lw["w_in"], cos_t, sin_t, qdec, lw["mu"], lw["w0"], lw["lora"],
            lw["a0"], lw["g_b"], lw["k_k"], lw["k_a"], lw["r_k"], ones_blk]
    feat, feat_t, new_shift = pl.pallas_call(
        _mixer_sample_pre_kernel,
        grid=(1,),
        in_specs=[_const_spec(a.shape) for a in args],
        out_specs=(_const_out((n, _N_ROW_FEATS * HEADS_W)), _const_out((_N_STATE_FEATS * HEADS_W, n)),
                   _const_out((n, SHIFT_W))),
        out_shape=(jax.ShapeDtypeStruct((n, _N_ROW_FEATS * HEADS_W), F32),
                   jax.ShapeDtypeStruct((_N_STATE_FEATS * HEADS_W, n), F32),
                   jax.ShapeDtypeStruct((n, SHIFT_W), F32)),
        compiler_params=pltpu.CompilerParams(vmem_limit_bytes=VMEM_LIMIT),
        name="mixer_sample_pre",
    )(*args)

    sdec = _const(np.broadcast_to(cdec[:, None, None], (N_HEADS, 1, n)))
    state_spec = pl.BlockSpec((1, HEAD_DIM, HEAD_DIM, n), lambda h: (h, 0, 0, 0))
    o_spec = pl.BlockSpec((2, 1, HEAD_DIM, n), lambda h: (0, h, 0, 0))
    o_t, ret_new, wkv_new = pl.pallas_call(
        _mixer_sample_state_kernel,
        grid=(N_HEADS,),
        in_specs=[_const_spec(feat_t.shape), _const_spec(sdec.shape), state_spec, state_spec],
        out_specs=(o_spec, state_spec, state_spec),
        out_shape=(jax.ShapeDtypeStruct((2, N_HEADS, HEAD_DIM, n), F32),
                   jax.ShapeDtypeStruct(ret_t.shape, F32), jax.ShapeDtypeStruct(wkv_t.shape, F32)),
        compiler_params=pltpu.CompilerParams(dimension_semantics=("arbitrary",),
                                             vmem_limit_bytes=VMEM_LIMIT),
        name="mixer_sample_state",
    )(feat_t, sdec, ret_t, wkv_t)

    args = [x, feat, o_t.reshape(2 * HEADS_W, n), lw["ret_gn"], lw["ln_w"], lw["ln_b"], lw["w_out"], ones_blk]
    x1 = pl.pallas_call(
        _mixer_sample_post_kernel,
        grid=(1,),
        in_specs=[_const_spec(a.shape) for a in args],
        out_specs=_const_out((n, D_MODEL)),
        out_shape=jax.ShapeDtypeStruct((n, D_MODEL), F32),
        compiler_params=pltpu.CompilerParams(vmem_limit_bytes=VMEM_LIMIT),
        name="mixer_sample_post",
    )(*args)
    return x1, ret_new, wkv_new, new_shift


def _mlp_kernel(xa_ref, xb_ref, normw_ref, wup_ref, wdown_ref, normf_ref, oa_ref, ob_ref,
                *, final_norm, steps_a):
    i = pl.program_id(0)

    def run(x_ref, o_ref):
        x = x_ref[...]
        hn = _rms(x, normw_ref[...]).astype(BF16)
        acc = x
        for c in range(D_FF // FF_CHUNK):
            sl = slice(c * FF_CHUNK, (c + 1) * FF_CHUNK)
            hid = jnp.dot(hn, wup_ref[:, sl].astype(BF16), preferred_element_type=F32)
            hid = jnp.square(jnp.maximum(hid, 0.0)).astype(BF16)
            acc = acc + jnp.dot(hid, wdown_ref[sl, :].astype(BF16), preferred_element_type=F32)
        if final_norm:
            acc = _rms(acc, normf_ref[...])
        o_ref[...] = acc

    @pl.when(i < steps_a)
    def _():
        run(xa_ref, oa_ref)

    @pl.when(i == steps_a)
    def _():
        run(xb_ref, ob_ref)


def _mlp(xa, xb, norm_w, w_up, w_down, norm_f, layer, final_norm):
    rows_a, rows_b = xa.shape[0], xb.shape[0]
    steps_a = rows_a // MLP_ROWS
    pick = lambda *_: (layer, 0, 0)
    block_a = pl.BlockSpec((MLP_ROWS, D_MODEL), lambda i: (jnp.minimum(i, steps_a - 1), 0))
    return pl.pallas_call(
        functools.partial(_mlp_kernel, final_norm=final_norm, steps_a=steps_a),
        grid=(steps_a + 1,),
        in_specs=[block_a, _const_spec((rows_b, D_MODEL)),
                  _const_spec((1, D_MODEL)),
                  pl.BlockSpec((None, D_MODEL, D_FF), pick, pipeline_mode=pl.Buffered(1)),
                  pl.BlockSpec((None, D_FF, D_MODEL), pick, pipeline_mode=pl.Buffered(1)),
                  _const_spec((1, D_MODEL))],
        out_specs=(block_a, _const_out((rows_b, D_MODEL))),
        out_shape=(jax.ShapeDtypeStruct((rows_a, D_MODEL), F32), jax.ShapeDtypeStruct((rows_b, D_MODEL), F32)),
        compiler_params=pltpu.CompilerParams(dimension_semantics=("arbitrary",),
                                             vmem_limit_bytes=VMEM_LIMIT),
        name="mlp",
    )(xa, xb, norm_w, w_up, w_down, norm_f)


def _ssm_prep_kernel(lre_ref, lim_ref, logdt_ref, bre_ref, bim_ref, are_ref, aim_ref, bbre_ref, bbim_ref):
    lre = jnp.minimum(lre_ref[...], -1e-4)
    lim = lim_ref[...]
    dt = jnp.exp(logdt_ref[...])
    mag = jnp.exp(lre * dt)
    are = mag * jnp.cos(lim * dt)
    aim = mag * jnp.sin(lim * dt)
    are_ref[...] = are
    aim_ref[...] = aim
    den = lre * lre + lim * lim
    nre = are - 1.0
    cre = (nre * lre + aim * lim) / den
    cim = (aim * lre - nre * lim) / den
    bre = bre_ref[...]
    bim = bim_ref[...]
    bbre_ref[...] = cre * bre - cim * bim
    bbim_ref[...] = cre * bim + cim * bre


def _gelu_exact(x):
    return 0.5 * x * (1.0 + lax.erf(x * (2.0 ** -0.5)))


def _ssm_kernel(x_ref, hre0_ref, him0_ref, normw_ref, are_ref, aim_ref, wb_ref, cre_ref, cim_ref, dskip_ref,
                wglu_ref, xo_ref, hre_ref, him_ref, xt_s, u_s, bu_s, y_s, hg_s, hre_s, him_s,
                *, n_seq, chunk, batch_major):
    i = pl.program_id(0)
    n_steps = pl.num_programs(0)
    rows_all = chunk * n_seq
    n_slabs = D_MODEL // LANES

    @pl.when(i == 0)
    def _():
        hre_s[...] = hre0_ref[...]
        him_s[...] = him0_ref[...]

    if batch_major:
        for b in range(n_seq):
            for sl in range(n_slabs):
                xt_s[sl, pl.ds(b, chunk, stride=n_seq), :] = x_ref[b, :, sl * LANES:(sl + 1) * LANES]
        ssq = sum(jnp.sum(jnp.square(xt_s[sl]), axis=-1, keepdims=True) for sl in range(n_slabs))
        inv = lax.rsqrt(ssq * (1.0 / D_MODEL) + RMS_EPS)
        for sl in range(n_slabs):
            cols = slice(sl * LANES, (sl + 1) * LANES)
            u_s[:, cols] = xt_s[sl] * inv * normw_ref[:, cols]
    else:
        u_s[...] = _rms(x_ref[...].reshape(rows_all, D_MODEL), normw_ref[...])

    def input_proj(blk, part):
        c = slice(part * SSM_HALF, (part + 1) * SSM_HALF)
        bu_s[blk % 2, :, c] = _dot(u_s[:, blk * LANES:(blk + 1) * LANES], wb_ref[blk, :, c])

    def output_proj(blk, part):
        buf = bu_s.at[blk % 2]
        cols = slice(blk * LANES, (blk + 1) * LANES)
        if part == 0:
            y_s[:, cols] = _dot(buf[:, 0:SSM_HALF], cre_ref[blk])
        else:
            y_s[:, cols] = y_s[:, cols] - _dot(buf[:, SSM_HALF:2 * SSM_HALF], cim_ref[blk])

    input_proj(0, 0)
    input_proj(0, 1)
    for blk in range(SSM_BLOCKS):
        neighbours = []
        if blk >= 1:
            neighbours += [functools.partial(output_proj, blk - 1, 0), functools.partial(output_proj, blk - 1, 1)]
        if blk + 1 < SSM_BLOCKS:
            neighbours += [functools.partial(input_proj, blk + 1, 0), functools.partial(input_proj, blk + 1, 1)]
        buf = bu_s.at[blk % 2]
        cols = slice(blk * SSM_HALF, (blk + 1) * SSM_HALF)
        a_re = jnp.broadcast_to(are_ref[:, cols], (n_seq, SSM_HALF))
        a_im = jnp.broadcast_to(aim_ref[:, cols], (n_seq, SSM_HALF))
        h_re = hre_s[:, cols]
        h_im = him_s[:, cols]
        seg = -(-chunk // max(len(neighbours), 1))
        for t in range(chunk):
            if t % seg == 0 and neighbours:
                neighbours.pop(0)()
            rows = slice(t * n_seq, (t + 1) * n_seq)
            n_re = a_re * h_re - a_im * h_im + buf[rows, 0:SSM_HALF]
            n_im = a_re * h_im + a_im * h_re + buf[rows, SSM_HALF:2 * SSM_HALF]
            buf[rows, 0:SSM_HALF] = n_re
            buf[rows, SSM_HALF:2 * SSM_HALF] = n_im
            h_re, h_im = n_re, n_im
        for f in neighbours:
            f()
        hre_s[:, cols] = h_re
        him_s[:, cols] = h_im
    output_proj(SSM_BLOCKS - 1, 0)
    output_proj(SSM_BLOCKS - 1, 1)
    hg_s[...] = _gelu_exact(y_s[...] + dskip_ref[...] * u_s[...]).astype(BF16)
    glu_w = MXU_TILE
    for c in range(D_MODEL // glu_w):
        cols = slice(c * glu_w, (c + 1) * glu_w)
        hg = hg_s[...]
        val = jnp.dot(hg, wglu_ref[:, cols], preferred_element_type=F32)
        gate = jnp.dot(hg, wglu_ref[:, D_MODEL + c * glu_w:D_MODEL + (c + 1) * glu_w],
                       preferred_element_type=F32)
        out = val * _sigmoid(gate)
        if batch_major:
            for sl in range(c * glu_w // LANES, (c + 1) * glu_w // LANES):
                xt_s[sl] = xt_s[sl] + out[:, sl * LANES - c * glu_w:(sl + 1) * LANES - c * glu_w]
        else:
            y_s[:, cols] = x_ref[...].reshape(rows_all, D_MODEL)[:, cols] + out
    if batch_major:
        for b in range(n_seq):
            for sl in range(n_slabs):
                xo_ref[b, :, sl * LANES:(sl + 1) * LANES] = xt_s[sl, pl.ds(b, chunk, stride=n_seq), :]
    else:
        xo_ref[...] = y_s[...].reshape(chunk, n_seq, D_MODEL)

    @pl.when(i == n_steps - 1)
    def _():
        hre_ref[...] = hre_s[...]
        him_ref[...] = him_s[...]


def _ssm_weights(lam_re, lam_im, log_dt, b_re, b_im, c_re, c_im):
    g, p = SSM_GROUPS, SSM_P
    n = g * SSM_GROUP
    rep = lambda z: jnp.repeat(z, SSM_GROUP, axis=0)
    bt_re = jnp.swapaxes(b_re, 1, 2).reshape(n, p)
    bt_im = jnp.swapaxes(b_im, 1, 2).reshape(n, p)
    args = [rep(lam_re), rep(lam_im), rep(log_dt.reshape(g, 1)), bt_re, bt_im]
    a_re, a_im, bb_re, bb_im = pl.pallas_call(
        _ssm_prep_kernel,
        grid=(1,),
        in_specs=[_const_spec(a.shape) for a in args],
        out_specs=tuple(_const_out((n, p)) for _ in range(4)),
        out_shape=tuple(jax.ShapeDtypeStruct((n, p), F32) for _ in range(4)),
        name="ssm_prep",
    )(*args)
    a_re = a_re[::SSM_GROUP]
    a_im = a_im[::SSM_GROUP]
    eye = jnp.eye(SSM_BLOCK_G, dtype=F32)

    def in_block(bb):
        bb = bb.reshape(SSM_BLOCKS, SSM_BLOCK_G, SSM_GROUP, p)
        return jnp.einsum("bgcp,gh->bgchp", bb, eye).reshape(SSM_BLOCKS, LANES, SSM_HALF)

    def out_block(cc):
        cc = cc.reshape(SSM_BLOCKS, SSM_BLOCK_G, SSM_GROUP, p)
        return jnp.einsum("bgcp,gh->bgphc", cc, eye).reshape(SSM_BLOCKS, SSM_HALF, LANES)

    w_b = jnp.concatenate([in_block(bb_re), in_block(bb_im)], axis=2).astype(BF16)
    n_state = SSM_GROUPS * SSM_P
    return (a_re.reshape(1, n_state), a_im.reshape(1, n_state), w_b,
            out_block(c_re).astype(BF16), out_block(c_im).astype(BF16))


def _ssm_layer(x, h_re0, h_im0, sw, chunk, batch_major):
    if batch_major:
        n_seq, t_len, _ = x.shape
        x_block = (n_seq, chunk, D_MODEL)
        x_map = lambda i: (0, i, 0)
    else:
        t_len, n_seq, _ = x.shape
        x_block = (chunk, n_seq, D_MODEL)
        x_map = lambda i: (i, 0, 0)
    rows = chunk * n_seq
    n_state = SSM_GROUPS * SSM_P
    args = [x, h_re0, h_im0, sw["norm"], sw["a_re"], sw["a_im"], sw["w_b"], sw["c_re"], sw["c_im"],
            sw["d_skip"], sw["w_glu"]]
    in_specs = [pl.BlockSpec(x_block, x_map)] + [_const_spec(a.shape) for a in args[1:]]
    state = jax.ShapeDtypeStruct((n_seq, n_state), F32)
    return pl.pallas_call(
        functools.partial(_ssm_kernel, n_seq=n_seq, chunk=chunk, batch_major=batch_major),
        grid=(t_len // chunk,),
        in_specs=in_specs,
        out_specs=(pl.BlockSpec(x_block, x_map), _const_out(state.shape), _const_out(state.shape)),
        out_shape=(jax.ShapeDtypeStruct(x.shape, F32), state, state),
        scratch_shapes=[pltpu.VMEM((D_MODEL // LANES, rows, LANES), F32),
                        pltpu.VMEM((rows, D_MODEL), F32), pltpu.VMEM((2, rows, 2 * SSM_HALF), F32),
                        pltpu.VMEM((rows, D_MODEL), F32), pltpu.VMEM((rows, D_MODEL), BF16),
                        pltpu.VMEM((n_seq, n_state), F32), pltpu.VMEM((n_seq, n_state), F32)],
        compiler_params=pltpu.CompilerParams(dimension_semantics=("arbitrary",),
                                             vmem_limit_bytes=VMEM_LIMIT),
        name="ssm_layer",
    )(*args)


def kernel(x_prompt, x_sample, state_ret, state_wkv, state_shift, state_ssm_re, state_ssm_im, norm_mix, w_in, ret_gn, mu_shift, wkv_w0, wkv_wB, wkv_a0, wkv_aB, wkv_gB, wkv_kk, wkv_ka, wkv_rk, wkv_ln_w, wkv_ln_b, w_out, ssm_lambda_re, ssm_lambda_im, ssm_log_dt, ssm_B_re, ssm_B_im, ssm_C_re, ssm_C_im, ssm_D, ssm_w_glu, mlp_norm, mlp_up, mlp_down, norm_f):
    lw = dict(
        norm=_row(norm_mix[0]), w_in=w_in[0].astype(BF16), ret_gn=_row(ret_gn[0]), mu=_row(mu_shift[0]),
        w0=_row(wkv_w0[0]), lora=_lora_block(wkv_wB[0], wkv_aB[0]), a0=_row(wkv_a0[0]),
        g_b=wkv_gB[0].astype(BF16), k_k=_row(wkv_kk[0]), k_a=_row(wkv_ka[0]), r_k=_row(wkv_rk[0]),
        ln_w=_row(wkv_ln_w[0]), ln_b=_row(wkv_ln_b[0]), w_out=w_out[0].astype(BF16))
    a_re, a_im, w_b, c_re, c_im = _ssm_weights(ssm_lambda_re[0], ssm_lambda_im[0], ssm_log_dt[0],
                                               ssm_B_re[0], ssm_B_im[0], ssm_C_re[0], ssm_C_im[0])
    sw = dict(norm=_row(norm_mix[1]), a_re=a_re, a_im=a_im, w_b=w_b, c_re=c_re, c_im=c_im,
              d_skip=_row(ssm_D[0]), w_glu=ssm_w_glu[0].astype(BF16))
    n_state = SSM_GROUPS * SSM_P
    nf = _row(norm_f)

    n_p, t_p, _ = x_prompt.shape
    n_s = x_sample.shape[0]
    x1, ret_p, wkv_p, shift_p = _mixer_prompt(x_prompt, lw)
    seq_last = lambda s: jnp.transpose(s, (1, 2, 3, 0))
    seq_first = lambda s: jnp.transpose(s, (3, 0, 1, 2))
    xs1, ret_s, wkv_s, shift_s = _mixer_sample(x_sample.reshape(n_s, D_MODEL), seq_last(state_ret[0]),
                                               seq_last(state_wkv[0]), state_shift[0], lw)
    ret_s, wkv_s = seq_first(ret_s), seq_first(wkv_s)
    x1, xs1 = _mlp(x1.reshape(n_p * t_p, D_MODEL), xs1, _row(mlp_norm[0]), mlp_up, mlp_down, nf, 0, False)
    zero_state = jnp.zeros((n_p, n_state), F32)
    x2, ssm_re_p, ssm_im_p = _ssm_layer(x1.reshape(n_p, t_p, D_MODEL), zero_state, zero_state, sw,
                                        SSM_CHUNK, True)
    xs2, ssm_re_s, ssm_im_s = _ssm_layer(
        xs1.reshape(1, n_s, D_MODEL), state_ssm_re[0].reshape(n_s, n_state),
        state_ssm_im[0].reshape(n_s, n_state), sw, 1, False)
    y_p, y_s = _mlp(x2.reshape(n_p * t_p, D_MODEL), xs2.reshape(n_s, D_MODEL), _row(mlp_norm[1]),
                    mlp_up, mlp_down, nf, 1, True)
    y_prompt = y_p.reshape(n_p, t_p, D_MODEL)
    ssm_re_p = ssm_re_p.reshape(n_p, SSM_GROUPS, SSM_P)
    ssm_im_p = ssm_im_p.reshape(n_p, SSM_GROUPS, SSM_P)
    ssm_re_s = ssm_re_s.reshape(n_s, SSM_GROUPS, SSM_P)
    ssm_im_s = ssm_im_s.reshape(n_s, SSM_GROUPS, SSM_P)

    return (y_prompt, y_s.reshape(n_s, 1, D_MODEL),
            ret_p[None], wkv_p[None], shift_p[None], ssm_re_p[None], ssm_im_p[None],
            ret_s[None], wkv_s[None], shift_s[None], ssm_re_s[None], ssm_im_s[None])
```

```python
import functools
import math

import numpy as np
import jax
import jax.numpy as jnp
from jax import lax
from jax.experimental import pallas as pl
from jax.experimental.pallas import tpu as pltpu

F32 = jnp.float32
BF16 = jnp.bfloat16

LANES = 128
MXU_TILE = 256
VMEM_BYTES = 64 * 1024 * 1024

D_MODEL = 1024
N_HEADS = 8
HEAD_DIM = 64
HEADS_W = N_HEADS * HEAD_DIM
N_PAIRS = N_HEADS // 2
PAIR_W = 2 * HEAD_DIM
assert PAIR_W == LANES
ROPE_BASE = 10000.0
DECAY_LORA = 64
AAA_LORA = 64
GATE_LORA = 128
SHIFT_W = 3 * HEADS_W + DECAY_LORA + AAA_LORA + GATE_LORA
RET_COLS = 4 * HEADS_W
IN_W = RET_COLS + SHIFT_W
SSM_GROUP = 16
SSM_GROUPS = D_MODEL // SSM_GROUP
SSM_P = 64
SSM_BLOCK_G = LANES // SSM_GROUP
SSM_BLOCKS = SSM_GROUPS // SSM_BLOCK_G
SSM_HALF = SSM_BLOCK_G * SSM_P
D_FF = 4 * D_MODEL
RMS_EPS = 1e-6
GN_EPS = 1e-5
WKV_GN_EPS = 64e-5
PAST_LEN = 16384

MIX_CHUNK = 64
PROJ_PIECE = MXU_TILE
MIX_GROUP = 4
SSM_CHUNK = 64
MLP_ROWS = 512
FF_CHUNK = 1024

VMEM_LIMIT = VMEM_BYTES - 6 * 1024 * 1024


def _dot(a, b):
    return jnp.dot(a.astype(BF16), b.astype(BF16), preferred_element_type=F32)


def _dot_nt(a, b):
    return lax.dot_general(a.astype(BF16), b.astype(BF16), (((1,), (1,)), ((), ())),
                           preferred_element_type=F32)


def _dot_tn(a, b):
    return lax.dot_general(a.astype(BF16), b.astype(BF16), (((0,), (0,)), ((), ())),
                           preferred_element_type=F32)


def _split3(x):
    hi = x.astype(BF16)
    r1 = x - hi.astype(F32)
    mid = r1.astype(BF16)
    lo = (r1 - mid.astype(F32)).astype(BF16)
    return hi, mid, lo


def _dot_exact_lhs(a_bf16, x):
    hi, mid, lo = _split3(x)
    f = lambda p: jnp.dot(a_bf16, p, preferred_element_type=F32)
    return f(hi) + f(mid) + f(lo)


def _segsum(x, ones_blk):
    xb = x.astype(BF16)
    outs = [jnp.dot(xb[:, c * MXU_TILE:(c + 1) * MXU_TILE], ones_blk, preferred_element_type=F32)
            for c in range(x.shape[1] // MXU_TILE)]
    return jnp.concatenate(outs, axis=1)


def _rms(x, w):
    return x * lax.rsqrt(jnp.mean(x * x, axis=-1, keepdims=True) + RMS_EPS) * w


def _sigmoid(x):
    return 1.0 / (1.0 + jnp.exp(-x))


def _softplus(x):
    return jnp.maximum(x, 0.0) + jnp.log1p(jnp.exp(-jnp.abs(x)))


def _head_norm(z, eps, ones_blk):
    mu = _segsum(z, ones_blk) * (1.0 / HEAD_DIM)
    zc = z - mu
    var = _segsum(zc * zc, ones_blk) * (1.0 / HEAD_DIM)
    return zc * lax.rsqrt(var + eps)


def _rope(z, cos, sin_signed):
    lane = lax.broadcasted_iota(jnp.int32, (1, HEADS_W), 1) % HEAD_DIM
    swapped = jnp.where(lane < HEAD_DIM // 2,
                        pltpu.roll(z, HEADS_W - HEAD_DIM // 2, axis=1),
                        pltpu.roll(z, HEAD_DIM // 2, axis=1))
    return z * cos + swapped * sin_signed


def _wkv_features(xs, w0, lora_w, a0, g_b, k_k, k_a, r_k, ones_blk, between=lambda: None):
    r = xs[:, 0:HEADS_W]
    kw = xs[:, HEADS_W:2 * HEADS_W]
    vw = xs[:, 2 * HEADS_W:3 * HEADS_W]
    lora_w_in = DECAY_LORA + AAA_LORA
    lo = xs[:, 3 * HEADS_W:3 * HEADS_W + lora_w_in]
    lane = lax.broadcasted_iota(jnp.int32, (1, lora_w_in), 1)
    lo = jnp.where(lane < DECAY_LORA, jnp.tanh(lo), lo)
    ll = _dot(lo, lora_w)
    w_log = -_softplus(-(w0 + ll[:, 0:HEADS_W])) - 0.5
    log_decay = -jnp.exp(w_log)
    between()
    alr = _sigmoid(a0 + ll[:, HEADS_W:2 * HEADS_W])
    gate = _dot(_sigmoid(xs[:, 3 * HEADS_W + lora_w_in:SHIFT_W]), g_b)
    between()
    kk = kw * k_k
    kk = kk / jnp.maximum(jnp.sqrt(_segsum(kk * kk, ones_blk)), 1e-12)
    k_mod = kw * (1.0 + (alr - 1.0) * k_a)
    between()
    bonus = _segsum(r * k_mod * r_k, ones_blk) * vw
    return r, log_decay, k_mod, vw, -kk, kk * alr, gate, bonus


def _stack_masked(x2, m0):
    return jnp.concatenate([jnp.where(m0, x2, 0.0), jnp.where(m0, 0.0, x2)], axis=0)


def _stack_dup(x2):
    return jnp.concatenate([x2, x2], axis=0)


def _mixer_prompt_kernel(
        xprev_ref, xnext_ref, normw_ref, win_ref, cos_ref, sin_ref, qdec_ref, kdec_ref, dmask_ref, sdec_ref,
        retgn_ref, mu_ref, w0_ref, lora_ref, a0_ref, gb_ref, kk_ref, ka_ref, rk_ref, lnw_ref,
        lnb_ref, wout_ref, ones_ref, tril_ref, strict_ref, incl_ref, bd_ref,
        xo_ref, rets_ref, wkvs_ref, shift_ref,
        p_s, cat_s, rs_s, ws_s, carry_s, *, n_seq, chunk, group):
    i = pl.program_id(0)
    n_chunks = pl.num_programs(0) - 1
    C = chunk
    slot = i % 2

    @pl.when(i == 0)
    def _():
        rs_s[...] = jnp.zeros_like(rs_s)
        ws_s[...] = jnp.zeros_like(ws_s)
        carry_s[...] = jnp.zeros_like(carry_s)
        cat_s[...] = jnp.zeros_like(cat_s)
        p_s[0] = _dot(_rms(xprev_ref[...].reshape(n_seq * C, D_MODEL), normw_ref[...]), win_ref[...])

    m0 = lax.broadcasted_iota(jnp.int32, (1, PAIR_W), 1) < HEAD_DIM
    ones_blk = ones_ref[...]
    NB = group
    R = NB * C
    row_id = lax.broadcasted_iota(jnp.int32, (R, 1), 0)
    tile_rows = lambda ref: jnp.concatenate([ref[...]] * NB, axis=0)
    pairs = range(N_PAIRS)
    sls = [slice(pr * PAIR_W, (pr + 1) * PAIR_W) for pr in pairs]
    chains = [(s, pr) for s in range(NB) for pr in pairs]
    seq_rows = [slice(s * C, (s + 1) * C) for s in range(NB)]

    def per_group(gi, carry):
        rows = pl.ds(pl.multiple_of(gi * R, R), R)
        b0 = gi * NB
        ret_states = {(s, pr): rs_s[pr, b0 + s] for s, pr in chains}
        wkv_states = {(s, pr): ws_s[pr, b0 + s] for s, pr in chains}
        shift_rows = [carry_s[pl.ds(b0 + s, 1), :] for s in range(NB)]
        p_cur = p_s.at[slot]
        wp = p_cur[rows, RET_COLS:IN_W]
        hn_next = _rms(xnext_ref[pl.ds(b0, NB)].reshape(R, D_MODEL), normw_ref[...]).astype(BF16)
        cat_prev = cat_s[rows, :]

        def next_in_proj(c0):
            p_s[1 - slot, rows, c0:c0 + PROJ_PIECE] = jnp.dot(
                hn_next, win_ref[:, c0:c0 + PROJ_PIECE], preferred_element_type=F32)

        def prev_out_proj(c0):
            cols = slice(c0, c0 + PROJ_PIECE)
            out = jnp.dot(cat_prev, wout_ref[:, cols], preferred_element_type=F32)
            xo_ref[pl.ds(b0, NB), :, cols] = xprev_ref[pl.ds(b0, NB), :, cols] + out.reshape(NB, C, PROJ_PIECE)

        pieces = [functools.partial(next_in_proj, c0) for c0 in range(0, IN_W, PROJ_PIECE)]
        pieces += [functools.partial(prev_out_proj, c0) for c0 in range(0, D_MODEL, PROJ_PIECE)]

        def fill(n=1):
            for _ in range(min(n, len(pieces))):
                pieces.pop(0)()

        cos, sin = tile_rows(cos_ref), tile_rows(sin_ref)
        q = _rope(p_cur[rows, 0:HEADS_W], cos, sin)
        fill()
        k = _rope(p_cur[rows, HEADS_W:2 * HEADS_W], cos, sin) * (HEAD_DIM ** -0.5)
        fill()
        v = p_cur[rows, 2 * HEADS_W:3 * HEADS_W]
        g = p_cur[rows, 3 * HEADS_W:4 * HEADS_W]
        qd = q * tile_rows(qdec_ref)
        kd = k * tile_rows(kdec_ref)
        fill()
        cut = lambda z, s, pr: z[seq_rows[s], sls[pr]]
        r_sc = {c: _dot_nt(_stack_masked(cut(q, *c), m0), _stack_dup(cut(k, *c))) * dmask_ref[c[1]]
                for c in chains}
        r_inner = {c: _dot(r_sc[c], _stack_dup(cut(v, *c))) for c in chains}
        r_cross = {c: _dot(cut(qd, *c), ret_states[c]) for c in chains}
        new_ret = {c: sdec_ref[c[1]] * ret_states[c] + _dot_tn(cut(kd, *c), cut(v, *c)) * bd_ref[...]
                   for c in chains}
        o = jnp.concatenate(
            [jnp.concatenate([jnp.where(m0, r_inner[(s, pr)][0:C], r_inner[(s, pr)][C:2 * C])
                              + r_cross[(s, pr)] for pr in pairs], axis=1) for s in range(NB)], axis=0)
        fill()
        ret = _head_norm(o, GN_EPS, ones_blk) * retgn_ref[...]
        ret_out = (g * _sigmoid(g) * ret).astype(BF16)
        fill()

        prev = pltpu.roll(wp, 1, axis=0)
        for s in range(NB):
            prev = jnp.where(row_id == s * C, shift_rows[s], prev)
        xs = wp + (prev - wp) * mu_ref[...]
        fill()
        r, lw, k_mod, vw, a_vec, b_vec, gate, bonus = _wkv_features(
            xs, w0_ref[...], lora_ref[...], a0_ref[...], gb_ref[...], kk_ref[...], ka_ref[...],
            rk_ref[...], ones_blk, fill)
        cw = _dot_exact_lhs(tril_ref[...], lw)
        fill()
        cw_last = [cw[s * C + C - 1:(s + 1) * C, :] for s in range(NB)]
        cwl = jnp.concatenate([jnp.broadcast_to(z, (C, HEADS_W)) for z in cw_last], axis=0)
        r_t = r * jnp.exp(cw)
        a_t = a_vec * jnp.exp(cw - lw)
        fill()
        w_inv = jnp.exp(-cw)
        b_t = b_vec * w_inv
        k_t = k_mod * w_inv
        fill()
        w_end = jnp.exp(cwl - cw)
        b_h = b_vec * w_end
        k_h = k_mod * w_end
        fill()
        w_all = [jnp.exp(z) for z in cw_last]
        lhs = {c: jnp.concatenate([_stack_masked(cut(a_t, *c), m0), _stack_masked(cut(r_t, *c), m0)], axis=0)
               for c in chains}
        fill(len(pieces) - 2)
        sc = {c: _dot_nt(lhs[c], jnp.concatenate([cut(b_t, *c), cut(k_t, *c)], axis=0)) for c in chains}
        sc_swapped = {c: pltpu.roll(sc[c], C, axis=1) for c in chains}
        sc_b = {c: jnp.where(m0, sc[c], sc_swapped[c]) for c in chains}
        sc_k = {c: jnp.where(m0, sc_swapped[c], sc[c]) for c in chains}
        on_state = {c: _dot_nt(lhs[c], wkv_states[c]) for c in chains}
        vv = {c: _stack_dup(cut(vw, *c)) for c in chains}
        n_pow = {c: sc_b[c][0:2 * C] * strict_ref[...] for c in chains}
        u = {c: on_state[c][0:2 * C] + _dot(sc_k[c][0:2 * C] * strict_ref[...], vv[c]) for c in chains}
        n_steps_solve = int(math.log2(C))
        for it in range(n_steps_solve):
            u = {c: u[c] + _dot(n_pow[c], u[c]) for c in chains}
            if it + 1 < n_steps_solve:
                n_pow = {c: _dot(n_pow[c], n_pow[c]) for c in chains}
        uv = {c: jnp.concatenate([u[c], vv[c]], axis=0) for c in chains}
        y_st = {c: on_state[c][2 * C:4 * C] + _dot(
            jnp.concatenate([sc_b[c][2 * C:4 * C] * incl_ref[...],
                             sc_k[c][2 * C:4 * C] * incl_ref[...]], axis=1), uv[c]) for c in chains}
        new_wkv = {c: wkv_states[c] * w_all[c[0]][:, sls[c[1]]] + bd_ref[...] * _dot_tn(
            uv[c], jnp.concatenate([_stack_masked(cut(b_h, *c), m0), _stack_masked(cut(k_h, *c), m0)], axis=0))
            for c in chains}
        y = jnp.concatenate(
            [jnp.concatenate([jnp.where(m0, y_st[(s, pr)][0:C], y_st[(s, pr)][C:2 * C]) for pr in pairs], axis=1)
             for s in range(NB)], axis=0)
        fill(len(pieces))
        yn = _head_norm(y, WKV_GN_EPS, ones_blk) * lnw_ref[...] + lnb_ref[...]
        cat_s[rows, 0:HEADS_W] = ret_out
        cat_s[rows, HEADS_W:2 * HEADS_W] = ((yn + bonus) * gate).astype(BF16)
        for s in range(NB):
            carry_s[pl.ds(b0 + s, 1), :] = wp[s * C + C - 1:(s + 1) * C, :]
        for s, pr in chains:
            rs_s[pr, b0 + s] = new_ret[(s, pr)]
            ws_s[pr, b0 + s] = new_wkv[(s, pr)]
        return carry

    @pl.when(i < n_chunks)
    def _():
        lax.fori_loop(0, n_seq // NB, per_group, 0)

    @pl.when(i == n_chunks)
    def _():
        out = jnp.dot(cat_s[...], wout_ref[...], preferred_element_type=F32)
        xo_ref[...] = xprev_ref[...] + out.reshape(n_seq, C, D_MODEL)

    @pl.when(i == n_chunks - 1)
    def _():
        shift_ref[...] = carry_s[...]
        for b in range(n_seq):
            for pr in range(N_PAIRS):
                rs = rs_s[pr, b]
                ws = ws_s[pr, b]
                rets_ref[b, 2 * pr] = rs[0:HEAD_DIM, 0:HEAD_DIM]
                rets_ref[b, 2 * pr + 1] = rs[HEAD_DIM:PAIR_W, HEAD_DIM:PAIR_W]
                wkvs_ref[b, 2 * pr] = ws[0:HEAD_DIM, 0:HEAD_DIM]
                wkvs_ref[b, 2 * pr + 1] = ws[HEAD_DIM:PAIR_W, HEAD_DIM:PAIR_W]


def _const_spec(shape):
    nd = len(shape)
    return pl.BlockSpec(shape, lambda *_: (0,) * nd, pipeline_mode=pl.Buffered(1))


def _const_out(shape):
    nd = len(shape)
    return pl.BlockSpec(shape, lambda *_: (0,) * nd)


def _const(a, dtype=F32):
    return jnp.asarray(np.asarray(a, np.float64), dtype=dtype)


def _retention_tables(chunk):
    log_g = np.log1p(-np.exp2(-5.0 - np.arange(N_HEADS, dtype=np.float64)))
    lane_g = np.repeat(log_g, HEAD_DIM)[None, :]
    idx = np.arange(chunk, dtype=np.float64)
    qdec = np.exp((idx + 1.0)[:, None] * lane_g)
    kdec = np.exp((chunk - 1.0 - idx)[:, None] * lane_g)
    rel = idx[:, None] - idx[None, :]
    dm = np.where(rel >= 0, np.exp(np.maximum(rel, 0.0)[None] * log_g[:, None, None]), 0.0)
    zero = np.zeros((chunk, chunk))
    dmask = np.stack([np.block([[dm[2 * p], zero], [zero, dm[2 * p + 1]]]) for p in range(N_PAIRS)])
    cdec = np.exp(chunk * log_g)
    hz = np.zeros((HEAD_DIM, HEAD_DIM))
    ho = np.ones((HEAD_DIM, HEAD_DIM))
    sdec = np.stack([np.block([[cdec[2 * p] * ho, hz], [hz, cdec[2 * p + 1] * ho]])
                     for p in range(N_PAIRS)])
    return _const(qdec), _const(kdec), _const(dmask), _const(sdec), cdec


def _rope_tables(pos):
    half = HEAD_DIM // 2
    inv_freq = ROPE_BASE ** (-np.arange(half, dtype=np.float64) / half)
    ang = np.asarray(pos, np.float64)[:, None] * inv_freq[None, :]
    cos = np.cos(ang)
    sin = np.sin(ang)
    cos_t = np.tile(np.concatenate([cos, cos], axis=1), (1, N_HEADS))
    sin_t = np.tile(np.concatenate([-sin, sin], axis=1), (1, N_HEADS))
    return _const(cos_t), _const(sin_t)


def _block_masks(chunk, group):
    i = np.arange(2 * chunk)
    same = (i[:, None] // chunk) == (i[None, :] // chunk)
    strict = same & (i[:, None] > i[None, :])
    incl = same & (i[:, None] >= i[None, :])
    j = np.arange(PAIR_W)
    bd = (j[:, None] // HEAD_DIM) == (j[None, :] // HEAD_DIM)
    t = np.arange(chunk)
    tril = np.kron(np.eye(group), t[:, None] >= t[None, :])
    o = np.arange(MXU_TILE)
    ones_blk = (o[:, None] // HEAD_DIM) == (o[None, :] // HEAD_DIM)
    return _const(strict), _const(incl), _const(bd), _const(tril, BF16), _const(ones_blk, BF16)


def _lora_block(w_b, a_b):
    z = jnp.zeros_like(w_b)
    return jnp.concatenate([jnp.concatenate([w_b, z], axis=1),
                            jnp.concatenate([z, a_b], axis=1)], axis=0).astype(BF16)


def _row(v):
    return v.reshape(1, -1).astype(F32)


def _mixer_prompt(x, lw):
    n_seq, t_len, _ = x.shape
    C = MIX_CHUNK
    n_chunks = t_len // C
    last = n_chunks - 1
    cos_t, sin_t = _rope_tables(np.arange(t_len))
    qdec, kdec, dmask, sdec, _ = _retention_tables(C)
    strict, incl, bd, tril, ones_blk = _block_masks(C, MIX_GROUP)
    prev_chunk = lambda i: (0, jnp.maximum(i - 1, 0), 0)
    in_specs = [
        pl.BlockSpec((n_seq, C, D_MODEL), prev_chunk),
        pl.BlockSpec((n_seq, C, D_MODEL), lambda i: (0, jnp.minimum(i + 1, last), 0)),
        _const_spec((1, D_MODEL)),
        _const_spec((D_MODEL, IN_W)),
        pl.BlockSpec((C, HEADS_W), lambda i: (jnp.minimum(i, last), 0)),
        pl.BlockSpec((C, HEADS_W), lambda i: (jnp.minimum(i, last), 0)),
    ]
    tail = [qdec, kdec, dmask, sdec, lw["ret_gn"], lw["mu"], lw["w0"], lw["lora"], lw["a0"], lw["g_b"],
            lw["k_k"], lw["k_a"], lw["r_k"], lw["ln_w"], lw["ln_b"], lw["w_out"], ones_blk, tril,
            strict, incl, bd]
    in_specs += [_const_spec(a.shape) for a in tail]
    out_shape = (
        jax.ShapeDtypeStruct((n_seq, t_len, D_MODEL), F32),
        jax.ShapeDtypeStruct((n_seq, N_HEADS, HEAD_DIM, HEAD_DIM), F32),
        jax.ShapeDtypeStruct((n_seq, N_HEADS, HEAD_DIM, HEAD_DIM), F32),
        jax.ShapeDtypeStruct((n_seq, SHIFT_W), F32),
    )
    out_specs = (
        pl.BlockSpec((n_seq, C, D_MODEL), prev_chunk),
        _const_out((n_seq, N_HEADS, HEAD_DIM, HEAD_DIM)),
        _const_out((n_seq, N_HEADS, HEAD_DIM, HEAD_DIM)),
        _const_out((n_seq, SHIFT_W)),
    )
    scratch = [
        pltpu.VMEM((2, n_seq * C, IN_W), F32),
        pltpu.VMEM((n_seq * C, 2 * HEADS_W), BF16),
        pltpu.VMEM((N_PAIRS, n_seq, PAIR_W, PAIR_W), F32),
        pltpu.VMEM((N_PAIRS, n_seq, PAIR_W, PAIR_W), F32),
        pltpu.VMEM((n_seq, SHIFT_W), F32),
    ]
    return pl.pallas_call(
        functools.partial(_mixer_prompt_kernel, n_seq=n_seq, chunk=C, group=MIX_GROUP),
        grid=(n_chunks + 1,),
        in_specs=in_specs, out_specs=out_specs, out_shape=out_shape, scratch_shapes=scratch,
        compiler_params=pltpu.CompilerParams(dimension_semantics=("arbitrary",),
                                             vmem_limit_bytes=VMEM_LIMIT),
        name="mixer_prompt",
    )(x, x, lw["norm"], lw["w_in"], cos_t, sin_t, *tail)


def _mixer_sample_pre_kernel(
        x_ref, shift_ref, normw_ref, win_ref, cos_ref, sin_ref, qdec_ref, mu_ref, w0_ref, lora_ref,
        a0_ref, gb_ref, kk_ref, ka_ref, rk_ref, ones_ref, feat_ref, feat_t_ref, newshift_ref):
    x = x_ref[...]
    p = _dot(_rms(x, normw_ref[...]), win_ref[...])
    q = _rope(p[:, 0:HEADS_W], cos_ref[...], sin_ref[...])
    k = _rope(p[:, HEADS_W:2 * HEADS_W], cos_ref[...], sin_ref[...]) * (HEAD_DIM ** -0.5)
    wp = p[:, RET_COLS:IN_W]
    xs = wp + (shift_ref[...] - wp) * mu_ref[...]
    r, lw, k_mod, vw, a_vec, b_vec, gate, bonus = _wkv_features(
        xs, w0_ref[...], lora_ref[...], a0_ref[...], gb_ref[...], kk_ref[...], ka_ref[...],
        rk_ref[...], ones_ref[...])
    newshift_ref[...] = wp
    state_feats = [q, q * qdec_ref[...], k, p[:, 2 * HEADS_W:3 * HEADS_W], r, jnp.exp(lw), k_mod, vw,
                   a_vec, b_vec]
    for n, f in enumerate(state_feats):
        feat_t_ref[n * HEADS_W:(n + 1) * HEADS_W, :] = f.T
    for n, f in enumerate([p[:, 3 * HEADS_W:4 * HEADS_W], gate, bonus]):
        feat_ref[:, n * HEADS_W:(n + 1) * HEADS_W] = f


_F_Q, _F_QD, _F_K, _F_V, _F_R, _F_W, _F_KM, _F_VW, _F_A, _F_B = range(10)
_N_STATE_FEATS = 10
_F_G, _F_GATE, _F_BONUS = range(3)
_N_ROW_FEATS = 3


def _mixer_sample_state_kernel(feat_t_ref, sdec_ref, ret_ref, wkv_ref, o_t_ref, reto_ref, wkvo_ref):
    h = pl.program_id(0)

    def head_rows(n):
        return feat_t_ref[pl.ds(pl.multiple_of(n * HEADS_W + h * HEAD_DIM, HEAD_DIM), HEAD_DIM), :]

    def head_row(n, i):
        return feat_t_ref[pl.ds(n * HEADS_W + h * HEAD_DIM + i, 1), :]

    a, w, b_vec, k_mod, r = (head_rows(n) for n in (_F_A, _F_W, _F_B, _F_KM, _F_R))

    def wkv_row(i, carry):
        s = wkv_ref[0, i]
        sa = jnp.sum(s * a, axis=0, keepdims=True)
        s_new = s * w + sa * b_vec + head_row(_F_VW, i) * k_mod
        wkvo_ref[0, i] = s_new
        o_t_ref[1, 0, pl.ds(i, 1), :] = jnp.sum(s_new * r, axis=0, keepdims=True)
        return carry

    lax.fori_loop(0, HEAD_DIM, wkv_row, 0, unroll=4)

    v, q, k = head_rows(_F_V), head_rows(_F_Q), head_rows(_F_K)
    g = sdec_ref[h]

    def ret_row(d, acc):
        s = ret_ref[0, d]
        reto_ref[0, d] = g * s + head_row(_F_K, d) * v
        return acc + head_row(_F_QD, d) * s

    cross = lax.fori_loop(0, HEAD_DIM, ret_row, jnp.zeros_like(v), unroll=4)
    o_t_ref[0, 0] = cross + jnp.sum(q * k, axis=0, keepdims=True) * v


def _mixer_sample_post_kernel(
        x_ref, feat_ref, o_t_ref, retgn_ref, lnw_ref, lnb_ref, wout_ref, ones_ref, xo_ref):
    ones_blk = ones_ref[...]
    g = feat_ref[:, _F_G * HEADS_W:(_F_G + 1) * HEADS_W]
    gate = feat_ref[:, _F_GATE * HEADS_W:(_F_GATE + 1) * HEADS_W]
    bonus = feat_ref[:, _F_BONUS * HEADS_W:(_F_BONUS + 1) * HEADS_W]
    o = o_t_ref[...].T
    ret = _head_norm(o[:, 0:HEADS_W], GN_EPS, ones_blk) * retgn_ref[...]
    ret_out = g * _sigmoid(g) * ret
    yn = _head_norm(o[:, HEADS_W:2 * HEADS_W], WKV_GN_EPS, ones_blk) * lnw_ref[...] + lnb_ref[...]
    wkv_out = (yn + bonus) * gate
    cat = jnp.concatenate([ret_out, wkv_out], axis=1)
    xo_ref[...] = x_ref[...] + _dot(cat, wout_ref[...])


def _mixer_sample(x, ret_t, wkv_t, shift0, lw):
    n = x.shape[0]
    cos_t, sin_t = _rope_tables(np.full((1,), PAST_LEN))
    qdec, _, _, _, cdec = _retention_tables(1)
    _, _, _, _, ones_blk = _block_masks(1, 1)
    args = [x, shift0, lw["norm"], lw["w_in"], cos_t, sin_t, qdec, lw["mu"], lw["w0"], lw["lora"],
            lw["a0"], lw["g_b"], lw["k_k"], lw["k_a"], lw["r_k"], ones_blk]
    feat, feat_t, new_shift = pl.pallas_call(
        _mixer_sample_pre_kernel,
        grid=(1,),
        in_specs=[_const_spec(a.shape) for a in args],
        out_specs=(_const_out((n, _N_ROW_FEATS * HEADS_W)), _const_out((_N_STATE_FEATS * HEADS_W, n)),
                   _const_out((n, SHIFT_W))),
        out_shape=(jax.ShapeDtypeStruct((n, _N_ROW_FEATS * HEADS_W), F32),
                   jax.ShapeDtypeStruct((_N_STATE_FEATS * HEADS_W, n), F32),
                   jax.ShapeDtypeStruct((n, SHIFT_W), F32)),
        compiler_params=pltpu.CompilerParams(vmem_limit_bytes=VMEM_LIMIT),
        name="mixer_sample_pre",
    )(*args)

    sdec = _const(np.broadcast_to(cdec[:, None, None], (N_HEADS, 1, n)))
    state_spec = pl.BlockSpec((1, HEAD_DIM, HEAD_DIM, n), lambda h: (h, 0, 0, 0))
    o_spec = pl.BlockSpec((2, 1, HEAD_DIM, n), lambda h: (0, h, 0, 0))
    o_t, ret_new, wkv_new = pl.pallas_call(
        _mixer_sample_state_kernel,
        grid=(N_HEADS,),
        in_specs=[_const_spec(feat_t.shape), _const_spec(sdec.shape), state_spec, state_spec],
        out_specs=(o_spec, state_spec, state_spec),
        out_shape=(jax.ShapeDtypeStruct((2, N_HEADS, HEAD_DIM, n), F32),
                   jax.ShapeDtypeStruct(ret_t.shape, F32), jax.ShapeDtypeStruct(wkv_t.shape, F32)),
        compiler_params=pltpu.CompilerParams(dimension_semantics=("arbitrary",),
                                             vmem_limit_bytes=VMEM_LIMIT),
        name="mixer_sample_state",
    )(feat_t, sdec, ret_t, wkv_t)

    args = [x, feat, o_t.reshape(2 * HEADS_W, n), lw["ret_gn"], lw["ln_w"], lw["ln_b"], lw["w_out"], ones_blk]
    x1 = pl.pallas_call(
        _mixer_sample_post_kernel,
        grid=(1,),
        in_specs=[_const_spec(a.shape) for a in args],
        out_specs=_const_out((n, D_MODEL)),
        out_shape=jax.ShapeDtypeStruct((n, D_MODEL), F32),
        compiler_params=pltpu.CompilerParams(vmem_limit_bytes=VMEM_LIMIT),
        name="mixer_sample_post",
    )(*args)
    return x1, ret_new, wkv_new, new_shift


def _mlp_kernel(xa_ref, xb_ref, normw_ref, wup_ref, wdown_ref, normf_ref, oa_ref, ob_ref,
                *, final_norm, steps_a):
    i = pl.program_id(0)

    def run(x_ref, o_ref):
        x = x_ref[...]
        hn = _rms(x, normw_ref[...]).astype(BF16)
        acc = x
        for c in range(D_FF // FF_CHUNK):
            sl = slice(c * FF_CHUNK, (c + 1) * FF_CHUNK)
            hid = jnp.dot(hn, wup_ref[:, sl].astype(BF16), preferred_element_type=F32)
            hid = jnp.square(jnp.maximum(hid, 0.0)).astype(BF16)
            acc = acc + jnp.dot(hid, wdown_ref[sl, :].astype(BF16), preferred_element_type=F32)
        if final_norm:
            acc = _rms(acc, normf_ref[...])
        o_ref[...] = acc

    @pl.when(i < steps_a)
    def _():
        run(xa_ref, oa_ref)

    @pl.when(i == steps_a)
    def _():
        run(xb_ref, ob_ref)


def _mlp(xa, xb, norm_w, w_up, w_down, norm_f, layer, final_norm):
    rows_a, rows_b = xa.shape[0], xb.shape[0]
    steps_a = rows_a // MLP_ROWS
    pick = lambda *_: (layer, 0, 0)
    block_a = pl.BlockSpec((MLP_ROWS, D_MODEL), lambda i: (jnp.minimum(i, steps_a - 1), 0))
    return pl.pallas_call(
        functools.partial(_mlp_kernel, final_norm=final_norm, steps_a=steps_a),
        grid=(steps_a + 1,),
        in_specs=[block_a, _const_spec((rows_b, D_MODEL)),
                  _const_spec((1, D_MODEL)),
                  pl.BlockSpec((None, D_MODEL, D_FF), pick, pipeline_mode=pl.Buffered(1)),
                  pl.BlockSpec((None, D_FF, D_MODEL), pick, pipeline_mode=pl.Buffered(1)),
                  _const_spec((1, D_MODEL))],
        out_specs=(block_a, _const_out((rows_b, D_MODEL))),
        out_shape=(jax.ShapeDtypeStruct((rows_a, D_MODEL), F32), jax.ShapeDtypeStruct((rows_b, D_MODEL), F32)),
        compiler_params=pltpu.CompilerParams(dimension_semantics=("arbitrary",),
                                             vmem_limit_bytes=VMEM_LIMIT),
        name="mlp",
    )(xa, xb, norm_w, w_up, w_down, norm_f)


def _ssm_prep_kernel(lre_ref, lim_ref, logdt_ref, bre_ref, bim_ref, are_ref, aim_ref, bbre_ref, bbim_ref):
    lre = jnp.minimum(lre_ref[...], -1e-4)
    lim = lim_ref[...]
    dt = jnp.exp(logdt_ref[...])
    mag = jnp.exp(lre * dt)
    are = mag * jnp.cos(lim * dt)
    aim = mag * jnp.sin(lim * dt)
    are_ref[...] = are
    aim_ref[...] = aim
    den = lre * lre + lim * lim
    nre = are - 1.0
    cre = (nre * lre + aim * lim) / den
    cim = (aim * lre - nre * lim) / den
    bre = bre_ref[...]
    bim = bim_ref[...]
    bbre_ref[...] = cre * bre - cim * bim
    bbim_ref[...] = cre * bim + cim * bre


def _gelu_exact(x):
    return 0.5 * x * (1.0 + lax.erf(x * (2.0 ** -0.5)))


def _ssm_kernel(x_ref, hre0_ref, him0_ref, normw_ref, are_ref, aim_ref, wb_ref, cre_ref, cim_ref, dskip_ref,
                wglu_ref, xo_ref, hre_ref, him_ref, xt_s, u_s, bu_s, y_s, hg_s, hre_s, him_s,
                *, n_seq, chunk, batch_major):
    i = pl.program_id(0)
    n_steps = pl.num_programs(0)
    rows_all = chunk * n_seq
    n_slabs = D_MODEL // LANES

    @pl.when(i == 0)
    def _():
        hre_s[...] = hre0_ref[...] if batch_major else hre0_ref[...].T
        him_s[...] = him0_ref[...] if batch_major else him0_ref[...].T

    if batch_major:
        for b in range(n_seq):
            for sl in range(n_slabs):
                xt_s[sl, pl.ds(b, chunk, stride=n_seq), :] = x_ref[b, :, sl * LANES:(sl + 1) * LANES]
        ssq = sum(jnp.sum(jnp.square(xt_s[sl]), axis=-1, keepdims=True) for sl in range(n_slabs))
        inv = lax.rsqrt(ssq * (1.0 / D_MODEL) + RMS_EPS)
        for sl in range(n_slabs):
            cols = slice(sl * LANES, (sl + 1) * LANES)
            u_s[:, cols] = xt_s[sl] * inv * normw_ref[:, cols]
    else:
        u_s[...] = _rms(x_ref[...].reshape(rows_all, D_MODEL), normw_ref[...])

    def input_proj(blk, part):
        c = slice(part * SSM_HALF, (part + 1) * SSM_HALF)
        bu_s[blk % 2, :, c] = _dot(u_s[:, blk * LANES:(blk + 1) * LANES], wb_ref[blk, :, c])

    def output_proj(blk, part):
        buf = bu_s.at[blk % 2]
        cols = slice(blk * LANES, (blk + 1) * LANES)
        if part == 0:
            y_s[:, cols] = _dot(buf[:, 0:SSM_HALF], cre_ref[blk])
        else:
            y_s[:, cols] = y_s[:, cols] - _dot(buf[:, SSM_HALF:2 * SSM_HALF], cim_ref[blk])

    input_proj(0, 0)
    input_proj(0, 1)
    for blk in range(SSM_BLOCKS):
        neighbours = []
        if blk >= 1:
            neighbours += [functools.partial(output_proj, blk - 1, 0), functools.partial(output_proj, blk - 1, 1)]
        if blk + 1 < SSM_BLOCKS:
            neighbours += [functools.partial(input_proj, blk + 1, 0), functools.partial(input_proj, blk + 1, 1)]
        buf = bu_s.at[blk % 2]
        cols = slice(blk * SSM_HALF, (blk + 1) * SSM_HALF)
        a_re = jnp.broadcast_to(are_ref[:, cols], (n_seq, SSM_HALF))
        a_im = jnp.broadcast_to(aim_ref[:, cols], (n_seq, SSM_HALF))
        h_re = hre_s[:, cols]
        h_im = him_s[:, cols]
        seg = -(-chunk // max(len(neighbours), 1))
        for t in range(chunk):
            if t % seg == 0 and neighbours:
                neighbours.pop(0)()
            rows = slice(t * n_seq, (t + 1) * n_seq)
            n_re = a_re * h_re - a_im * h_im + buf[rows, 0:SSM_HALF]
            n_im = a_re * h_im + a_im * h_re + buf[rows, SSM_HALF:2 * SSM_HALF]
            buf[rows, 0:SSM_HALF] = n_re
            buf[rows, SSM_HALF:2 * SSM_HALF] = n_im
            h_re, h_im = n_re, n_im
        for f in neighbours:
            f()
        hre_s[:, cols] = h_re
        him_s[:, cols] = h_im
    output_proj(SSM_BLOCKS - 1, 0)
    output_proj(SSM_BLOCKS - 1, 1)
    hg_s[...] = _gelu_exact(y_s[...] + dskip_ref[...] * u_s[...]).astype(BF16)
    glu_w = MXU_TILE
    for c in range(D_MODEL // glu_w):
        cols = slice(c * glu_w, (c + 1) * glu_w)
        hg = hg_s[...]
        val = jnp.dot(hg, wglu_ref[:, cols], preferred_element_type=F32)
        gate = jnp.dot(hg, wglu_ref[:, D_MODEL + c * glu_w:D_MODEL + (c + 1) * glu_w],
                       preferred_element_type=F32)
        out = val * _sigmoid(gate)
        if batch_major:
            for sl in range(c * glu_w // LANES, (c + 1) * glu_w // LANES):
                xt_s[sl] = xt_s[sl] + out[:, sl * LANES - c * glu_w:(sl + 1) * LANES - c * glu_w]
        else:
            y_s[:, cols] = x_ref[...].reshape(rows_all, D_MODEL)[:, cols] + out
    if batch_major:
        for b in range(n_seq):
            for sl in range(n_slabs):
                xo_ref[b, :, sl * LANES:(sl + 1) * LANES] = xt_s[sl, pl.ds(b, chunk, stride=n_seq), :]
    else:
        xo_ref[...] = y_s[...].reshape(chunk, n_seq, D_MODEL)

    @pl.when(i == n_steps - 1)
    def _():
        hre_ref[...] = hre_s[...] if batch_major else hre_s[...].T
        him_ref[...] = him_s[...] if batch_major else him_s[...].T


def _ssm_weights(lam_re, lam_im, log_dt, b_re, b_im, c_re, c_im):
    g, p = SSM_GROUPS, SSM_P
    n = g * SSM_GROUP
    rep = lambda z: jnp.repeat(z, SSM_GROUP, axis=0)
    bt_re = jnp.swapaxes(b_re, 1, 2).reshape(n, p)
    bt_im = jnp.swapaxes(b_im, 1, 2).reshape(n, p)
    args = [rep(lam_re), rep(lam_im), rep(log_dt.reshape(g, 1)), bt_re, bt_im]
    a_re, a_im, bb_re, bb_im = pl.pallas_call(
        _ssm_prep_kernel,
        grid=(1,),
        in_specs=[_const_spec(a.shape) for a in args],
        out_specs=tuple(_const_out((n, p)) for _ in range(4)),
        out_shape=tuple(jax.ShapeDtypeStruct((n, p), F32) for _ in range(4)),
        name="ssm_prep",
    )(*args)
    a_re = a_re[::SSM_GROUP]
    a_im = a_im[::SSM_GROUP]
    eye = jnp.eye(SSM_BLOCK_G, dtype=F32)

    def in_block(bb):
        bb = bb.reshape(SSM_BLOCKS, SSM_BLOCK_G, SSM_GROUP, p)
        return jnp.einsum("bgcp,gh->bgchp", bb, eye).reshape(SSM_BLOCKS, LANES, SSM_HALF)

    def out_block(cc):
        cc = cc.reshape(SSM_BLOCKS, SSM_BLOCK_G, SSM_GROUP, p)
        return jnp.einsum("bgcp,gh->bgphc", cc, eye).reshape(SSM_BLOCKS, SSM_HALF, LANES)

    w_b = jnp.concatenate([in_block(bb_re), in_block(bb_im)], axis=2).astype(BF16)
    n_state = SSM_GROUPS * SSM_P
    return (a_re.reshape(1, n_state), a_im.reshape(1, n_state), w_b,
            out_block(c_re).astype(BF16), out_block(c_im).astype(BF16))


def _ssm_layer(x, h_re0, h_im0, sw, chunk, batch_major):
    if batch_major:
        n_seq, t_len, _ = x.shape
        x_block = (n_seq, chunk, D_MODEL)
        x_map = lambda i: (0, i, 0)
    else:
        t_len, n_seq, _ = x.shape
        x_block = (chunk, n_seq, D_MODEL)
        x_map = lambda i: (i, 0, 0)
    rows = chunk * n_seq
    n_state = SSM_GROUPS * SSM_P
    args = [x, h_re0, h_im0, sw["norm"], sw["a_re"], sw["a_im"], sw["w_b"], sw["c_re"], sw["c_im"],
            sw["d_skip"], sw["w_glu"]]
    in_specs = [pl.BlockSpec(x_block, x_map)] + [_const_spec(a.shape) for a in args[1:]]
    state = jax.ShapeDtypeStruct((n_seq, n_state) if batch_major else (n_state, n_seq), F32)
    return pl.pallas_call(
        functools.partial(_ssm_kernel, n_seq=n_seq, chunk=chunk, batch_major=batch_major),
        grid=(t_len // chunk,),
        in_specs=in_specs,
        out_specs=(pl.BlockSpec(x_block, x_map), _const_out(state.shape), _const_out(state.shape)),
        out_shape=(jax.ShapeDtypeStruct(x.shape, F32), state, state),
        scratch_shapes=[pltpu.VMEM((D_MODEL // LANES, rows, LANES), F32),
                        pltpu.VMEM((rows, D_MODEL), F32), pltpu.VMEM((2, rows, 2 * SSM_HALF), F32),
                        pltpu.VMEM((rows, D_MODEL), F32), pltpu.VMEM((rows, D_MODEL), BF16),
                        pltpu.VMEM((n_seq, n_state), F32), pltpu.VMEM((n_seq, n_state), F32)],
        compiler_params=pltpu.CompilerParams(dimension_semantics=("arbitrary",),
                                             vmem_limit_bytes=VMEM_LIMIT),
        name="ssm_layer",
    )(*args)


def kernel(x_prompt, x_sample, state_ret, state_wkv, state_shift, state_ssm_re, state_ssm_im, norm_mix, w_in, ret_gn, mu_shift, wkv_w0, wkv_wB, wkv_a0, wkv_aB, wkv_gB, wkv_kk, wkv_ka, wkv_rk, wkv_ln_w, wkv_ln_b, w_out, ssm_lambda_re, ssm_lambda_im, ssm_log_dt, ssm_B_re, ssm_B_im, ssm_C_re, ssm_C_im, ssm_D, ssm_w_glu, mlp_norm, mlp_up, mlp_down, norm_f):
    lw = dict(
        norm=_row(norm_mix[0]), w_in=w_in[0].astype(BF16), ret_gn=_row(ret_gn[0]), mu=_row(mu_shift[0]),
        w0=_row(wkv_w0[0]), lora=_lora_block(wkv_wB[0], wkv_aB[0]), a0=_row(wkv_a0[0]),
        g_b=wkv_gB[0].astype(BF16), k_k=_row(wkv_kk[0]), k_a=_row(wkv_ka[0]), r_k=_row(wkv_rk[0]),
        ln_w=_row(wkv_ln_w[0]), ln_b=_row(wkv_ln_b[0]), w_out=w_out[0].astype(BF16))
    a_re, a_im, w_b, c_re, c_im = _ssm_weights(ssm_lambda_re[0], ssm_lambda_im[0], ssm_log_dt[0],
                                               ssm_B_re[0], ssm_B_im[0], ssm_C_re[0], ssm_C_im[0])
    sw = dict(norm=_row(norm_mix[1]), a_re=a_re, a_im=a_im, w_b=w_b, c_re=c_re, c_im=c_im,
              d_skip=_row(ssm_D[0]), w_glu=ssm_w_glu[0].astype(BF16))
    n_state = SSM_GROUPS * SSM_P
    nf = _row(norm_f)

    n_p, t_p, _ = x_prompt.shape
    n_s = x_sample.shape[0]
    x1, ret_p, wkv_p, shift_p = _mixer_prompt(x_prompt, lw)
    seq_last = lambda s: jnp.transpose(s, (1, 2, 3, 0))
    seq_first = lambda s: jnp.transpose(s, (3, 0, 1, 2))
    xs1, ret_s, wkv_s, shift_s = _mixer_sample(x_sample.reshape(n_s, D_MODEL), seq_last(state_ret[0]),
                                               seq_last(state_wkv[0]), state_shift[0], lw)
    ret_s, wkv_s = seq_first(ret_s), seq_first(wkv_s)
    x1, xs1 = _mlp(x1.reshape(n_p * t_p, D_MODEL), xs1, _row(mlp_norm[0]), mlp_up, mlp_down, nf, 0, False)
    zero_state = jnp.zeros((n_p, n_state), F32)
    x2, ssm_re_p, ssm_im_p = _ssm_layer(x1.reshape(n_p, t_p, D_MODEL), zero_state, zero_state, sw,
                                        SSM_CHUNK, True)
    ssm_seq_last = lambda s: jnp.transpose(s, (1, 2, 0)).reshape(n_state, n_s)
    xs2, ssm_re_s, ssm_im_s = _ssm_layer(
        xs1.reshape(1, n_s, D_MODEL), ssm_seq_last(state_ssm_re[0]), ssm_seq_last(state_ssm_im[0]),
        sw, 1, False)
    y_p, y_s = _mlp(x2.reshape(n_p * t_p, D_MODEL), xs2.reshape(n_s, D_MODEL), _row(mlp_norm[1]),
                    mlp_up, mlp_down, nf, 1, True)
    y_prompt = y_p.reshape(n_p, t_p, D_MODEL)
    ssm_re_p = ssm_re_p.reshape(n_p, SSM_GROUPS, SSM_P)
    ssm_im_p = ssm_im_p.reshape(n_p, SSM_GROUPS, SSM_P)
    ssm_seq_first = lambda s: jnp.transpose(s.reshape(SSM_GROUPS, SSM_P, n_s), (2, 0, 1))
    ssm_re_s, ssm_im_s = ssm_seq_first(ssm_re_s), ssm_seq_first(ssm_im_s)

    return (y_prompt, y_s.reshape(n_s, 1, D_MODEL),
            ret_p[None], wkv_p[None], shift_p[None], ssm_re_p[None], ssm_im_p[None],
            ret_s[None], wkv_s[None], shift_s[None], ssm_re_s[None], ssm_im_s[None])
```

```python
import functools
import math

import numpy as np
import jax
import jax.numpy as jnp
from jax import lax
from jax.experimental import pallas as pl
from jax.experimental.pallas import tpu as pltpu

F32 = jnp.float32
BF16 = jnp.bfloat16

LANES = 128
MXU_TILE = 256
VMEM_BYTES = 64 * 1024 * 1024

D_MODEL = 1024
N_HEADS = 8
HEAD_DIM = 64
HEADS_W = N_HEADS * HEAD_DIM
N_PAIRS = N_HEADS // 2
PAIR_W = 2 * HEAD_DIM
assert PAIR_W == LANES
ROPE_BASE = 10000.0
DECAY_LORA = 64
AAA_LORA = 64
GATE_LORA = 128
SHIFT_W = 3 * HEADS_W + DECAY_LORA + AAA_LORA + GATE_LORA
RET_COLS = 4 * HEADS_W
IN_W = RET_COLS + SHIFT_W
SSM_GROUP = 16
SSM_GROUPS = D_MODEL // SSM_GROUP
SSM_P = 64
SSM_BLOCK_G = LANES // SSM_GROUP
SSM_BLOCKS = SSM_GROUPS // SSM_BLOCK_G
SSM_HALF = SSM_BLOCK_G * SSM_P
D_FF = 4 * D_MODEL
RMS_EPS = 1e-6
GN_EPS = 1e-5
WKV_GN_EPS = 64e-5
PAST_LEN = 16384

MIX_CHUNK = 64
PROJ_PIECE = MXU_TILE
MIX_GROUP = 4
SSM_CHUNK = 64
MLP_ROWS = 512
FF_CHUNK = 1024

VMEM_LIMIT = VMEM_BYTES - 6 * 1024 * 1024


def _dot(a, b):
    return jnp.dot(a.astype(BF16), b.astype(BF16), preferred_element_type=F32)


def _dot_nt(a, b):
    return lax.dot_general(a.astype(BF16), b.astype(BF16), (((1,), (1,)), ((), ())),
                           preferred_element_type=F32)


def _dot_tn(a, b):
    return lax.dot_general(a.astype(BF16), b.astype(BF16), (((0,), (0,)), ((), ())),
                           preferred_element_type=F32)


def _split3(x):
    hi = x.astype(BF16)
    r1 = x - hi.astype(F32)
    mid = r1.astype(BF16)
    lo = (r1 - mid.astype(F32)).astype(BF16)
    return hi, mid, lo


def _dot_exact_lhs(a_bf16, x):
    hi, mid, lo = _split3(x)
    f = lambda p: jnp.dot(a_bf16, p, preferred_element_type=F32)
    return f(hi) + f(mid) + f(lo)


def _segsum(x, ones_blk):
    xb = x.astype(BF16)
    outs = [jnp.dot(xb[:, c * MXU_TILE:(c + 1) * MXU_TILE], ones_blk, preferred_element_type=F32)
            for c in range(x.shape[1] // MXU_TILE)]
    return jnp.concatenate(outs, axis=1)


def _rms(x, w):
    return x * lax.rsqrt(jnp.mean(x * x, axis=-1, keepdims=True) + RMS_EPS) * w


def _sigmoid(x):
    return 1.0 / (1.0 + jnp.exp(-x))


def _softplus(x):
    return jnp.maximum(x, 0.0) + jnp.log1p(jnp.exp(-jnp.abs(x)))


def _head_norm(z, eps, ones_blk):
    mu = _segsum(z, ones_blk) * (1.0 / HEAD_DIM)
    zc = z - mu
    var = _segsum(zc * zc, ones_blk) * (1.0 / HEAD_DIM)
    return zc * lax.rsqrt(var + eps)


def _rope(z, cos, sin_signed):
    lane = lax.broadcasted_iota(jnp.int32, (1, HEADS_W), 1) % HEAD_DIM
    swapped = jnp.where(lane < HEAD_DIM // 2,
                        pltpu.roll(z, HEADS_W - HEAD_DIM // 2, axis=1),
                        pltpu.roll(z, HEAD_DIM // 2, axis=1))
    return z * cos + swapped * sin_signed


def _wkv_features(xs, w0, lora_w, a0, g_b, k_k, k_a, r_k, ones_blk, between=lambda: None):
    r = xs[:, 0:HEADS_W]
    kw = xs[:, HEADS_W:2 * HEADS_W]
    vw = xs[:, 2 * HEADS_W:3 * HEADS_W]
    lora_w_in = DECAY_LORA + AAA_LORA
    lo = xs[:, 3 * HEADS_W:3 * HEADS_W + lora_w_in]
    lane = lax.broadcasted_iota(jnp.int32, (1, lora_w_in), 1)
    lo = jnp.where(lane < DECAY_LORA, jnp.tanh(lo), lo)
    ll = _dot(lo, lora_w)
    w_log = -_softplus(-(w0 + ll[:, 0:HEADS_W])) - 0.5
    log_decay = -jnp.exp(w_log)
    between()
    alr = _sigmoid(a0 + ll[:, HEADS_W:2 * HEADS_W])
    gate = _dot(_sigmoid(xs[:, 3 * HEADS_W + lora_w_in:SHIFT_W]), g_b)
    between()
    kk = kw * k_k
    kk = kk / jnp.maximum(jnp.sqrt(_segsum(kk * kk, ones_blk)), 1e-12)
    k_mod = kw * (1.0 + (alr - 1.0) * k_a)
    between()
    bonus = _segsum(r * k_mod * r_k, ones_blk) * vw
    return r, log_decay, k_mod, vw, -kk, kk * alr, gate, bonus


def _stack_masked(x2, m0):
    return jnp.concatenate([jnp.where(m0, x2, 0.0), jnp.where(m0, 0.0, x2)], axis=0)


def _stack_dup(x2):
    return jnp.concatenate([x2, x2], axis=0)


def _mixer_prompt_kernel(
        xprev_ref, xnext_ref, normw_ref, win_ref, cos_ref, sin_ref, qdec_ref, kdec_ref, dmask_ref, sdec_ref,
        retgn_ref, mu_ref, w0_ref, lora_ref, a0_ref, gb_ref, kk_ref, ka_ref, rk_ref, lnw_ref,
        lnb_ref, wout_ref, ones_ref, tril_ref, strict_ref, incl_ref, bd_ref,
        xo_ref, rets_ref, wkvs_ref, shift_ref,
        p_s, cat_s, rs_s, ws_s, carry_s, *, n_seq, chunk, group):
    i = pl.program_id(0)
    n_chunks = pl.num_programs(0) - 1
    C = chunk
    slot = i % 2

    @pl.when(i == 0)
    def _():
        rs_s[...] = jnp.zeros_like(rs_s)
        ws_s[...] = jnp.zeros_like(ws_s)
        carry_s[...] = jnp.zeros_like(carry_s)
        cat_s[...] = jnp.zeros_like(cat_s)
        p_s[0] = _dot(_rms(xprev_ref[...].reshape(n_seq * C, D_MODEL), normw_ref[...]), win_ref[...])

    m0 = lax.broadcasted_iota(jnp.int32, (1, PAIR_W), 1) < HEAD_DIM
    ones_blk = ones_ref[...]
    NB = group
    R = NB * C
    row_id = lax.broadcasted_iota(jnp.int32, (R, 1), 0)
    tile_rows = lambda ref: jnp.concatenate([ref[...]] * NB, axis=0)
    pairs = range(N_PAIRS)
    sls = [slice(pr * PAIR_W, (pr + 1) * PAIR_W) for pr in pairs]
    chains = [(s, pr) for s in range(NB) for pr in pairs]
    seq_rows = [slice(s * C, (s + 1) * C) for s in range(NB)]

    def per_group(gi, carry):
        rows = pl.ds(pl.multiple_of(gi * R, R), R)
        b0 = gi * NB
        ret_states = {(s, pr): rs_s[pr, b0 + s] for s, pr in chains}
        wkv_states = {(s, pr): ws_s[pr, b0 + s] for s, pr in chains}
        shift_rows = [carry_s[pl.ds(b0 + s, 1), :] for s in range(NB)]
        p_cur = p_s.at[slot]
        wp = p_cur[rows, RET_COLS:IN_W]
        hn_next = _rms(xnext_ref[pl.ds(b0, NB)].reshape(R, D_MODEL), normw_ref[...]).astype(BF16)
        cat_prev = cat_s[rows, :]

        def next_in_proj(c0):
            p_s[1 - slot, rows, c0:c0 + PROJ_PIECE] = jnp.dot(
                hn_next, win_ref[:, c0:c0 + PROJ_PIECE], preferred_element_type=F32)

        def prev_out_proj(c0):
            cols = slice(c0, c0 + PROJ_PIECE)
            out = jnp.dot(cat_prev, wout_ref[:, cols], preferred_element_type=F32)
            xo_ref[pl.ds(b0, NB), :, cols] = xprev_ref[pl.ds(b0, NB), :, cols] + out.reshape(NB, C, PROJ_PIECE)

        pieces = [functools.partial(next_in_proj, c0) for c0 in range(0, IN_W, PROJ_PIECE)]
        pieces += [functools.partial(prev_out_proj, c0) for c0 in range(0, D_MODEL, PROJ_PIECE)]

        def fill(n=1):
            for _ in range(min(n, len(pieces))):
                pieces.pop(0)()

        cos, sin = tile_rows(cos_ref), tile_rows(sin_ref)
        q = _rope(p_cur[rows, 0:HEADS_W], cos, sin)
        fill()
        k = _rope(p_cur[rows, HEADS_W:2 * HEADS_W], cos, sin) * (HEAD_DIM ** -0.5)
        fill()
        v = p_cur[rows, 2 * HEADS_W:3 * HEADS_W]
        g = p_cur[rows, 3 * HEADS_W:4 * HEADS_W]
        qd = q * tile_rows(qdec_ref)
        kd = k * tile_rows(kdec_ref)
        fill()
        cut = lambda z, s, pr: z[seq_rows[s], sls[pr]]
        r_sc = {c: _dot_nt(_stack_masked(cut(q, *c), m0), _stack_dup(cut(k, *c))) * dmask_ref[c[1]]
                for c in chains}
        r_inner = {c: _dot(r_sc[c], _stack_dup(cut(v, *c))) for c in chains}
        r_cross = {c: _dot(cut(qd, *c), ret_states[c]) for c in chains}
        new_ret = {c: sdec_ref[c[1]] * ret_states[c] + _dot_tn(cut(kd, *c), cut(v, *c)) * bd_ref[...]
                   for c in chains}
        o = jnp.concatenate(
            [jnp.concatenate([jnp.where(m0, r_inner[(s, pr)][0:C], r_inner[(s, pr)][C:2 * C])
                              + r_cross[(s, pr)] for pr in pairs], axis=1) for s in range(NB)], axis=0)
        fill()
        ret = _head_norm(o, GN_EPS, ones_blk) * retgn_ref[...]
        ret_out = (g * _sigmoid(g) * ret).astype(BF16)
        fill()

        prev = pltpu.roll(wp, 1, axis=0)
        for s in range(NB):
            prev = jnp.where(row_id == s * C, shift_rows[s], prev)
        xs = wp + (prev - wp) * mu_ref[...]
        fill()
        r, lw, k_mod, vw, a_vec, b_vec, gate, bonus = _wkv_features(
            xs, w0_ref[...], lora_ref[...], a0_ref[...], gb_ref[...], kk_ref[...], ka_ref[...],
            rk_ref[...], ones_blk, fill)
        cw = _dot_exact_lhs(tril_ref[...], lw)
        fill()
        cw_last = [cw[s * C + C - 1:(s + 1) * C, :] for s in range(NB)]
        cwl = jnp.concatenate([jnp.broadcast_to(z, (C, HEADS_W)) for z in cw_last], axis=0)
        r_t = r * jnp.exp(cw)
        a_t = a_vec * jnp.exp(cw - lw)
        fill()
        w_inv = jnp.exp(-cw)
        b_t = b_vec * w_inv
        k_t = k_mod * w_inv
        fill()
        w_end = jnp.exp(cwl - cw)
        b_h = b_vec * w_end
        k_h = k_mod * w_end
        fill()
        w_all = [jnp.exp(z) for z in cw_last]
        lhs = {c: jnp.concatenate([_stack_masked(cut(a_t, *c), m0), _stack_masked(cut(r_t, *c), m0)], axis=0)
               for c in chains}
        fill(len(pieces) - 2)
        sc = {c: _dot_nt(lhs[c], jnp.concatenate([cut(b_t, *c), cut(k_t, *c)], axis=0)) for c in chains}
        sc_swapped = {c: pltpu.roll(sc[c], C, axis=1) for c in chains}
        sc_b = {c: jnp.where(m0, sc[c], sc_swapped[c]) for c in chains}
        sc_k = {c: jnp.where(m0, sc_swapped[c], sc[c]) for c in chains}
        on_state = {c: _dot_nt(lhs[c], wkv_states[c]) for c in chains}
        vv = {c: _stack_dup(cut(vw, *c)) for c in chains}
        n_pow = {c: sc_b[c][0:2 * C] * strict_ref[...] for c in chains}
        u = {c: on_state[c][0:2 * C] + _dot(sc_k[c][0:2 * C] * strict_ref[...], vv[c]) for c in chains}
        n_steps_solve = int(math.log2(C))
        for it in range(n_steps_solve):
            u = {c: u[c] + _dot(n_pow[c], u[c]) for c in chains}
            if it + 1 < n_steps_solve:
                n_pow = {c: _dot(n_pow[c], n_pow[c]) for c in chains}
        uv = {c: jnp.concatenate([u[c], vv[c]], axis=0) for c in chains}
        y_st = {c: on_state[c][2 * C:4 * C] + _dot(
            jnp.concatenate([sc_b[c][2 * C:4 * C] * incl_ref[...],
                             sc_k[c][2 * C:4 * C] * incl_ref[...]], axis=1), uv[c]) for c in chains}
        new_wkv = {c: wkv_states[c] * w_all[c[0]][:, sls[c[1]]] + bd_ref[...] * _dot_tn(
            uv[c], jnp.concatenate([_stack_masked(cut(b_h, *c), m0), _stack_masked(cut(k_h, *c), m0)], axis=0))
            for c in chains}
        y = jnp.concatenate(
            [jnp.concatenate([jnp.where(m0, y_st[(s, pr)][0:C], y_st[(s, pr)][C:2 * C]) for pr in pairs], axis=1)
             for s in range(NB)], axis=0)
        fill(len(pieces))
        yn = _head_norm(y, WKV_GN_EPS, ones_blk) * lnw_ref[...] + lnb_ref[...]
        cat_s[rows, 0:HEADS_W] = ret_out
        cat_s[rows, HEADS_W:2 * HEADS_W] = ((yn + bonus) * gate).astype(BF16)
        for s in range(NB):
            carry_s[pl.ds(b0 + s, 1), :] = wp[s * C + C - 1:(s + 1) * C, :]
        for s, pr in chains:
            rs_s[pr, b0 + s] = new_ret[(s, pr)]
            ws_s[pr, b0 + s] = new_wkv[(s, pr)]
        return carry

    @pl.when(i < n_chunks)
    def _():
        lax.fori_loop(0, n_seq // NB, per_group, 0)

    @pl.when(i == n_chunks)
    def _():
        out = jnp.dot(cat_s[...], wout_ref[...], preferred_element_type=F32)
        xo_ref[...] = xprev_ref[...] + out.reshape(n_seq, C, D_MODEL)

    @pl.when(i == n_chunks - 1)
    def _():
        shift_ref[...] = carry_s[...]
        for b in range(n_seq):
            for pr in range(N_PAIRS):
                rs = rs_s[pr, b]
                ws = ws_s[pr, b]
                rets_ref[b, 2 * pr] = rs[0:HEAD_DIM, 0:HEAD_DIM]
                rets_ref[b, 2 * pr + 1] = rs[HEAD_DIM:PAIR_W, HEAD_DIM:PAIR_W]
                wkvs_ref[b, 2 * pr] = ws[0:HEAD_DIM, 0:HEAD_DIM]
                wkvs_ref[b, 2 * pr + 1] = ws[HEAD_DIM:PAIR_W, HEAD_DIM:PAIR_W]


def _const_spec(shape):
    nd = len(shape)
    return pl.BlockSpec(shape, lambda *_: (0,) * nd, pipeline_mode=pl.Buffered(1))


def _const_out(shape):
    nd = len(shape)
    return pl.BlockSpec(shape, lambda *_: (0,) * nd)


def _const(a, dtype=F32):
    return jnp.asarray(np.asarray(a, np.float64), dtype=dtype)


def _retention_tables(chunk):
    log_g = np.log1p(-np.exp2(-5.0 - np.arange(N_HEADS, dtype=np.float64)))
    lane_g = np.repeat(log_g, HEAD_DIM)[None, :]
    idx = np.arange(chunk, dtype=np.float64)
    qdec = np.exp((idx + 1.0)[:, None] * lane_g)
    kdec = np.exp((chunk - 1.0 - idx)[:, None] * lane_g)
    rel = idx[:, None] - idx[None, :]
    dm = np.where(rel >= 0, np.exp(np.maximum(rel, 0.0)[None] * log_g[:, None, None]), 0.0)
    zero = np.zeros((chunk, chunk))
    dmask = np.stack([np.block([[dm[2 * p], zero], [zero, dm[2 * p + 1]]]) for p in range(N_PAIRS)])
    cdec = np.exp(chunk * log_g)
    hz = np.zeros((HEAD_DIM, HEAD_DIM))
    ho = np.ones((HEAD_DIM, HEAD_DIM))
    sdec = np.stack([np.block([[cdec[2 * p] * ho, hz], [hz, cdec[2 * p + 1] * ho]])
                     for p in range(N_PAIRS)])
    return _const(qdec), _const(kdec), _const(dmask), _const(sdec), cdec


def _rope_tables(pos):
    half = HEAD_DIM // 2
    inv_freq = ROPE_BASE ** (-np.arange(half, dtype=np.float64) / half)
    ang = np.asarray(pos, np.float64)[:, None] * inv_freq[None, :]
    cos = np.cos(ang)
    sin = np.sin(ang)
    cos_t = np.tile(np.concatenate([cos, cos], axis=1), (1, N_HEADS))
    sin_t = np.tile(np.concatenate([-sin, sin], axis=1), (1, N_HEADS))
    return _const(cos_t), _const(sin_t)


def _block_masks(chunk, group):
    i = np.arange(2 * chunk)
    same = (i[:, None] // chunk) == (i[None, :] // chunk)
    strict = same & (i[:, None] > i[None, :])
    incl = same & (i[:, None] >= i[None, :])
    j = np.arange(PAIR_W)
    bd = (j[:, None] // HEAD_DIM) == (j[None, :] // HEAD_DIM)
    t = np.arange(chunk)
    tril = np.kron(np.eye(group), t[:, None] >= t[None, :])
    o = np.arange(MXU_TILE)
    ones_blk = (o[:, None] // HEAD_DIM) == (o[None, :] // HEAD_DIM)
    return _const(strict), _const(incl), _const(bd), _const(tril, BF16), _const(ones_blk, BF16)


def _lora_block(w_b, a_b):
    z = jnp.zeros_like(w_b)
    return jnp.concatenate([jnp.concatenate([w_b, z], axis=1),
                            jnp.concatenate([z, a_b], axis=1)], axis=0).astype(BF16)


def _row(v):
    return v.reshape(1, -1).astype(F32)


def _mixer_prompt(x, lw):
    n_seq, t_len, _ = x.shape
    C = MIX_CHUNK
    n_chunks = t_len // C
    last = n_chunks - 1
    cos_t, sin_t = _rope_tables(np.arange(t_len))
    qdec, kdec, dmask, sdec, _ = _retention_tables(C)
    strict, incl, bd, tril, ones_blk = _block_masks(C, MIX_GROUP)
    prev_chunk = lambda i: (0, jnp.maximum(i - 1, 0), 0)
    in_specs = [
        pl.BlockSpec((n_seq, C, D_MODEL), prev_chunk),
        pl.BlockSpec((n_seq, C, D_MODEL), lambda i: (0, jnp.minimum(i + 1, last), 0)),
        _const_spec((1, D_MODEL)),
        _const_spec((D_MODEL, IN_W)),
        pl.BlockSpec((C, HEADS_W), lambda i: (jnp.minimum(i, last), 0)),
        pl.BlockSpec((C, HEADS_W), lambda i: (jnp.minimum(i, last), 0)),
    ]
    tail = [qdec, kdec, dmask, sdec, lw["ret_gn"], lw["mu"], lw["w0"], lw["lora"], lw["a0"], lw["g_b"],
            lw["k_k"], lw["k_a"], lw["r_k"], lw["ln_w"], lw["ln_b"], lw["w_out"], ones_blk, tril,
            strict, incl, bd]
    in_specs += [_const_spec(a.shape) for a in tail]
    out_shape = (
        jax.ShapeDtypeStruct((n_seq, t_len, D_MODEL), F32),
        jax.ShapeDtypeStruct((n_seq, N_HEADS, HEAD_DIM, HEAD_DIM), F32),
        jax.ShapeDtypeStruct((n_seq, N_HEADS, HEAD_DIM, HEAD_DIM), F32),
        jax.ShapeDtypeStruct((n_seq, SHIFT_W), F32),
    )
    out_specs = (
        pl.BlockSpec((n_seq, C, D_MODEL), prev_chunk),
        _const_out((n_seq, N_HEADS, HEAD_DIM, HEAD_DIM)),
        _const_out((n_seq, N_HEADS, HEAD_DIM, HEAD_DIM)),
        _const_out((n_seq, SHIFT_W)),
    )
    scratch = [
        pltpu.VMEM((2, n_seq * C, IN_W), F32),
        pltpu.VMEM((n_seq * C, 2 * HEADS_W), BF16),
        pltpu.VMEM((N_PAIRS, n_seq, PAIR_W, PAIR_W), F32),
        pltpu.VMEM((N_PAIRS, n_seq, PAIR_W, PAIR_W), F32),
        pltpu.VMEM((n_seq, SHIFT_W), F32),
    ]
    return pl.pallas_call(
        functools.partial(_mixer_prompt_kernel, n_seq=n_seq, chunk=C, group=MIX_GROUP),
        grid=(n_chunks + 1,),
        in_specs=in_specs, out_specs=out_specs, out_shape=out_shape, scratch_shapes=scratch,
        compiler_params=pltpu.CompilerParams(dimension_semantics=("arbitrary",),
                                             vmem_limit_bytes=VMEM_LIMIT),
        name="mixer_prompt",
    )(x, x, lw["norm"], lw["w_in"], cos_t, sin_t, *tail)


def _mixer_sample_pre_kernel(
        x_ref, shift_ref, normw_ref, win_ref, cos_ref, sin_ref, qdec_ref, mu_ref, w0_ref, lora_ref,
        a0_ref, gb_ref, kk_ref, ka_ref, rk_ref, ones_ref, feat_ref, feat_t_ref, newshift_ref):
    x = x_ref[...]
    p = _dot(_rms(x, normw_ref[...]), win_ref[...])
    q = _rope(p[:, 0:HEADS_W], cos_ref[...], sin_ref[...])
    k = _rope(p[:, HEADS_W:2 * HEADS_W], cos_ref[...], sin_ref[...]) * (HEAD_DIM ** -0.5)
    wp = p[:, RET_COLS:IN_W]
    xs = wp + (shift_ref[...] - wp) * mu_ref[...]
    r, lw, k_mod, vw, a_vec, b_vec, gate, bonus = _wkv_features(
        xs, w0_ref[...], lora_ref[...], a0_ref[...], gb_ref[...], kk_ref[...], ka_ref[...],
        rk_ref[...], ones_ref[...])
    newshift_ref[...] = wp
    state_feats = [q, q * qdec_ref[...], k, p[:, 2 * HEADS_W:3 * HEADS_W], r, jnp.exp(lw), k_mod, vw,
                   a_vec, b_vec]
    for n, f in enumerate(state_feats):
        feat_t_ref[n * HEADS_W:(n + 1) * HEADS_W, :] = f.T
    for n, f in enumerate([p[:, 3 * HEADS_W:4 * HEADS_W], gate, bonus]):
        feat_ref[:, n * HEADS_W:(n + 1) * HEADS_W] = f


_F_Q, _F_QD, _F_K, _F_V, _F_R, _F_W, _F_KM, _F_VW, _F_A, _F_B = range(10)
_N_STATE_FEATS = 10
_F_G, _F_GATE, _F_BONUS = range(3)
_N_ROW_FEATS = 3


def _mixer_sample_state_kernel(feat_t_ref, sdec_ref, ret_ref, wkv_ref, o_t_ref, reto_ref, wkvo_ref):
    h = pl.program_id(0)

    def head_rows(n):
        return feat_t_ref[pl.ds(pl.multiple_of(n * HEADS_W + h * HEAD_DIM, HEAD_DIM), HEAD_DIM), :]

    def head_row(n, i):
        return feat_t_ref[pl.ds(n * HEADS_W + h * HEAD_DIM + i, 1), :]

    a, w, b_vec, k_mod, r = (head_rows(n) for n in (_F_A, _F_W, _F_B, _F_KM, _F_R))

    def wkv_row(i, carry):
        s = wkv_ref[0, i]
        sa = jnp.sum(s * a, axis=0, keepdims=True)
        s_new = s * w + sa * b_vec + head_row(_F_VW, i) * k_mod
        wkvo_ref[0, i] = s_new
        o_t_ref[1, 0, pl.ds(i, 1), :] = jnp.sum(s_new * r, axis=0, keepdims=True)
        return carry

    lax.fori_loop(0, HEAD_DIM, wkv_row, 0, unroll=4)

    v, q, k = head_rows(_F_V), head_rows(_F_Q), head_rows(_F_K)
    g = sdec_ref[h]

    def ret_row(d, acc):
        s = ret_ref[0, d]
        reto_ref[0, d] = g * s + head_row(_F_K, d) * v
        return acc + head_row(_F_QD, d) * s

    cross = lax.fori_loop(0, HEAD_DIM, ret_row, jnp.zeros_like(v), unroll=4)
    o_t_ref[0, 0] = cross + jnp.sum(q * k, axis=0, keepdims=True) * v


def _mixer_sample_post_kernel(
        x_ref, feat_ref, o_t_ref, retgn_ref, lnw_ref, lnb_ref, wout_ref, ones_ref, xo_ref):
    ones_blk = ones_ref[...]
    g = feat_ref[:, _F_G * HEADS_W:(_F_G + 1) * HEADS_W]
    gate = feat_ref[:, _F_GATE * HEADS_W:(_F_GATE + 1) * HEADS_W]
    bonus = feat_ref[:, _F_BONUS * HEADS_W:(_F_BONUS + 1) * HEADS_W]
    o = o_t_ref[...].T
    ret = _head_norm(o[:, 0:HEADS_W], GN_EPS, ones_blk) * retgn_ref[...]
    ret_out = g * _sigmoid(g) * ret
    yn = _head_norm(o[:, HEADS_W:2 * HEADS_W], WKV_GN_EPS, ones_blk) * lnw_ref[...] + lnb_ref[...]
    wkv_out = (yn + bonus) * gate
    cat = jnp.concatenate([ret_out, wkv_out], axis=1)
    xo_ref[...] = x_ref[...] + _dot(cat, wout_ref[...])


def _mixer_sample(x, ret_t, wkv_t, shift0, lw):
    n = x.shape[0]
    cos_t, sin_t = _rope_tables(np.full((1,), PAST_LEN))
    qdec, _, _, _, cdec = _retention_tables(1)
    _, _, _, _, ones_blk = _block_masks(1, 1)
    args = [x, shift0, lw["norm"], lw["w_in"], cos_t, sin_t, qdec, lw["mu"], lw["w0"], lw["lora"],
            lw["a0"], lw["g_b"], lw["k_k"], lw["k_a"], lw["r_k"], ones_blk]
    feat, feat_t, new_shift = pl.pallas_call(
        _mixer_sample_pre_kernel,
        grid=(1,),
        in_specs=[_const_spec(a.shape) for a in args],
        out_specs=(_const_out((n, _N_ROW_FEATS * HEADS_W)), _const_out((_N_STATE_FEATS * HEADS_W, n)),
                   _const_out((n, SHIFT_W))),
        out_shape=(jax.ShapeDtypeStruct((n, _N_ROW_FEATS * HEADS_W), F32),
                   jax.ShapeDtypeStruct((_N_STATE_FEATS * HEADS_W, n), F32),
                   jax.ShapeDtypeStruct((n, SHIFT_W), F32)),
        compiler_params=pltpu.CompilerParams(vmem_limit_bytes=VMEM_LIMIT),
        name="mixer_sample_pre",
    )(*args)

    sdec = _const(np.broadcast_to(cdec[:, None, None], (N_HEADS, 1, n)))
    state_spec = pl.BlockSpec((1, HEAD_DIM, HEAD_DIM, n), lambda h: (h, 0, 0, 0))
    o_spec = pl.BlockSpec((2, 1, HEAD_DIM, n), lambda h: (0, h, 0, 0))
    o_t, ret_new, wkv_new = pl.pallas_call(
        _mixer_sample_state_kernel,
        grid=(N_HEADS,),
        in_specs=[_const_spec(feat_t.shape), _const_spec(sdec.shape), state_spec, state_spec],
        out_specs=(o_spec, state_spec, state_spec),
        out_shape=(jax.ShapeDtypeStruct((2, N_HEADS, HEAD_DIM, n), F32),
                   jax.ShapeDtypeStruct(ret_t.shape, F32), jax.ShapeDtypeStruct(wkv_t.shape, F32)),
        compiler_params=pltpu.CompilerParams(dimension_semantics=("arbitrary",),
                                             vmem_limit_bytes=VMEM_LIMIT),
        name="mixer_sample_state",
    )(feat_t, sdec, ret_t, wkv_t)

    args = [x, feat, o_t.reshape(2 * HEADS_W, n), lw["ret_gn"], lw["ln_w"], lw["ln_b"], lw["w_out"], ones_blk]
    x1 = pl.pallas_call(
        _mixer_sample_post_kernel,
        grid=(1,),
        in_specs=[_const_spec(a.shape) for a in args],
        out_specs=_const_out((n, D_MODEL)),
        out_shape=jax.ShapeDtypeStruct((n, D_MODEL), F32),
        compiler_params=pltpu.CompilerParams(vmem_limit_bytes=VMEM_LIMIT),
        name="mixer_sample_post",
    )(*args)
    return x1, ret_new, wkv_new, new_shift


def _mlp_kernel(xa_ref, xb_ref, normw_ref, wup_ref, wdown_ref, normf_ref, oa_ref, ob_ref,
                *, final_norm, steps_a):
    i = pl.program_id(0)

    def run(x_ref, o_ref):
        x = x_ref[...]
        hn = _rms(x, normw_ref[...]).astype(BF16)
        acc = x
        for c in range(D_FF // FF_CHUNK):
            sl = slice(c * FF_CHUNK, (c + 1) * FF_CHUNK)
            hid = jnp.dot(hn, wup_ref[:, sl].astype(BF16), preferred_element_type=F32)
            hid = jnp.square(jnp.maximum(hid, 0.0)).astype(BF16)
            acc = acc + jnp.dot(hid, wdown_ref[sl, :].astype(BF16), preferred_element_type=F32)
        if final_norm:
            acc = _rms(acc, normf_ref[...])
        o_ref[...] = acc

    @pl.when(i < steps_a)
    def _():
        run(xa_ref, oa_ref)

    @pl.when(i == steps_a)
    def _():
        run(xb_ref, ob_ref)


def _mlp(xa, xb, norm_w, w_up, w_down, norm_f, layer, final_norm):
    rows_a, rows_b = xa.shape[0], xb.shape[0]
    steps_a = rows_a // MLP_ROWS
    pick = lambda *_: (layer, 0, 0)
    block_a = pl.BlockSpec((MLP_ROWS, D_MODEL), lambda i: (jnp.minimum(i, steps_a - 1), 0))
    return pl.pallas_call(
        functools.partial(_mlp_kernel, final_norm=final_norm, steps_a=steps_a),
        grid=(steps_a + 1,),
        in_specs=[block_a, _const_spec((rows_b, D_MODEL)),
                  _const_spec((1, D_MODEL)),
                  pl.BlockSpec((None, D_MODEL, D_FF), pick, pipeline_mode=pl.Buffered(1)),
                  pl.BlockSpec((None, D_FF, D_MODEL), pick, pipeline_mode=pl.Buffered(1)),
                  _const_spec((1, D_MODEL))],
        out_specs=(block_a, _const_out((rows_b, D_MODEL))),
        out_shape=(jax.ShapeDtypeStruct((rows_a, D_MODEL), F32), jax.ShapeDtypeStruct((rows_b, D_MODEL), F32)),
        compiler_params=pltpu.CompilerParams(dimension_semantics=("arbitrary",),
                                             vmem_limit_bytes=VMEM_LIMIT),
        name="mlp",
    )(xa, xb, norm_w, w_up, w_down, norm_f)


def _ssm_prep_kernel(lre_ref, lim_ref, logdt_ref, lre_rep_ref, lim_rep_ref, logdt_rep_ref, bre_ref, bim_ref,
                     cre_in_ref, cim_in_ref, tile_ref, eye_ref, are_ref, aim_ref, wb_ref, cre_ref, cim_ref):
    def discretise(lre, lim, logdt):
        lre = jnp.minimum(lre, -1e-4)
        dt = jnp.exp(logdt)
        mag = jnp.exp(lre * dt)
        return lre, mag * jnp.cos(lim * dt), mag * jnp.sin(lim * dt)

    _, are_ref[...], aim_ref[...] = discretise(lre_ref[...], lim_ref[...], logdt_ref[...])
    lim = lim_rep_ref[...]
    lre, are, aim = discretise(lre_rep_ref[...], lim, logdt_rep_ref[...])
    den = lre * lre + lim * lim
    nre = are - 1.0
    cre = (nre * lre + aim * lim) / den
    cim = (aim * lre - nre * lim) / den
    bre = bre_ref[...]
    bim = bim_ref[...]
    bb = (cre * bre - cim * bim, cre * bim + cim * bre)

    in_mask = (lax.broadcasted_iota(jnp.int32, (LANES, 1), 0) // SSM_GROUP
               == lax.broadcasted_iota(jnp.int32, (1, SSM_HALF), 1) // SSM_P)
    out_group = lax.broadcasted_iota(jnp.int32, (1, LANES), 1) // SSM_GROUP
    for blk in range(SSM_BLOCKS):
        rows = slice(blk * LANES, (blk + 1) * LANES)
        for part in range(2):
            tiled = jnp.dot(bb[part][rows].astype(BF16), tile_ref[...], preferred_element_type=F32)
            wb_ref[blk, :, part * SSM_HALF:(part + 1) * SSM_HALF] = jnp.where(in_mask, tiled, 0.0).astype(BF16)
        for src, dst in ((cre_in_ref, cre_ref), (cim_in_ref, cim_ref)):
            c_t = _dot_tn(src[rows, :], eye_ref[...])
            dst[blk] = jnp.concatenate([jnp.where(out_group == gl, c_t, 0.0) for gl in range(SSM_BLOCK_G)],
                                       axis=0).astype(BF16)


def _gelu_exact(x):
    return 0.5 * x * (1.0 + lax.erf(x * (2.0 ** -0.5)))


def _ssm_kernel(x_ref, hre0_ref, him0_ref, normw_ref, are_ref, aim_ref, wb_ref, cre_ref, cim_ref, dskip_ref,
                wglu_ref, xo_ref, hre_ref, him_ref, xt_s, u_s, bu_s, y_s, hg_s, hre_s, him_s,
                *, n_seq, chunk, batch_major):
    i = pl.program_id(0)
    n_steps = pl.num_programs(0)
    rows_all = chunk * n_seq
    n_slabs = D_MODEL // LANES

    @pl.when(i == 0)
    def _():
        hre_s[...] = hre0_ref[...] if batch_major else hre0_ref[...].T
        him_s[...] = him0_ref[...] if batch_major else him0_ref[...].T

    if batch_major:
        for b in range(n_seq):
            for sl in range(n_slabs):
                xt_s[sl, pl.ds(b, chunk, stride=n_seq), :] = x_ref[b, :, sl * LANES:(sl + 1) * LANES]
        ssq = sum(jnp.sum(jnp.square(xt_s[sl]), axis=-1, keepdims=True) for sl in range(n_slabs))
        inv = lax.rsqrt(ssq * (1.0 / D_MODEL) + RMS_EPS)
        for sl in range(n_slabs):
            cols = slice(sl * LANES, (sl + 1) * LANES)
            u_s[:, cols] = xt_s[sl] * inv * normw_ref[:, cols]
    else:
        u_s[...] = _rms(x_ref[...].reshape(rows_all, D_MODEL), normw_ref[...])

    def input_proj(blk, part):
        c = slice(part * SSM_HALF, (part + 1) * SSM_HALF)
        bu_s[blk % 2, :, c] = _dot(u_s[:, blk * LANES:(blk + 1) * LANES], wb_ref[blk, :, c])

    def output_proj(blk, part):
        buf = bu_s.at[blk % 2]
        cols = slice(blk * LANES, (blk + 1) * LANES)
        if part == 0:
            y_s[:, cols] = _dot(buf[:, 0:SSM_HALF], cre_ref[blk])
        else:
            y_s[:, cols] = y_s[:, cols] - _dot(buf[:, SSM_HALF:2 * SSM_HALF], cim_ref[blk])

    input_proj(0, 0)
    input_proj(0, 1)
    for blk in range(SSM_BLOCKS):
        neighbours = []
        if blk >= 1:
            neighbours += [functools.partial(output_proj, blk - 1, 0), functools.partial(output_proj, blk - 1, 1)]
        if blk + 1 < SSM_BLOCKS:
            neighbours += [functools.partial(input_proj, blk + 1, 0), functools.partial(input_proj, blk + 1, 1)]
        buf = bu_s.at[blk % 2]
        cols = slice(blk * SSM_HALF, (blk + 1) * SSM_HALF)
        a_re = jnp.broadcast_to(are_ref[:, cols], (n_seq, SSM_HALF))
        a_im = jnp.broadcast_to(aim_ref[:, cols], (n_seq, SSM_HALF))
        h_re = hre_s[:, cols]
        h_im = him_s[:, cols]
        seg = -(-chunk // max(len(neighbours), 1))
        for t in range(chunk):
            if t % seg == 0 and neighbours:
                neighbours.pop(0)()
            rows = slice(t * n_seq, (t + 1) * n_seq)
            n_re = a_re * h_re - a_im * h_im + buf[rows, 0:SSM_HALF]
            n_im = a_re * h_im + a_im * h_re + buf[rows, SSM_HALF:2 * SSM_HALF]
            buf[rows, 0:SSM_HALF] = n_re
            buf[rows, SSM_HALF:2 * SSM_HALF] = n_im
            h_re, h_im = n_re, n_im
        for f in neighbours:
            f()
        hre_s[:, cols] = h_re
        him_s[:, cols] = h_im
    output_proj(SSM_BLOCKS - 1, 0)
    output_proj(SSM_BLOCKS - 1, 1)
    hg_s[...] = _gelu_exact(y_s[...] + dskip_ref[...] * u_s[...]).astype(BF16)
    glu_w = MXU_TILE
    for c in range(D_MODEL // glu_w):
        cols = slice(c * glu_w, (c + 1) * glu_w)
        hg = hg_s[...]
        val = jnp.dot(hg, wglu_ref[:, cols], preferred_element_type=F32)
        gate = jnp.dot(hg, wglu_ref[:, D_MODEL + c * glu_w:D_MODEL + (c + 1) * glu_w],
                       preferred_element_type=F32)
        out = val * _sigmoid(gate)
        if batch_major:
            for sl in range(c * glu_w // LANES, (c + 1) * glu_w // LANES):
                xt_s[sl] = xt_s[sl] + out[:, sl * LANES - c * glu_w:(sl + 1) * LANES - c * glu_w]
        else:
            y_s[:, cols] = x_ref[...].reshape(rows_all, D_MODEL)[:, cols] + out
    if batch_major:
        for b in range(n_seq):
            for sl in range(n_slabs):
                xo_ref[b, :, sl * LANES:(sl + 1) * LANES] = xt_s[sl, pl.ds(b, chunk, stride=n_seq), :]
    else:
        xo_ref[...] = y_s[...].reshape(chunk, n_seq, D_MODEL)

    @pl.when(i == n_steps - 1)
    def _():
        hre_ref[...] = hre_s[...] if batch_major else hre_s[...].T
        him_ref[...] = him_s[...] if batch_major else him_s[...].T


def _ssm_weights(lam_re, lam_im, log_dt, b_re, b_im, c_re, c_im):
    g, p = SSM_GROUPS, SSM_P
    n = g * SSM_GROUP
    rep = lambda z: jnp.repeat(z, SSM_GROUP, axis=0)
    log_dt = log_dt.reshape(g, 1)
    bt_re = jnp.swapaxes(b_re, 1, 2).reshape(n, p)
    bt_im = jnp.swapaxes(b_im, 1, 2).reshape(n, p)
    tile = _const(np.tile(np.eye(p), (1, SSM_BLOCK_G)), BF16)
    eye = _const(np.eye(LANES), BF16)
    args = [lam_re, lam_im, log_dt, rep(lam_re), rep(lam_im), rep(log_dt), bt_re, bt_im,
            c_re.reshape(n, p), c_im.reshape(n, p), tile, eye]
    out_shapes = ((g, p), (g, p), (SSM_BLOCKS, LANES, 2 * SSM_HALF), (SSM_BLOCKS, SSM_HALF, LANES),
                  (SSM_BLOCKS, SSM_HALF, LANES))
    out_dtypes = (F32, F32, BF16, BF16, BF16)
    a_re, a_im, w_b, c_re_bd, c_im_bd = pl.pallas_call(
        _ssm_prep_kernel,
        grid=(1,),
        in_specs=[_const_spec(a.shape) for a in args],
        out_specs=tuple(_const_out(s) for s in out_shapes),
        out_shape=tuple(jax.ShapeDtypeStruct(s, d) for s, d in zip(out_shapes, out_dtypes)),
        name="ssm_prep",
    )(*args)
    n_state = g * p
    return a_re.reshape(1, n_state), a_im.reshape(1, n_state), w_b, c_re_bd, c_im_bd


def _ssm_layer(x, h_re0, h_im0, sw, chunk, batch_major):
    if batch_major:
        n_seq, t_len, _ = x.shape
        x_block = (n_seq, chunk, D_MODEL)
        x_map = lambda i: (0, i, 0)
    else:
        t_len, n_seq, _ = x.shape
        x_block = (chunk, n_seq, D_MODEL)
        x_map = lambda i: (i, 0, 0)
    rows = chunk * n_seq
    n_state = SSM_GROUPS * SSM_P
    args = [x, h_re0, h_im0, sw["norm"], sw["a_re"], sw["a_im"], sw["w_b"], sw["c_re"], sw["c_im"],
            sw["d_skip"], sw["w_glu"]]
    in_specs = [pl.BlockSpec(x_block, x_map)] + [_const_spec(a.shape) for a in args[1:]]
    state = jax.ShapeDtypeStruct((n_seq, n_state) if batch_major else (n_state, n_seq), F32)
    return pl.pallas_call(
        functools.partial(_ssm_kernel, n_seq=n_seq, chunk=chunk, batch_major=batch_major),
        grid=(t_len // chunk,),
        in_specs=in_specs,
        out_specs=(pl.BlockSpec(x_block, x_map), _const_out(state.shape), _const_out(state.shape)),
        out_shape=(jax.ShapeDtypeStruct(x.shape, F32), state, state),
        scratch_shapes=[pltpu.VMEM((D_MODEL // LANES, rows, LANES), F32),
                        pltpu.VMEM((rows, D_MODEL), F32), pltpu.VMEM((2, rows, 2 * SSM_HALF), F32),
                        pltpu.VMEM((rows, D_MODEL), F32), pltpu.VMEM((rows, D_MODEL), BF16),
                        pltpu.VMEM((n_seq, n_state), F32), pltpu.VMEM((n_seq, n_state), F32)],
        compiler_params=pltpu.CompilerParams(dimension_semantics=("arbitrary",),
                                             vmem_limit_bytes=VMEM_LIMIT),
        name="ssm_layer",
    )(*args)


def kernel(x_prompt, x_sample, state_ret, state_wkv, state_shift, state_ssm_re, state_ssm_im, norm_mix, w_in, ret_gn, mu_shift, wkv_w0, wkv_wB, wkv_a0, wkv_aB, wkv_gB, wkv_kk, wkv_ka, wkv_rk, wkv_ln_w, wkv_ln_b, w_out, ssm_lambda_re, ssm_lambda_im, ssm_log_dt, ssm_B_re, ssm_B_im, ssm_C_re, ssm_C_im, ssm_D, ssm_w_glu, mlp_norm, mlp_up, mlp_down, norm_f):
    lw = dict(
        norm=_row(norm_mix[0]), w_in=w_in[0].astype(BF16), ret_gn=_row(ret_gn[0]), mu=_row(mu_shift[0]),
        w0=_row(wkv_w0[0]), lora=_lora_block(wkv_wB[0], wkv_aB[0]), a0=_row(wkv_a0[0]),
        g_b=wkv_gB[0].astype(BF16), k_k=_row(wkv_kk[0]), k_a=_row(wkv_ka[0]), r_k=_row(wkv_rk[0]),
        ln_w=_row(wkv_ln_w[0]), ln_b=_row(wkv_ln_b[0]), w_out=w_out[0].astype(BF16))
    a_re, a_im, w_b, c_re, c_im = _ssm_weights(ssm_lambda_re[0], ssm_lambda_im[0], ssm_log_dt[0],
                                               ssm_B_re[0], ssm_B_im[0], ssm_C_re[0], ssm_C_im[0])
    sw = dict(norm=_row(norm_mix[1]), a_re=a_re, a_im=a_im, w_b=w_b, c_re=c_re, c_im=c_im,
              d_skip=_row(ssm_D[0]), w_glu=ssm_w_glu[0].astype(BF16))
    n_state = SSM_GROUPS * SSM_P
    nf = _row(norm_f)

    n_p, t_p, _ = x_prompt.shape
    n_s = x_sample.shape[0]
    x1, ret_p, wkv_p, shift_p = _mixer_prompt(x_prompt, lw)
    seq_last = lambda s: jnp.transpose(s, (1, 2, 3, 0))
    seq_first = lambda s: jnp.transpose(s, (3, 0, 1, 2))
    xs1, ret_s, wkv_s, shift_s = _mixer_sample(x_sample.reshape(n_s, D_MODEL), seq_last(state_ret[0]),
                                               seq_last(state_wkv[0]), state_shift[0], lw)
    ret_s, wkv_s = seq_first(ret_s), seq_first(wkv_s)
    x1, xs1 = _mlp(x1.reshape(n_p * t_p, D_MODEL), xs1, _row(mlp_norm[0]), mlp_up, mlp_down, nf, 0, False)
    zero_state = jnp.zeros((n_p, n_state), F32)
    x2, ssm_re_p, ssm_im_p = _ssm_layer(x1.reshape(n_p, t_p, D_MODEL), zero_state, zero_state, sw,
                                        SSM_CHUNK, True)
    ssm_seq_last = lambda s: jnp.transpose(s, (1, 2, 0)).reshape(n_state, n_s)
    xs2, ssm_re_s, ssm_im_s = _ssm_layer(
        xs1.reshape(1, n_s, D_MODEL), ssm_seq_last(state_ssm_re[0]), ssm_seq_last(state_ssm_im[0]),
        sw, 1, False)
    y_p, y_s = _mlp(x2.reshape(n_p * t_p, D_MODEL), xs2.reshape(n_s, D_MODEL), _row(mlp_norm[1]),
                    mlp_up, mlp_down, nf, 1, True)
    y_prompt = y_p.reshape(n_p, t_p, D_MODEL)
    ssm_re_p = ssm_re_p.reshape(n_p, SSM_GROUPS, SSM_P)
    ssm_im_p = ssm_im_p.reshape(n_p, SSM_GROUPS, SSM_P)
    ssm_seq_first = lambda s: jnp.transpose(s.reshape(SSM_GROUPS, SSM_P, n_s), (2, 0, 1))
    ssm_re_s, ssm_im_s = ssm_seq_first(ssm_re_s), ssm_seq_first(ssm_im_s)

    return (y_prompt, y_s.reshape(n_s, 1, D_MODEL),
            ret_p[None], wkv_p[None], shift_p[None], ssm_re_p[None], ssm_im_p[None],
            ret_s[None], wkv_s[None], shift_s[None], ssm_re_s[None], ssm_im_s[None])
```

```python
import functools
import math

import numpy as np
import jax
import jax.numpy as jnp
from jax import lax
from jax.experimental import pallas as pl
from jax.experimental.pallas import tpu as pltpu

F32 = jnp.float32
BF16 = jnp.bfloat16

LANES = 128
MXU_TILE = 256
VMEM_BYTES = 64 * 1024 * 1024

D_MODEL = 1024
N_HEADS = 8
HEAD_DIM = 64
HEADS_W = N_HEADS * HEAD_DIM
N_PAIRS = N_HEADS // 2
PAIR_W = 2 * HEAD_DIM
assert PAIR_W == LANES
ROPE_BASE = 10000.0
DECAY_LORA = 64
AAA_LORA = 64
GATE_LORA = 128
SHIFT_W = 3 * HEADS_W + DECAY_LORA + AAA_LORA + GATE_LORA
RET_COLS = 4 * HEADS_W
IN_W = RET_COLS + SHIFT_W
SSM_GROUP = 16
SSM_GROUPS = D_MODEL // SSM_GROUP
SSM_P = 64
SSM_BLOCK_G = LANES // SSM_GROUP
SSM_BLOCKS = SSM_GROUPS // SSM_BLOCK_G
SSM_HALF = SSM_BLOCK_G * SSM_P
D_FF = 4 * D_MODEL
RMS_EPS = 1e-6
GN_EPS = 1e-5
WKV_GN_EPS = 64e-5
PAST_LEN = 16384

MIX_CHUNK = 64
PROJ_PIECE = MXU_TILE
MIX_GROUP = 4
SSM_CHUNK = 64
MLP_ROWS = 1024
FF_CHUNK = 1024

VMEM_LIMIT = VMEM_BYTES - 6 * 1024 * 1024


def _dot(a, b):
    return jnp.dot(a.astype(BF16), b.astype(BF16), preferred_element_type=F32)


def _dot_nt(a, b):
    return lax.dot_general(a.astype(BF16), b.astype(BF16), (((1,), (1,)), ((), ())),
                           preferred_element_type=F32)


def _dot_tn(a, b):
    return lax.dot_general(a.astype(BF16), b.astype(BF16), (((0,), (0,)), ((), ())),
                           preferred_element_type=F32)


def _split3(x):
    hi = x.astype(BF16)
    r1 = x - hi.astype(F32)
    mid = r1.astype(BF16)
    lo = (r1 - mid.astype(F32)).astype(BF16)
    return hi, mid, lo


def _dot_exact_lhs(a_bf16, x):
    hi, mid, lo = _split3(x)
    f = lambda p: jnp.dot(a_bf16, p, preferred_element_type=F32)
    return f(hi) + f(mid) + f(lo)


def _segsum(x, ones_blk):
    xb = x.astype(BF16)
    outs = [jnp.dot(xb[:, c * MXU_TILE:(c + 1) * MXU_TILE], ones_blk, preferred_element_type=F32)
            for c in range(x.shape[1] // MXU_TILE)]
    return jnp.concatenate(outs, axis=1)


def _rms(x, w):
    return x * lax.rsqrt(jnp.mean(x * x, axis=-1, keepdims=True) + RMS_EPS) * w


def _sigmoid(x):
    return 1.0 / (1.0 + jnp.exp(-x))


def _softplus(x):
    return jnp.maximum(x, 0.0) + jnp.log1p(jnp.exp(-jnp.abs(x)))


def _head_norm(z, eps, ones_blk):
    mu = _segsum(z, ones_blk) * (1.0 / HEAD_DIM)
    zc = z - mu
    var = _segsum(zc * zc, ones_blk) * (1.0 / HEAD_DIM)
    return zc * lax.rsqrt(var + eps)


def _rope(z, cos, sin_signed):
    lane = lax.broadcasted_iota(jnp.int32, (1, HEADS_W), 1) % HEAD_DIM
    swapped = jnp.where(lane < HEAD_DIM // 2,
                        pltpu.roll(z, HEADS_W - HEAD_DIM // 2, axis=1),
                        pltpu.roll(z, HEAD_DIM // 2, axis=1))
    return z * cos + swapped * sin_signed


def _wkv_features(xs, w0, lora_w, a0, g_b, k_k, k_a, r_k, ones_blk, between=lambda: None):
    r = xs[:, 0:HEADS_W]
    kw = xs[:, HEADS_W:2 * HEADS_W]
    vw = xs[:, 2 * HEADS_W:3 * HEADS_W]
    lora_w_in = DECAY_LORA + AAA_LORA
    lo = xs[:, 3 * HEADS_W:3 * HEADS_W + lora_w_in]
    lane = lax.broadcasted_iota(jnp.int32, (1, lora_w_in), 1)
    lo = jnp.where(lane < DECAY_LORA, jnp.tanh(lo), lo)
    ll = _dot(lo, lora_w)
    w_log = -_softplus(-(w0 + ll[:, 0:HEADS_W])) - 0.5
    log_decay = -jnp.exp(w_log)
    between()
    alr = _sigmoid(a0 + ll[:, HEADS_W:2 * HEADS_W])
    gate = _dot(_sigmoid(xs[:, 3 * HEADS_W + lora_w_in:SHIFT_W]), g_b)
    between()
    kk = kw * k_k
    kk = kk / jnp.maximum(jnp.sqrt(_segsum(kk * kk, ones_blk)), 1e-12)
    k_mod = kw * (1.0 + (alr - 1.0) * k_a)
    between()
    bonus = _segsum(r * k_mod * r_k, ones_blk) * vw
    return r, log_decay, k_mod, vw, -kk, kk * alr, gate, bonus


def _stack_masked(x2, m0):
    return jnp.concatenate([jnp.where(m0, x2, 0.0), jnp.where(m0, 0.0, x2)], axis=0)


def _stack_dup(x2):
    return jnp.concatenate([x2, x2], axis=0)


def _mixer_prompt_kernel(
        xprev_ref, xnext_ref, normw_ref, win_ref, cos_ref, sin_ref, qdec_ref, kdec_ref, dmask_ref, sdec_ref,
        retgn_ref, mu_ref, w0_ref, lora_ref, a0_ref, gb_ref, kk_ref, ka_ref, rk_ref, lnw_ref,
        lnb_ref, wout_ref, ones_ref, tril_ref, strict_ref, incl_ref, bd_ref,
        xo_ref, rets_ref, wkvs_ref, shift_ref,
        p_s, cat_s, rs_s, ws_s, carry_s, *, n_seq, chunk, group):
    i = pl.program_id(0)
    n_chunks = pl.num_programs(0) - 1
    C = chunk
    slot = i % 2

    @pl.when(i == 0)
    def _():
        rs_s[...] = jnp.zeros_like(rs_s)
        ws_s[...] = jnp.zeros_like(ws_s)
        carry_s[...] = jnp.zeros_like(carry_s)
        cat_s[...] = jnp.zeros_like(cat_s)
        p_s[0] = _dot(_rms(xprev_ref[...].reshape(n_seq * C, D_MODEL), normw_ref[...]), win_ref[...])

    m0 = lax.broadcasted_iota(jnp.int32, (1, PAIR_W), 1) < HEAD_DIM
    ones_blk = ones_ref[...]
    NB = group
    R = NB * C
    row_id = lax.broadcasted_iota(jnp.int32, (R, 1), 0)
    tile_rows = lambda ref: jnp.concatenate([ref[...]] * NB, axis=0)
    pairs = range(N_PAIRS)
    sls = [slice(pr * PAIR_W, (pr + 1) * PAIR_W) for pr in pairs]
    chains = [(s, pr) for s in range(NB) for pr in pairs]
    seq_rows = [slice(s * C, (s + 1) * C) for s in range(NB)]

    def per_group(gi, carry):
        rows = pl.ds(pl.multiple_of(gi * R, R), R)
        b0 = gi * NB
        ret_states = {(s, pr): rs_s[pr, b0 + s] for s, pr in chains}
        wkv_states = {(s, pr): ws_s[pr, b0 + s] for s, pr in chains}
        shift_rows = [carry_s[pl.ds(b0 + s, 1), :] for s in range(NB)]
        p_cur = p_s.at[slot]
        wp = p_cur[rows, RET_COLS:IN_W]
        hn_next = _rms(xnext_ref[pl.ds(b0, NB)].reshape(R, D_MODEL), normw_ref[...]).astype(BF16)
        cat_prev = cat_s[rows, :]

        def next_in_proj(c0):
            p_s[1 - slot, rows, c0:c0 + PROJ_PIECE] = jnp.dot(
                hn_next, win_ref[:, c0:c0 + PROJ_PIECE], preferred_element_type=F32)

        def prev_out_proj(c0):
            cols = slice(c0, c0 + PROJ_PIECE)
            out = jnp.dot(cat_prev, wout_ref[:, cols], preferred_element_type=F32)
            xo_ref[pl.ds(b0, NB), :, cols] = xprev_ref[pl.ds(b0, NB), :, cols] + out.reshape(NB, C, PROJ_PIECE)

        pieces = [functools.partial(next_in_proj, c0) for c0 in range(0, IN_W, PROJ_PIECE)]
        pieces += [functools.partial(prev_out_proj, c0) for c0 in range(0, D_MODEL, PROJ_PIECE)]

        def fill(n=1):
            for _ in range(min(n, len(pieces))):
                pieces.pop(0)()

        cos, sin = tile_rows(cos_ref), tile_rows(sin_ref)
        q = _rope(p_cur[rows, 0:HEADS_W], cos, sin)
        fill()
        k = _rope(p_cur[rows, HEADS_W:2 * HEADS_W], cos, sin) * (HEAD_DIM ** -0.5)
        fill()
        v = p_cur[rows, 2 * HEADS_W:3 * HEADS_W]
        g = p_cur[rows, 3 * HEADS_W:4 * HEADS_W]
        qd = q * tile_rows(qdec_ref)
        kd = k * tile_rows(kdec_ref)
        fill()
        cut = lambda z, s, pr: z[seq_rows[s], sls[pr]]
        r_sc = {c: _dot_nt(_stack_masked(cut(q, *c), m0), _stack_dup(cut(k, *c))) * dmask_ref[c[1]]
                for c in chains}
        r_inner = {c: _dot(r_sc[c], _stack_dup(cut(v, *c))) for c in chains}
        r_cross = {c: _dot(cut(qd, *c), ret_states[c]) for c in chains}
        new_ret = {c: sdec_ref[c[1]] * ret_states[c] + _dot_tn(cut(kd, *c), cut(v, *c)) * bd_ref[...]
                   for c in chains}
        o = jnp.concatenate(
            [jnp.concatenate([jnp.where(m0, r_inner[(s, pr)][0:C], r_inner[(s, pr)][C:2 * C])
                              + r_cross[(s, pr)] for pr in pairs], axis=1) for s in range(NB)], axis=0)
        fill()
        ret = _head_norm(o, GN_EPS, ones_blk) * retgn_ref[...]
        ret_out = (g * _sigmoid(g) * ret).astype(BF16)
        fill()

        prev = pltpu.roll(wp, 1, axis=0)
        for s in range(NB):
            prev = jnp.where(row_id == s * C, shift_rows[s], prev)
        xs = wp + (prev - wp) * mu_ref[...]
        fill()
        r, lw, k_mod, vw, a_vec, b_vec, gate, bonus = _wkv_features(
            xs, w0_ref[...], lora_ref[...], a0_ref[...], gb_ref[...], kk_ref[...], ka_ref[...],
            rk_ref[...], ones_blk, fill)
        cw = _dot_exact_lhs(tril_ref[...], lw)
        fill()
        cw_last = [cw[s * C + C - 1:(s + 1) * C, :] for s in range(NB)]
        cwl = jnp.concatenate([jnp.broadcast_to(z, (C, HEADS_W)) for z in cw_last], axis=0)
        r_t = r * jnp.exp(cw)
        a_t = a_vec * jnp.exp(cw - lw)
        fill()
        w_inv = jnp.exp(-cw)
        b_t = b_vec * w_inv
        k_t = k_mod * w_inv
        fill()
        w_end = jnp.exp(cwl - cw)
        b_h = b_vec * w_end
        k_h = k_mod * w_end
        fill()
        w_all = [jnp.exp(z) for z in cw_last]
        lhs = {c: jnp.concatenate([_stack_masked(cut(a_t, *c), m0), _stack_masked(cut(r_t, *c), m0)], axis=0)
               for c in chains}
        fill(len(pieces) - 2)
        sc = {c: _dot_nt(lhs[c], jnp.concatenate([cut(b_t, *c), cut(k_t, *c)], axis=0)) for c in chains}
        sc_swapped = {c: pltpu.roll(sc[c], C, axis=1) for c in chains}
        sc_b = {c: jnp.where(m0, sc[c], sc_swapped[c]) for c in chains}
        sc_k = {c: jnp.where(m0, sc_swapped[c], sc[c]) for c in chains}
        on_state = {c: _dot_nt(lhs[c], wkv_states[c]) for c in chains}
        vv = {c: _stack_dup(cut(vw, *c)) for c in chains}
        n_pow = {c: sc_b[c][0:2 * C] * strict_ref[...] for c in chains}
        u = {c: on_state[c][0:2 * C] + _dot(sc_k[c][0:2 * C] * strict_ref[...], vv[c]) for c in chains}
        n_steps_solve = int(math.log2(C))
        for it in range(n_steps_solve):
            u = {c: u[c] + _dot(n_pow[c], u[c]) for c in chains}
            if it + 1 < n_steps_solve:
                n_pow = {c: _dot(n_pow[c], n_pow[c]) for c in chains}
        uv = {c: jnp.concatenate([u[c], vv[c]], axis=0) for c in chains}
        y_st = {c: on_state[c][2 * C:4 * C] + _dot(
            jnp.concatenate([sc_b[c][2 * C:4 * C] * incl_ref[...],
                             sc_k[c][2 * C:4 * C] * incl_ref[...]], axis=1), uv[c]) for c in chains}
        new_wkv = {c: wkv_states[c] * w_all[c[0]][:, sls[c[1]]] + bd_ref[...] * _dot_tn(
            uv[c], jnp.concatenate([_stack_masked(cut(b_h, *c), m0), _stack_masked(cut(k_h, *c), m0)], axis=0))
            for c in chains}
        y = jnp.concatenate(
            [jnp.concatenate([jnp.where(m0, y_st[(s, pr)][0:C], y_st[(s, pr)][C:2 * C]) for pr in pairs], axis=1)
             for s in range(NB)], axis=0)
        fill(len(pieces))
        yn = _head_norm(y, WKV_GN_EPS, ones_blk) * lnw_ref[...] + lnb_ref[...]
        cat_s[rows, 0:HEADS_W] = ret_out
        cat_s[rows, HEADS_W:2 * HEADS_W] = ((yn + bonus) * gate).astype(BF16)
        for s in range(NB):
            carry_s[pl.ds(b0 + s, 1), :] = wp[s * C + C - 1:(s + 1) * C, :]
        for s, pr in chains:
            rs_s[pr, b0 + s] = new_ret[(s, pr)]
            ws_s[pr, b0 + s] = new_wkv[(s, pr)]
        return carry

    @pl.when(i < n_chunks)
    def _():
        lax.fori_loop(0, n_seq // NB, per_group, 0)

    @pl.when(i == n_chunks)
    def _():
        out = jnp.dot(cat_s[...], wout_ref[...], preferred_element_type=F32)
        xo_ref[...] = xprev_ref[...] + out.reshape(n_seq, C, D_MODEL)

    @pl.when(i == n_chunks - 1)
    def _():
        shift_ref[...] = carry_s[...]
        for b in range(n_seq):
            for pr in range(N_PAIRS):
                rs = rs_s[pr, b]
                ws = ws_s[pr, b]
                rets_ref[b, 2 * pr] = rs[0:HEAD_DIM, 0:HEAD_DIM]
                rets_ref[b, 2 * pr + 1] = rs[HEAD_DIM:PAIR_W, HEAD_DIM:PAIR_W]
                wkvs_ref[b, 2 * pr] = ws[0:HEAD_DIM, 0:HEAD_DIM]
                wkvs_ref[b, 2 * pr + 1] = ws[HEAD_DIM:PAIR_W, HEAD_DIM:PAIR_W]


def _const_spec(shape):
    nd = len(shape)
    return pl.BlockSpec(shape, lambda *_: (0,) * nd, pipeline_mode=pl.Buffered(1))


def _const_out(shape):
    nd = len(shape)
    return pl.BlockSpec(shape, lambda *_: (0,) * nd)


def _const(a, dtype=F32):
    return jnp.asarray(np.asarray(a, np.float64), dtype=dtype)


def _retention_tables(chunk):
    log_g = np.log1p(-np.exp2(-5.0 - np.arange(N_HEADS, dtype=np.float64)))
    lane_g = np.repeat(log_g, HEAD_DIM)[None, :]
    idx = np.arange(chunk, dtype=np.float64)
    qdec = np.exp((idx + 1.0)[:, None] * lane_g)
    kdec = np.exp((chunk - 1.0 - idx)[:, None] * lane_g)
    rel = idx[:, None] - idx[None, :]
    dm = np.where(rel >= 0, np.exp(np.maximum(rel, 0.0)[None] * log_g[:, None, None]), 0.0)
    zero = np.zeros((chunk, chunk))
    dmask = np.stack([np.block([[dm[2 * p], zero], [zero, dm[2 * p + 1]]]) for p in range(N_PAIRS)])
    cdec = np.exp(chunk * log_g)
    hz = np.zeros((HEAD_DIM, HEAD_DIM))
    ho = np.ones((HEAD_DIM, HEAD_DIM))
    sdec = np.stack([np.block([[cdec[2 * p] * ho, hz], [hz, cdec[2 * p + 1] * ho]])
                     for p in range(N_PAIRS)])
    return _const(qdec), _const(kdec), _const(dmask), _const(sdec), cdec


def _rope_tables(pos):
    half = HEAD_DIM // 2
    inv_freq = ROPE_BASE ** (-np.arange(half, dtype=np.float64) / half)
    ang = np.asarray(pos, np.float64)[:, None] * inv_freq[None, :]
    cos = np.cos(ang)
    sin = np.sin(ang)
    cos_t = np.tile(np.concatenate([cos, cos], axis=1), (1, N_HEADS))
    sin_t = np.tile(np.concatenate([-sin, sin], axis=1), (1, N_HEADS))
    return _const(cos_t), _const(sin_t)


def _block_masks(chunk, group):
    i = np.arange(2 * chunk)
    same = (i[:, None] // chunk) == (i[None, :] // chunk)
    strict = same & (i[:, None] > i[None, :])
    incl = same & (i[:, None] >= i[None, :])
    j = np.arange(PAIR_W)
    bd = (j[:, None] // HEAD_DIM) == (j[None, :] // HEAD_DIM)
    t = np.arange(chunk)
    tril = np.kron(np.eye(group), t[:, None] >= t[None, :])
    o = np.arange(MXU_TILE)
    ones_blk = (o[:, None] // HEAD_DIM) == (o[None, :] // HEAD_DIM)
    return _const(strict), _const(incl), _const(bd), _const(tril, BF16), _const(ones_blk, BF16)


def _lora_block(w_b, a_b):
    z = jnp.zeros_like(w_b)
    return jnp.concatenate([jnp.concatenate([w_b, z], axis=1),
                            jnp.concatenate([z, a_b], axis=1)], axis=0).astype(BF16)


def _row(v):
    return v.reshape(1, -1).astype(F32)


def _mixer_prompt(x, lw):
    n_seq, t_len, _ = x.shape
    C = MIX_CHUNK
    n_chunks = t_len // C
    last = n_chunks - 1
    cos_t, sin_t = _rope_tables(np.arange(t_len))
    qdec, kdec, dmask, sdec, _ = _retention_tables(C)
    strict, incl, bd, tril, ones_blk = _block_masks(C, MIX_GROUP)
    prev_chunk = lambda i: (0, jnp.maximum(i - 1, 0), 0)
    in_specs = [
        pl.BlockSpec((n_seq, C, D_MODEL), prev_chunk),
        pl.BlockSpec((n_seq, C, D_MODEL), lambda i: (0, jnp.minimum(i + 1, last), 0)),
        _const_spec((1, D_MODEL)),
        _const_spec((D_MODEL, IN_W)),
        pl.BlockSpec((C, HEADS_W), lambda i: (jnp.minimum(i, last), 0)),
        pl.BlockSpec((C, HEADS_W), lambda i: (jnp.minimum(i, last), 0)),
    ]
    tail = [qdec, kdec, dmask, sdec, lw["ret_gn"], lw["mu"], lw["w0"], lw["lora"], lw["a0"], lw["g_b"],
            lw["k_k"], lw["k_a"], lw["r_k"], lw["ln_w"], lw["ln_b"], lw["w_out"], ones_blk, tril,
            strict, incl, bd]
    in_specs += [_const_spec(a.shape) for a in tail]
    out_shape = (
        jax.ShapeDtypeStruct((n_seq, t_len, D_MODEL), F32),
        jax.ShapeDtypeStruct((n_seq, N_HEADS, HEAD_DIM, HEAD_DIM), F32),
        jax.ShapeDtypeStruct((n_seq, N_HEADS, HEAD_DIM, HEAD_DIM), F32),
        jax.ShapeDtypeStruct((n_seq, SHIFT_W), F32),
    )
    out_specs = (
        pl.BlockSpec((n_seq, C, D_MODEL), prev_chunk),
        _const_out((n_seq, N_HEADS, HEAD_DIM, HEAD_DIM)),
        _const_out((n_seq, N_HEADS, HEAD_DIM, HEAD_DIM)),
        _const_out((n_seq, SHIFT_W)),
    )
    scratch = [
        pltpu.VMEM((2, n_seq * C, IN_W), F32),
        pltpu.VMEM((n_seq * C, 2 * HEADS_W), BF16),
        pltpu.VMEM((N_PAIRS, n_seq, PAIR_W, PAIR_W), F32),
        pltpu.VMEM((N_PAIRS, n_seq, PAIR_W, PAIR_W), F32),
        pltpu.VMEM((n_seq, SHIFT_W), F32),
    ]
    return pl.pallas_call(
        functools.partial(_mixer_prompt_kernel, n_seq=n_seq, chunk=C, group=MIX_GROUP),
        grid=(n_chunks + 1,),
        in_specs=in_specs, out_specs=out_specs, out_shape=out_shape, scratch_shapes=scratch,
        compiler_params=pltpu.CompilerParams(dimension_semantics=("arbitrary",),
                                             vmem_limit_bytes=VMEM_LIMIT),
        name="mixer_prompt",
    )(x, x, lw["norm"], lw["w_in"], cos_t, sin_t, *tail)


def _mixer_sample_pre_kernel(
        x_ref, shift_ref, normw_ref, win_ref, cos_ref, sin_ref, qdec_ref, mu_ref, w0_ref, lora_ref,
        a0_ref, gb_ref, kk_ref, ka_ref, rk_ref, ones_ref, feat_ref, feat_t_ref, newshift_ref):
    x = x_ref[...]
    p = _dot(_rms(x, normw_ref[...]), win_ref[...])
    q = _rope(p[:, 0:HEADS_W], cos_ref[...], sin_ref[...])
    k = _rope(p[:, HEADS_W:2 * HEADS_W], cos_ref[...], sin_ref[...]) * (HEAD_DIM ** -0.5)
    wp = p[:, RET_COLS:IN_W]
    xs = wp + (shift_ref[...] - wp) * mu_ref[...]
    r, lw, k_mod, vw, a_vec, b_vec, gate, bonus = _wkv_features(
        xs, w0_ref[...], lora_ref[...], a0_ref[...], gb_ref[...], kk_ref[...], ka_ref[...],
        rk_ref[...], ones_ref[...])
    newshift_ref[...] = wp
    state_feats = [q, q * qdec_ref[...], k, p[:, 2 * HEADS_W:3 * HEADS_W], r, jnp.exp(lw), k_mod, vw,
                   a_vec, b_vec]
    for n, f in enumerate(state_feats):
        feat_t_ref[n * HEADS_W:(n + 1) * HEADS_W, :] = f.T
    for n, f in enumerate([p[:, 3 * HEADS_W:4 * HEADS_W], gate, bonus]):
        feat_ref[:, n * HEADS_W:(n + 1) * HEADS_W] = f


_F_Q, _F_QD, _F_K, _F_V, _F_R, _F_W, _F_KM, _F_VW, _F_A, _F_B = range(10)
_N_STATE_FEATS = 10
_F_G, _F_GATE, _F_BONUS = range(3)
_N_ROW_FEATS = 3


def _mixer_sample_state_kernel(feat_t_ref, sdec_ref, ret_ref, wkv_ref, o_t_ref, reto_ref, wkvo_ref):
    h = pl.program_id(0)

    def head_rows(n):
        return feat_t_ref[pl.ds(pl.multiple_of(n * HEADS_W + h * HEAD_DIM, HEAD_DIM), HEAD_DIM), :]

    def head_row(n, i):
        return feat_t_ref[pl.ds(n * HEADS_W + h * HEAD_DIM + i, 1), :]

    a, w, b_vec, k_mod, r = (head_rows(n) for n in (_F_A, _F_W, _F_B, _F_KM, _F_R))

    def wkv_row(i, carry):
        s = wkv_ref[0, i]
        sa = jnp.sum(s * a, axis=0, keepdims=True)
        s_new = s * w + sa * b_vec + head_row(_F_VW, i) * k_mod
        wkvo_ref[0, i] = s_new
        o_t_ref[1, 0, pl.ds(i, 1), :] = jnp.sum(s_new * r, axis=0, keepdims=True)
        return carry

    lax.fori_loop(0, HEAD_DIM, wkv_row, 0, unroll=4)

    v, q, k = head_rows(_F_V), head_rows(_F_Q), head_rows(_F_K)
    g = sdec_ref[h]

    def ret_row(d, acc):
        s = ret_ref[0, d]
        reto_ref[0, d] = g * s + head_row(_F_K, d) * v
        return acc + head_row(_F_QD, d) * s

    cross = lax.fori_loop(0, HEAD_DIM, ret_row, jnp.zeros_like(v), unroll=4)
    o_t_ref[0, 0] = cross + jnp.sum(q * k, axis=0, keepdims=True) * v


def _mixer_sample_post_kernel(
        x_ref, feat_ref, o_t_ref, retgn_ref, lnw_ref, lnb_ref, wout_ref, ones_ref, xo_ref):
    ones_blk = ones_ref[...]
    g = feat_ref[:, _F_G * HEADS_W:(_F_G + 1) * HEADS_W]
    gate = feat_ref[:, _F_GATE * HEADS_W:(_F_GATE + 1) * HEADS_W]
    bonus = feat_ref[:, _F_BONUS * HEADS_W:(_F_BONUS + 1) * HEADS_W]
    o = o_t_ref[...].T
    ret = _head_norm(o[:, 0:HEADS_W], GN_EPS, ones_blk) * retgn_ref[...]
    ret_out = g * _sigmoid(g) * ret
    yn = _head_norm(o[:, HEADS_W:2 * HEADS_W], WKV_GN_EPS, ones_blk) * lnw_ref[...] + lnb_ref[...]
    wkv_out = (yn + bonus) * gate
    cat = jnp.concatenate([ret_out, wkv_out], axis=1)
    xo_ref[...] = x_ref[...] + _dot(cat, wout_ref[...])


def _mixer_sample(x, ret_t, wkv_t, shift0, lw):
    n = x.shape[0]
    cos_t, sin_t = _rope_tables(np.full((1,), PAST_LEN))
    qdec, _, _, _, cdec = _retention_tables(1)
    _, _, _, _, ones_blk = _block_masks(1, 1)
    args = [x, shift0, lw["norm"], lw["w_in"], cos_t, sin_t, qdec, lw["mu"], lw["w0"], lw["lora"],
            lw["a0"], lw["g_b"], lw["k_k"], lw["k_a"], lw["r_k"], ones_blk]
    feat, feat_t, new_shift = pl.pallas_call(
        _mixer_sample_pre_kernel,
        grid=(1,),
        in_specs=[_const_spec(a.shape) for a in args],
        out_specs=(_const_out((n, _N_ROW_FEATS * HEADS_W)), _const_out((_N_STATE_FEATS * HEADS_W, n)),
                   _const_out((n, SHIFT_W))),
        out_shape=(jax.ShapeDtypeStruct((n, _N_ROW_FEATS * HEADS_W), F32),
                   jax.ShapeDtypeStruct((_N_STATE_FEATS * HEADS_W, n), F32),
                   jax.ShapeDtypeStruct((n, SHIFT_W), F32)),
        compiler_params=pltpu.CompilerParams(vmem_limit_bytes=VMEM_LIMIT),
        name="mixer_sample_pre",
    )(*args)

    sdec = _const(np.broadcast_to(cdec[:, None, None], (N_HEADS, 1, n)))
    state_spec = pl.BlockSpec((1, HEAD_DIM, HEAD_DIM, n), lambda h: (h, 0, 0, 0))
    o_spec = pl.BlockSpec((2, 1, HEAD_DIM, n), lambda h: (0, h, 0, 0))
    o_t, ret_new, wkv_new = pl.pallas_call(
        _mixer_sample_state_kernel,
        grid=(N_HEADS,),
        in_specs=[_const_spec(feat_t.shape), _const_spec(sdec.shape), state_spec, state_spec],
        out_specs=(o_spec, state_spec, state_spec),
        out_shape=(jax.ShapeDtypeStruct((2, N_HEADS, HEAD_DIM, n), F32),
                   jax.ShapeDtypeStruct(ret_t.shape, F32), jax.ShapeDtypeStruct(wkv_t.shape, F32)),
        compiler_params=pltpu.CompilerParams(dimension_semantics=("arbitrary",),
                                             vmem_limit_bytes=VMEM_LIMIT),
        name="mixer_sample_state",
    )(feat_t, sdec, ret_t, wkv_t)

    args = [x, feat, o_t.reshape(2 * HEADS_W, n), lw["ret_gn"], lw["ln_w"], lw["ln_b"], lw["w_out"], ones_blk]
    x1 = pl.pallas_call(
        _mixer_sample_post_kernel,
        grid=(1,),
        in_specs=[_const_spec(a.shape) for a in args],
        out_specs=_const_out((n, D_MODEL)),
        out_shape=jax.ShapeDtypeStruct((n, D_MODEL), F32),
        compiler_params=pltpu.CompilerParams(vmem_limit_bytes=VMEM_LIMIT),
        name="mixer_sample_post",
    )(*args)
    return x1, ret_new, wkv_new, new_shift


def _mlp_kernel(xa_ref, xb_ref, normw_ref, wup_ref, wdown_ref, normf_ref, oa_ref, ob_ref,
                *, final_norm, steps_a):
    i = pl.program_id(0)

    def run(x_ref, o_ref):
        x = x_ref[...]
        hn = _rms(x, normw_ref[...]).astype(BF16)
        acc = x
        for c in range(D_FF // FF_CHUNK):
            sl = slice(c * FF_CHUNK, (c + 1) * FF_CHUNK)
            hid = jnp.dot(hn, wup_ref[:, sl].astype(BF16), preferred_element_type=F32)
            hid = jnp.square(jnp.maximum(hid, 0.0)).astype(BF16)
            acc = acc + jnp.dot(hid, wdown_ref[sl, :].astype(BF16), preferred_element_type=F32)
        if final_norm:
            acc = _rms(acc, normf_ref[...])
        o_ref[...] = acc

    @pl.when(i < steps_a)
    def _():
        run(xa_ref, oa_ref)

    @pl.when(i == steps_a)
    def _():
        run(xb_ref, ob_ref)


def _mlp(xa, xb, norm_w, w_up, w_down, norm_f, layer, final_norm):
    rows_a, rows_b = xa.shape[0], xb.shape[0]
    steps_a = rows_a // MLP_ROWS
    pick = lambda *_: (layer, 0, 0)
    block_a = pl.BlockSpec((MLP_ROWS, D_MODEL), lambda i: (jnp.minimum(i, steps_a - 1), 0))
    return pl.pallas_call(
        functools.partial(_mlp_kernel, final_norm=final_norm, steps_a=steps_a),
        grid=(steps_a + 1,),
        in_specs=[block_a, _const_spec((rows_b, D_MODEL)),
                  _const_spec((1, D_MODEL)),
                  pl.BlockSpec((None, D_MODEL, D_FF), pick, pipeline_mode=pl.Buffered(1)),
                  pl.BlockSpec((None, D_FF, D_MODEL), pick, pipeline_mode=pl.Buffered(1)),
                  _const_spec((1, D_MODEL))],
        out_specs=(block_a, _const_out((rows_b, D_MODEL))),
        out_shape=(jax.ShapeDtypeStruct((rows_a, D_MODEL), F32), jax.ShapeDtypeStruct((rows_b, D_MODEL), F32)),
        compiler_params=pltpu.CompilerParams(dimension_semantics=("arbitrary",),
                                             vmem_limit_bytes=VMEM_LIMIT),
        name="mlp",
    )(xa, xb, norm_w, w_up, w_down, norm_f)


def _ssm_prep_kernel(lre_ref, lim_ref, logdt_ref, lre_rep_ref, lim_rep_ref, logdt_rep_ref, bre_ref, bim_ref,
                     cre_in_ref, cim_in_ref, tile_ref, eye_ref, are_ref, aim_ref, wb_ref, cre_ref, cim_ref):
    def discretise(lre, lim, logdt):
        lre = jnp.minimum(lre, -1e-4)
        dt = jnp.exp(logdt)
        mag = jnp.exp(lre * dt)
        return lre, mag * jnp.cos(lim * dt), mag * jnp.sin(lim * dt)

    _, are_ref[...], aim_ref[...] = discretise(lre_ref[...], lim_ref[...], logdt_ref[...])
    lim = lim_rep_ref[...]
    lre, are, aim = discretise(lre_rep_ref[...], lim, logdt_rep_ref[...])
    den = lre * lre + lim * lim
    nre = are - 1.0
    cre = (nre * lre + aim * lim) / den
    cim = (aim * lre - nre * lim) / den
    bre = bre_ref[...]
    bim = bim_ref[...]
    bb = (cre * bre - cim * bim, cre * bim + cim * bre)

    in_mask = (lax.broadcasted_iota(jnp.int32, (LANES, 1), 0) // SSM_GROUP
               == lax.broadcasted_iota(jnp.int32, (1, SSM_HALF), 1) // SSM_P)
    out_group = lax.broadcasted_iota(jnp.int32, (1, LANES), 1) // SSM_GROUP
    for blk in range(SSM_BLOCKS):
        rows = slice(blk * LANES, (blk + 1) * LANES)
        for part in range(2):
            tiled = jnp.dot(bb[part][rows].astype(BF16), tile_ref[...], preferred_element_type=F32)
            wb_ref[blk, :, part * SSM_HALF:(part + 1) * SSM_HALF] = jnp.where(in_mask, tiled, 0.0).astype(BF16)
        for src, dst in ((cre_in_ref, cre_ref), (cim_in_ref, cim_ref)):
            c_t = _dot_tn(src[rows, :], eye_ref[...])
            dst[blk] = jnp.concatenate([jnp.where(out_group == gl, c_t, 0.0) for gl in range(SSM_BLOCK_G)],
                                       axis=0).astype(BF16)


def _gelu_exact(x):
    return 0.5 * x * (1.0 + lax.erf(x * (2.0 ** -0.5)))


def _ssm_kernel(x_ref, hre0_ref, him0_ref, normw_ref, are_ref, aim_ref, wb_ref, cre_ref, cim_ref, dskip_ref,
                wglu_ref, xo_ref, hre_ref, him_ref, xt_s, u_s, bu_s, y_s, hg_s, hre_s, him_s,
                *, n_seq, chunk, batch_major):
    i = pl.program_id(0)
    n_steps = pl.num_programs(0)
    rows_all = chunk * n_seq
    n_slabs = D_MODEL // LANES

    @pl.when(i == 0)
    def _():
        hre_s[...] = hre0_ref[...] if batch_major else hre0_ref[...].T
        him_s[...] = him0_ref[...] if batch_major else him0_ref[...].T

    if batch_major:
        for b in range(n_seq):
            for sl in range(n_slabs):
                xt_s[sl, pl.ds(b, chunk, stride=n_seq), :] = x_ref[b, :, sl * LANES:(sl + 1) * LANES]
        ssq = sum(jnp.sum(jnp.square(xt_s[sl]), axis=-1, keepdims=True) for sl in range(n_slabs))
        inv = lax.rsqrt(ssq * (1.0 / D_MODEL) + RMS_EPS)
        for sl in range(n_slabs):
            cols = slice(sl * LANES, (sl + 1) * LANES)
            u_s[:, cols] = xt_s[sl] * inv * normw_ref[:, cols]
    else:
        u_s[...] = _rms(x_ref[...].reshape(rows_all, D_MODEL), normw_ref[...])

    def input_proj(blk, part):
        c = slice(part * SSM_HALF, (part + 1) * SSM_HALF)
        bu_s[blk % 2, :, c] = _dot(u_s[:, blk * LANES:(blk + 1) * LANES], wb_ref[blk, :, c])

    def output_proj(blk, part):
        buf = bu_s.at[blk % 2]
        cols = slice(blk * LANES, (blk + 1) * LANES)
        if part == 0:
            y_s[:, cols] = _dot(buf[:, 0:SSM_HALF], cre_ref[blk])
        else:
            y_s[:, cols] = y_s[:, cols] - _dot(buf[:, SSM_HALF:2 * SSM_HALF], cim_ref[blk])

    input_proj(0, 0)
    input_proj(0, 1)
    for blk in range(SSM_BLOCKS):
        neighbours = []
        if blk >= 1:
            neighbours += [functools.partial(output_proj, blk - 1, 0), functools.partial(output_proj, blk - 1, 1)]
        if blk + 1 < SSM_BLOCKS:
            neighbours += [functools.partial(input_proj, blk + 1, 0), functools.partial(input_proj, blk + 1, 1)]
        buf = bu_s.at[blk % 2]
        cols = slice(blk * SSM_HALF, (blk + 1) * SSM_HALF)
        a_re = jnp.broadcast_to(are_ref[:, cols], (n_seq, SSM_HALF))
        a_im = jnp.broadcast_to(aim_ref[:, cols], (n_seq, SSM_HALF))
        h_re = hre_s[:, cols]
        h_im = him_s[:, cols]
        seg = -(-chunk // max(len(neighbours), 1))
        for t in range(chunk):
            if t % seg == 0 and neighbours:
                neighbours.pop(0)()
            rows = slice(t * n_seq, (t + 1) * n_seq)
            n_re = a_re * h_re - a_im * h_im + buf[rows, 0:SSM_HALF]
            n_im = a_re * h_im + a_im * h_re + buf[rows, SSM_HALF:2 * SSM_HALF]
            buf[rows, 0:SSM_HALF] = n_re
            buf[rows, SSM_HALF:2 * SSM_HALF] = n_im
            h_re, h_im = n_re, n_im
        for f in neighbours:
            f()
        hre_s[:, cols] = h_re
        him_s[:, cols] = h_im
    output_proj(SSM_BLOCKS - 1, 0)
    output_proj(SSM_BLOCKS - 1, 1)
    hg_s[...] = _gelu_exact(y_s[...] + dskip_ref[...] * u_s[...]).astype(BF16)
    glu_w = MXU_TILE
    for c in range(D_MODEL // glu_w):
        cols = slice(c * glu_w, (c + 1) * glu_w)
        hg = hg_s[...]
        val = jnp.dot(hg, wglu_ref[:, cols], preferred_element_type=F32)
        gate = jnp.dot(hg, wglu_ref[:, D_MODEL + c * glu_w:D_MODEL + (c + 1) * glu_w],
                       preferred_element_type=F32)
        out = val * _sigmoid(gate)
        if batch_major:
            for sl in range(c * glu_w // LANES, (c + 1) * glu_w // LANES):
                xt_s[sl] = xt_s[sl] + out[:, sl * LANES - c * glu_w:(sl + 1) * LANES - c * glu_w]
        else:
            y_s[:, cols] = x_ref[...].reshape(rows_all, D_MODEL)[:, cols] + out
    if batch_major:
        for b in range(n_seq):
            for sl in range(n_slabs):
                xo_ref[b, :, sl * LANES:(sl + 1) * LANES] = xt_s[sl, pl.ds(b, chunk, stride=n_seq), :]
    else:
        xo_ref[...] = y_s[...].reshape(chunk, n_seq, D_MODEL)

    @pl.when(i == n_steps - 1)
    def _():
        hre_ref[...] = hre_s[...] if batch_major else hre_s[...].T
        him_ref[...] = him_s[...] if batch_major else him_s[...].T


def _ssm_weights(lam_re, lam_im, log_dt, b_re, b_im, c_re, c_im):
    g, p = SSM_GROUPS, SSM_P
    n = g * SSM_GROUP
    rep = lambda z: jnp.repeat(z, SSM_GROUP, axis=0)
    log_dt = log_dt.reshape(g, 1)
    bt_re = jnp.swapaxes(b_re, 1, 2).reshape(n, p)
    bt_im = jnp.swapaxes(b_im, 1, 2).reshape(n, p)
    tile = _const(np.tile(np.eye(p), (1, SSM_BLOCK_G)), BF16)
    eye = _const(np.eye(LANES), BF16)
    args = [lam_re, lam_im, log_dt, rep(lam_re), rep(lam_im), rep(log_dt), bt_re, bt_im,
            c_re.reshape(n, p), c_im.reshape(n, p), tile, eye]
    out_shapes = ((g, p), (g, p), (SSM_BLOCKS, LANES, 2 * SSM_HALF), (SSM_BLOCKS, SSM_HALF, LANES),
                  (SSM_BLOCKS, SSM_HALF, LANES))
    out_dtypes = (F32, F32, BF16, BF16, BF16)
    a_re, a_im, w_b, c_re_bd, c_im_bd = pl.pallas_call(
        _ssm_prep_kernel,
        grid=(1,),
        in_specs=[_const_spec(a.shape) for a in args],
        out_specs=tuple(_const_out(s) for s in out_shapes),
        out_shape=tuple(jax.ShapeDtypeStruct(s, d) for s, d in zip(out_shapes, out_dtypes)),
        name="ssm_prep",
    )(*args)
    n_state = g * p
    return a_re.reshape(1, n_state), a_im.reshape(1, n_state), w_b, c_re_bd, c_im_bd


def _ssm_layer(x, h_re0, h_im0, sw, chunk, batch_major):
    if batch_major:
        n_seq, t_len, _ = x.shape
        x_block = (n_seq, chunk, D_MODEL)
        x_map = lambda i: (0, i, 0)
    else:
        t_len, n_seq, _ = x.shape
        x_block = (chunk, n_seq, D_MODEL)
        x_map = lambda i: (i, 0, 0)
    rows = chunk * n_seq
    n_state = SSM_GROUPS * SSM_P
    args = [x, h_re0, h_im0, sw["norm"], sw["a_re"], sw["a_im"], sw["w_b"], sw["c_re"], sw["c_im"],
            sw["d_skip"], sw["w_glu"]]
    in_specs = [pl.BlockSpec(x_block, x_map)] + [_const_spec(a.shape) for a in args[1:]]
    state = jax.ShapeDtypeStruct((n_seq, n_state) if batch_major else (n_state, n_seq), F32)
    return pl.pallas_call(
        functools.partial(_ssm_kernel, n_seq=n_seq, chunk=chunk, batch_major=batch_major),
        grid=(t_len // chunk,),
        in_specs=in_specs,
        out_specs=(pl.BlockSpec(x_block, x_map), _const_out(state.shape), _const_out(state.shape)),
        out_shape=(jax.ShapeDtypeStruct(x.shape, F32), state, state),
        scratch_shapes=[pltpu.VMEM((D_MODEL // LANES, rows, LANES), F32),
                        pltpu.VMEM((rows, D_MODEL), F32), pltpu.VMEM((2, rows, 2 * SSM_HALF), F32),
                        pltpu.VMEM((rows, D_MODEL), F32), pltpu.VMEM((rows, D_MODEL), BF16),
                        pltpu.VMEM((n_seq, n_state), F32), pltpu.VMEM((n_seq, n_state), F32)],
        compiler_params=pltpu.CompilerParams(dimension_semantics=("arbitrary",),
                                             vmem_limit_bytes=VMEM_LIMIT),
        name="ssm_layer",
    )(*args)


def kernel(x_prompt, x_sample, state_ret, state_wkv, state_shift, state_ssm_re, state_ssm_im, norm_mix, w_in, ret_gn, mu_shift, wkv_w0, wkv_wB, wkv_a0, wkv_aB, wkv_gB, wkv_kk, wkv_ka, wkv_rk, wkv_ln_w, wkv_ln_b, w_out, ssm_lambda_re, ssm_lambda_im, ssm_log_dt, ssm_B_re, ssm_B_im, ssm_C_re, ssm_C_im, ssm_D, ssm_w_glu, mlp_norm, mlp_up, mlp_down, norm_f):
    lw = dict(
        norm=_row(norm_mix[0]), w_in=w_in[0].astype(BF16), ret_gn=_row(ret_gn[0]), mu=_row(mu_shift[0]),
        w0=_row(wkv_w0[0]), lora=_lora_block(wkv_wB[0], wkv_aB[0]), a0=_row(wkv_a0[0]),
        g_b=wkv_gB[0].astype(BF16), k_k=_row(wkv_kk[0]), k_a=_row(wkv_ka[0]), r_k=_row(wkv_rk[0]),
        ln_w=_row(wkv_ln_w[0]), ln_b=_row(wkv_ln_b[0]), w_out=w_out[0].astype(BF16))
    a_re, a_im, w_b, c_re, c_im = _ssm_weights(ssm_lambda_re[0], ssm_lambda_im[0], ssm_log_dt[0],
                                               ssm_B_re[0], ssm_B_im[0], ssm_C_re[0], ssm_C_im[0])
    sw = dict(norm=_row(norm_mix[1]), a_re=a_re, a_im=a_im, w_b=w_b, c_re=c_re, c_im=c_im,
              d_skip=_row(ssm_D[0]), w_glu=ssm_w_glu[0].astype(BF16))
    n_state = SSM_GROUPS * SSM_P
    nf = _row(norm_f)

    n_p, t_p, _ = x_prompt.shape
    n_s = x_sample.shape[0]
    x1, ret_p, wkv_p, shift_p = _mixer_prompt(x_prompt, lw)
    seq_last = lambda s: jnp.transpose(s, (1, 2, 3, 0))
    seq_first = lambda s: jnp.transpose(s, (3, 0, 1, 2))
    xs1, ret_s, wkv_s, shift_s = _mixer_sample(x_sample.reshape(n_s, D_MODEL), seq_last(state_ret[0]),
                                               seq_last(state_wkv[0]), state_shift[0], lw)
    ret_s, wkv_s = seq_first(ret_s), seq_first(wkv_s)
    x1, xs1 = _mlp(x1.reshape(n_p * t_p, D_MODEL), xs1, _row(mlp_norm[0]), mlp_up, mlp_down, nf, 0, False)
    zero_state = jnp.zeros((n_p, n_state), F32)
    x2, ssm_re_p, ssm_im_p = _ssm_layer(x1.reshape(n_p, t_p, D_MODEL), zero_state, zero_state, sw,
                                        SSM_CHUNK, True)
    ssm_seq_last = lambda s: jnp.transpose(s, (1, 2, 0)).reshape(n_state, n_s)
    xs2, ssm_re_s, ssm_im_s = _ssm_layer(
        xs1.reshape(1, n_s, D_MODEL), ssm_seq_last(state_ssm_re[0]), ssm_seq_last(state_ssm_im[0]),
        sw, 1, False)
    y_p, y_s = _mlp(x2.reshape(n_p * t_p, D_MODEL), xs2.reshape(n_s, D_MODEL), _row(mlp_norm[1]),
                    mlp_up, mlp_down, nf, 1, True)
    y_prompt = y_p.reshape(n_p, t_p, D_MODEL)
    ssm_re_p = ssm_re_p.reshape(n_p, SSM_GROUPS, SSM_P)
    ssm_im_p = ssm_im_p.reshape(n_p, SSM_GROUPS, SSM_P)
    ssm_seq_first = lambda s: jnp.transpose(s.reshape(SSM_GROUPS, SSM_P, n_s), (2, 0, 1))
    ssm_re_s, ssm_im_s = ssm_seq_first(ssm_re_s), ssm_seq_first(ssm_im_s)

    return (y_prompt, y_s.reshape(n_s, 1, D_MODEL),
            ret_p[None], wkv_p[None], shift_p[None], ssm_re_p[None], ssm_im_p[None],
            ret_s[None], wkv_s[None], shift_s[None], ssm_re_s[None], ssm_im_s[None])
```

```python
import functools
import math

import numpy as np
import jax
import jax.numpy as jnp
from jax import lax
from jax.experimental import pallas as pl
from jax.experimental.pallas import tpu as pltpu

F32 = jnp.float32
BF16 = jnp.bfloat16

LANES = 128
MXU_TILE = 256
VMEM_BYTES = 64 * 1024 * 1024

D_MODEL = 1024
N_HEADS = 8
HEAD_DIM = 64
HEADS_W = N_HEADS * HEAD_DIM
N_PAIRS = N_HEADS // 2
PAIR_W = 2 * HEAD_DIM
assert PAIR_W == LANES
ROPE_BASE = 10000.0
DECAY_LORA = 64
AAA_LORA = 64
GATE_LORA = 128
SHIFT_W = 3 * HEADS_W + DECAY_LORA + AAA_LORA + GATE_LORA
RET_COLS = 4 * HEADS_W
IN_W = RET_COLS + SHIFT_W
SSM_GROUP = 16
SSM_GROUPS = D_MODEL // SSM_GROUP
SSM_P = 64
SSM_BLOCK_G = LANES // SSM_GROUP
SSM_BLOCKS = SSM_GROUPS // SSM_BLOCK_G
SSM_HALF = SSM_BLOCK_G * SSM_P
D_FF = 4 * D_MODEL
RMS_EPS = 1e-6
GN_EPS = 1e-5
WKV_GN_EPS = 64e-5
PAST_LEN = 16384

MIX_CHUNK = 64
PROJ_PIECE = MXU_TILE
MIX_GROUP = 4
SSM_CHUNK = 128
MLP_ROWS = 1024
FF_CHUNK = 1024

VMEM_LIMIT = VMEM_BYTES - 6 * 1024 * 1024


def _dot(a, b):
    return jnp.dot(a.astype(BF16), b.astype(BF16), preferred_element_type=F32)


def _dot_nt(a, b):
    return lax.dot_general(a.astype(BF16), b.astype(BF16), (((1,), (1,)), ((), ())),
                           preferred_element_type=F32)


def _dot_tn(a, b):
    return lax.dot_general(a.astype(BF16), b.astype(BF16), (((0,), (0,)), ((), ())),
                           preferred_element_type=F32)


def _split3(x):
    hi = x.astype(BF16)
    r1 = x - hi.astype(F32)
    mid = r1.astype(BF16)
    lo = (r1 - mid.astype(F32)).astype(BF16)
    return hi, mid, lo


def _dot_exact_lhs(a_bf16, x):
    hi, mid, lo = _split3(x)
    f = lambda p: jnp.dot(a_bf16, p, preferred_element_type=F32)
    return f(hi) + f(mid) + f(lo)


def _segsum(x, ones_blk):
    xb = x.astype(BF16)
    outs = [jnp.dot(xb[:, c * MXU_TILE:(c + 1) * MXU_TILE], ones_blk, preferred_element_type=F32)
            for c in range(x.shape[1] // MXU_TILE)]
    return jnp.concatenate(outs, axis=1)


def _rms(x, w):
    return x * lax.rsqrt(jnp.mean(x * x, axis=-1, keepdims=True) + RMS_EPS) * w


def _sigmoid(x):
    return 1.0 / (1.0 + jnp.exp(-x))


def _softplus(x):
    return jnp.maximum(x, 0.0) + jnp.log1p(jnp.exp(-jnp.abs(x)))


def _head_norm(z, eps, ones_blk):
    mu = _segsum(z, ones_blk) * (1.0 / HEAD_DIM)
    zc = z - mu
    var = _segsum(zc * zc, ones_blk) * (1.0 / HEAD_DIM)
    return zc * lax.rsqrt(var + eps)


def _rope(z, cos, sin_signed):
    lane = lax.broadcasted_iota(jnp.int32, (1, HEADS_W), 1) % HEAD_DIM
    swapped = jnp.where(lane < HEAD_DIM // 2,
                        pltpu.roll(z, HEADS_W - HEAD_DIM // 2, axis=1),
                        pltpu.roll(z, HEAD_DIM // 2, axis=1))
    return z * cos + swapped * sin_signed


def _wkv_features(xs, w0, lora_w, a0, g_b, k_k, k_a, r_k, ones_blk, between=lambda: None):
    r = xs[:, 0:HEADS_W]
    kw = xs[:, HEADS_W:2 * HEADS_W]
    vw = xs[:, 2 * HEADS_W:3 * HEADS_W]
    lora_w_in = DECAY_LORA + AAA_LORA
    lo = xs[:, 3 * HEADS_W:3 * HEADS_W + lora_w_in]
    lane = lax.broadcasted_iota(jnp.int32, (1, lora_w_in), 1)
    lo = jnp.where(lane < DECAY_LORA, jnp.tanh(lo), lo)
    ll = _dot(lo, lora_w)
    w_log = -_softplus(-(w0 + ll[:, 0:HEADS_W])) - 0.5
    log_decay = -jnp.exp(w_log)
    between()
    alr = _sigmoid(a0 + ll[:, HEADS_W:2 * HEADS_W])
    gate = _dot(_sigmoid(xs[:, 3 * HEADS_W + lora_w_in:SHIFT_W]), g_b)
    between()
    kk = kw * k_k
    kk = kk / jnp.maximum(jnp.sqrt(_segsum(kk * kk, ones_blk)), 1e-12)
    k_mod = kw * (1.0 + (alr - 1.0) * k_a)
    between()
    bonus = _segsum(r * k_mod * r_k, ones_blk) * vw
    return r, log_decay, k_mod, vw, -kk, kk * alr, gate, bonus


def _stack_masked(x2, m0):
    return jnp.concatenate([jnp.where(m0, x2, 0.0), jnp.where(m0, 0.0, x2)], axis=0)


def _stack_dup(x2):
    return jnp.concatenate([x2, x2], axis=0)


def _mixer_prompt_kernel(
        xprev_ref, xnext_ref, normw_ref, win_ref, cos_ref, sin_ref, qdec_ref, kdec_ref, dmask_ref, sdec_ref,
        retgn_ref, mu_ref, w0_ref, lora_ref, a0_ref, gb_ref, kk_ref, ka_ref, rk_ref, lnw_ref,
        lnb_ref, wout_ref, ones_ref, tril_ref, strict_ref, incl_ref, bd_ref,
        xo_ref, rets_ref, wkvs_ref, shift_ref,
        p_s, cat_s, rs_s, ws_s, carry_s, *, n_seq, chunk, group):
    i = pl.program_id(0)
    n_chunks = pl.num_programs(0) - 1
    C = chunk
    slot = i % 2

    @pl.when(i == 0)
    def _():
        rs_s[...] = jnp.zeros_like(rs_s)
        ws_s[...] = jnp.zeros_like(ws_s)
        carry_s[...] = jnp.zeros_like(carry_s)
        cat_s[...] = jnp.zeros_like(cat_s)
        p_s[0] = _dot(_rms(xprev_ref[...].reshape(n_seq * C, D_MODEL), normw_ref[...]), win_ref[...])

    m0 = lax.broadcasted_iota(jnp.int32, (1, PAIR_W), 1) < HEAD_DIM
    ones_blk = ones_ref[...]
    NB = group
    R = NB * C
    row_id = lax.broadcasted_iota(jnp.int32, (R, 1), 0)
    tile_rows = lambda ref: jnp.concatenate([ref[...]] * NB, axis=0)
    pairs = range(N_PAIRS)
    sls = [slice(pr * PAIR_W, (pr + 1) * PAIR_W) for pr in pairs]
    chains = [(s, pr) for s in range(NB) for pr in pairs]
    seq_rows = [slice(s * C, (s + 1) * C) for s in range(NB)]

    def per_group(gi, carry):
        rows = pl.ds(pl.multiple_of(gi * R, R), R)
        b0 = gi * NB
        ret_states = {(s, pr): rs_s[pr, b0 + s] for s, pr in chains}
        wkv_states = {(s, pr): ws_s[pr, b0 + s] for s, pr in chains}
        shift_rows = [carry_s[pl.ds(b0 + s, 1), :] for s in range(NB)]
        p_cur = p_s.at[slot]
        wp = p_cur[rows, RET_COLS:IN_W]
        hn_next = _rms(xnext_ref[pl.ds(b0, NB)].reshape(R, D_MODEL), normw_ref[...]).astype(BF16)
        cat_prev = cat_s[rows, :]

        def next_in_proj(c0):
            p_s[1 - slot, rows, c0:c0 + PROJ_PIECE] = jnp.dot(
                hn_next, win_ref[:, c0:c0 + PROJ_PIECE], preferred_element_type=F32)

        def prev_out_proj(c0):
            cols = slice(c0, c0 + PROJ_PIECE)
            out = jnp.dot(cat_prev, wout_ref[:, cols], preferred_element_type=F32)
            xo_ref[pl.ds(b0, NB), :, cols] = xprev_ref[pl.ds(b0, NB), :, cols] + out.reshape(NB, C, PROJ_PIECE)

        pieces = [functools.partial(next_in_proj, c0) for c0 in range(0, IN_W, PROJ_PIECE)]
        pieces += [functools.partial(prev_out_proj, c0) for c0 in range(0, D_MODEL, PROJ_PIECE)]

        def fill(n=1):
            for _ in range(min(n, len(pieces))):
                pieces.pop(0)()

        cos, sin = tile_rows(cos_ref), tile_rows(sin_ref)
        q = _rope(p_cur[rows, 0:HEADS_W], cos, sin)
        fill()
        k = _rope(p_cur[rows, HEADS_W:2 * HEADS_W], cos, sin) * (HEAD_DIM ** -0.5)
        fill()
        v = p_cur[rows, 2 * HEADS_W:3 * HEADS_W]
        g = p_cur[rows, 3 * HEADS_W:4 * HEADS_W]
        qd = q * tile_rows(qdec_ref)
        kd = k * tile_rows(kdec_ref)
        fill()
        cut = lambda z, s, pr: z[seq_rows[s], sls[pr]]
        r_sc = {c: _dot_nt(_stack_masked(cut(q, *c), m0), _stack_dup(cut(k, *c))) * dmask_ref[c[1]]
                for c in chains}
        r_inner = {c: _dot(r_sc[c], _stack_dup(cut(v, *c))) for c in chains}
        r_cross = {c: _dot(cut(qd, *c), ret_states[c]) for c in chains}
        new_ret = {c: sdec_ref[c[1]] * ret_states[c] + _dot_tn(cut(kd, *c), cut(v, *c)) * bd_ref[...]
                   for c in chains}
        o = jnp.concatenate(
            [jnp.concatenate([jnp.where(m0, r_inner[(s, pr)][0:C], r_inner[(s, pr)][C:2 * C])
                              + r_cross[(s, pr)] for pr in pairs], axis=1) for s in range(NB)], axis=0)
        fill()
        ret = _head_norm(o, GN_EPS, ones_blk) * retgn_ref[...]
        ret_out = (g * _sigmoid(g) * ret).astype(BF16)
        fill()

        prev = pltpu.roll(wp, 1, axis=0)
        for s in range(NB):
            prev = jnp.where(row_id == s * C, shift_rows[s], prev)
        xs = wp + (prev - wp) * mu_ref[...]
        fill()
        r, lw, k_mod, vw, a_vec, b_vec, gate, bonus = _wkv_features(
            xs, w0_ref[...], lora_ref[...], a0_ref[...], gb_ref[...], kk_ref[...], ka_ref[...],
            rk_ref[...], ones_blk, fill)
        cw = _dot_exact_lhs(tril_ref[...], lw)
        fill()
        cw_last = [cw[s * C + C - 1:(s + 1) * C, :] for s in range(NB)]
        cwl = jnp.concatenate([jnp.broadcast_to(z, (C, HEADS_W)) for z in cw_last], axis=0)
        r_t = r * jnp.exp(cw)
        a_t = a_vec * jnp.exp(cw - lw)
        fill()
        w_inv = jnp.exp(-cw)
        b_t = b_vec * w_inv
        k_t = k_mod * w_inv
        fill()
        w_end = jnp.exp(cwl - cw)
        b_h = b_vec * w_end
        k_h = k_mod * w_end
        fill()
        w_all = [jnp.exp(z) for z in cw_last]
        lhs = {c: jnp.concatenate([_stack_masked(cut(a_t, *c), m0), _stack_masked(cut(r_t, *c), m0)], axis=0)
               for c in chains}
        fill(len(pieces) - 2)
        sc = {c: _dot_nt(lhs[c], jnp.concatenate([cut(b_t, *c), cut(k_t, *c)], axis=0)) for c in chains}
        sc_swapped = {c: pltpu.roll(sc[c], C, axis=1) for c in chains}
        sc_b = {c: jnp.where(m0, sc[c], sc_swapped[c]) for c in chains}
        sc_k = {c: jnp.where(m0, sc_swapped[c], sc[c]) for c in chains}
        on_state = {c: _dot_nt(lhs[c], wkv_states[c]) for c in chains}
        vv = {c: _stack_dup(cut(vw, *c)) for c in chains}
        n_pow = {c: sc_b[c][0:2 * C] * strict_ref[...] for c in chains}
        u = {c: on_state[c][0:2 * C] + _dot(sc_k[c][0:2 * C] * strict_ref[...], vv[c]) for c in chains}
        n_steps_solve = int(math.log2(C))
        for it in range(n_steps_solve):
            u = {c: u[c] + _dot(n_pow[c], u[c]) for c in chains}
            if it + 1 < n_steps_solve:
                n_pow = {c: _dot(n_pow[c], n_pow[c]) for c in chains}
        uv = {c: jnp.concatenate([u[c], vv[c]], axis=0) for c in chains}
        y_st = {c: on_state[c][2 * C:4 * C] + _dot(
            jnp.concatenate([sc_b[c][2 * C:4 * C] * incl_ref[...],
                             sc_k[c][2 * C:4 * C] * incl_ref[...]], axis=1), uv[c]) for c in chains}
        new_wkv = {c: wkv_states[c] * w_all[c[0]][:, sls[c[1]]] + bd_ref[...] * _dot_tn(
            uv[c], jnp.concatenate([_stack_masked(cut(b_h, *c), m0), _stack_masked(cut(k_h, *c), m0)], axis=0))
            for c in chains}
        y = jnp.concatenate(
            [jnp.concatenate([jnp.where(m0, y_st[(s, pr)][0:C], y_st[(s, pr)][C:2 * C]) for pr in pairs], axis=1)
             for s in range(NB)], axis=0)
        fill(len(pieces))
        yn = _head_norm(y, WKV_GN_EPS, ones_blk) * lnw_ref[...] + lnb_ref[...]
        cat_s[rows, 0:HEADS_W] = ret_out
        cat_s[rows, HEADS_W:2 * HEADS_W] = ((yn + bonus) * gate).astype(BF16)
        for s in range(NB):
            carry_s[pl.ds(b0 + s, 1), :] = wp[s * C + C - 1:(s + 1) * C, :]
        for s, pr in chains:
            rs_s[pr, b0 + s] = new_ret[(s, pr)]
            ws_s[pr, b0 + s] = new_wkv[(s, pr)]
        return carry

    @pl.when(i < n_chunks)
    def _():
        lax.fori_loop(0, n_seq // NB, per_group, 0)

    @pl.when(i == n_chunks)
    def _():
        out = jnp.dot(cat_s[...], wout_ref[...], preferred_element_type=F32)
        xo_ref[...] = xprev_ref[...] + out.reshape(n_seq, C, D_MODEL)

    @pl.when(i == n_chunks - 1)
    def _():
        shift_ref[...] = carry_s[...]
        for b in range(n_seq):
            for pr in range(N_PAIRS):
                rs = rs_s[pr, b]
                ws = ws_s[pr, b]
                rets_ref[b, 2 * pr] = rs[0:HEAD_DIM, 0:HEAD_DIM]
                rets_ref[b, 2 * pr + 1] = rs[HEAD_DIM:PAIR_W, HEAD_DIM:PAIR_W]
                wkvs_ref[b, 2 * pr] = ws[0:HEAD_DIM, 0:HEAD_DIM]
                wkvs_ref[b, 2 * pr + 1] = ws[HEAD_DIM:PAIR_W, HEAD_DIM:PAIR_W]


def _const_spec(shape):
    nd = len(shape)
    return pl.BlockSpec(shape, lambda *_: (0,) * nd, pipeline_mode=pl.Buffered(1))


def _const_out(shape):
    nd = len(shape)
    return pl.BlockSpec(shape, lambda *_: (0,) * nd)


def _const(a, dtype=F32):
    return jnp.asarray(np.asarray(a, np.float64), dtype=dtype)


def _retention_tables(chunk):
    log_g = np.log1p(-np.exp2(-5.0 - np.arange(N_HEADS, dtype=np.float64)))
    lane_g = np.repeat(log_g, HEAD_DIM)[None, :]
    idx = np.arange(chunk, dtype=np.float64)
    qdec = np.exp((idx + 1.0)[:, None] * lane_g)
    kdec = np.exp((chunk - 1.0 - idx)[:, None] * lane_g)
    rel = idx[:, None] - idx[None, :]
    dm = np.where(rel >= 0, np.exp(np.maximum(rel, 0.0)[None] * log_g[:, None, None]), 0.0)
    zero = np.zeros((chunk, chunk))
    dmask = np.stack([np.block([[dm[2 * p], zero], [zero, dm[2 * p + 1]]]) for p in range(N_PAIRS)])
    cdec = np.exp(chunk * log_g)
    hz = np.zeros((HEAD_DIM, HEAD_DIM))
    ho = np.ones((HEAD_DIM, HEAD_DIM))
    sdec = np.stack([np.block([[cdec[2 * p] * ho, hz], [hz, cdec[2 * p + 1] * ho]])
                     for p in range(N_PAIRS)])
    return _const(qdec), _const(kdec), _const(dmask), _const(sdec), cdec


def _rope_tables(pos):
    half = HEAD_DIM // 2
    inv_freq = ROPE_BASE ** (-np.arange(half, dtype=np.float64) / half)
    ang = np.asarray(pos, np.float64)[:, None] * inv_freq[None, :]
    cos = np.cos(ang)
    sin = np.sin(ang)
    cos_t = np.tile(np.concatenate([cos, cos], axis=1), (1, N_HEADS))
    sin_t = np.tile(np.concatenate([-sin, sin], axis=1), (1, N_HEADS))
    return _const(cos_t), _const(sin_t)


def _block_masks(chunk, group):
    i = np.arange(2 * chunk)
    same = (i[:, None] // chunk) == (i[None, :] // chunk)
    strict = same & (i[:, None] > i[None, :])
    incl = same & (i[:, None] >= i[None, :])
    j = np.arange(PAIR_W)
    bd = (j[:, None] // HEAD_DIM) == (j[None, :] // HEAD_DIM)
    t = np.arange(chunk)
    tril = np.kron(np.eye(group), t[:, None] >= t[None, :])
    o = np.arange(MXU_TILE)
    ones_blk = (o[:, None] // HEAD_DIM) == (o[None, :] // HEAD_DIM)
    return _const(strict), _const(incl), _const(bd), _const(tril, BF16), _const(ones_blk, BF16)


def _lora_block(w_b, a_b):
    z = jnp.zeros_like(w_b)
    return jnp.concatenate([jnp.concatenate([w_b, z], axis=1),
                            jnp.concatenate([z, a_b], axis=1)], axis=0).astype(BF16)


def _row(v):
    return v.reshape(1, -1).astype(F32)


def _mixer_prompt(x, lw):
    n_seq, t_len, _ = x.shape
    C = MIX_CHUNK
    n_chunks = t_len // C
    last = n_chunks - 1
    cos_t, sin_t = _rope_tables(np.arange(t_len))
    qdec, kdec, dmask, sdec, _ = _retention_tables(C)
    strict, incl, bd, tril, ones_blk = _block_masks(C, MIX_GROUP)
    prev_chunk = lambda i: (0, jnp.maximum(i - 1, 0), 0)
    in_specs = [
        pl.BlockSpec((n_seq, C, D_MODEL), prev_chunk),
        pl.BlockSpec((n_seq, C, D_MODEL), lambda i: (0, jnp.minimum(i + 1, last), 0)),
        _const_spec((1, D_MODEL)),
        _const_spec((D_MODEL, IN_W)),
        pl.BlockSpec((C, HEADS_W), lambda i: (jnp.minimum(i, last), 0)),
        pl.BlockSpec((C, HEADS_W), lambda i: (jnp.minimum(i, last), 0)),
    ]
    tail = [qdec, kdec, dmask, sdec, lw["ret_gn"], lw["mu"], lw["w0"], lw["lora"], lw["a0"], lw["g_b"],
            lw["k_k"], lw["k_a"], lw["r_k"], lw["ln_w"], lw["ln_b"], lw["w_out"], ones_blk, tril,
            strict, incl, bd]
    in_specs += [_const_spec(a.shape) for a in tail]
    out_shape = (
        jax.ShapeDtypeStruct((n_seq, t_len, D_MODEL), F32),
        jax.ShapeDtypeStruct((n_seq, N_HEADS, HEAD_DIM, HEAD_DIM), F32),
        jax.ShapeDtypeStruct((n_seq, N_HEADS, HEAD_DIM, HEAD_DIM), F32),
        jax.ShapeDtypeStruct((n_seq, SHIFT_W), F32),
    )
    out_specs = (
        pl.BlockSpec((n_seq, C, D_MODEL), prev_chunk),
        _const_out((n_seq, N_HEADS, HEAD_DIM, HEAD_DIM)),
        _const_out((n_seq, N_HEADS, HEAD_DIM, HEAD_DIM)),
        _const_out((n_seq, SHIFT_W)),
    )
    scratch = [
        pltpu.VMEM((2, n_seq * C, IN_W), F32),
        pltpu.VMEM((n_seq * C, 2 * HEADS_W), BF16),
        pltpu.VMEM((N_PAIRS, n_seq, PAIR_W, PAIR_W), F32),
        pltpu.VMEM((N_PAIRS, n_seq, PAIR_W, PAIR_W), F32),
        pltpu.VMEM((n_seq, SHIFT_W), F32),
    ]
    return pl.pallas_call(
        functools.partial(_mixer_prompt_kernel, n_seq=n_seq, chunk=C, group=MIX_GROUP),
        grid=(n_chunks + 1,),
        in_specs=in_specs, out_specs=out_specs, out_shape=out_shape, scratch_shapes=scratch,
        compiler_params=pltpu.CompilerParams(dimension_semantics=("arbitrary",),
                                             vmem_limit_bytes=VMEM_LIMIT),
        name="mixer_prompt",
    )(x, x, lw["norm"], lw["w_in"], cos_t, sin_t, *tail)


def _mixer_sample_pre_kernel(
        x_ref, shift_ref, normw_ref, win_ref, cos_ref, sin_ref, qdec_ref, mu_ref, w0_ref, lora_ref,
        a0_ref, gb_ref, kk_ref, ka_ref, rk_ref, ones_ref, feat_ref, feat_t_ref, newshift_ref):
    x = x_ref[...]
    p = _dot(_rms(x, normw_ref[...]), win_ref[...])
    q = _rope(p[:, 0:HEADS_W], cos_ref[...], sin_ref[...])
    k = _rope(p[:, HEADS_W:2 * HEADS_W], cos_ref[...], sin_ref[...]) * (HEAD_DIM ** -0.5)
    wp = p[:, RET_COLS:IN_W]
    xs = wp + (shift_ref[...] - wp) * mu_ref[...]
    r, lw, k_mod, vw, a_vec, b_vec, gate, bonus = _wkv_features(
        xs, w0_ref[...], lora_ref[...], a0_ref[...], gb_ref[...], kk_ref[...], ka_ref[...],
        rk_ref[...], ones_ref[...])
    newshift_ref[...] = wp
    state_feats = [q, q * qdec_ref[...], k, p[:, 2 * HEADS_W:3 * HEADS_W], r, jnp.exp(lw), k_mod, vw,
                   a_vec, b_vec]
    for n, f in enumerate(state_feats):
        feat_t_ref[n * HEADS_W:(n + 1) * HEADS_W, :] = f.T
    for n, f in enumerate([p[:, 3 * HEADS_W:4 * HEADS_W], gate, bonus]):
        feat_ref[:, n * HEADS_W:(n + 1) * HEADS_W] = f


_F_Q, _F_QD, _F_K, _F_V, _F_R, _F_W, _F_KM, _F_VW, _F_A, _F_B = range(10)
_N_STATE_FEATS = 10
_F_G, _F_GATE, _F_BONUS = range(3)
_N_ROW_FEATS = 3


def _mixer_sample_state_kernel(feat_t_ref, sdec_ref, ret_ref, wkv_ref, o_t_ref, reto_ref, wkvo_ref):
    h = pl.program_id(0)

    def head_rows(n):
        return feat_t_ref[pl.ds(pl.multiple_of(n * HEADS_W + h * HEAD_DIM, HEAD_DIM), HEAD_DIM), :]

    def head_row(n, i):
        return feat_t_ref[pl.ds(n * HEADS_W + h * HEAD_DIM + i, 1), :]

    a, w, b_vec, k_mod, r = (head_rows(n) for n in (_F_A, _F_W, _F_B, _F_KM, _F_R))

    def wkv_row(i, carry):
        s = wkv_ref[0, i]
        sa = jnp.sum(s * a, axis=0, keepdims=True)
        s_new = s * w + sa * b_vec + head_row(_F_VW, i) * k_mod
        wkvo_ref[0, i] = s_new
        o_t_ref[1, 0, pl.ds(i, 1), :] = jnp.sum(s_new * r, axis=0, keepdims=True)
        return carry

    lax.fori_loop(0, HEAD_DIM, wkv_row, 0, unroll=4)

    v, q, k = head_rows(_F_V), head_rows(_F_Q), head_rows(_F_K)
    g = sdec_ref[h]

    def ret_row(d, acc):
        s = ret_ref[0, d]
        reto_ref[0, d] = g * s + head_row(_F_K, d) * v
        return acc + head_row(_F_QD, d) * s

    cross = lax.fori_loop(0, HEAD_DIM, ret_row, jnp.zeros_like(v), unroll=4)
    o_t_ref[0, 0] = cross + jnp.sum(q * k, axis=0, keepdims=True) * v


def _mixer_sample_post_kernel(
        x_ref, feat_ref, o_t_ref, retgn_ref, lnw_ref, lnb_ref, wout_ref, ones_ref, xo_ref):
    ones_blk = ones_ref[...]
    g = feat_ref[:, _F_G * HEADS_W:(_F_G + 1) * HEADS_W]
    gate = feat_ref[:, _F_GATE * HEADS_W:(_F_GATE + 1) * HEADS_W]
    bonus = feat_ref[:, _F_BONUS * HEADS_W:(_F_BONUS + 1) * HEADS_W]
    o = o_t_ref[...].T
    ret = _head_norm(o[:, 0:HEADS_W], GN_EPS, ones_blk) * retgn_ref[...]
    ret_out = g * _sigmoid(g) * ret
    yn = _head_norm(o[:, HEADS_W:2 * HEADS_W], WKV_GN_EPS, ones_blk) * lnw_ref[...] + lnb_ref[...]
    wkv_out = (yn + bonus) * gate
    cat = jnp.concatenate([ret_out, wkv_out], axis=1)
    xo_ref[...] = x_ref[...] + _dot(cat, wout_ref[...])


def _mixer_sample(x, ret_t, wkv_t, shift0, lw):
    n = x.shape[0]
    cos_t, sin_t = _rope_tables(np.full((1,), PAST_LEN))
    qdec, _, _, _, cdec = _retention_tables(1)
    _, _, _, _, ones_blk = _block_masks(1, 1)
    args = [x, shift0, lw["norm"], lw["w_in"], cos_t, sin_t, qdec, lw["mu"], lw["w0"], lw["lora"],
            lw["a0"], lw["g_b"], lw["k_k"], lw["k_a"], lw["r_k"], ones_blk]
    feat, feat_t, new_shift = pl.pallas_call(
        _mixer_sample_pre_kernel,
        grid=(1,),
        in_specs=[_const_spec(a.shape) for a in args],
        out_specs=(_const_out((n, _N_ROW_FEATS * HEADS_W)), _const_out((_N_STATE_FEATS * HEADS_W, n)),
                   _const_out((n, SHIFT_W))),
        out_shape=(jax.ShapeDtypeStruct((n, _N_ROW_FEATS * HEADS_W), F32),
                   jax.ShapeDtypeStruct((_N_STATE_FEATS * HEADS_W, n), F32),
                   jax.ShapeDtypeStruct((n, SHIFT_W), F32)),
        compiler_params=pltpu.CompilerParams(vmem_limit_bytes=VMEM_LIMIT),
        name="mixer_sample_pre",
    )(*args)

    sdec = _const(np.broadcast_to(cdec[:, None, None], (N_HEADS, 1, n)))
    state_spec = pl.BlockSpec((1, HEAD_DIM, HEAD_DIM, n), lambda h: (h, 0, 0, 0))
    o_spec = pl.BlockSpec((2, 1, HEAD_DIM, n), lambda h: (0, h, 0, 0))
    o_t, ret_new, wkv_new = pl.pallas_call(
        _mixer_sample_state_kernel,
        grid=(N_HEADS,),
        in_specs=[_const_spec(feat_t.shape), _const_spec(sdec.shape), state_spec, state_spec],
        out_specs=(o_spec, state_spec, state_spec),
        out_shape=(jax.ShapeDtypeStruct((2, N_HEADS, HEAD_DIM, n), F32),
                   jax.ShapeDtypeStruct(ret_t.shape, F32), jax.ShapeDtypeStruct(wkv_t.shape, F32)),
        compiler_params=pltpu.CompilerParams(dimension_semantics=("arbitrary",),
                                             vmem_limit_bytes=VMEM_LIMIT),
        name="mixer_sample_state",
    )(feat_t, sdec, ret_t, wkv_t)

    args = [x, feat, o_t.reshape(2 * HEADS_W, n), lw["ret_gn"], lw["ln_w"], lw["ln_b"], lw["w_out"], ones_blk]
    x1 = pl.pallas_call(
        _mixer_sample_post_kernel,
        grid=(1,),
        in_specs=[_const_spec(a.shape) for a in args],
        out_specs=_const_out((n, D_MODEL)),
        out_shape=jax.ShapeDtypeStruct((n, D_MODEL), F32),
        compiler_params=pltpu.CompilerParams(vmem_limit_bytes=VMEM_LIMIT),
        name="mixer_sample_post",
    )(*args)
    return x1, ret_new, wkv_new, new_shift


def _mlp_kernel(xa_ref, xb_ref, normw_ref, wup_ref, wdown_ref, normf_ref, oa_ref, ob_ref,
                *, final_norm, steps_a):
    i = pl.program_id(0)

    def run(x_ref, o_ref):
        x = x_ref[...]
        hn = _rms(x, normw_ref[...]).astype(BF16)
        acc = x
        for c in range(D_FF // FF_CHUNK):
            sl = slice(c * FF_CHUNK, (c + 1) * FF_CHUNK)
            hid = jnp.dot(hn, wup_ref[:, sl].astype(BF16), preferred_element_type=F32)
            hid = jnp.square(jnp.maximum(hid, 0.0)).astype(BF16)
            acc = acc + jnp.dot(hid, wdown_ref[sl, :].astype(BF16), preferred_element_type=F32)
        if final_norm:
            acc = _rms(acc, normf_ref[...])
        o_ref[...] = acc

    @pl.when(i < steps_a)
    def _():
        run(xa_ref, oa_ref)

    @pl.when(i == steps_a)
    def _():
        run(xb_ref, ob_ref)


def _mlp(xa, xb, norm_w, w_up, w_down, norm_f, layer, final_norm):
    rows_a, rows_b = xa.shape[0], xb.shape[0]
    steps_a = rows_a // MLP_ROWS
    pick = lambda *_: (layer, 0, 0)
    block_a = pl.BlockSpec((MLP_ROWS, D_MODEL), lambda i: (jnp.minimum(i, steps_a - 1), 0))
    return pl.pallas_call(
        functools.partial(_mlp_kernel, final_norm=final_norm, steps_a=steps_a),
        grid=(steps_a + 1,),
        in_specs=[block_a, _const_spec((rows_b, D_MODEL)),
                  _const_spec((1, D_MODEL)),
                  pl.BlockSpec((None, D_MODEL, D_FF), pick, pipeline_mode=pl.Buffered(1)),
                  pl.BlockSpec((None, D_FF, D_MODEL), pick, pipeline_mode=pl.Buffered(1)),
                  _const_spec((1, D_MODEL))],
        out_specs=(block_a, _const_out((rows_b, D_MODEL))),
        out_shape=(jax.ShapeDtypeStruct((rows_a, D_MODEL), F32), jax.ShapeDtypeStruct((rows_b, D_MODEL), F32)),
        compiler_params=pltpu.CompilerParams(dimension_semantics=("arbitrary",),
                                             vmem_limit_bytes=VMEM_LIMIT),
        name="mlp",
    )(xa, xb, norm_w, w_up, w_down, norm_f)


def _ssm_prep_kernel(lre_ref, lim_ref, logdt_ref, lre_rep_ref, lim_rep_ref, logdt_rep_ref, bre_ref, bim_ref,
                     cre_in_ref, cim_in_ref, tile_ref, eye_ref, are_ref, aim_ref, wb_ref, cre_ref, cim_ref):
    def discretise(lre, lim, logdt):
        lre = jnp.minimum(lre, -1e-4)
        dt = jnp.exp(logdt)
        mag = jnp.exp(lre * dt)
        return lre, mag * jnp.cos(lim * dt), mag * jnp.sin(lim * dt)

    _, are_ref[...], aim_ref[...] = discretise(lre_ref[...], lim_ref[...], logdt_ref[...])
    lim = lim_rep_ref[...]
    lre, are, aim = discretise(lre_rep_ref[...], lim, logdt_rep_ref[...])
    den = lre * lre + lim * lim
    nre = are - 1.0
    cre = (nre * lre + aim * lim) / den
    cim = (aim * lre - nre * lim) / den
    bre = bre_ref[...]
    bim = bim_ref[...]
    bb = (cre * bre - cim * bim, cre * bim + cim * bre)

    in_mask = (lax.broadcasted_iota(jnp.int32, (LANES, 1), 0) // SSM_GROUP
               == lax.broadcasted_iota(jnp.int32, (1, SSM_HALF), 1) // SSM_P)
    out_group = lax.broadcasted_iota(jnp.int32, (1, LANES), 1) // SSM_GROUP
    for blk in range(SSM_BLOCKS):
        rows = slice(blk * LANES, (blk + 1) * LANES)
        for part in range(2):
            tiled = jnp.dot(bb[part][rows].astype(BF16), tile_ref[...], preferred_element_type=F32)
            wb_ref[blk, :, part * SSM_HALF:(part + 1) * SSM_HALF] = jnp.where(in_mask, tiled, 0.0).astype(BF16)
        for src, dst in ((cre_in_ref, cre_ref), (cim_in_ref, cim_ref)):
            c_t = _dot_tn(src[rows, :], eye_ref[...])
            dst[blk] = jnp.concatenate([jnp.where(out_group == gl, c_t, 0.0) for gl in range(SSM_BLOCK_G)],
                                       axis=0).astype(BF16)


def _gelu_exact(x):
    return 0.5 * x * (1.0 + lax.erf(x * (2.0 ** -0.5)))


def _ssm_kernel(x_ref, hre0_ref, him0_ref, normw_ref, are_ref, aim_ref, wb_ref, cre_ref, cim_ref, dskip_ref,
                wglu_ref, xo_ref, hre_ref, him_ref, xt_s, u_s, bu_s, y_s, hg_s, hre_s, him_s,
                *, n_seq, chunk, batch_major):
    i = pl.program_id(0)
    n_steps = pl.num_programs(0)
    rows_all = chunk * n_seq
    n_slabs = D_MODEL // LANES

    @pl.when(i == 0)
    def _():
        hre_s[...] = hre0_ref[...] if batch_major else hre0_ref[...].T
        him_s[...] = him0_ref[...] if batch_major else him0_ref[...].T

    if batch_major:
        for b in range(n_seq):
            for sl in range(n_slabs):
                xt_s[sl, pl.ds(b, chunk, stride=n_seq), :] = x_ref[b, :, sl * LANES:(sl + 1) * LANES]
        ssq = sum(jnp.sum(jnp.square(xt_s[sl]), axis=-1, keepdims=True) for sl in range(n_slabs))
        inv = lax.rsqrt(ssq * (1.0 / D_MODEL) + RMS_EPS)
        for sl in range(n_slabs):
            cols = slice(sl * LANES, (sl + 1) * LANES)
            u_s[:, cols] = xt_s[sl] * inv * normw_ref[:, cols]
    else:
        u_s[...] = _rms(x_ref[...].reshape(rows_all, D_MODEL), normw_ref[...])

    def input_proj(blk, part):
        c = slice(part * SSM_HALF, (part + 1) * SSM_HALF)
        bu_s[blk % 2, :, c] = _dot(u_s[:, blk * LANES:(blk + 1) * LANES], wb_ref[blk, :, c])

    def output_proj(blk, part):
        buf = bu_s.at[blk % 2]
        cols = slice(blk * LANES, (blk + 1) * LANES)
        if part == 0:
            y_s[:, cols] = _dot(buf[:, 0:SSM_HALF], cre_ref[blk])
        else:
            y_s[:, cols] = y_s[:, cols] - _dot(buf[:, SSM_HALF:2 * SSM_HALF], cim_ref[blk])

    input_proj(0, 0)
    input_proj(0, 1)
    for blk in range(SSM_BLOCKS):
        neighbours = []
        if blk >= 1:
            neighbours += [functools.partial(output_proj, blk - 1, 0), functools.partial(output_proj, blk - 1, 1)]
        if blk + 1 < SSM_BLOCKS:
            neighbours += [functools.partial(input_proj, blk + 1, 0), functools.partial(input_proj, blk + 1, 1)]
        buf = bu_s.at[blk % 2]
        cols = slice(blk * SSM_HALF, (blk + 1) * SSM_HALF)
        a_re = jnp.broadcast_to(are_ref[:, cols], (n_seq, SSM_HALF))
        a_im = jnp.broadcast_to(aim_ref[:, cols], (n_seq, SSM_HALF))
        h_re = hre_s[:, cols]
        h_im = him_s[:, cols]
        seg = -(-chunk // max(len(neighbours), 1))
        for t in range(chunk):
            if t % seg == 0 and neighbours:
                neighbours.pop(0)()
            rows = slice(t * n_seq, (t + 1) * n_seq)
            n_re = a_re * h_re - a_im * h_im + buf[rows, 0:SSM_HALF]
            n_im = a_re * h_im + a_im * h_re + buf[rows, SSM_HALF:2 * SSM_HALF]
            buf[rows, 0:SSM_HALF] = n_re
            buf[rows, SSM_HALF:2 * SSM_HALF] = n_im
            h_re, h_im = n_re, n_im
        for f in neighbours:
            f()
        hre_s[:, cols] = h_re
        him_s[:, cols] = h_im
    output_proj(SSM_BLOCKS - 1, 0)
    output_proj(SSM_BLOCKS - 1, 1)
    hg_s[...] = _gelu_exact(y_s[...] + dskip_ref[...] * u_s[...]).astype(BF16)
    glu_w = MXU_TILE
    for c in range(D_MODEL // glu_w):
        cols = slice(c * glu_w, (c + 1) * glu_w)
        hg = hg_s[...]
        val = jnp.dot(hg, wglu_ref[:, cols], preferred_element_type=F32)
        gate = jnp.dot(hg, wglu_ref[:, D_MODEL + c * glu_w:D_MODEL + (c + 1) * glu_w],
                       preferred_element_type=F32)
        out = val * _sigmoid(gate)
        if batch_major:
            for sl in range(c * glu_w // LANES, (c + 1) * glu_w // LANES):
                xt_s[sl] = xt_s[sl] + out[:, sl * LANES - c * glu_w:(sl + 1) * LANES - c * glu_w]
        else:
            y_s[:, cols] = x_ref[...].reshape(rows_all, D_MODEL)[:, cols] + out
    if batch_major:
        for b in range(n_seq):
            for sl in range(n_slabs):
                xo_ref[b, :, sl * LANES:(sl + 1) * LANES] = xt_s[sl, pl.ds(b, chunk, stride=n_seq), :]
    else:
        xo_ref[...] = y_s[...].reshape(chunk, n_seq, D_MODEL)

    @pl.when(i == n_steps - 1)
    def _():
        hre_ref[...] = hre_s[...] if batch_major else hre_s[...].T
        him_ref[...] = him_s[...] if batch_major else him_s[...].T


def _ssm_weights(lam_re, lam_im, log_dt, b_re, b_im, c_re, c_im):
    g, p = SSM_GROUPS, SSM_P
    n = g * SSM_GROUP
    rep = lambda z: jnp.repeat(z, SSM_GROUP, axis=0)
    log_dt = log_dt.reshape(g, 1)
    bt_re = jnp.swapaxes(b_re, 1, 2).reshape(n, p)
    bt_im = jnp.swapaxes(b_im, 1, 2).reshape(n, p)
    tile = _const(np.tile(np.eye(p), (1, SSM_BLOCK_G)), BF16)
    eye = _const(np.eye(LANES), BF16)
    args = [lam_re, lam_im, log_dt, rep(lam_re), rep(lam_im), rep(log_dt), bt_re, bt_im,
            c_re.reshape(n, p), c_im.reshape(n, p), tile, eye]
    out_shapes = ((g, p), (g, p), (SSM_BLOCKS, LANES, 2 * SSM_HALF), (SSM_BLOCKS, SSM_HALF, LANES),
                  (SSM_BLOCKS, SSM_HALF, LANES))
    out_dtypes = (F32, F32, BF16, BF16, BF16)
    a_re, a_im, w_b, c_re_bd, c_im_bd = pl.pallas_call(
        _ssm_prep_kernel,
        grid=(1,),
        in_specs=[_const_spec(a.shape) for a in args],
        out_specs=tuple(_const_out(s) for s in out_shapes),
        out_shape=tuple(jax.ShapeDtypeStruct(s, d) for s, d in zip(out_shapes, out_dtypes)),
        name="ssm_prep",
    )(*args)
    n_state = g * p
    return a_re.reshape(1, n_state), a_im.reshape(1, n_state), w_b, c_re_bd, c_im_bd


def _ssm_layer(x, h_re0, h_im0, sw, chunk, batch_major):
    if batch_major:
        n_seq, t_len, _ = x.shape
        x_block = (n_seq, chunk, D_MODEL)
        x_map = lambda i: (0, i, 0)
    else:
        t_len, n_seq, _ = x.shape
        x_block = (chunk, n_seq, D_MODEL)
        x_map = lambda i: (i, 0, 0)
    rows = chunk * n_seq
    n_state = SSM_GROUPS * SSM_P
    args = [x, h_re0, h_im0, sw["norm"], sw["a_re"], sw["a_im"], sw["w_b"], sw["c_re"], sw["c_im"],
            sw["d_skip"], sw["w_glu"]]
    in_specs = [pl.BlockSpec(x_block, x_map)] + [_const_spec(a.shape) for a in args[1:]]
    state = jax.ShapeDtypeStruct((n_seq, n_state) if batch_major else (n_state, n_seq), F32)
    return pl.pallas_call(
        functools.partial(_ssm_kernel, n_seq=n_seq, chunk=chunk, batch_major=batch_major),
        grid=(t_len // chunk,),
        in_specs=in_specs,
        out_specs=(pl.BlockSpec(x_block, x_map), _const_out(state.shape), _const_out(state.shape)),
        out_shape=(jax.ShapeDtypeStruct(x.shape, F32), state, state),
        scratch_shapes=[pltpu.VMEM((D_MODEL // LANES, rows, LANES), F32),
                        pltpu.VMEM((rows, D_MODEL), F32), pltpu.VMEM((2, rows, 2 * SSM_HALF), F32),
                        pltpu.VMEM((rows, D_MODEL), F32), pltpu.VMEM((rows, D_MODEL), BF16),
                        pltpu.VMEM((n_seq, n_state), F32), pltpu.VMEM((n_seq, n_state), F32)],
        compiler_params=pltpu.CompilerParams(dimension_semantics=("arbitrary",),
                                             vmem_limit_bytes=VMEM_LIMIT),
        name="ssm_layer",
    )(*args)


def kernel(x_prompt, x_sample, state_ret, state_wkv, state_shift, state_ssm_re, state_ssm_im, norm_mix, w_in, ret_gn, mu_shift, wkv_w0, wkv_wB, wkv_a0, wkv_aB, wkv_gB, wkv_kk, wkv_ka, wkv_rk, wkv_ln_w, wkv_ln_b, w_out, ssm_lambda_re, ssm_lambda_im, ssm_log_dt, ssm_B_re, ssm_B_im, ssm_C_re, ssm_C_im, ssm_D, ssm_w_glu, mlp_norm, mlp_up, mlp_down, norm_f):
    lw = dict(
        norm=_row(norm_mix[0]), w_in=w_in[0].astype(BF16), ret_gn=_row(ret_gn[0]), mu=_row(mu_shift[0]),
        w0=_row(wkv_w0[0]), lora=_lora_block(wkv_wB[0], wkv_aB[0]), a0=_row(wkv_a0[0]),
        g_b=wkv_gB[0].astype(BF16), k_k=_row(wkv_kk[0]), k_a=_row(wkv_ka[0]), r_k=_row(wkv_rk[0]),
        ln_w=_row(wkv_ln_w[0]), ln_b=_row(wkv_ln_b[0]), w_out=w_out[0].astype(BF16))
    a_re, a_im, w_b, c_re, c_im = _ssm_weights(ssm_lambda_re[0], ssm_lambda_im[0], ssm_log_dt[0],
                                               ssm_B_re[0], ssm_B_im[0], ssm_C_re[0], ssm_C_im[0])
    sw = dict(norm=_row(norm_mix[1]), a_re=a_re, a_im=a_im, w_b=w_b, c_re=c_re, c_im=c_im,
              d_skip=_row(ssm_D[0]), w_glu=ssm_w_glu[0].astype(BF16))
    n_state = SSM_GROUPS * SSM_P
    nf = _row(norm_f)

    n_p, t_p, _ = x_prompt.shape
    n_s = x_sample.shape[0]
    x1, ret_p, wkv_p, shift_p = _mixer_prompt(x_prompt, lw)
    seq_last = lambda s: jnp.transpose(s, (1, 2, 3, 0))
    seq_first = lambda s: jnp.transpose(s, (3, 0, 1, 2))
    xs1, ret_s, wkv_s, shift_s = _mixer_sample(x_sample.reshape(n_s, D_MODEL), seq_last(state_ret[0]),
                                               seq_last(state_wkv[0]), state_shift[0], lw)
    ret_s, wkv_s = seq_first(ret_s), seq_first(wkv_s)
    x1, xs1 = _mlp(x1.reshape(n_p * t_p, D_MODEL), xs1, _row(mlp_norm[0]), mlp_up, mlp_down, nf, 0, False)
    zero_state = jnp.zeros((n_p, n_state), F32)
    x2, ssm_re_p, ssm_im_p = _ssm_layer(x1.reshape(n_p, t_p, D_MODEL), zero_state, zero_state, sw,
                                        SSM_CHUNK, True)
    ssm_seq_last = lambda s: jnp.transpose(s, (1, 2, 0)).reshape(n_state, n_s)
    xs2, ssm_re_s, ssm_im_s = _ssm_layer(
        xs1.reshape(1, n_s, D_MODEL), ssm_seq_last(state_ssm_re[0]), ssm_seq_last(state_ssm_im[0]),
        sw, 1, False)
    y_p, y_s = _mlp(x2.reshape(n_p * t_p, D_MODEL), xs2.reshape(n_s, D_MODEL), _row(mlp_norm[1]),
                    mlp_up, mlp_down, nf, 1, True)
    y_prompt = y_p.reshape(n_p, t_p, D_MODEL)
    ssm_re_p = ssm_re_p.reshape(n_p, SSM_GROUPS, SSM_P)
    ssm_im_p = ssm_im_p.reshape(n_p, SSM_GROUPS, SSM_P)
    ssm_seq_first = lambda s: jnp.transpose(s.reshape(SSM_GROUPS, SSM_P, n_s), (2, 0, 1))
    ssm_re_s, ssm_im_s = ssm_seq_first(ssm_re_s), ssm_seq_first(ssm_im_s)

    return (y_prompt, y_s.reshape(n_s, 1, D_MODEL),
            ret_p[None], wkv_p[None], shift_p[None], ssm_re_p[None], ssm_im_p[None],
            ret_s[None], wkv_s[None], shift_s[None], ssm_re_s[None], ssm_im_s[None])
```

```python
import functools
import math

import numpy as np
import jax
import jax.numpy as jnp
from jax import lax
from jax.experimental import pallas as pl
from jax.experimental.pallas import tpu as pltpu

F32 = jnp.float32
BF16 = jnp.bfloat16

LANES = 128
MXU_TILE = 256
VMEM_BYTES = 64 * 1024 * 1024

D_MODEL = 1024
N_HEADS = 8
HEAD_DIM = 64
HEADS_W = N_HEADS * HEAD_DIM
N_PAIRS = N_HEADS // 2
PAIR_W = 2 * HEAD_DIM
assert PAIR_W == LANES
ROPE_BASE = 10000.0
DECAY_LORA = 64
AAA_LORA = 64
GATE_LORA = 128
SHIFT_W = 3 * HEADS_W + DECAY_LORA + AAA_LORA + GATE_LORA
RET_COLS = 4 * HEADS_W
IN_W = RET_COLS + SHIFT_W
SSM_GROUP = 16
SSM_GROUPS = D_MODEL // SSM_GROUP
SSM_P = 64
SSM_BLOCK_G = LANES // SSM_GROUP
SSM_BLOCKS = SSM_GROUPS // SSM_BLOCK_G
SSM_HALF = SSM_BLOCK_G * SSM_P
D_FF = 4 * D_MODEL
RMS_EPS = 1e-6
GN_EPS = 1e-5
WKV_GN_EPS = 64e-5
PAST_LEN = 16384

MIX_CHUNK = 64
PROJ_PIECE = MXU_TILE
MIX_GROUP = 4
SSM_CHUNK = 128
MLP_ROWS = 1024
FF_CHUNK = 1024

VMEM_LIMIT = VMEM_BYTES - 6 * 1024 * 1024


def _dot(a, b):
    return jnp.dot(a.astype(BF16), b.astype(BF16), preferred_element_type=F32)


def _dot_nt(a, b):
    return lax.dot_general(a.astype(BF16), b.astype(BF16), (((1,), (1,)), ((), ())),
                           preferred_element_type=F32)


def _dot_tn(a, b):
    return lax.dot_general(a.astype(BF16), b.astype(BF16), (((0,), (0,)), ((), ())),
                           preferred_element_type=F32)


def _split3(x):
    hi = x.astype(BF16)
    r1 = x - hi.astype(F32)
    mid = r1.astype(BF16)
    lo = (r1 - mid.astype(F32)).astype(BF16)
    return hi, mid, lo


def _dot_exact_lhs(a_bf16, x):
    hi, mid, lo = _split3(x)
    f = lambda p: jnp.dot(a_bf16, p, preferred_element_type=F32)
    return f(hi) + f(mid) + f(lo)


def _segsum(x, ones_blk):
    xb = x.astype(BF16)
    outs = [jnp.dot(xb[:, c * MXU_TILE:(c + 1) * MXU_TILE], ones_blk, preferred_element_type=F32)
            for c in range(x.shape[1] // MXU_TILE)]
    return jnp.concatenate(outs, axis=1)


def _rms(x, w):
    return x * lax.rsqrt(jnp.mean(x * x, axis=-1, keepdims=True) + RMS_EPS) * w


def _sigmoid(x):
    return 1.0 / (1.0 + jnp.exp(-x))


def _softplus(x):
    return jnp.maximum(x, 0.0) + jnp.log1p(jnp.exp(-jnp.abs(x)))


def _head_norm(z, eps, ones_blk):
    mu = _segsum(z, ones_blk) * (1.0 / HEAD_DIM)
    zc = z - mu
    var = _segsum(zc * zc, ones_blk) * (1.0 / HEAD_DIM)
    return zc * lax.rsqrt(var + eps)


def _rope(z, cos, sin_signed):
    lane = lax.broadcasted_iota(jnp.int32, (1, HEADS_W), 1) % HEAD_DIM
    swapped = jnp.where(lane < HEAD_DIM // 2,
                        pltpu.roll(z, HEADS_W - HEAD_DIM // 2, axis=1),
                        pltpu.roll(z, HEAD_DIM // 2, axis=1))
    return z * cos + swapped * sin_signed


def _wkv_features(xs, w0, lora_w, a0, g_b, k_k, k_a, r_k, ones_blk, between=lambda: None):
    r = xs[:, 0:HEADS_W]
    kw = xs[:, HEADS_W:2 * HEADS_W]
    vw = xs[:, 2 * HEADS_W:3 * HEADS_W]
    lora_w_in = DECAY_LORA + AAA_LORA
    lo = xs[:, 3 * HEADS_W:3 * HEADS_W + lora_w_in]
    lane = lax.broadcasted_iota(jnp.int32, (1, lora_w_in), 1)
    lo = jnp.where(lane < DECAY_LORA, jnp.tanh(lo), lo)
    ll = _dot(lo, lora_w)
    w_log = -_softplus(-(w0 + ll[:, 0:HEADS_W])) - 0.5
    log_decay = -jnp.exp(w_log)
    between()
    alr = _sigmoid(a0 + ll[:, HEADS_W:2 * HEADS_W])
    gate = _dot(_sigmoid(xs[:, 3 * HEADS_W + lora_w_in:SHIFT_W]), g_b)
    between()
    kk = kw * k_k
    kk = kk / jnp.maximum(jnp.sqrt(_segsum(kk * kk, ones_blk)), 1e-12)
    k_mod = kw * (1.0 + (alr - 1.0) * k_a)
    between()
    bonus = _segsum(r * k_mod * r_k, ones_blk) * vw
    return r, log_decay, k_mod, vw, -kk, kk * alr, gate, bonus


def _stack_masked(x2, m0):
    return jnp.concatenate([jnp.where(m0, x2, 0.0), jnp.where(m0, 0.0, x2)], axis=0)


def _stack_dup(x2):
    return jnp.concatenate([x2, x2], axis=0)


def _mixer_prompt_kernel(
        xprev_ref, xnext_ref, normw_ref, win_ref, cos_ref, sin_ref, qdec_ref, kdec_ref, dmask_ref, sdec_ref,
        retgn_ref, mu_ref, w0_ref, lora_ref, a0_ref, gb_ref, kk_ref, ka_ref, rk_ref, lnw_ref,
        lnb_ref, wout_ref, ones_ref, tril_ref, strict_ref, incl_ref, bd_ref,
        xo_ref, rets_ref, wkvs_ref, shift_ref,
        p_s, cat_s, rs_s, ws_s, carry_s, *, n_seq, chunk, group):
    i = pl.program_id(0)
    n_chunks = pl.num_programs(0) - 1
    C = chunk
    slot = i % 2

    @pl.when(i == 0)
    def _():
        rs_s[...] = jnp.zeros_like(rs_s)
        ws_s[...] = jnp.zeros_like(ws_s)
        carry_s[...] = jnp.zeros_like(carry_s)
        cat_s[...] = jnp.zeros_like(cat_s)
        p_s[0] = _dot(_rms(xprev_ref[...].reshape(n_seq * C, D_MODEL), normw_ref[...]), win_ref[...])

    m0 = lax.broadcasted_iota(jnp.int32, (1, PAIR_W), 1) < HEAD_DIM
    ones_blk = ones_ref[...]
    NB = group
    R = NB * C
    row_id = lax.broadcasted_iota(jnp.int32, (R, 1), 0)
    tile_rows = lambda ref: jnp.concatenate([ref[...]] * NB, axis=0)
    pairs = range(N_PAIRS)
    sls = [slice(pr * PAIR_W, (pr + 1) * PAIR_W) for pr in pairs]
    chains = [(s, pr) for s in range(NB) for pr in pairs]
    seq_rows = [slice(s * C, (s + 1) * C) for s in range(NB)]

    def per_group(gi, carry):
        rows = pl.ds(pl.multiple_of(gi * R, R), R)
        b0 = gi * NB
        ret_states = {(s, pr): rs_s[pr, b0 + s] for s, pr in chains}
        wkv_states = {(s, pr): ws_s[pr, b0 + s] for s, pr in chains}
        shift_rows = [carry_s[pl.ds(b0 + s, 1), :] for s in range(NB)]
        p_cur = p_s.at[slot]
        wp = p_cur[rows, RET_COLS:IN_W]
        hn_next = _rms(xnext_ref[pl.ds(b0, NB)].reshape(R, D_MODEL), normw_ref[...]).astype(BF16)
        cat_prev = cat_s[rows, :]

        def next_in_proj(c0):
            p_s[1 - slot, rows, c0:c0 + PROJ_PIECE] = jnp.dot(
                hn_next, win_ref[:, c0:c0 + PROJ_PIECE], preferred_element_type=F32)

        def prev_out_proj(c0):
            cols = slice(c0, c0 + PROJ_PIECE)
            out = jnp.dot(cat_prev, wout_ref[:, cols], preferred_element_type=F32)
            xo_ref[pl.ds(b0, NB), :, cols] = xprev_ref[pl.ds(b0, NB), :, cols] + out.reshape(NB, C, PROJ_PIECE)

        pieces = [functools.partial(next_in_proj, c0) for c0 in range(0, IN_W, PROJ_PIECE)]
        pieces += [functools.partial(prev_out_proj, c0) for c0 in range(0, D_MODEL, PROJ_PIECE)]

        def fill(n=1):
            for _ in range(min(n, len(pieces))):
                pieces.pop(0)()

        cos, sin = tile_rows(cos_ref), tile_rows(sin_ref)
        q = _rope(p_cur[rows, 0:HEADS_W], cos, sin)
        fill()
        k = _rope(p_cur[rows, HEADS_W:2 * HEADS_W], cos, sin) * (HEAD_DIM ** -0.5)
        fill()
        v = p_cur[rows, 2 * HEADS_W:3 * HEADS_W]
        g = p_cur[rows, 3 * HEADS_W:4 * HEADS_W]
        qd = q * tile_rows(qdec_ref)
        kd = k * tile_rows(kdec_ref)
        fill()
        cut = lambda z, s, pr: z[seq_rows[s], sls[pr]]
        r_sc = {c: _dot_nt(_stack_masked(cut(q, *c), m0), _stack_dup(cut(k, *c))) * dmask_ref[c[1]]
                for c in chains}
        r_inner = {c: _dot(r_sc[c], _stack_dup(cut(v, *c))) for c in chains}
        r_cross = {c: _dot(cut(qd, *c), ret_states[c]) for c in chains}
        new_ret = {c: sdec_ref[c[1]] * ret_states[c] + _dot_tn(cut(kd, *c), cut(v, *c)) * bd_ref[...]
                   for c in chains}
        o = jnp.concatenate(
            [jnp.concatenate([jnp.where(m0, r_inner[(s, pr)][0:C], r_inner[(s, pr)][C:2 * C])
                              + r_cross[(s, pr)] for pr in pairs], axis=1) for s in range(NB)], axis=0)
        fill()
        ret = _head_norm(o, GN_EPS, ones_blk) * retgn_ref[...]
        ret_out = (g * _sigmoid(g) * ret).astype(BF16)
        fill()

        prev = pltpu.roll(wp, 1, axis=0)
        for s in range(NB):
            prev = jnp.where(row_id == s * C, shift_rows[s], prev)
        xs = wp + (prev - wp) * mu_ref[...]
        fill()
        r, lw, k_mod, vw, a_vec, b_vec, gate, bonus = _wkv_features(
            xs, w0_ref[...], lora_ref[...], a0_ref[...], gb_ref[...], kk_ref[...], ka_ref[...],
            rk_ref[...], ones_blk, fill)
        cw = _dot_exact_lhs(tril_ref[...], lw)
        fill()
        cw_last = [cw[s * C + C - 1:(s + 1) * C, :] for s in range(NB)]
        cwl = jnp.concatenate([jnp.broadcast_to(z, (C, HEADS_W)) for z in cw_last], axis=0)
        r_t = r * jnp.exp(cw)
        a_t = a_vec * jnp.exp(cw - lw)
        fill()
        w_inv = jnp.exp(-cw)
        b_t = b_vec * w_inv
        k_t = k_mod * w_inv
        fill()
        w_end = jnp.exp(cwl - cw)
        b_h = b_vec * w_end
        k_h = k_mod * w_end
        fill()
        w_all = [jnp.exp(z) for z in cw_last]
        lhs = {c: jnp.concatenate([_stack_masked(cut(a_t, *c), m0), _stack_masked(cut(r_t, *c), m0)], axis=0)
               for c in chains}
        fill(len(pieces) - 1)
        sc = {c: _dot_nt(lhs[c], jnp.concatenate([cut(b_t, *c), cut(k_t, *c)], axis=0)) for c in chains}
        sc_swapped = {c: pltpu.roll(sc[c], C, axis=1) for c in chains}
        sc_b = {c: jnp.where(m0, sc[c], sc_swapped[c]) for c in chains}
        sc_k = {c: jnp.where(m0, sc_swapped[c], sc[c]) for c in chains}
        on_state = {c: _dot_nt(lhs[c], wkv_states[c]) for c in chains}
        vv = {c: _stack_dup(cut(vw, *c)) for c in chains}
        n_pow = {c: sc_b[c][0:2 * C] * strict_ref[...] for c in chains}
        u = {c: on_state[c][0:2 * C] + _dot(sc_k[c][0:2 * C] * strict_ref[...], vv[c]) for c in chains}
        n_steps_solve = int(math.log2(C))
        for it in range(n_steps_solve):
            u = {c: u[c] + _dot(n_pow[c], u[c]) for c in chains}
            if it + 1 < n_steps_solve:
                n_pow = {c: _dot(n_pow[c], n_pow[c]) for c in chains}
        uv = {c: jnp.concatenate([u[c], vv[c]], axis=0) for c in chains}
        y_st = {c: on_state[c][2 * C:4 * C] + _dot(
            jnp.concatenate([sc_b[c][2 * C:4 * C] * incl_ref[...],
                             sc_k[c][2 * C:4 * C] * incl_ref[...]], axis=1), uv[c]) for c in chains}
        new_wkv = {c: wkv_states[c] * w_all[c[0]][:, sls[c[1]]] + bd_ref[...] * _dot_tn(
            uv[c], jnp.concatenate([_stack_masked(cut(b_h, *c), m0), _stack_masked(cut(k_h, *c), m0)], axis=0))
            for c in chains}
        y = jnp.concatenate(
            [jnp.concatenate([jnp.where(m0, y_st[(s, pr)][0:C], y_st[(s, pr)][C:2 * C]) for pr in pairs], axis=1)
             for s in range(NB)], axis=0)
        fill(len(pieces))
        yn = _head_norm(y, WKV_GN_EPS, ones_blk) * lnw_ref[...] + lnb_ref[...]
        cat_s[rows, 0:HEADS_W] = ret_out
        cat_s[rows, HEADS_W:2 * HEADS_W] = ((yn + bonus) * gate).astype(BF16)
        for s in range(NB):
            carry_s[pl.ds(b0 + s, 1), :] = wp[s * C + C - 1:(s + 1) * C, :]
        for s, pr in chains:
            rs_s[pr, b0 + s] = new_ret[(s, pr)]
            ws_s[pr, b0 + s] = new_wkv[(s, pr)]
        return carry

    @pl.when(i < n_chunks)
    def _():
        lax.fori_loop(0, n_seq // NB, per_group, 0)

    @pl.when(i == n_chunks)
    def _():
        out = jnp.dot(cat_s[...], wout_ref[...], preferred_element_type=F32)
        xo_ref[...] = xprev_ref[...] + out.reshape(n_seq, C, D_MODEL)

    @pl.when(i == n_chunks - 1)
    def _():
        shift_ref[...] = carry_s[...]
        for b in range(n_seq):
            for pr in range(N_PAIRS):
                rs = rs_s[pr, b]
                ws = ws_s[pr, b]
                rets_ref[b, 2 * pr] = rs[0:HEAD_DIM, 0:HEAD_DIM]
                rets_ref[b, 2 * pr + 1] = rs[HEAD_DIM:PAIR_W, HEAD_DIM:PAIR_W]
                wkvs_ref[b, 2 * pr] = ws[0:HEAD_DIM, 0:HEAD_DIM]
                wkvs_ref[b, 2 * pr + 1] = ws[HEAD_DIM:PAIR_W, HEAD_DIM:PAIR_W]


def _const_spec(shape):
    nd = len(shape)
    return pl.BlockSpec(shape, lambda *_: (0,) * nd, pipeline_mode=pl.Buffered(1))


def _const_out(shape):
    nd = len(shape)
    return pl.BlockSpec(shape, lambda *_: (0,) * nd)


def _const(a, dtype=F32):
    return jnp.asarray(np.asarray(a, np.float64), dtype=dtype)


def _retention_tables(chunk):
    log_g = np.log1p(-np.exp2(-5.0 - np.arange(N_HEADS, dtype=np.float64)))
    lane_g = np.repeat(log_g, HEAD_DIM)[None, :]
    idx = np.arange(chunk, dtype=np.float64)
    qdec = np.exp((idx + 1.0)[:, None] * lane_g)
    kdec = np.exp((chunk - 1.0 - idx)[:, None] * lane_g)
    rel = idx[:, None] - idx[None, :]
    dm = np.where(rel >= 0, np.exp(np.maximum(rel, 0.0)[None] * log_g[:, None, None]), 0.0)
    zero = np.zeros((chunk, chunk))
    dmask = np.stack([np.block([[dm[2 * p], zero], [zero, dm[2 * p + 1]]]) for p in range(N_PAIRS)])
    cdec = np.exp(chunk * log_g)
    hz = np.zeros((HEAD_DIM, HEAD_DIM))
    ho = np.ones((HEAD_DIM, HEAD_DIM))
    sdec = np.stack([np.block([[cdec[2 * p] * ho, hz], [hz, cdec[2 * p + 1] * ho]])
                     for p in range(N_PAIRS)])
    return _const(qdec), _const(kdec), _const(dmask), _const(sdec), cdec


def _rope_tables(pos):
    half = HEAD_DIM // 2
    inv_freq = ROPE_BASE ** (-np.arange(half, dtype=np.float64) / half)
    ang = np.asarray(pos, np.float64)[:, None] * inv_freq[None, :]
    cos = np.cos(ang)
    sin = np.sin(ang)
    cos_t = np.tile(np.concatenate([cos, cos], axis=1), (1, N_HEADS))
    sin_t = np.tile(np.concatenate([-sin, sin], axis=1), (1, N_HEADS))
    return _const(cos_t), _const(sin_t)


def _block_masks(chunk, group):
    i = np.arange(2 * chunk)
    same = (i[:, None] // chunk) == (i[None, :] // chunk)
    strict = same & (i[:, None] > i[None, :])
    incl = same & (i[:, None] >= i[None, :])
    j = np.arange(PAIR_W)
    bd = (j[:, None] // HEAD_DIM) == (j[None, :] // HEAD_DIM)
    t = np.arange(chunk)
    tril = np.kron(np.eye(group), t[:, None] >= t[None, :])
    o = np.arange(MXU_TILE)
    ones_blk = (o[:, None] // HEAD_DIM) == (o[None, :] // HEAD_DIM)
    return _const(strict), _const(incl), _const(bd), _const(tril, BF16), _const(ones_blk, BF16)


def _lora_block(w_b, a_b):
    z = jnp.zeros_like(w_b)
    return jnp.concatenate([jnp.concatenate([w_b, z], axis=1),
                            jnp.concatenate([z, a_b], axis=1)], axis=0).astype(BF16)


def _row(v):
    return v.reshape(1, -1).astype(F32)


def _mixer_prompt(x, lw):
    n_seq, t_len, _ = x.shape
    C = MIX_CHUNK
    n_chunks = t_len // C
    last = n_chunks - 1
    cos_t, sin_t = _rope_tables(np.arange(t_len))
    qdec, kdec, dmask, sdec, _ = _retention_tables(C)
    strict, incl, bd, tril, ones_blk = _block_masks(C, MIX_GROUP)
    prev_chunk = lambda i: (0, jnp.maximum(i - 1, 0), 0)
    in_specs = [
        pl.BlockSpec((n_seq, C, D_MODEL), prev_chunk),
        pl.BlockSpec((n_seq, C, D_MODEL), lambda i: (0, jnp.minimum(i + 1, last), 0)),
        _const_spec((1, D_MODEL)),
        _const_spec((D_MODEL, IN_W)),
        pl.BlockSpec((C, HEADS_W), lambda i: (jnp.minimum(i, last), 0)),
        pl.BlockSpec((C, HEADS_W), lambda i: (jnp.minimum(i, last), 0)),
    ]
    tail = [qdec, kdec, dmask, sdec, lw["ret_gn"], lw["mu"], lw["w0"], lw["lora"], lw["a0"], lw["g_b"],
            lw["k_k"], lw["k_a"], lw["r_k"], lw["ln_w"], lw["ln_b"], lw["w_out"], ones_blk, tril,
            strict, incl, bd]
    in_specs += [_const_spec(a.shape) for a in tail]
    out_shape = (
        jax.ShapeDtypeStruct((n_seq, t_len, D_MODEL), F32),
        jax.ShapeDtypeStruct((n_seq, N_HEADS, HEAD_DIM, HEAD_DIM), F32),
        jax.ShapeDtypeStruct((n_seq, N_HEADS, HEAD_DIM, HEAD_DIM), F32),
        jax.ShapeDtypeStruct((n_seq, SHIFT_W), F32),
    )
    out_specs = (
        pl.BlockSpec((n_seq, C, D_MODEL), prev_chunk),
        _const_out((n_seq, N_HEADS, HEAD_DIM, HEAD_DIM)),
        _const_out((n_seq, N_HEADS, HEAD_DIM, HEAD_DIM)),
        _const_out((n_seq, SHIFT_W)),
    )
    scratch = [
        pltpu.VMEM((2, n_seq * C, IN_W), F32),
        pltpu.VMEM((n_seq * C, 2 * HEADS_W), BF16),
        pltpu.VMEM((N_PAIRS, n_seq, PAIR_W, PAIR_W), F32),
        pltpu.VMEM((N_PAIRS, n_seq, PAIR_W, PAIR_W), F32),
        pltpu.VMEM((n_seq, SHIFT_W), F32),
    ]
    return pl.pallas_call(
        functools.partial(_mixer_prompt_kernel, n_seq=n_seq, chunk=C, group=MIX_GROUP),
        grid=(n_chunks + 1,),
        in_specs=in_specs, out_specs=out_specs, out_shape=out_shape, scratch_shapes=scratch,
        compiler_params=pltpu.CompilerParams(dimension_semantics=("arbitrary",),
                                             vmem_limit_bytes=VMEM_LIMIT),
        name="mixer_prompt",
    )(x, x, lw["norm"], lw["w_in"], cos_t, sin_t, *tail)


def _mixer_sample_pre_kernel(
        x_ref, shift_ref, normw_ref, win_ref, cos_ref, sin_ref, qdec_ref, mu_ref, w0_ref, lora_ref,
        a0_ref, gb_ref, kk_ref, ka_ref, rk_ref, ones_ref, feat_ref, feat_t_ref, newshift_ref):
    x = x_ref[...]
    p = _dot(_rms(x, normw_ref[...]), win_ref[...])
    q = _rope(p[:, 0:HEADS_W], cos_ref[...], sin_ref[...])
    k = _rope(p[:, HEADS_W:2 * HEADS_W], cos_ref[...], sin_ref[...]) * (HEAD_DIM ** -0.5)
    wp = p[:, RET_COLS:IN_W]
    xs = wp + (shift_ref[...] - wp) * mu_ref[...]
    r, lw, k_mod, vw, a_vec, b_vec, gate, bonus = _wkv_features(
        xs, w0_ref[...], lora_ref[...], a0_ref[...], gb_ref[...], kk_ref[...], ka_ref[...],
        rk_ref[...], ones_ref[...])
    newshift_ref[...] = wp
    state_feats = [q, q * qdec_ref[...], k, p[:, 2 * HEADS_W:3 * HEADS_W], r, jnp.exp(lw), k_mod, vw,
                   a_vec, b_vec]
    for n, f in enumerate(state_feats):
        feat_t_ref[n * HEADS_W:(n + 1) * HEADS_W, :] = f.T
    for n, f in enumerate([p[:, 3 * HEADS_W:4 * HEADS_W], gate, bonus]):
        feat_ref[:, n * HEADS_W:(n + 1) * HEADS_W] = f


_F_Q, _F_QD, _F_K, _F_V, _F_R, _F_W, _F_KM, _F_VW, _F_A, _F_B = range(10)
_N_STATE_FEATS = 10
_F_G, _F_GATE, _F_BONUS = range(3)
_N_ROW_FEATS = 3


def _mixer_sample_state_kernel(feat_t_ref, sdec_ref, ret_ref, wkv_ref, o_t_ref, reto_ref, wkvo_ref):
    h = pl.program_id(0)

    def head_rows(n):
        return feat_t_ref[pl.ds(pl.multiple_of(n * HEADS_W + h * HEAD_DIM, HEAD_DIM), HEAD_DIM), :]

    def head_row(n, i):
        return feat_t_ref[pl.ds(n * HEADS_W + h * HEAD_DIM + i, 1), :]

    a, w, b_vec, k_mod, r = (head_rows(n) for n in (_F_A, _F_W, _F_B, _F_KM, _F_R))

    def wkv_row(i, carry):
        s = wkv_ref[0, i]
        sa = jnp.sum(s * a, axis=0, keepdims=True)
        s_new = s * w + sa * b_vec + head_row(_F_VW, i) * k_mod
        wkvo_ref[0, i] = s_new
        o_t_ref[1, 0, pl.ds(i, 1), :] = jnp.sum(s_new * r, axis=0, keepdims=True)
        return carry

    lax.fori_loop(0, HEAD_DIM, wkv_row, 0, unroll=4)

    v, q, k = head_rows(_F_V), head_rows(_F_Q), head_rows(_F_K)
    g = sdec_ref[h]

    def ret_row(d, acc):
        s = ret_ref[0, d]
        reto_ref[0, d] = g * s + head_row(_F_K, d) * v
        return acc + head_row(_F_QD, d) * s

    cross = lax.fori_loop(0, HEAD_DIM, ret_row, jnp.zeros_like(v), unroll=4)
    o_t_ref[0, 0] = cross + jnp.sum(q * k, axis=0, keepdims=True) * v


def _mixer_sample_post_kernel(
        x_ref, feat_ref, o_t_ref, retgn_ref, lnw_ref, lnb_ref, wout_ref, ones_ref, xo_ref):
    ones_blk = ones_ref[...]
    g = feat_ref[:, _F_G * HEADS_W:(_F_G + 1) * HEADS_W]
    gate = feat_ref[:, _F_GATE * HEADS_W:(_F_GATE + 1) * HEADS_W]
    bonus = feat_ref[:, _F_BONUS * HEADS_W:(_F_BONUS + 1) * HEADS_W]
    o = o_t_ref[...].T
    ret = _head_norm(o[:, 0:HEADS_W], GN_EPS, ones_blk) * retgn_ref[...]
    ret_out = g * _sigmoid(g) * ret
    yn = _head_norm(o[:, HEADS_W:2 * HEADS_W], WKV_GN_EPS, ones_blk) * lnw_ref[...] + lnb_ref[...]
    wkv_out = (yn + bonus) * gate
    cat = jnp.concatenate([ret_out, wkv_out], axis=1)
    xo_ref[...] = x_ref[...] + _dot(cat, wout_ref[...])


def _mixer_sample(x, ret_t, wkv_t, shift0, lw):
    n = x.shape[0]
    cos_t, sin_t = _rope_tables(np.full((1,), PAST_LEN))
    qdec, _, _, _, cdec = _retention_tables(1)
    _, _, _, _, ones_blk = _block_masks(1, 1)
    args = [x, shift0, lw["norm"], lw["w_in"], cos_t, sin_t, qdec, lw["mu"], lw["w0"], lw["lora"],
            lw["a0"], lw["g_b"], lw["k_k"], lw["k_a"], lw["r_k"], ones_blk]
    feat, feat_t, new_shift = pl.pallas_call(
        _mixer_sample_pre_kernel,
        grid=(1,),
        in_specs=[_const_spec(a.shape) for a in args],
        out_specs=(_const_out((n, _N_ROW_FEATS * HEADS_W)), _const_out((_N_STATE_FEATS * HEADS_W, n)),
                   _const_out((n, SHIFT_W))),
        out_shape=(jax.ShapeDtypeStruct((n, _N_ROW_FEATS * HEADS_W), F32),
                   jax.ShapeDtypeStruct((_N_STATE_FEATS * HEADS_W, n), F32),
                   jax.ShapeDtypeStruct((n, SHIFT_W), F32)),
        compiler_params=pltpu.CompilerParams(vmem_limit_bytes=VMEM_LIMIT),
        name="mixer_sample_pre",
    )(*args)

    sdec = _const(np.broadcast_to(cdec[:, None, None], (N_HEADS, 1, n)))
    state_spec = pl.BlockSpec((1, HEAD_DIM, HEAD_DIM, n), lambda h: (h, 0, 0, 0))
    o_spec = pl.BlockSpec((2, 1, HEAD_DIM, n), lambda h: (0, h, 0, 0))
    o_t, ret_new, wkv_new = pl.pallas_call(
        _mixer_sample_state_kernel,
        grid=(N_HEADS,),
        in_specs=[_const_spec(feat_t.shape), _const_spec(sdec.shape), state_spec, state_spec],
        out_specs=(o_spec, state_spec, state_spec),
        out_shape=(jax.ShapeDtypeStruct((2, N_HEADS, HEAD_DIM, n), F32),
                   jax.ShapeDtypeStruct(ret_t.shape, F32), jax.ShapeDtypeStruct(wkv_t.shape, F32)),
        compiler_params=pltpu.CompilerParams(dimension_semantics=("arbitrary",),
                                             vmem_limit_bytes=VMEM_LIMIT),
        name="mixer_sample_state",
    )(feat_t, sdec, ret_t, wkv_t)

    args = [x, feat, o_t.reshape(2 * HEADS_W, n), lw["ret_gn"], lw["ln_w"], lw["ln_b"], lw["w_out"], ones_blk]
    x1 = pl.pallas_call(
        _mixer_sample_post_kernel,
        grid=(1,),
        in_specs=[_const_spec(a.shape) for a in args],
        out_specs=_const_out((n, D_MODEL)),
        out_shape=jax.ShapeDtypeStruct((n, D_MODEL), F32),
        compiler_params=pltpu.CompilerParams(vmem_limit_bytes=VMEM_LIMIT),
        name="mixer_sample_post",
    )(*args)
    return x1, ret_new, wkv_new, new_shift


def _mlp_kernel(xa_ref, xb_ref, normw_ref, wup_ref, wdown_ref, normf_ref, oa_ref, ob_ref,
                *, final_norm, steps_a):
    i = pl.program_id(0)

    def run(x_ref, o_ref):
        x = x_ref[...]
        hn = _rms(x, normw_ref[...]).astype(BF16)
        acc = x
        for c in range(D_FF // FF_CHUNK):
            sl = slice(c * FF_CHUNK, (c + 1) * FF_CHUNK)
            hid = jnp.dot(hn, wup_ref[:, sl].astype(BF16), preferred_element_type=F32)
            hid = jnp.square(jnp.maximum(hid, 0.0)).astype(BF16)
            acc = acc + jnp.dot(hid, wdown_ref[sl, :].astype(BF16), preferred_element_type=F32)
        if final_norm:
            acc = _rms(acc, normf_ref[...])
        o_ref[...] = acc

    @pl.when(i < steps_a)
    def _():
        run(xa_ref, oa_ref)

    @pl.when(i == steps_a)
    def _():
        run(xb_ref, ob_ref)


def _mlp(xa, xb, norm_w, w_up, w_down, norm_f, layer, final_norm):
    rows_a, rows_b = xa.shape[0], xb.shape[0]
    steps_a = rows_a // MLP_ROWS
    pick = lambda *_: (layer, 0, 0)
    block_a = pl.BlockSpec((MLP_ROWS, D_MODEL), lambda i: (jnp.minimum(i, steps_a - 1), 0))
    return pl.pallas_call(
        functools.partial(_mlp_kernel, final_norm=final_norm, steps_a=steps_a),
        grid=(steps_a + 1,),
        in_specs=[block_a, _const_spec((rows_b, D_MODEL)),
                  _const_spec((1, D_MODEL)),
                  pl.BlockSpec((None, D_MODEL, D_FF), pick, pipeline_mode=pl.Buffered(1)),
                  pl.BlockSpec((None, D_FF, D_MODEL), pick, pipeline_mode=pl.Buffered(1)),
                  _const_spec((1, D_MODEL))],
        out_specs=(block_a, _const_out((rows_b, D_MODEL))),
        out_shape=(jax.ShapeDtypeStruct((rows_a, D_MODEL), F32), jax.ShapeDtypeStruct((rows_b, D_MODEL), F32)),
        compiler_params=pltpu.CompilerParams(dimension_semantics=("arbitrary",),
                                             vmem_limit_bytes=VMEM_LIMIT),
        name="mlp",
    )(xa, xb, norm_w, w_up, w_down, norm_f)


def _ssm_prep_kernel(lre_ref, lim_ref, logdt_ref, lre_rep_ref, lim_rep_ref, logdt_rep_ref, bre_ref, bim_ref,
                     cre_in_ref, cim_in_ref, tile_ref, eye_ref, are_ref, aim_ref, wb_ref, cre_ref, cim_ref):
    def discretise(lre, lim, logdt):
        lre = jnp.minimum(lre, -1e-4)
        dt = jnp.exp(logdt)
        mag = jnp.exp(lre * dt)
        return lre, mag * jnp.cos(lim * dt), mag * jnp.sin(lim * dt)

    _, are_ref[...], aim_ref[...] = discretise(lre_ref[...], lim_ref[...], logdt_ref[...])
    lim = lim_rep_ref[...]
    lre, are, aim = discretise(lre_rep_ref[...], lim, logdt_rep_ref[...])
    den = lre * lre + lim * lim
    nre = are - 1.0
    cre = (nre * lre + aim * lim) / den
    cim = (aim * lre - nre * lim) / den
    bre = bre_ref[...]
    bim = bim_ref[...]
    bb = (cre * bre - cim * bim, cre * bim + cim * bre)

    in_mask = (lax.broadcasted_iota(jnp.int32, (LANES, 1), 0) // SSM_GROUP
               == lax.broadcasted_iota(jnp.int32, (1, SSM_HALF), 1) // SSM_P)
    out_group = lax.broadcasted_iota(jnp.int32, (1, LANES), 1) // SSM_GROUP
    for blk in range(SSM_BLOCKS):
        rows = slice(blk * LANES, (blk + 1) * LANES)
        for part in range(2):
            tiled = jnp.dot(bb[part][rows].astype(BF16), tile_ref[...], preferred_element_type=F32)
            wb_ref[blk, :, part * SSM_HALF:(part + 1) * SSM_HALF] = jnp.where(in_mask, tiled, 0.0).astype(BF16)
        for src, dst in ((cre_in_ref, cre_ref), (cim_in_ref, cim_ref)):
            c_t = _dot_tn(src[rows, :], eye_ref[...])
            dst[blk] = jnp.concatenate([jnp.where(out_group == gl, c_t, 0.0) for gl in range(SSM_BLOCK_G)],
                                       axis=0).astype(BF16)


def _gelu_exact(x):
    return 0.5 * x * (1.0 + lax.erf(x * (2.0 ** -0.5)))


def _ssm_kernel(x_ref, hre0_ref, him0_ref, normw_ref, are_ref, aim_ref, wb_ref, cre_ref, cim_ref, dskip_ref,
                wglu_ref, xo_ref, hre_ref, him_ref, xt_s, u_s, bu_s, y_s, hg_s, hre_s, him_s,
                *, n_seq, chunk, batch_major):
    i = pl.program_id(0)
    n_steps = pl.num_programs(0)
    rows_all = chunk * n_seq
    n_slabs = D_MODEL // LANES

    @pl.when(i == 0)
    def _():
        hre_s[...] = hre0_ref[...] if batch_major else hre0_ref[...].T
        him_s[...] = him0_ref[...] if batch_major else him0_ref[...].T

    if batch_major:
        for b in range(n_seq):
            for sl in range(n_slabs):
                xt_s[sl, pl.ds(b, chunk, stride=n_seq), :] = x_ref[b, :, sl * LANES:(sl + 1) * LANES]
        ssq = sum(jnp.sum(jnp.square(xt_s[sl]), axis=-1, keepdims=True) for sl in range(n_slabs))
        inv = lax.rsqrt(ssq * (1.0 / D_MODEL) + RMS_EPS)
        for sl in range(n_slabs):
            cols = slice(sl * LANES, (sl + 1) * LANES)
            u_s[:, cols] = xt_s[sl] * inv * normw_ref[:, cols]
    else:
        u_s[...] = _rms(x_ref[...].reshape(rows_all, D_MODEL), normw_ref[...])

    def input_proj(blk, part):
        c = slice(part * SSM_HALF, (part + 1) * SSM_HALF)
        bu_s[blk % 2, :, c] = _dot(u_s[:, blk * LANES:(blk + 1) * LANES], wb_ref[blk, :, c])

    def output_proj(blk, part):
        buf = bu_s.at[blk % 2]
        cols = slice(blk * LANES, (blk + 1) * LANES)
        if part == 0:
            y_s[:, cols] = _dot(buf[:, 0:SSM_HALF], cre_ref[blk])
        else:
            y_s[:, cols] = y_s[:, cols] - _dot(buf[:, SSM_HALF:2 * SSM_HALF], cim_ref[blk])

    input_proj(0, 0)
    input_proj(0, 1)
    for blk in range(SSM_BLOCKS):
        neighbours = []
        if blk >= 1:
            neighbours += [functools.partial(output_proj, blk - 1, 0), functools.partial(output_proj, blk - 1, 1)]
        if blk + 1 < SSM_BLOCKS:
            neighbours += [functools.partial(input_proj, blk + 1, 0), functools.partial(input_proj, blk + 1, 1)]
        buf = bu_s.at[blk % 2]
        cols = slice(blk * SSM_HALF, (blk + 1) * SSM_HALF)
        a_re = jnp.broadcast_to(are_ref[:, cols], (n_seq, SSM_HALF))
        a_im = jnp.broadcast_to(aim_ref[:, cols], (n_seq, SSM_HALF))
        h_re = hre_s[:, cols]
        h_im = him_s[:, cols]
        seg = -(-chunk // max(len(neighbours), 1))
        for t in range(chunk):
            if t % seg == 0 and neighbours:
                neighbours.pop(0)()
            rows = slice(t * n_seq, (t + 1) * n_seq)
            n_re = a_re * h_re - a_im * h_im + buf[rows, 0:SSM_HALF]
            n_im = a_re * h_im + a_im * h_re + buf[rows, SSM_HALF:2 * SSM_HALF]
            buf[rows, 0:SSM_HALF] = n_re
            buf[rows, SSM_HALF:2 * SSM_HALF] = n_im
            h_re, h_im = n_re, n_im
        for f in neighbours:
            f()
        hre_s[:, cols] = h_re
        him_s[:, cols] = h_im
    output_proj(SSM_BLOCKS - 1, 0)
    output_proj(SSM_BLOCKS - 1, 1)
    hg_s[...] = _gelu_exact(y_s[...] + dskip_ref[...] * u_s[...]).astype(BF16)
    glu_w = MXU_TILE
    for c in range(D_MODEL // glu_w):
        cols = slice(c * glu_w, (c + 1) * glu_w)
        hg = hg_s[...]
        val = jnp.dot(hg, wglu_ref[:, cols], preferred_element_type=F32)
        gate = jnp.dot(hg, wglu_ref[:, D_MODEL + c * glu_w:D_MODEL + (c + 1) * glu_w],
                       preferred_element_type=F32)
        out = val * _sigmoid(gate)
        if batch_major:
            for sl in range(c * glu_w // LANES, (c + 1) * glu_w // LANES):
                xt_s[sl] = xt_s[sl] + out[:, sl * LANES - c * glu_w:(sl + 1) * LANES - c * glu_w]
        else:
            y_s[:, cols] = x_ref[...].reshape(rows_all, D_MODEL)[:, cols] + out
    if batch_major:
        for b in range(n_seq):
            for sl in range(n_slabs):
                xo_ref[b, :, sl * LANES:(sl + 1) * LANES] = xt_s[sl, pl.ds(b, chunk, stride=n_seq), :]
    else:
        xo_ref[...] = y_s[...].reshape(chunk, n_seq, D_MODEL)

    @pl.when(i == n_steps - 1)
    def _():
        hre_ref[...] = hre_s[...] if batch_major else hre_s[...].T
        him_ref[...] = him_s[...] if batch_major else him_s[...].T


def _ssm_weights(lam_re, lam_im, log_dt, b_re, b_im, c_re, c_im):
    g, p = SSM_GROUPS, SSM_P
    n = g * SSM_GROUP
    rep = lambda z: jnp.repeat(z, SSM_GROUP, axis=0)
    log_dt = log_dt.reshape(g, 1)
    bt_re = jnp.swapaxes(b_re, 1, 2).reshape(n, p)
    bt_im = jnp.swapaxes(b_im, 1, 2).reshape(n, p)
    tile = _const(np.tile(np.eye(p), (1, SSM_BLOCK_G)), BF16)
    eye = _const(np.eye(LANES), BF16)
    args = [lam_re, lam_im, log_dt, rep(lam_re), rep(lam_im), rep(log_dt), bt_re, bt_im,
            c_re.reshape(n, p), c_im.reshape(n, p), tile, eye]
    out_shapes = ((g, p), (g, p), (SSM_BLOCKS, LANES, 2 * SSM_HALF), (SSM_BLOCKS, SSM_HALF, LANES),
                  (SSM_BLOCKS, SSM_HALF, LANES))
    out_dtypes = (F32, F32, BF16, BF16, BF16)
    a_re, a_im, w_b, c_re_bd, c_im_bd = pl.pallas_call(
        _ssm_prep_kernel,
        grid=(1,),
        in_specs=[_const_spec(a.shape) for a in args],
        out_specs=tuple(_const_out(s) for s in out_shapes),
        out_shape=tuple(jax.ShapeDtypeStruct(s, d) for s, d in zip(out_shapes, out_dtypes)),
        name="ssm_prep",
    )(*args)
    n_state = g * p
    return a_re.reshape(1, n_state), a_im.reshape(1, n_state), w_b, c_re_bd, c_im_bd


def _ssm_layer(x, h_re0, h_im0, sw, chunk, batch_major):
    if batch_major:
        n_seq, t_len, _ = x.shape
        x_block = (n_seq, chunk, D_MODEL)
        x_map = lambda i: (0, i, 0)
    else:
        t_len, n_seq, _ = x.shape
        x_block = (chunk, n_seq, D_MODEL)
        x_map = lambda i: (i, 0, 0)
    rows = chunk * n_seq
    n_state = SSM_GROUPS * SSM_P
    args = [x, h_re0, h_im0, sw["norm"], sw["a_re"], sw["a_im"], sw["w_b"], sw["c_re"], sw["c_im"],
            sw["d_skip"], sw["w_glu"]]
    in_specs = [pl.BlockSpec(x_block, x_map)] + [_const_spec(a.shape) for a in args[1:]]
    state = jax.ShapeDtypeStruct((n_seq, n_state) if batch_major else (n_state, n_seq), F32)
    return pl.pallas_call(
        functools.partial(_ssm_kernel, n_seq=n_seq, chunk=chunk, batch_major=batch_major),
        grid=(t_len // chunk,),
        in_specs=in_specs,
        out_specs=(pl.BlockSpec(x_block, x_map), _const_out(state.shape), _const_out(state.shape)),
        out_shape=(jax.ShapeDtypeStruct(x.shape, F32), state, state),
        scratch_shapes=[pltpu.VMEM((D_MODEL // LANES, rows, LANES), F32),
                        pltpu.VMEM((rows, D_MODEL), F32), pltpu.VMEM((2, rows, 2 * SSM_HALF), F32),
                        pltpu.VMEM((rows, D_MODEL), F32), pltpu.VMEM((rows, D_MODEL), BF16),
                        pltpu.VMEM((n_seq, n_state), F32), pltpu.VMEM((n_seq, n_state), F32)],
        compiler_params=pltpu.CompilerParams(dimension_semantics=("arbitrary",),
                                             vmem_limit_bytes=VMEM_LIMIT),
        name="ssm_layer",
    )(*args)


def kernel(x_prompt, x_sample, state_ret, state_wkv, state_shift, state_ssm_re, state_ssm_im, norm_mix, w_in, ret_gn, mu_shift, wkv_w0, wkv_wB, wkv_a0, wkv_aB, wkv_gB, wkv_kk, wkv_ka, wkv_rk, wkv_ln_w, wkv_ln_b, w_out, ssm_lambda_re, ssm_lambda_im, ssm_log_dt, ssm_B_re, ssm_B_im, ssm_C_re, ssm_C_im, ssm_D, ssm_w_glu, mlp_norm, mlp_up, mlp_down, norm_f):
    lw = dict(
        norm=_row(norm_mix[0]), w_in=w_in[0].astype(BF16), ret_gn=_row(ret_gn[0]), mu=_row(mu_shift[0]),
        w0=_row(wkv_w0[0]), lora=_lora_block(wkv_wB[0], wkv_aB[0]), a0=_row(wkv_a0[0]),
        g_b=wkv_gB[0].astype(BF16), k_k=_row(wkv_kk[0]), k_a=_row(wkv_ka[0]), r_k=_row(wkv_rk[0]),
        ln_w=_row(wkv_ln_w[0]), ln_b=_row(wkv_ln_b[0]), w_out=w_out[0].astype(BF16))
    a_re, a_im, w_b, c_re, c_im = _ssm_weights(ssm_lambda_re[0], ssm_lambda_im[0], ssm_log_dt[0],
                                               ssm_B_re[0], ssm_B_im[0], ssm_C_re[0], ssm_C_im[0])
    sw = dict(norm=_row(norm_mix[1]), a_re=a_re, a_im=a_im, w_b=w_b, c_re=c_re, c_im=c_im,
              d_skip=_row(ssm_D[0]), w_glu=ssm_w_glu[0].astype(BF16))
    n_state = SSM_GROUPS * SSM_P
    nf = _row(norm_f)

    n_p, t_p, _ = x_prompt.shape
    n_s = x_sample.shape[0]
    x1, ret_p, wkv_p, shift_p = _mixer_prompt(x_prompt, lw)
    seq_last = lambda s: jnp.transpose(s, (1, 2, 3, 0))
    seq_first = lambda s: jnp.transpose(s, (3, 0, 1, 2))
    xs1, ret_s, wkv_s, shift_s = _mixer_sample(x_sample.reshape(n_s, D_MODEL), seq_last(state_ret[0]),
                                               seq_last(state_wkv[0]), state_shift[0], lw)
    ret_s, wkv_s = seq_first(ret_s), seq_first(wkv_s)
    x1, xs1 = _mlp(x1.reshape(n_p * t_p, D_MODEL), xs1, _row(mlp_norm[0]), mlp_up, mlp_down, nf, 0, False)
    zero_state = jnp.zeros((n_p, n_state), F32)
    x2, ssm_re_p, ssm_im_p = _ssm_layer(x1.reshape(n_p, t_p, D_MODEL), zero_state, zero_state, sw,
                                        SSM_CHUNK, True)
    ssm_seq_last = lambda s: jnp.transpose(s, (1, 2, 0)).reshape(n_state, n_s)
    xs2, ssm_re_s, ssm_im_s = _ssm_layer(
        xs1.reshape(1, n_s, D_MODEL), ssm_seq_last(state_ssm_re[0]), ssm_seq_last(state_ssm_im[0]),
        sw, 1, False)
    y_p, y_s = _mlp(x2.reshape(n_p * t_p, D_MODEL), xs2.reshape(n_s, D_MODEL), _row(mlp_norm[1]),
                    mlp_up, mlp_down, nf, 1, True)
    y_prompt = y_p.reshape(n_p, t_p, D_MODEL)
    ssm_re_p = ssm_re_p.reshape(n_p, SSM_GROUPS, SSM_P)
    ssm_im_p = ssm_im_p.reshape(n_p, SSM_GROUPS, SSM_P)
    ssm_seq_first = lambda s: jnp.transpose(s.reshape(SSM_GROUPS, SSM_P, n_s), (2, 0, 1))
    ssm_re_s, ssm_im_s = ssm_seq_first(ssm_re_s), ssm_seq_first(ssm_im_s)

    return (y_prompt, y_s.reshape(n_s, 1, D_MODEL),
            ret_p[None], wkv_p[None], shift_p[None], ssm_re_p[None], ssm_im_p[None],
            ret_s[None], wkv_s[None], shift_s[None], ssm_re_s[None], ssm_im_s[None])
```

```python
import functools
import math

import numpy as np
import jax
import jax.numpy as jnp
from jax import lax
from jax.experimental import pallas as pl
from jax.experimental.pallas import tpu as pltpu

F32 = jnp.float32
BF16 = jnp.bfloat16

LANES = 128
MXU_TILE = 256
VMEM_BYTES = 64 * 1024 * 1024

D_MODEL = 1024
N_HEADS = 8
HEAD_DIM = 64
HEADS_W = N_HEADS * HEAD_DIM
N_PAIRS = N_HEADS // 2
PAIR_W = 2 * HEAD_DIM
assert PAIR_W == LANES
ROPE_BASE = 10000.0
DECAY_LORA = 64
AAA_LORA = 64
GATE_LORA = 128
SHIFT_W = 3 * HEADS_W + DECAY_LORA + AAA_LORA + GATE_LORA
RET_COLS = 4 * HEADS_W
IN_W = RET_COLS + SHIFT_W
SSM_GROUP = 16
SSM_GROUPS = D_MODEL // SSM_GROUP
SSM_P = 64
SSM_BLOCK_G = LANES // SSM_GROUP
SSM_BLOCKS = SSM_GROUPS // SSM_BLOCK_G
SSM_HALF = SSM_BLOCK_G * SSM_P
D_FF = 4 * D_MODEL
RMS_EPS = 1e-6
GN_EPS = 1e-5
WKV_GN_EPS = 64e-5
PAST_LEN = 16384

MIX_CHUNK = 64
PROJ_PIECE = MXU_TILE
MIX_GROUP = 4
SSM_CHUNK = 128
MLP_ROWS = 1024
FF_CHUNK = 1024

VMEM_LIMIT = VMEM_BYTES - 6 * 1024 * 1024


def _dot(a, b):
    return jnp.dot(a.astype(BF16), b.astype(BF16), preferred_element_type=F32)


def _dot_nt(a, b):
    return lax.dot_general(a.astype(BF16), b.astype(BF16), (((1,), (1,)), ((), ())),
                           preferred_element_type=F32)


def _dot_tn(a, b):
    return lax.dot_general(a.astype(BF16), b.astype(BF16), (((0,), (0,)), ((), ())),
                           preferred_element_type=F32)


def _split3(x):
    hi = x.astype(BF16)
    r1 = x - hi.astype(F32)
    mid = r1.astype(BF16)
    lo = (r1 - mid.astype(F32)).astype(BF16)
    return hi, mid, lo


def _dot_exact_lhs(a_bf16, x):
    hi, mid, lo = _split3(x)
    f = lambda p: jnp.dot(a_bf16, p, preferred_element_type=F32)
    return f(hi) + f(mid) + f(lo)


def _segsum(x, ones_blk):
    xb = x.astype(BF16)
    outs = [jnp.dot(xb[:, c * MXU_TILE:(c + 1) * MXU_TILE], ones_blk, preferred_element_type=F32)
            for c in range(x.shape[1] // MXU_TILE)]
    return jnp.concatenate(outs, axis=1)


def _rms(x, w):
    return x * lax.rsqrt(jnp.mean(x * x, axis=-1, keepdims=True) + RMS_EPS) * w


def _sigmoid(x):
    return 1.0 / (1.0 + jnp.exp(-x))


def _softplus(x):
    return jnp.maximum(x, 0.0) + jnp.log1p(jnp.exp(-jnp.abs(x)))


def _head_norm(z, eps, ones_blk):
    mu = _segsum(z, ones_blk) * (1.0 / HEAD_DIM)
    zc = z - mu
    var = _segsum(zc * zc, ones_blk) * (1.0 / HEAD_DIM)
    return zc * lax.rsqrt(var + eps)


def _rope(z, cos, sin_signed):
    lane = lax.broadcasted_iota(jnp.int32, (1, HEADS_W), 1) % HEAD_DIM
    swapped = jnp.where(lane < HEAD_DIM // 2,
                        pltpu.roll(z, HEADS_W - HEAD_DIM // 2, axis=1),
                        pltpu.roll(z, HEAD_DIM // 2, axis=1))
    return z * cos + swapped * sin_signed


def _wkv_features(xs, w0, lora_w, a0, g_b, k_k, k_a, r_k, ones_blk, between=lambda: None):
    r = xs[:, 0:HEADS_W]
    kw = xs[:, HEADS_W:2 * HEADS_W]
    vw = xs[:, 2 * HEADS_W:3 * HEADS_W]
    lora_w_in = DECAY_LORA + AAA_LORA
    lo = xs[:, 3 * HEADS_W:3 * HEADS_W + lora_w_in]
    lane = lax.broadcasted_iota(jnp.int32, (1, lora_w_in), 1)
    lo = jnp.where(lane < DECAY_LORA, jnp.tanh(lo), lo)
    ll = _dot(lo, lora_w)
    w_log = -_softplus(-(w0 + ll[:, 0:HEADS_W])) - 0.5
    log_decay = -jnp.exp(w_log)
    between()
    alr = _sigmoid(a0 + ll[:, HEADS_W:2 * HEADS_W])
    gate = _dot(_sigmoid(xs[:, 3 * HEADS_W + lora_w_in:SHIFT_W]), g_b)
    between()
    kk = kw * k_k
    kk = kk / jnp.maximum(jnp.sqrt(_segsum(kk * kk, ones_blk)), 1e-12)
    k_mod = kw * (1.0 + (alr - 1.0) * k_a)
    between()
    bonus = _segsum(r * k_mod * r_k, ones_blk) * vw
    return r, log_decay, k_mod, vw, -kk, kk * alr, gate, bonus


def _stack_masked(x2, m0):
    return jnp.concatenate([jnp.where(m0, x2, 0.0), jnp.where(m0, 0.0, x2)], axis=0)


def _stack_dup(x2):
    return jnp.concatenate([x2, x2], axis=0)


def _mixer_prompt_kernel(
        xprev_ref, xnext_ref, normw_ref, win_ref, cos_ref, sin_ref, qdec_ref, kdec_ref, dmask_ref, sdec_ref,
        retgn_ref, mu_ref, w0_ref, lora_ref, a0_ref, gb_ref, kk_ref, ka_ref, rk_ref, lnw_ref,
        lnb_ref, wout_ref, ones_ref, tril_ref, strict_ref, incl_ref, bd_ref,
        xo_ref, rets_ref, wkvs_ref, shift_ref,
        p_s, cat_s, rs_s, ws_s, carry_s, *, n_seq, chunk, group):
    i = pl.program_id(0)
    n_chunks = pl.num_programs(0) - 1
    C = chunk
    slot = i % 2

    @pl.when(i == 0)
    def _():
        rs_s[...] = jnp.zeros_like(rs_s)
        ws_s[...] = jnp.zeros_like(ws_s)
        carry_s[...] = jnp.zeros_like(carry_s)
        cat_s[...] = jnp.zeros_like(cat_s)
        p_s[0] = _dot(_rms(xprev_ref[...].reshape(n_seq * C, D_MODEL), normw_ref[...]), win_ref[...])

    m0 = lax.broadcasted_iota(jnp.int32, (1, PAIR_W), 1) < HEAD_DIM
    ones_blk = ones_ref[...]
    NB = group
    R = NB * C
    row_id = lax.broadcasted_iota(jnp.int32, (R, 1), 0)
    tile_rows = lambda ref: jnp.concatenate([ref[...]] * NB, axis=0)
    pairs = range(N_PAIRS)
    sls = [slice(pr * PAIR_W, (pr + 1) * PAIR_W) for pr in pairs]
    chains = [(s, pr) for s in range(NB) for pr in pairs]
    seq_rows = [slice(s * C, (s + 1) * C) for s in range(NB)]

    def per_group(gi, carry):
        rows = pl.ds(pl.multiple_of(gi * R, R), R)
        b0 = gi * NB
        ret_states = {(s, pr): rs_s[pr, b0 + s] for s, pr in chains}
        wkv_states = {(s, pr): ws_s[pr, b0 + s] for s, pr in chains}
        shift_rows = [carry_s[pl.ds(b0 + s, 1), :] for s in range(NB)]
        p_cur = p_s.at[slot]
        wp = p_cur[rows, RET_COLS:IN_W]
        hn_next = _rms(xnext_ref[pl.ds(b0, NB)].reshape(R, D_MODEL), normw_ref[...]).astype(BF16)
        cat_prev = cat_s[rows, :]

        def next_in_proj(c0):
            p_s[1 - slot, rows, c0:c0 + PROJ_PIECE] = jnp.dot(
                hn_next, win_ref[:, c0:c0 + PROJ_PIECE], preferred_element_type=F32)

        def prev_out_proj(c0):
            cols = slice(c0, c0 + PROJ_PIECE)
            out = jnp.dot(cat_prev, wout_ref[:, cols], preferred_element_type=F32)
            xo_ref[pl.ds(b0, NB), :, cols] = xprev_ref[pl.ds(b0, NB), :, cols] + out.reshape(NB, C, PROJ_PIECE)

        pieces = [functools.partial(next_in_proj, c0) for c0 in range(0, IN_W, PROJ_PIECE)]
        pieces += [functools.partial(prev_out_proj, c0) for c0 in range(0, D_MODEL, PROJ_PIECE)]

        def fill(n=1):
            for _ in range(min(n, len(pieces))):
                pieces.pop(0)()

        cos, sin = tile_rows(cos_ref), tile_rows(sin_ref)
        q = _rope(p_cur[rows, 0:HEADS_W], cos, sin)
        fill()
        k = _rope(p_cur[rows, HEADS_W:2 * HEADS_W], cos, sin) * (HEAD_DIM ** -0.5)
        fill()
        v = p_cur[rows, 2 * HEADS_W:3 * HEADS_W]
        g = p_cur[rows, 3 * HEADS_W:4 * HEADS_W]
        qd = q * tile_rows(qdec_ref)
        kd = k * tile_rows(kdec_ref)
        fill()
        cut = lambda z, s, pr: z[seq_rows[s], sls[pr]]
        r_sc = {c: _dot_nt(_stack_masked(cut(q, *c), m0), _stack_dup(cut(k, *c))) * dmask_ref[c[1]]
                for c in chains}
        r_inner = {c: _dot(r_sc[c], _stack_dup(cut(v, *c))) for c in chains}
        r_cross = {c: _dot(cut(qd, *c), ret_states[c]) for c in chains}
        new_ret = {c: sdec_ref[c[1]] * ret_states[c] + _dot_tn(cut(kd, *c), cut(v, *c)) * bd_ref[...]
                   for c in chains}
        o = jnp.concatenate(
            [jnp.concatenate([jnp.where(m0, r_inner[(s, pr)][0:C], r_inner[(s, pr)][C:2 * C])
                              + r_cross[(s, pr)] for pr in pairs], axis=1) for s in range(NB)], axis=0)
        fill()
        ret = _head_norm(o, GN_EPS, ones_blk) * retgn_ref[...]
        ret_out = (g * _sigmoid(g) * ret).astype(BF16)
        fill()

        prev = pltpu.roll(wp, 1, axis=0)
        for s in range(NB):
            prev = jnp.where(row_id == s * C, shift_rows[s], prev)
        xs = wp + (prev - wp) * mu_ref[...]
        fill()
        r, lw, k_mod, vw, a_vec, b_vec, gate, bonus = _wkv_features(
            xs, w0_ref[...], lora_ref[...], a0_ref[...], gb_ref[...], kk_ref[...], ka_ref[...],
            rk_ref[...], ones_blk, fill)
        cw = _dot_exact_lhs(tril_ref[...], lw)
        fill()
        cw_last = [cw[s * C + C - 1:(s + 1) * C, :] for s in range(NB)]
        cwl = jnp.concatenate([jnp.broadcast_to(z, (C, HEADS_W)) for z in cw_last], axis=0)
        r_t = r * jnp.exp(cw)
        a_t = a_vec * jnp.exp(cw - lw)
        fill()
        w_inv = jnp.exp(-cw)
        b_t = b_vec * w_inv
        k_t = k_mod * w_inv
        fill()
        w_end = jnp.exp(cwl - cw)
        b_h = b_vec * w_end
        k_h = k_mod * w_end
        fill()
        w_all = [jnp.exp(z) for z in cw_last]
        lhs = {c: jnp.concatenate([_stack_masked(cut(a_t, *c), m0), _stack_masked(cut(r_t, *c), m0)], axis=0)
               for c in chains}
        fill(len(pieces) - 1)
        sc = {c: _dot_nt(lhs[c], jnp.concatenate([cut(b_t, *c), cut(k_t, *c)], axis=0)) for c in chains}
        sc_swapped = {c: pltpu.roll(sc[c], C, axis=1) for c in chains}
        sc_b = {c: jnp.where(m0, sc[c], sc_swapped[c]) for c in chains}
        sc_k = {c: jnp.where(m0, sc_swapped[c], sc[c]) for c in chains}
        on_state = {c: _dot_nt(lhs[c], wkv_states[c]) for c in chains}
        vv = {c: _stack_dup(cut(vw, *c)) for c in chains}
        n_pow = {c: sc_b[c][0:2 * C] * strict_ref[...] for c in chains}
        u = {c: on_state[c][0:2 * C] + _dot(sc_k[c][0:2 * C] * strict_ref[...], vv[c]) for c in chains}
        n_steps_solve = int(math.log2(C))
        for it in range(n_steps_solve):
            u = {c: u[c] + _dot(n_pow[c], u[c]) for c in chains}
            if it + 1 < n_steps_solve:
                n_pow = {c: _dot(n_pow[c], n_pow[c]) for c in chains}
        uv = {c: jnp.concatenate([u[c], vv[c]], axis=0) for c in chains}
        y_st = {c: on_state[c][2 * C:4 * C] + _dot(
            jnp.concatenate([sc_b[c][2 * C:4 * C] * incl_ref[...],
                             sc_k[c][2 * C:4 * C] * incl_ref[...]], axis=1), uv[c]) for c in chains}
        new_wkv = {c: wkv_states[c] * w_all[c[0]][:, sls[c[1]]] + bd_ref[...] * _dot_tn(
            uv[c], jnp.concatenate([_stack_masked(cut(b_h, *c), m0), _stack_masked(cut(k_h, *c), m0)], axis=0))
            for c in chains}
        y = jnp.concatenate(
            [jnp.concatenate([jnp.where(m0, y_st[(s, pr)][0:C], y_st[(s, pr)][C:2 * C]) for pr in pairs], axis=1)
             for s in range(NB)], axis=0)
        fill(len(pieces))
        yn = _head_norm(y, WKV_GN_EPS, ones_blk) * lnw_ref[...] + lnb_ref[...]
        cat_s[rows, 0:HEADS_W] = ret_out
        cat_s[rows, HEADS_W:2 * HEADS_W] = ((yn + bonus) * gate).astype(BF16)
        for s in range(NB):
            carry_s[pl.ds(b0 + s, 1), :] = wp[s * C + C - 1:(s + 1) * C, :]
        for s, pr in chains:
            rs_s[pr, b0 + s] = new_ret[(s, pr)]
            ws_s[pr, b0 + s] = new_wkv[(s, pr)]
        return carry

    @pl.when(i < n_chunks)
    def _():
        lax.fori_loop(0, n_seq // NB, per_group, 0)

    @pl.when(i == n_chunks)
    def _():
        out = jnp.dot(cat_s[...], wout_ref[...], preferred_element_type=F32)
        xo_ref[...] = xprev_ref[...] + out.reshape(n_seq, C, D_MODEL)

    @pl.when(i == n_chunks - 1)
    def _():
        shift_ref[...] = carry_s[...]
        for b in range(n_seq):
            for pr in range(N_PAIRS):
                rs = rs_s[pr, b]
                ws = ws_s[pr, b]
                rets_ref[b, 2 * pr] = rs[0:HEAD_DIM, 0:HEAD_DIM]
                rets_ref[b, 2 * pr + 1] = rs[HEAD_DIM:PAIR_W, HEAD_DIM:PAIR_W]
                wkvs_ref[b, 2 * pr] = ws[0:HEAD_DIM, 0:HEAD_DIM]
                wkvs_ref[b, 2 * pr + 1] = ws[HEAD_DIM:PAIR_W, HEAD_DIM:PAIR_W]


def _const_spec(shape):
    nd = len(shape)
    return pl.BlockSpec(shape, lambda *_: (0,) * nd, pipeline_mode=pl.Buffered(1))


def _const_out(shape):
    nd = len(shape)
    return pl.BlockSpec(shape, lambda *_: (0,) * nd)


def _const(a, dtype=F32):
    return jnp.asarray(np.asarray(a, np.float64), dtype=dtype)


def _retention_tables(chunk):
    log_g = np.log1p(-np.exp2(-5.0 - np.arange(N_HEADS, dtype=np.float64)))
    lane_g = np.repeat(log_g, HEAD_DIM)[None, :]
    idx = np.arange(chunk, dtype=np.float64)
    qdec = np.exp((idx + 1.0)[:, None] * lane_g)
    kdec = np.exp((chunk - 1.0 - idx)[:, None] * lane_g)
    rel = idx[:, None] - idx[None, :]
    dm = np.where(rel >= 0, np.exp(np.maximum(rel, 0.0)[None] * log_g[:, None, None]), 0.0)
    zero = np.zeros((chunk, chunk))
    dmask = np.stack([np.block([[dm[2 * p], zero], [zero, dm[2 * p + 1]]]) for p in range(N_PAIRS)])
    cdec = np.exp(chunk * log_g)
    hz = np.zeros((HEAD_DIM, HEAD_DIM))
    ho = np.ones((HEAD_DIM, HEAD_DIM))
    sdec = np.stack([np.block([[cdec[2 * p] * ho, hz], [hz, cdec[2 * p + 1] * ho]])
                     for p in range(N_PAIRS)])
    return _const(qdec), _const(kdec), _const(dmask), _const(sdec), cdec


def _rope_tables(pos):
    half = HEAD_DIM // 2
    inv_freq = ROPE_BASE ** (-np.arange(half, dtype=np.float64) / half)
    ang = np.asarray(pos, np.float64)[:, None] * inv_freq[None, :]
    cos = np.cos(ang)
    sin = np.sin(ang)
    cos_t = np.tile(np.concatenate([cos, cos], axis=1), (1, N_HEADS))
    sin_t = np.tile(np.concatenate([-sin, sin], axis=1), (1, N_HEADS))
    return _const(cos_t), _const(sin_t)


def _block_masks(chunk, group):
    i = np.arange(2 * chunk)
    same = (i[:, None] // chunk) == (i[None, :] // chunk)
    strict = same & (i[:, None] > i[None, :])
    incl = same & (i[:, None] >= i[None, :])
    j = np.arange(PAIR_W)
    bd = (j[:, None] // HEAD_DIM) == (j[None, :] // HEAD_DIM)
    t = np.arange(chunk)
    tril = np.kron(np.eye(group), t[:, None] >= t[None, :])
    o = np.arange(MXU_TILE)
    ones_blk = (o[:, None] // HEAD_DIM) == (o[None, :] // HEAD_DIM)
    return _const(strict), _const(incl), _const(bd), _const(tril, BF16), _const(ones_blk, BF16)


def _lora_block(w_b, a_b):
    z = jnp.zeros_like(w_b)
    return jnp.concatenate([jnp.concatenate([w_b, z], axis=1),
                            jnp.concatenate([z, a_b], axis=1)], axis=0).astype(BF16)


def _row(v):
    return v.reshape(1, -1).astype(F32)


def _mixer_prompt(x, lw):
    n_seq, t_len, _ = x.shape
    C = MIX_CHUNK
    n_chunks = t_len // C
    last = n_chunks - 1
    cos_t, sin_t = _rope_tables(np.arange(t_len))
    qdec, kdec, dmask, sdec, _ = _retention_tables(C)
    strict, incl, bd, tril, ones_blk = _block_masks(C, MIX_GROUP)
    prev_chunk = lambda i: (0, jnp.maximum(i - 1, 0), 0)
    in_specs = [
        pl.BlockSpec((n_seq, C, D_MODEL), prev_chunk),
        pl.BlockSpec((n_seq, C, D_MODEL), lambda i: (0, jnp.minimum(i + 1, last), 0)),
        _const_spec((1, D_MODEL)),
        _const_spec((D_MODEL, IN_W)),
        pl.BlockSpec((C, HEADS_W), lambda i: (jnp.minimum(i, last), 0)),
        pl.BlockSpec((C, HEADS_W), lambda i: (jnp.minimum(i, last), 0)),
    ]
    tail = [qdec, kdec, dmask, sdec, lw["ret_gn"], lw["mu"], lw["w0"], lw["lora"], lw["a0"], lw["g_b"],
            lw["k_k"], lw["k_a"], lw["r_k"], lw["ln_w"], lw["ln_b"], lw["w_out"], ones_blk, tril,
            strict, incl, bd]
    in_specs += [_const_spec(a.shape) for a in tail]
    out_shape = (
        jax.ShapeDtypeStruct((n_seq, t_len, D_MODEL), F32),
        jax.ShapeDtypeStruct((n_seq, N_HEADS, HEAD_DIM, HEAD_DIM), F32),
        jax.ShapeDtypeStruct((n_seq, N_HEADS, HEAD_DIM, HEAD_DIM), F32),
        jax.ShapeDtypeStruct((n_seq, SHIFT_W), F32),
    )
    out_specs = (
        pl.BlockSpec((n_seq, C, D_MODEL), prev_chunk),
        _const_out((n_seq, N_HEADS, HEAD_DIM, HEAD_DIM)),
        _const_out((n_seq, N_HEADS, HEAD_DIM, HEAD_DIM)),
        _const_out((n_seq, SHIFT_W)),
    )
    scratch = [
        pltpu.VMEM((2, n_seq * C, IN_W), F32),
        pltpu.VMEM((n_seq * C, 2 * HEADS_W), BF16),
        pltpu.VMEM((N_PAIRS, n_seq, PAIR_W, PAIR_W), F32),
        pltpu.VMEM((N_PAIRS, n_seq, PAIR_W, PAIR_W), F32),
        pltpu.VMEM((n_seq, SHIFT_W), F32),
    ]
    return pl.pallas_call(
        functools.partial(_mixer_prompt_kernel, n_seq=n_seq, chunk=C, group=MIX_GROUP),
        grid=(n_chunks + 1,),
        in_specs=in_specs, out_specs=out_specs, out_shape=out_shape, scratch_shapes=scratch,
        compiler_params=pltpu.CompilerParams(dimension_semantics=("arbitrary",),
                                             vmem_limit_bytes=VMEM_LIMIT),
        name="mixer_prompt",
    )(x, x, lw["norm"], lw["w_in"], cos_t, sin_t, *tail)


def _mixer_sample_pre_kernel(
        x_ref, shift_ref, normw_ref, win_ref, cos_ref, sin_ref, qdec_ref, mu_ref, w0_ref, lora_ref,
        a0_ref, gb_ref, kk_ref, ka_ref, rk_ref, ones_ref, feat_ref, feat_t_ref, newshift_ref):
    x = x_ref[...]
    p = _dot(_rms(x, normw_ref[...]), win_ref[...])
    q = _rope(p[:, 0:HEADS_W], cos_ref[...], sin_ref[...])
    k = _rope(p[:, HEADS_W:2 * HEADS_W], cos_ref[...], sin_ref[...]) * (HEAD_DIM ** -0.5)
    wp = p[:, RET_COLS:IN_W]
    xs = wp + (shift_ref[...] - wp) * mu_ref[...]
    r, lw, k_mod, vw, a_vec, b_vec, gate, bonus = _wkv_features(
        xs, w0_ref[...], lora_ref[...], a0_ref[...], gb_ref[...], kk_ref[...], ka_ref[...],
        rk_ref[...], ones_ref[...])
    newshift_ref[...] = wp
    state_feats = [q, q * qdec_ref[...], k, p[:, 2 * HEADS_W:3 * HEADS_W], r, jnp.exp(lw), k_mod, vw,
                   a_vec, b_vec]
    for n, f in enumerate(state_feats):
        feat_t_ref[n * HEADS_W:(n + 1) * HEADS_W, :] = f.T
    for n, f in enumerate([p[:, 3 * HEADS_W:4 * HEADS_W], gate, bonus]):
        feat_ref[:, n * HEADS_W:(n + 1) * HEADS_W] = f


_F_Q, _F_QD, _F_K, _F_V, _F_R, _F_W, _F_KM, _F_VW, _F_A, _F_B = range(10)
_N_STATE_FEATS = 10
_F_G, _F_GATE, _F_BONUS = range(3)
_N_ROW_FEATS = 3


def _mixer_sample_state_kernel(feat_t_ref, sdec_ref, ret_ref, wkv_ref, o_t_ref, reto_ref, wkvo_ref):
    h = pl.program_id(0)

    def head_rows(n):
        return feat_t_ref[pl.ds(pl.multiple_of(n * HEADS_W + h * HEAD_DIM, HEAD_DIM), HEAD_DIM), :]

    def head_row(n, i):
        return feat_t_ref[pl.ds(n * HEADS_W + h * HEAD_DIM + i, 1), :]

    a, w, b_vec, k_mod, r = (head_rows(n) for n in (_F_A, _F_W, _F_B, _F_KM, _F_R))

    def wkv_row(i, carry):
        s = wkv_ref[0, i]
        sa = jnp.sum(s * a, axis=0, keepdims=True)
        s_new = s * w + sa * b_vec + head_row(_F_VW, i) * k_mod
        wkvo_ref[0, i] = s_new
        o_t_ref[1, 0, pl.ds(i, 1), :] = jnp.sum(s_new * r, axis=0, keepdims=True)
        return carry

    lax.fori_loop(0, HEAD_DIM, wkv_row, 0, unroll=4)

    v, q, k = head_rows(_F_V), head_rows(_F_Q), head_rows(_F_K)
    g = sdec_ref[h]

    def ret_row(d, acc):
        s = ret_ref[0, d]
        reto_ref[0, d] = g * s + head_row(_F_K, d) * v
        return acc + head_row(_F_QD, d) * s

    cross = lax.fori_loop(0, HEAD_DIM, ret_row, jnp.zeros_like(v), unroll=4)
    o_t_ref[0, 0] = cross + jnp.sum(q * k, axis=0, keepdims=True) * v


def _mixer_sample_post_kernel(
        x_ref, feat_ref, o_t_ref, retgn_ref, lnw_ref, lnb_ref, wout_ref, ones_ref, xo_ref):
    ones_blk = ones_ref[...]
    g = feat_ref[:, _F_G * HEADS_W:(_F_G + 1) * HEADS_W]
    gate = feat_ref[:, _F_GATE * HEADS_W:(_F_GATE + 1) * HEADS_W]
    bonus = feat_ref[:, _F_BONUS * HEADS_W:(_F_BONUS + 1) * HEADS_W]
    o = o_t_ref[...].T
    ret = _head_norm(o[:, 0:HEADS_W], GN_EPS, ones_blk) * retgn_ref[...]
    ret_out = g * _sigmoid(g) * ret
    yn = _head_norm(o[:, HEADS_W:2 * HEADS_W], WKV_GN_EPS, ones_blk) * lnw_ref[...] + lnb_ref[...]
    wkv_out = (yn + bonus) * gate
    cat = jnp.concatenate([ret_out, wkv_out], axis=1)
    xo_ref[...] = x_ref[...] + _dot(cat, wout_ref[...])


def _mixer_sample(x, ret_t, wkv_t, shift0, lw):
    n = x.shape[0]
    cos_t, sin_t = _rope_tables(np.full((1,), PAST_LEN))
    qdec, _, _, _, cdec = _retention_tables(1)
    _, _, _, _, ones_blk = _block_masks(1, 1)
    args = [x, shift0, lw["norm"], lw["w_in"], cos_t, sin_t, qdec, lw["mu"], lw["w0"], lw["lora"],
            lw["a0"], lw["g_b"], lw["k_k"], lw["k_a"], lw["r_k"], ones_blk]
    feat, feat_t, new_shift = pl.pallas_call(
        _mixer_sample_pre_kernel,
        grid=(1,),
        in_specs=[_const_spec(a.shape) for a in args],
        out_specs=(_const_out((n, _N_ROW_FEATS * HEADS_W)), _const_out((_N_STATE_FEATS * HEADS_W, n)),
                   _const_out((n, SHIFT_W))),
        out_shape=(jax.ShapeDtypeStruct((n, _N_ROW_FEATS * HEADS_W), F32),
                   jax.ShapeDtypeStruct((_N_STATE_FEATS * HEADS_W, n), F32),
                   jax.ShapeDtypeStruct((n, SHIFT_W), F32)),
        compiler_params=pltpu.CompilerParams(vmem_limit_bytes=VMEM_LIMIT),
        name="mixer_sample_pre",
    )(*args)

    sdec = _const(np.broadcast_to(cdec[:, None, None], (N_HEADS, 1, n)))
    state_spec = pl.BlockSpec((1, HEAD_DIM, HEAD_DIM, n), lambda h: (h, 0, 0, 0))
    o_spec = pl.BlockSpec((2, 1, HEAD_DIM, n), lambda h: (0, h, 0, 0))
    o_t, ret_new, wkv_new = pl.pallas_call(
        _mixer_sample_state_kernel,
        grid=(N_HEADS,),
        in_specs=[_const_spec(feat_t.shape), _const_spec(sdec.shape), state_spec, state_spec],
        out_specs=(o_spec, state_spec, state_spec),
        out_shape=(jax.ShapeDtypeStruct((2, N_HEADS, HEAD_DIM, n), F32),
                   jax.ShapeDtypeStruct(ret_t.shape, F32), jax.ShapeDtypeStruct(wkv_t.shape, F32)),
        compiler_params=pltpu.CompilerParams(dimension_semantics=("arbitrary",),
                                             vmem_limit_bytes=VMEM_LIMIT),
        name="mixer_sample_state",
    )(feat_t, sdec, ret_t, wkv_t)

    args = [x, feat, o_t.reshape(2 * HEADS_W, n), lw["ret_gn"], lw["ln_w"], lw["ln_b"], lw["w_out"], ones_blk]
    x1 = pl.pallas_call(
        _mixer_sample_post_kernel,
        grid=(1,),
        in_specs=[_const_spec(a.shape) for a in args],
        out_specs=_const_out((n, D_MODEL)),
        out_shape=jax.ShapeDtypeStruct((n, D_MODEL), F32),
        compiler_params=pltpu.CompilerParams(vmem_limit_bytes=VMEM_LIMIT),
        name="mixer_sample_post",
    )(*args)
    return x1, ret_new, wkv_new, new_shift


def _mlp_kernel(xa_ref, xb_ref, normw_ref, wup_ref, wdown_ref, normf_ref, oa_ref, ob_ref,
                *, final_norm, steps_a):
    i = pl.program_id(0)

    def run(x_ref, o_ref):
        hn = _rms(x_ref[...], normw_ref[...]).astype(BF16)
        o_ref[...] = x_ref[...]
        for c in range(D_FF // FF_CHUNK):
            sl = slice(c * FF_CHUNK, (c + 1) * FF_CHUNK)
            hid = jnp.dot(hn, wup_ref[:, sl].astype(BF16), preferred_element_type=F32)
            hid = jnp.square(jnp.maximum(hid, 0.0)).astype(BF16)
            o_ref[...] = o_ref[...] + jnp.dot(hid, wdown_ref[sl, :].astype(BF16), preferred_element_type=F32)
        if final_norm:
            o_ref[...] = _rms(o_ref[...], normf_ref[...])

    @pl.when(i < steps_a)
    def _():
        run(xa_ref, oa_ref)

    @pl.when(i == steps_a)
    def _():
        run(xb_ref, ob_ref)


def _mlp(xa, xb, norm_w, w_up, w_down, norm_f, layer, final_norm):
    rows_a, rows_b = xa.shape[0], xb.shape[0]
    steps_a = rows_a // MLP_ROWS
    pick = lambda *_: (layer, 0, 0)
    block_a = pl.BlockSpec((MLP_ROWS, D_MODEL), lambda i: (jnp.minimum(i, steps_a - 1), 0))
    return pl.pallas_call(
        functools.partial(_mlp_kernel, final_norm=final_norm, steps_a=steps_a),
        grid=(steps_a + 1,),
        in_specs=[block_a, _const_spec((rows_b, D_MODEL)),
                  _const_spec((1, D_MODEL)),
                  pl.BlockSpec((None, D_MODEL, D_FF), pick, pipeline_mode=pl.Buffered(1)),
                  pl.BlockSpec((None, D_FF, D_MODEL), pick, pipeline_mode=pl.Buffered(1)),
                  _const_spec((1, D_MODEL))],
        out_specs=(block_a, _const_out((rows_b, D_MODEL))),
        out_shape=(jax.ShapeDtypeStruct((rows_a, D_MODEL), F32), jax.ShapeDtypeStruct((rows_b, D_MODEL), F32)),
        compiler_params=pltpu.CompilerParams(dimension_semantics=("arbitrary",),
                                             vmem_limit_bytes=VMEM_LIMIT),
        name="mlp",
    )(xa, xb, norm_w, w_up, w_down, norm_f)


def _ssm_prep_kernel(lre_ref, lim_ref, logdt_ref, lre_rep_ref, lim_rep_ref, logdt_rep_ref, bre_ref, bim_ref,
                     cre_in_ref, cim_in_ref, tile_ref, eye_ref, are_ref, aim_ref, wb_ref, cre_ref, cim_ref):
    def discretise(lre, lim, logdt):
        lre = jnp.minimum(lre, -1e-4)
        dt = jnp.exp(logdt)
        mag = jnp.exp(lre * dt)
        return lre, mag * jnp.cos(lim * dt), mag * jnp.sin(lim * dt)

    _, are_ref[...], aim_ref[...] = discretise(lre_ref[...], lim_ref[...], logdt_ref[...])
    lim = lim_rep_ref[...]
    lre, are, aim = discretise(lre_rep_ref[...], lim, logdt_rep_ref[...])
    den = lre * lre + lim * lim
    nre = are - 1.0
    cre = (nre * lre + aim * lim) / den
    cim = (aim * lre - nre * lim) / den
    bre = bre_ref[...]
    bim = bim_ref[...]
    bb = (cre * bre - cim * bim, cre * bim + cim * bre)

    in_mask = (lax.broadcasted_iota(jnp.int32, (LANES, 1), 0) // SSM_GROUP
               == lax.broadcasted_iota(jnp.int32, (1, SSM_HALF), 1) // SSM_P)
    out_group = lax.broadcasted_iota(jnp.int32, (1, LANES), 1) // SSM_GROUP
    for blk in range(SSM_BLOCKS):
        rows = slice(blk * LANES, (blk + 1) * LANES)
        for part in range(2):
            tiled = jnp.dot(bb[part][rows].astype(BF16), tile_ref[...], preferred_element_type=F32)
            wb_ref[blk, :, part * SSM_HALF:(part + 1) * SSM_HALF] = jnp.where(in_mask, tiled, 0.0).astype(BF16)
        for src, dst in ((cre_in_ref, cre_ref), (cim_in_ref, cim_ref)):
            c_t = _dot_tn(src[rows, :], eye_ref[...])
            dst[blk] = jnp.concatenate([jnp.where(out_group == gl, c_t, 0.0) for gl in range(SSM_BLOCK_G)],
                                       axis=0).astype(BF16)


def _gelu_exact(x):
    return 0.5 * x * (1.0 + lax.erf(x * (2.0 ** -0.5)))


def _ssm_kernel(x_ref, hre0_ref, him0_ref, normw_ref, are_ref, aim_ref, wb_ref, cre_ref, cim_ref, dskip_ref,
                wglu_ref, xo_ref, hre_ref, him_ref, xt_s, u_s, bu_s, y_s, hg_s, hre_s, him_s,
                *, n_seq, chunk, batch_major):
    i = pl.program_id(0)
    n_steps = pl.num_programs(0)
    rows_all = chunk * n_seq
    n_slabs = D_MODEL // LANES

    @pl.when(i == 0)
    def _():
        hre_s[...] = hre0_ref[...] if batch_major else hre0_ref[...].T
        him_s[...] = him0_ref[...] if batch_major else him0_ref[...].T

    if batch_major:
        for b in range(n_seq):
            for sl in range(n_slabs):
                xt_s[sl, pl.ds(b, chunk, stride=n_seq), :] = x_ref[b, :, sl * LANES:(sl + 1) * LANES]
        ssq = sum(jnp.sum(jnp.square(xt_s[sl]), axis=-1, keepdims=True) for sl in range(n_slabs))
        inv = lax.rsqrt(ssq * (1.0 / D_MODEL) + RMS_EPS)
        for sl in range(n_slabs):
            cols = slice(sl * LANES, (sl + 1) * LANES)
            u_s[:, cols] = xt_s[sl] * inv * normw_ref[:, cols]
    else:
        u_s[...] = _rms(x_ref[...].reshape(rows_all, D_MODEL), normw_ref[...])

    def input_proj(blk, part):
        c = slice(part * SSM_HALF, (part + 1) * SSM_HALF)
        bu_s[blk % 2, :, c] = _dot(u_s[:, blk * LANES:(blk + 1) * LANES], wb_ref[blk, :, c])

    def output_proj(blk, part):
        buf = bu_s.at[blk % 2]
        cols = slice(blk * LANES, (blk + 1) * LANES)
        if part == 0:
            y_s[:, cols] = _dot(buf[:, 0:SSM_HALF], cre_ref[blk])
        else:
            y_s[:, cols] = y_s[:, cols] - _dot(buf[:, SSM_HALF:2 * SSM_HALF], cim_ref[blk])

    input_proj(0, 0)
    input_proj(0, 1)
    for blk in range(SSM_BLOCKS):
        neighbours = []
        if blk >= 1:
            neighbours += [functools.partial(output_proj, blk - 1, 0), functools.partial(output_proj, blk - 1, 1)]
        if blk + 1 < SSM_BLOCKS:
            neighbours += [functools.partial(input_proj, blk + 1, 0), functools.partial(input_proj, blk + 1, 1)]
        buf = bu_s.at[blk % 2]
        cols = slice(blk * SSM_HALF, (blk + 1) * SSM_HALF)
        a_re = jnp.broadcast_to(are_ref[:, cols], (n_seq, SSM_HALF))
        a_im = jnp.broadcast_to(aim_ref[:, cols], (n_seq, SSM_HALF))
        h_re = hre_s[:, cols]
        h_im = him_s[:, cols]
        seg = -(-chunk // max(len(neighbours), 1))
        for t in range(chunk):
            if t % seg == 0 and neighbours:
                neighbours.pop(0)()
            rows = slice(t * n_seq, (t + 1) * n_seq)
            n_re = a_re * h_re - a_im * h_im + buf[rows, 0:SSM_HALF]
            n_im = a_re * h_im + a_im * h_re + buf[rows, SSM_HALF:2 * SSM_HALF]
            buf[rows, 0:SSM_HALF] = n_re
            buf[rows, SSM_HALF:2 * SSM_HALF] = n_im
            h_re, h_im = n_re, n_im
        for f in neighbours:
            f()
        hre_s[:, cols] = h_re
        him_s[:, cols] = h_im
    output_proj(SSM_BLOCKS - 1, 0)
    output_proj(SSM_BLOCKS - 1, 1)
    hg_s[...] = _gelu_exact(y_s[...] + dskip_ref[...] * u_s[...]).astype(BF16)
    glu_w = MXU_TILE
    for c in range(D_MODEL // glu_w):
        cols = slice(c * glu_w, (c + 1) * glu_w)
        hg = hg_s[...]
        val = jnp.dot(hg, wglu_ref[:, cols], preferred_element_type=F32)
        gate = jnp.dot(hg, wglu_ref[:, D_MODEL + c * glu_w:D_MODEL + (c + 1) * glu_w],
                       preferred_element_type=F32)
        out = val * _sigmoid(gate)
        if batch_major:
            for sl in range(c * glu_w // LANES, (c + 1) * glu_w // LANES):
                xt_s[sl] = xt_s[sl] + out[:, sl * LANES - c * glu_w:(sl + 1) * LANES - c * glu_w]
        else:
            y_s[:, cols] = x_ref[...].reshape(rows_all, D_MODEL)[:, cols] + out
    if batch_major:
        for b in range(n_seq):
            for sl in range(n_slabs):
                xo_ref[b, :, sl * LANES:(sl + 1) * LANES] = xt_s[sl, pl.ds(b, chunk, stride=n_seq), :]
    else:
        xo_ref[...] = y_s[...].reshape(chunk, n_seq, D_MODEL)

    @pl.when(i == n_steps - 1)
    def _():
        hre_ref[...] = hre_s[...] if batch_major else hre_s[...].T
        him_ref[...] = him_s[...] if batch_major else him_s[...].T


def _ssm_weights(lam_re, lam_im, log_dt, b_re, b_im, c_re, c_im):
    g, p = SSM_GROUPS, SSM_P
    n = g * SSM_GROUP
    rep = lambda z: jnp.repeat(z, SSM_GROUP, axis=0)
    log_dt = log_dt.reshape(g, 1)
    bt_re = jnp.swapaxes(b_re, 1, 2).reshape(n, p)
    bt_im = jnp.swapaxes(b_im, 1, 2).reshape(n, p)
    tile = _const(np.tile(np.eye(p), (1, SSM_BLOCK_G)), BF16)
    eye = _const(np.eye(LANES), BF16)
    args = [lam_re, lam_im, log_dt, rep(lam_re), rep(lam_im), rep(log_dt), bt_re, bt_im,
            c_re.reshape(n, p), c_im.reshape(n, p), tile, eye]
    out_shapes = ((g, p), (g, p), (SSM_BLOCKS, LANES, 2 * SSM_HALF), (SSM_BLOCKS, SSM_HALF, LANES),
                  (SSM_BLOCKS, SSM_HALF, LANES))
    out_dtypes = (F32, F32, BF16, BF16, BF16)
    a_re, a_im, w_b, c_re_bd, c_im_bd = pl.pallas_call(
        _ssm_prep_kernel,
        grid=(1,),
        in_specs=[_const_spec(a.shape) for a in args],
        out_specs=tuple(_const_out(s) for s in out_shapes),
        out_shape=tuple(jax.ShapeDtypeStruct(s, d) for s, d in zip(out_shapes, out_dtypes)),
        name="ssm_prep",
    )(*args)
    n_state = g * p
    return a_re.reshape(1, n_state), a_im.reshape(1, n_state), w_b, c_re_bd, c_im_bd


def _ssm_layer(x, h_re0, h_im0, sw, chunk, batch_major):
    if batch_major:
        n_seq, t_len, _ = x.shape
        x_block = (n_seq, chunk, D_MODEL)
        x_map = lambda i: (0, i, 0)
    else:
        t_len, n_seq, _ = x.shape
        x_block = (chunk, n_seq, D_MODEL)
        x_map = lambda i: (i, 0, 0)
    rows = chunk * n_seq
    n_state = SSM_GROUPS * SSM_P
    args = [x, h_re0, h_im0, sw["norm"], sw["a_re"], sw["a_im"], sw["w_b"], sw["c_re"], sw["c_im"],
            sw["d_skip"], sw["w_glu"]]
    in_specs = [pl.BlockSpec(x_block, x_map)] + [_const_spec(a.shape) for a in args[1:]]
    state = jax.ShapeDtypeStruct((n_seq, n_state) if batch_major else (n_state, n_seq), F32)
    return pl.pallas_call(
        functools.partial(_ssm_kernel, n_seq=n_seq, chunk=chunk, batch_major=batch_major),
        grid=(t_len // chunk,),
        in_specs=in_specs,
        out_specs=(pl.BlockSpec(x_block, x_map), _const_out(state.shape), _const_out(state.shape)),
        out_shape=(jax.ShapeDtypeStruct(x.shape, F32), state, state),
        scratch_shapes=[pltpu.VMEM((D_MODEL // LANES, rows, LANES), F32),
                        pltpu.VMEM((rows, D_MODEL), F32), pltpu.VMEM((2, rows, 2 * SSM_HALF), F32),
                        pltpu.VMEM((rows, D_MODEL), F32), pltpu.VMEM((rows, D_MODEL), BF16),
                        pltpu.VMEM((n_seq, n_state), F32), pltpu.VMEM((n_seq, n_state), F32)],
        compiler_params=pltpu.CompilerParams(dimension_semantics=("arbitrary",),
                                             vmem_limit_bytes=VMEM_LIMIT),
        name="ssm_layer",
    )(*args)


def kernel(x_prompt, x_sample, state_ret, state_wkv, state_shift, state_ssm_re, state_ssm_im, norm_mix, w_in, ret_gn, mu_shift, wkv_w0, wkv_wB, wkv_a0, wkv_aB, wkv_gB, wkv_kk, wkv_ka, wkv_rk, wkv_ln_w, wkv_ln_b, w_out, ssm_lambda_re, ssm_lambda_im, ssm_log_dt, ssm_B_re, ssm_B_im, ssm_C_re, ssm_C_im, ssm_D, ssm_w_glu, mlp_norm, mlp_up, mlp_down, norm_f):
    lw = dict(
        norm=_row(norm_mix[0]), w_in=w_in[0].astype(BF16), ret_gn=_row(ret_gn[0]), mu=_row(mu_shift[0]),
        w0=_row(wkv_w0[0]), lora=_lora_block(wkv_wB[0], wkv_aB[0]), a0=_row(wkv_a0[0]),
        g_b=wkv_gB[0].astype(BF16), k_k=_row(wkv_kk[0]), k_a=_row(wkv_ka[0]), r_k=_row(wkv_rk[0]),
        ln_w=_row(wkv_ln_w[0]), ln_b=_row(wkv_ln_b[0]), w_out=w_out[0].astype(BF16))
    a_re, a_im, w_b, c_re, c_im = _ssm_weights(ssm_lambda_re[0], ssm_lambda_im[0], ssm_log_dt[0],
                                               ssm_B_re[0], ssm_B_im[0], ssm_C_re[0], ssm_C_im[0])
    sw = dict(norm=_row(norm_mix[1]), a_re=a_re, a_im=a_im, w_b=w_b, c_re=c_re, c_im=c_im,
              d_skip=_row(ssm_D[0]), w_glu=ssm_w_glu[0].astype(BF16))
    n_state = SSM_GROUPS * SSM_P
    nf = _row(norm_f)

    n_p, t_p, _ = x_prompt.shape
    n_s = x_sample.shape[0]
    x1, ret_p, wkv_p, shift_p = _mixer_prompt(x_prompt, lw)
    seq_last = lambda s: jnp.transpose(s, (1, 2, 3, 0))
    seq_first = lambda s: jnp.transpose(s, (3, 0, 1, 2))
    xs1, ret_s, wkv_s, shift_s = _mixer_sample(x_sample.reshape(n_s, D_MODEL), seq_last(state_ret[0]),
                                               seq_last(state_wkv[0]), state_shift[0], lw)
    ret_s, wkv_s = seq_first(ret_s), seq_first(wkv_s)
    x1, xs1 = _mlp(x1.reshape(n_p * t_p, D_MODEL), xs1, _row(mlp_norm[0]), mlp_up, mlp_down, nf, 0, False)
    zero_state = jnp.zeros((n_p, n_state), F32)
    x2, ssm_re_p, ssm_im_p = _ssm_layer(x1.reshape(n_p, t_p, D_MODEL), zero_state, zero_state, sw,
                                        SSM_CHUNK, True)
    ssm_seq_last = lambda s: jnp.transpose(s, (1, 2, 0)).reshape(n_state, n_s)
    xs2, ssm_re_s, ssm_im_s = _ssm_layer(
        xs1.reshape(1, n_s, D_MODEL), ssm_seq_last(state_ssm_re[0]), ssm_seq_last(state_ssm_im[0]),
        sw, 1, False)
    y_p, y_s = _mlp(x2.reshape(n_p * t_p, D_MODEL), xs2.reshape(n_s, D_MODEL), _row(mlp_norm[1]),
                    mlp_up, mlp_down, nf, 1, True)
    y_prompt = y_p.reshape(n_p, t_p, D_MODEL)
    ssm_re_p = ssm_re_p.reshape(n_p, SSM_GROUPS, SSM_P)
    ssm_im_p = ssm_im_p.reshape(n_p, SSM_GROUPS, SSM_P)
    ssm_seq_first = lambda s: jnp.transpose(s.reshape(SSM_GROUPS, SSM_P, n_s), (2, 0, 1))
    ssm_re_s, ssm_im_s = ssm_seq_first(ssm_re_s), ssm_seq_first(ssm_im_s)

    return (y_prompt, y_s.reshape(n_s, 1, D_MODEL),
            ret_p[None], wkv_p[None], shift_p[None], ssm_re_p[None], ssm_im_p[None],
            ret_s[None], wkv_s[None], shift_s[None], ssm_re_s[None], ssm_im_s[None])
```
